```python
import math
import jax, jax.numpy as jnp
from jax import lax
import numpy as np

D_MODEL = 1024
BATCH = 16
SEQ = 4096
DEPTH = 2
DEC_BATCH = 8
DEC_SEQ = 8192
PAST_LEN = 128

ATT_HEADS = 4
ATT_QK_DIM = 64
ATT_V_DIM = 2 * ATT_QK_DIM
ATT_WIDTH = ATT_HEADS * ATT_V_DIM
Q_BLOCK = 128
HG_HEADS = 4
HG_K = 128
HG_V = 128
HG_WIDTH = HG_HEADS * HG_V
HG_CHUNK = 64
MIX_WIDTH = ATT_WIDTH + HG_WIDTH
IN_SPLITS = (ATT_HEADS * 2 * ATT_QK_DIM, ATT_HEADS * 2 * ATT_QK_DIM, ATT_WIDTH,
             HG_HEADS * HG_K, HG_HEADS * HG_K, HG_HEADS * HG_K, HG_WIDTH, HG_WIDTH)
IN_WIDTH = sum(IN_SPLITS)
N_EXPERTS = 16
EC_CAPACITY = 2
EXPERT_FF = 1024
EPS = 1e-6

kernel_name = 'hybrid_diffattn_hgrn2_ec_encoder'


def rms_norm(x, g):
    xf = x.astype(jnp.float32)
    y = xf * lax.rsqrt(jnp.mean(xf * xf, axis=-1, keepdims=True) + EPS)
    return (y * g.astype(jnp.float32)).astype(x.dtype)


def diff_attention(q, k, v, lam):
    B, T = q.shape[0], q.shape[1]
    nb = T // Q_BLOCK
    scale = ATT_QK_DIM ** -0.5
    slopes = 2.0 ** (-8.0 * (jnp.arange(ATT_HEADS, dtype=jnp.float32) + 1.0) / ATT_HEADS)
    qb = q.reshape(B, nb, Q_BLOCK, ATT_HEADS, 2, ATT_QK_DIM).transpose(1, 0, 3, 4, 2, 5)
    kt = k.transpose(0, 2, 3, 1, 4)
    vt = v.transpose(0, 2, 1, 3)
    kpos = jnp.arange(T, dtype=jnp.float32)

    def block(args):
        qblk, bi = args
        s = jnp.einsum('bhjqd,bhjkd->bhjqk', qblk, kt).astype(jnp.float32) * scale
        qpos = bi.astype(jnp.float32) * Q_BLOCK + jnp.arange(Q_BLOCK, dtype=jnp.float32)
        dist = jnp.abs(qpos[:, None] - kpos[None, :])
        s = s - (slopes[:, None, None] * dist[None])[None, :, None]
        p = jax.nn.softmax(s, axis=-1)
        w = (p[:, :, 0] - lam * p[:, :, 1]).astype(vt.dtype)
        return jnp.einsum('bhqk,bhkv->bhqv', w, vt)

    o = lax.map(block, (qb, jnp.arange(nb)))
    return o.transpose(1, 0, 3, 2, 4).reshape(B, T, ATT_HEADS, ATT_V_DIM)


def hgrn2_chunk_scan(q, k, v, logf):
    B, T, H, K = q.shape
    V = v.shape[-1]
    n = T // HG_CHUNK

    def to_chunks(a):
        return a.reshape(B, n, HG_CHUNK, H, a.shape[-1]).transpose(1, 0, 3, 2, 4).astype(jnp.float32)

    mask = jnp.tril(jnp.ones((HG_CHUNK, HG_CHUNK), dtype=bool))[:, :, None]

    def step(S, xs):
        qc, kc, vc, gc = xs
        b = jnp.cumsum(gc, axis=2)
        diff = b[:, :, :, None, :] - b[:, :, None, :, :]
        dec = jnp.exp(jnp.where(mask, diff, -jnp.inf))
        A = jnp.einsum('bhtk,bhsk,bhtsk->bhts', qc, kc, dec)
        o = jnp.einsum('bhts,bhsv->bhtv', A, vc) + jnp.einsum('bhtk,bhkv->bhtv', qc * jnp.exp(b), S)
        b_last = b[:, :, -1, :]
        S_new = jnp.exp(b_last)[..., None] * S + jnp.einsum(
            'bhsk,bhsv->bhkv', kc * jnp.exp(b_last[:, :, None, :] - b), vc)
        return S_new, o

    S0 = jnp.zeros((B, H, K, V), jnp.float32)
    _, o = lax.scan(step, S0, (to_chunks(q), to_chunks(k), to_chunks(v), to_chunks(logf)))
    return o.transpose(1, 0, 3, 2, 4).reshape(B, T, H, V)


def mixer(h, layer, w_in, lq1, lk1, lq2, lk2, subln_g, lb, hg_norm_g, w_out):
    B, T, _ = h.shape
    proj = h @ w_in
    aq, ak, av, hq, hff, hfb, hi, hg = jnp.split(proj, list(np.cumsum(IN_SPLITS)[:-1]), axis=-1)
    lam_init = 0.8 - 0.6 * math.exp(-0.3 * layer)
    lam = (jnp.exp(jnp.sum(lq1.astype(jnp.float32) * lk1.astype(jnp.float32)))
           - jnp.exp(jnp.sum(lq2.astype(jnp.float32) * lk2.astype(jnp.float32))) + lam_init)
    att = diff_attention(aq.reshape(B, T, ATT_HEADS, 2, ATT_QK_DIM),
                         ak.reshape(B, T, ATT_HEADS, 2, ATT_QK_DIM),
                         av.reshape(B, T, ATT_HEADS, ATT_V_DIM), lam)
    att = (rms_norm(att, subln_g) * (1.0 - lam_init)).reshape(B, T, ATT_WIDTH)
    shp = (B, T, HG_HEADS, HG_K)
    q = jax.nn.silu(hq.astype(jnp.float32)).reshape(shp)
    v = hi.astype(jnp.float32).reshape(B, T, HG_HEADS, HG_V)
    f_f = (lb[0] + (1.0 - lb[0]) * jax.nn.sigmoid(hff.astype(jnp.float32))).reshape(shp)
    f_b = (lb[1] + (1.0 - lb[1]) * jax.nn.sigmoid(hfb.astype(jnp.float32))).reshape(shp)
    o_f = hgrn2_chunk_scan(q, 1.0 - f_f, v, jnp.log(f_f))
    o_b = jnp.flip(hgrn2_chunk_scan(jnp.flip(q, 1), jnp.flip(1.0 - f_b, 1), jnp.flip(v, 1),
                                    jnp.flip(jnp.log(f_b), 1)), 1)
    hgo = rms_norm(o_f + o_b, hg_norm_g).reshape(B, T, HG_WIDTH)
    hgo = (hgo * jax.nn.silu(hg.astype(jnp.float32))).astype(h.dtype)
    return jnp.concatenate([att.astype(h.dtype), hgo], axis=-1) @ w_out


def expert_choice_ffn(h, w_router, w_gate, w_up, w_down):
    B, T, D = h.shape
    n_tok = B * T
    cap = max(1, EC_CAPACITY * n_tok // N_EXPERTS)
    flat = h.reshape(n_tok, D)
    aff = jax.nn.softmax((flat @ w_router).astype(jnp.float32), axis=-1)
    g, idx = lax.top_k(aff.T, cap)
    xs = flat[idx]
    hid = jax.nn.silu(jnp.einsum('ecd,edf->ecf', xs, w_gate)) * jnp.einsum('ecd,edf->ecf', xs, w_up)
    out = jnp.einsum('ecf,efd->ecd', hid, w_down) * g[..., None].astype(h.dtype)
    y = jnp.zeros((n_tok, D), h.dtype).at[idx.reshape(-1)].add(out.reshape(-1, D))
    return y.reshape(B, T, D)


def trunk(x, norm_mix_g, w_in, lambda_q1, lambda_k1, lambda_q2, lambda_k2, diff_subln_g,
          hgrn_lb, hgrn_norm_g, w_out, norm_ffn_g, w_router, w_gate, w_up, w_down, final_norm_g):
    lb_all = jnp.cumsum(jax.nn.softmax(hgrn_lb.astype(jnp.float32), axis=0), axis=0)
    lb_all = lb_all - lb_all[0]
    for layer in range(DEPTH):
        h = rms_norm(x, norm_mix_g[layer])
        x = x + mixer(h, layer, w_in[layer], lambda_q1[layer], lambda_k1[layer], lambda_q2[layer],
                      lambda_k2[layer], diff_subln_g[layer], lb_all[layer], hgrn_norm_g[layer], w_out[layer])
        h = rms_norm(x, norm_ffn_g[layer])
        x = x + expert_choice_ffn(h, w_router[layer], w_gate[layer], w_up[layer], w_down[layer])
    return rms_norm(x, final_norm_g)


def setup_inputs(seed: int = 0) -> dict:
    key = jax.random.key(seed)
    ks = jax.random.split(key, 20)
    nrm = jax.random.normal
    f32 = jnp.float32
    return {
        'x_prompt': nrm(ks[0], (BATCH, SEQ, D_MODEL), f32),
        'x_sample': nrm(ks[1], (DEC_BATCH, DEC_SEQ, D_MODEL), f32),
        'norm_mix_g': 1.0 + 0.02 * nrm(ks[2], (DEPTH, D_MODEL), f32),
        'w_in': nrm(ks[3], (DEPTH, D_MODEL, IN_WIDTH), f32) * D_MODEL ** -0.5,
        'lambda_q1': 0.1 * nrm(ks[4], (DEPTH, ATT_QK_DIM), f32),
        'lambda_k1': 0.1 * nrm(ks[5], (DEPTH, ATT_QK_DIM), f32),
        'lambda_q2': 0.1 * nrm(ks[6], (DEPTH, ATT_QK_DIM), f32),
        'lambda_k2': 0.1 * nrm(ks[7], (DEPTH, ATT_QK_DIM), f32),
        'diff_subln_g': 1.0 + 0.02 * nrm(ks[8], (DEPTH, ATT_V_DIM), f32),
        'hgrn_lb': 0.5 * nrm(ks[9], (DEPTH, 2, HG_HEADS * HG_K), f32),
        'hgrn_norm_g': 1.0 + 0.02 * nrm(ks[10], (DEPTH, HG_V), f32),
        'w_out': nrm(ks[11], (DEPTH, MIX_WIDTH, D_MODEL), f32) * MIX_WIDTH ** -0.5,
        'norm_ffn_g': 1.0 + 0.02 * nrm(ks[12], (DEPTH, D_MODEL), f32),
        'w_router': nrm(ks[13], (DEPTH, D_MODEL, N_EXPERTS), f32) * D_MODEL ** -0.5,
        'w_gate': nrm(ks[14], (DEPTH, N_EXPERTS, D_MODEL, EXPERT_FF), f32) * D_MODEL ** -0.5,
        'w_up': nrm(ks[15], (DEPTH, N_EXPERTS, D_MODEL, EXPERT_FF), f32) * D_MODEL ** -0.5,
        'w_down': nrm(ks[16], (DEPTH, N_EXPERTS, EXPERT_FF, D_MODEL), f32) * EXPERT_FF ** -0.5,
        'final_norm_g': 1.0 + 0.02 * nrm(ks[17], (D_MODEL,), f32),
    }


def reference(x_prompt, x_sample, norm_mix_g, w_in, lambda_q1, lambda_k1, lambda_q2, lambda_k2,
              diff_subln_g, hgrn_lb, hgrn_norm_g, w_out, norm_ffn_g, w_router, w_gate, w_up,
              w_down, final_norm_g):
    y_prompt = trunk(x_prompt, norm_mix_g, w_in, lambda_q1, lambda_k1, lambda_q2, lambda_k2,
                     diff_subln_g, hgrn_lb, hgrn_norm_g, w_out, norm_ffn_g, w_router, w_gate,
                     w_up, w_down, final_norm_g)
    y_sample = trunk(x_sample, norm_mix_g, w_in, lambda_q1, lambda_k1, lambda_q2, lambda_k2,
                     diff_subln_g, hgrn_lb, hgrn_norm_g, w_out, norm_ffn_g, w_router, w_gate,
                     w_up, w_down, final_norm_g)
    return (y_prompt, y_sample)
```

```python
import functools
import math

import jax
import jax.numpy as jnp
from jax import lax
from jax.experimental import pallas as pl
from jax.experimental.pallas import tpu as pltpu

ATT_HEADS = 4
ATT_QK_DIM = 64
ATT_V_DIM = 128
HG_HEADS = 4
HG_K = 128
HG_V = 128
HG_CHUNK = 64
N_EXPERTS = 16
EC_CAPACITY = 2
EPS = 1e-6
LANES = 128
SUBLANES = 8

_QBLK, _KBLK, _VBLK = 0, 4, 8
_HQ, _HFF, _HFB, _HI, _HG = 12, 16, 20, 24, 28

_VMEM_LIMIT = 56 * 1024 * 1024

_NT = (((1,), (1,)), ((), ()))


def _cparams(sem):
    return pltpu.CompilerParams(dimension_semantics=sem, vmem_limit_bytes=_VMEM_LIMIT)


def _norm_proj_kernel(x_ref, g_ref, w_ref, o_ref, *, n_split):
    x = x_ref[...]
    y = x * lax.rsqrt(jnp.mean(x * x, axis=-1, keepdims=True) + EPS) * g_ref[...]
    h = y.astype(jnp.bfloat16)
    wn = w_ref.shape[1] // n_split
    for c in range(n_split):
        o_ref[:, c * wn:(c + 1) * wn] = jnp.dot(
            h, w_ref[:, c * wn:(c + 1) * wn], preferred_element_type=jnp.float32
        ).astype(o_ref.dtype)


def norm_proj(x2d, g, w_bf16, tm=512):
    n, d = x2d.shape
    tm = min(tm, n)
    dout = w_bf16.shape[1]
    return pl.pallas_call(
        functools.partial(_norm_proj_kernel, n_split=max(1, dout // 1024)),
        grid=(n // tm,),
        in_specs=[
            pl.BlockSpec((tm, d), lambda i: (i, 0)),
            pl.BlockSpec((1, d), lambda i: (0, 0)),
            pl.BlockSpec((d, dout), lambda i: (0, 0)),
        ],
        out_specs=pl.BlockSpec((tm, dout), lambda i: (i, 0)),
        out_shape=jax.ShapeDtypeStruct((n, dout), jnp.bfloat16),
        compiler_params=_cparams(("parallel",)),
        name="norm_proj",
    )(x2d, g.reshape(1, d), w_bf16)


def _attn_kernel(slope_ref, lq1_ref, lk1_ref, lq2_ref, lk2_ref, subg_ref,
                 q_ref, k_ref, v_ref, o_ref, m_sc, l_sc, acc_sc, *, lam_init, tq, tk):
    qi = pl.program_id(2)
    ki = pl.program_id(3)
    nk = pl.num_programs(3)

    @pl.when(ki == 0)
    def _():
        m_sc[...] = jnp.full(m_sc.shape, -jnp.inf, jnp.float32)
        l_sc[...] = jnp.zeros(l_sc.shape, jnp.float32)
        acc_sc[...] = jnp.zeros(acc_sc.shape, jnp.float32)

    q = q_ref[...] * jnp.asarray(ATT_QK_DIM ** -0.5, jnp.bfloat16)
    k = k_ref[...]
    lane = lax.broadcasted_iota(jnp.int32, k.shape, 1)
    zero = jnp.zeros_like(k)
    kbd = jnp.concatenate([jnp.where(lane < ATT_QK_DIM, k, zero),
                           jnp.where(lane >= ATT_QK_DIM, k, zero)], axis=0)
    s = lax.dot_general(q, kbd, _NT, preferred_element_type=jnp.float32)

    qpos = qi * tq + lax.broadcasted_iota(jnp.int32, (tq, tk), 0)
    kpos = ki * tk + lax.broadcasted_iota(jnp.int32, (tq, tk), 1)
    bias = jnp.abs(qpos - kpos).astype(jnp.float32) * slope_ref[0:1, 0:1]

    ps = []
    for j in range(2):
        sj = s[:, j * tk:(j + 1) * tk] - bias
        m_old = m_sc[j]
        m_new = jnp.maximum(m_old, jnp.max(sj, axis=-1, keepdims=True))
        alpha = jnp.exp(m_old - m_new)
        p = jnp.exp(sj - m_new)
        l_sc[j] = alpha * l_sc[j] + jnp.sum(p, axis=-1, keepdims=True)
        acc_sc[j] = alpha * acc_sc[j]
        m_sc[j] = m_new
        ps.append(p.astype(jnp.bfloat16))
    pv = jnp.dot(jnp.concatenate(ps, axis=0), v_ref[...], preferred_element_type=jnp.float32)
    acc_sc[0] += pv[:tq]
    acc_sc[1] += pv[tq:]

    @pl.when(ki == nk - 1)
    def _():
        lam = (jnp.exp(jnp.sum(lq1_ref[...] * lk1_ref[...], axis=-1, keepdims=True))
               - jnp.exp(jnp.sum(lq2_ref[...] * lk2_ref[...], axis=-1, keepdims=True)) + lam_init)
        o = acc_sc[0] / l_sc[0] - lam * (acc_sc[1] / l_sc[1])
        y = o * lax.rsqrt(jnp.mean(o * o, axis=-1, keepdims=True) + EPS) * subg_ref[...]
        o_ref[...] = (y * (1.0 - lam_init)).astype(o_ref.dtype)


def diff_attention(proj3d, lq1, lk1, lq2, lk2, subg, layer, tq=512, tk=512):
    b, t, _ = proj3d.shape
    tq, tk = min(tq, t), min(tk, t)
    lam_init = 0.8 - 0.6 * math.exp(-0.3 * layer)
    slopes = 2.0 ** (-8.0 * (jnp.arange(ATT_HEADS, dtype=jnp.float32) + 1.0) / ATT_HEADS)
    slopes = jnp.broadcast_to(slopes[:, None, None], (ATT_HEADS, 1, LANES))
    vec = lambda a: a.reshape(1, -1).astype(jnp.float32)
    small = lambda n: pl.BlockSpec((1, n), lambda bi, h, qi, ki: (0, 0))
    return pl.pallas_call(
        functools.partial(_attn_kernel, lam_init=lam_init, tq=tq, tk=tk),
        grid=(b, ATT_HEADS, t // tq, t // tk),
        in_specs=[
            pl.BlockSpec((None, 1, LANES), lambda bi, h, qi, ki: (h, 0, 0)),
            small(ATT_QK_DIM), small(ATT_QK_DIM), small(ATT_QK_DIM), small(ATT_QK_DIM),
            small(ATT_V_DIM),
            pl.BlockSpec((None, tq, LANES), lambda bi, h, qi, ki: (bi, qi, _QBLK + h)),
            pl.BlockSpec((None, tk, LANES), lambda bi, h, qi, ki: (bi, ki, _KBLK + h)),
            pl.BlockSpec((None, tk, LANES), lambda bi, h, qi, ki: (bi, ki, _VBLK + h)),
        ],
        out_specs=pl.BlockSpec((None, tq, LANES), lambda bi, h, qi, ki: (bi, qi, h)),
        out_shape=jax.ShapeDtypeStruct((b, t, ATT_HEADS * ATT_V_DIM), jnp.bfloat16),
        scratch_shapes=[
            pltpu.VMEM((2, tq, 1), jnp.float32),
            pltpu.VMEM((2, tq, 1), jnp.float32),
            pltpu.VMEM((2, tq, ATT_V_DIM), jnp.float32),
        ],
        compiler_params=_cparams(("parallel", "parallel", "parallel", "arbitrary")),
        name="diff_attn",
    )(slopes, vec(lq1), vec(lk1), vec(lq2), vec(lk2), vec(subg), proj3d, proj3d, proj3d)


def _scan_rows(g, rev):
    c = g.shape[0]
    row = lax.broadcasted_iota(jnp.int32, g.shape, 0)
    b = g
    s = 1
    while s < c:
        if rev:
            b = b + jnp.where(row < c - s, pltpu.roll(b, c - s, axis=0), 0.0)
        else:
            b = b + jnp.where(row >= s, pltpu.roll(b, s, axis=0), 0.0)
        s *= 2
    return b


def _level_ref(b, m, rev):
    c, w = b.shape
    r = m // 2 if rev else m // 2 - 1
    pieces = []
    if m >= SUBLANES:
        for j in range(c // m):
            pieces.append(jnp.broadcast_to(b[j * m + r:j * m + r + 1, :], (m, w)))
    else:
        sub = lax.broadcasted_iota(jnp.int32, (SUBLANES, w), 0)
        for j in range(c // SUBLANES):
            base = j * SUBLANES
            acc = jnp.broadcast_to(b[base + r:base + r + 1, :], (SUBLANES, w))
            for i in range(1, SUBLANES // m):
                cand = jnp.broadcast_to(b[base + i * m + r:base + i * m + r + 1, :], (SUBLANES, w))
                acc = jnp.where(sub >= i * m, cand, acc)
            pieces.append(acc)
    return jnp.concatenate(pieces, axis=0)


def _hgrn_chunk(q, f, v, st_ref, rev):
    c = q.shape[0]
    bf = jnp.bfloat16
    k_ = 1.0 - f
    b = _scan_rows(jnp.log(f), rev)
    row = lax.broadcasted_iota(jnp.int32, q.shape, 0)
    ti = lax.broadcasted_iota(jnp.int32, (c, c), 0)
    si = lax.broadcasted_iota(jnp.int32, (c, c), 1)
    a = jnp.where(ti == si,
                  lax.dot_general(q.astype(bf), k_.astype(bf), _NT,
                                  preferred_element_type=jnp.float32), 0.0)
    m = c
    while m >= 2:
        cref = _level_ref(b, m, rev)
        late = (row & (m - 1)) >= m // 2
        qrows = jnp.logical_not(late) if rev else late
        e = jnp.exp(jnp.where(qrows, b - cref, cref - b))
        qt = jnp.where(qrows, q * e, 0.0).astype(bf)
        kt = jnp.where(qrows, 0.0, k_ * e).astype(bf)
        p = lax.dot_general(qt, kt, _NT, preferred_element_type=jnp.float32)
        a = a + jnp.where((ti ^ si) < m, p, 0.0)
        m //= 2
    st = st_ref[...]
    o = jnp.dot(a.astype(bf), v.astype(bf), preferred_element_type=jnp.float32)
    o = o + lax.dot_general((q * jnp.exp(b)).astype(bf), st.astype(bf), _NT,
                            preferred_element_type=jnp.float32)
    btot = b[0:1, :] if rev else b[c - 1:c, :]
    khat = (k_ * jnp.exp(btot - b)).astype(bf)
    st_ref[...] = st * jnp.exp(btot) + jnp.dot(v.T.astype(bf), khat,
                                               preferred_element_type=jnp.float32)
    return o


def _hgrn_kernel(lb_ref, ng_ref, q_ref, ff_ref, fb_ref, i_ref, g_ref, o_ref,
                 of_sc, ob_sc, stf_sc, stb_sc, *, layer, chunk):
    t = q_ref.shape[0]
    n = t // chunk
    lb_raw = lb_ref[...].astype(jnp.float32)
    ex = jnp.exp(lb_raw - jnp.max(lb_raw, axis=0, keepdims=True))
    sm = ex / jnp.sum(ex, axis=0, keepdims=True)
    lb = jnp.zeros_like(sm[0])
    for l_ in range(1, layer + 1):
        lb = lb + sm[l_]
    lb_f = lb[0:1, :]
    lb_b = lb[1:2, :]

    stf_sc[...] = jnp.zeros(stf_sc.shape, jnp.float32)
    stb_sc[...] = jnp.zeros(stb_sc.shape, jnp.float32)

    def load(ref, r0):
        return ref[pl.ds(r0, chunk), :].astype(jnp.float32)

    def body(ci, carry):
        r0 = pl.multiple_of(ci * chunk, chunk)
        qr = load(q_ref, r0)
        f = lb_f + (1.0 - lb_f) * jax.nn.sigmoid(load(ff_ref, r0))
        of_sc[pl.ds(r0, chunk), :] = _hgrn_chunk(qr * jax.nn.sigmoid(qr), f, load(i_ref, r0),
                                                 stf_sc, False)
        r1 = pl.multiple_of((n - 1 - ci) * chunk, chunk)
        qr = load(q_ref, r1)
        f = lb_b + (1.0 - lb_b) * jax.nn.sigmoid(load(fb_ref, r1))
        ob_sc[pl.ds(r1, chunk), :] = _hgrn_chunk(qr * jax.nn.sigmoid(qr), f, load(i_ref, r1),
                                                 stb_sc, True)
        return carry

    lax.fori_loop(0, n, body, 0)

    def fin(ci, carry):
        r0 = pl.multiple_of(ci * chunk, chunk)
        o = of_sc[pl.ds(r0, chunk), :] + ob_sc[pl.ds(r0, chunk), :]
        y = o * lax.rsqrt(jnp.mean(o * o, axis=-1, keepdims=True) + EPS) * ng_ref[...]
        gr = load(g_ref, r0)
        o_ref[pl.ds(r0, chunk), :] = (y * (gr * jax.nn.sigmoid(gr))).astype(o_ref.dtype)
        return carry

    lax.fori_loop(0, n, fin, 0)


def hgrn_bidir(proj3d, hgrn_lb, norm_g, layer):
    b, t, _ = proj3d.shape
    depth = hgrn_lb.shape[0]
    col = lambda blk: pl.BlockSpec((None, t, LANES), lambda bi, h: (bi, 0, blk + h))
    return pl.pallas_call(
        functools.partial(_hgrn_kernel, layer=layer, chunk=HG_CHUNK),
        grid=(b, HG_HEADS),
        in_specs=[
            pl.BlockSpec((depth, 2, LANES), lambda bi, h: (0, 0, h)),
            pl.BlockSpec((1, HG_V), lambda bi, h: (0, 0)),
            col(_HQ), col(_HFF), col(_HFB), col(_HI), col(_HG),
        ],
        out_specs=pl.BlockSpec((None, t, LANES), lambda bi, h: (bi, 0, h)),
        out_shape=jax.ShapeDtypeStruct((b, t, HG_HEADS * HG_V), jnp.bfloat16),
        scratch_shapes=[
            pltpu.VMEM((t, HG_V), jnp.float32),
            pltpu.VMEM((t, HG_V), jnp.float32),
            pltpu.VMEM((HG_V, HG_K), jnp.float32),
            pltpu.VMEM((HG_V, HG_K), jnp.float32),
        ],
        compiler_params=_cparams(("parallel", "parallel")),
        name="hgrn_bidir",
    )(hgrn_lb, norm_g.reshape(1, HG_V).astype(jnp.float32),
      proj3d, proj3d, proj3d, proj3d, proj3d)


def _out_proj_kernel(a_ref, h_ref, wa_ref, wh_ref, x_ref, o_ref):
    y = jnp.dot(a_ref[...], wa_ref[...], preferred_element_type=jnp.float32)
    y = y + jnp.dot(h_ref[...], wh_ref[...], preferred_element_type=jnp.float32)
    o_ref[...] = x_ref[...] + y


def out_proj(att2d, hgo2d, w_out_bf16, x2d, tm=512):
    n, d = x2d.shape
    tm = min(tm, n)
    wa = att2d.shape[1]
    wh = hgo2d.shape[1]
    return pl.pallas_call(
        _out_proj_kernel,
        grid=(n // tm,),
        in_specs=[
            pl.BlockSpec((tm, wa), lambda i: (i, 0)),
            pl.BlockSpec((tm, wh), lambda i: (i, 0)),
            pl.BlockSpec((wa, d), lambda i: (0, 0)),
            pl.BlockSpec((wh, d), lambda i: (wa // wh, 0)),
            pl.BlockSpec((tm, d), lambda i: (i, 0)),
        ],
        out_specs=pl.BlockSpec((tm, d), lambda i: (i, 0)),
        out_shape=jax.ShapeDtypeStruct((n, d), jnp.float32),
        compiler_params=_cparams(("parallel",)),
        name="out_proj",
    )(att2d, hgo2d, w_out_bf16, w_out_bf16, x2d)


def _router_kernel(x_ref, g_ref, wr_ref, h_ref, aff_ref):
    x = x_ref[...]
    h = x * lax.rsqrt(jnp.mean(x * x, axis=-1, keepdims=True) + EPS) * g_ref[...]
    h_ref[...] = h.astype(h_ref.dtype)
    logits = jnp.dot(h, wr_ref[...], preferred_element_type=jnp.float32,
                     precision=lax.Precision.HIGHEST)
    ex = jnp.exp(logits - jnp.max(logits, axis=-1, keepdims=True))
    aff_ref[...] = ex / jnp.sum(ex, axis=-1, keepdims=True)


def router(x2d, g, w_router, tm=512):
    n, d = x2d.shape
    tm = min(tm, n)
    e = w_router.shape[1]
    return pl.pallas_call(
        _router_kernel,
        grid=(n // tm,),
        in_specs=[
            pl.BlockSpec((tm, d), lambda i: (i, 0)),
            pl.BlockSpec((1, d), lambda i: (0, 0)),
            pl.BlockSpec((d, e), lambda i: (0, 0)),
        ],
        out_specs=[
            pl.BlockSpec((tm, d), lambda i: (i, 0)),
            pl.BlockSpec((tm, e), lambda i: (i, 0)),
        ],
        out_shape=[
            jax.ShapeDtypeStruct((n, d), jnp.bfloat16),
            jax.ShapeDtypeStruct((n, e), jnp.float32),
        ],
        compiler_params=_cparams(("parallel",)),
        name="router",
    )(x2d, g.reshape(1, d), w_router)


def _expert_kernel(xs_ref, gate_ref, wg_ref, wu_ref, wd_ref, o_ref):
    xs = xs_ref[...]
    a = jnp.dot(xs, wg_ref[...], preferred_element_type=jnp.float32)
    u = jnp.dot(xs, wu_ref[...], preferred_element_type=jnp.float32)
    hid = (a * jax.nn.sigmoid(a) * u).astype(jnp.bfloat16)
    y = jnp.dot(hid, wd_ref[...], preferred_element_type=jnp.float32)
    o_ref[...] = y * gate_ref[...]


def expert_ffn(xs, gates, wg, wu, wd, tm=512):
    e, c, d = xs.shape
    tm = min(tm, c)
    f = wg.shape[2]
    return pl.pallas_call(
        _expert_kernel,
        grid=(e, c // tm),
        in_specs=[
            pl.BlockSpec((None, tm, d), lambda ei, i: (ei, i, 0)),
            pl.BlockSpec((None, tm, 1), lambda ei, i: (ei, i, 0)),
            pl.BlockSpec((None, d, f), lambda ei, i: (ei, 0, 0)),
            pl.BlockSpec((None, d, f), lambda ei, i: (ei, 0, 0)),
            pl.BlockSpec((None, f, d), lambda ei, i: (ei, 0, 0)),
        ],
        out_specs=pl.BlockSpec((None, tm, d), lambda ei, i: (ei, i, 0)),
        out_shape=jax.ShapeDtypeStruct((e, c, d), jnp.float32),
        compiler_params=_cparams(("parallel", "arbitrary")),
        name="expert_ffn",
    )(xs, gates.reshape(e, c, 1), wg, wu, wd)


def _rms_kernel(x_ref, g_ref, o_ref):
    x = x_ref[...]
    o_ref[...] = x * lax.rsqrt(jnp.mean(x * x, axis=-1, keepdims=True) + EPS) * g_ref[...]


def rms_norm_call(x2d, g, tm=1024):
    n, d = x2d.shape
    tm = min(tm, n)
    return pl.pallas_call(
        _rms_kernel,
        grid=(n // tm,),
        in_specs=[pl.BlockSpec((tm, d), lambda i: (i, 0)),
                  pl.BlockSpec((1, d), lambda i: (0, 0))],
        out_specs=pl.BlockSpec((tm, d), lambda i: (i, 0)),
        out_shape=jax.ShapeDtypeStruct((n, d), jnp.float32),
        compiler_params=_cparams(("parallel",)),
        name="final_norm",
    )(x2d, g.reshape(1, d))


def _trunk(x, p):
    b, t, d = x.shape
    n = b * t
    depth = p["w_in"].shape[0]
    x2d = x.reshape(n, d)
    cap = max(1, EC_CAPACITY * n // N_EXPERTS)
    for layer in range(depth):
        proj = norm_proj(x2d, p["norm_mix_g"][layer], p["w_in"][layer])
        proj3d = proj.reshape(b, t, -1)
        att = diff_attention(proj3d, p["lambda_q1"][layer], p["lambda_k1"][layer],
                             p["lambda_q2"][layer], p["lambda_k2"][layer],
                             p["diff_subln_g"][layer], layer)
        hgo = hgrn_bidir(proj3d, p["hgrn_lb"], p["hgrn_norm_g"][layer], layer)
        x2d = out_proj(att.reshape(n, -1), hgo.reshape(n, -1), p["w_out"][layer], x2d)
        h, aff = router(x2d, p["norm_ffn_g"][layer], p["w_router"][layer])
        gates, idx = lax.top_k(aff.T, cap)
        xs = h[idx]
        out = expert_ffn(xs, gates, p["w_gate"][layer], p["w_up"][layer], p["w_down"][layer])
        x2d = x2d.at[idx.reshape(-1)].add(out.reshape(-1, d))
    return rms_norm_call(x2d, p["final_norm_g"]).reshape(b, t, d)


def kernel(x_prompt, x_sample, norm_mix_g, w_in, lambda_q1, lambda_k1, lambda_q2, lambda_k2,
           diff_subln_g, hgrn_lb, hgrn_norm_g, w_out, norm_ffn_g, w_router, w_gate, w_up,
           w_down, final_norm_g):
    bf = jnp.bfloat16
    p = dict(norm_mix_g=norm_mix_g, w_in=w_in.astype(bf), lambda_q1=lambda_q1,
             lambda_k1=lambda_k1, lambda_q2=lambda_q2, lambda_k2=lambda_k2,
             diff_subln_g=diff_subln_g, hgrn_lb=hgrn_lb, hgrn_norm_g=hgrn_norm_g,
             w_out=w_out.astype(bf), norm_ffn_g=norm_ffn_g, w_router=w_router,
             w_gate=w_gate.astype(bf), w_up=w_up.astype(bf), w_down=w_down.astype(bf),
             final_norm_g=final_norm_g)
    return (_trunk(x_prompt, p), _trunk(x_sample, p))
```

```python
import functools
import math

import jax
import jax.numpy as jnp
from jax import lax
from jax.experimental import pallas as pl
from jax.experimental.pallas import tpu as pltpu

ATT_HEADS = 4
ATT_QK_DIM = 64
ATT_V_DIM = 128
HG_HEADS = 4
HG_K = 128
HG_V = 128
HG_CHUNK = 64
N_EXPERTS = 16
EC_CAPACITY = 2
EPS = 1e-6
LANES = 128
SUBLANES = 8

_QBLK, _KBLK, _VBLK = 0, 4, 8
_HQ, _HFF, _HFB, _HI, _HG = 12, 16, 20, 24, 28

_VMEM_LIMIT = 56 * 1024 * 1024

_NT = (((1,), (1,)), ((), ()))


def _cparams(sem):
    return pltpu.CompilerParams(dimension_semantics=sem, vmem_limit_bytes=_VMEM_LIMIT)


def _norm_proj_kernel(x_ref, g_ref, w_ref, o_ref, *, n_split):
    x = x_ref[...]
    y = x * lax.rsqrt(jnp.mean(x * x, axis=-1, keepdims=True) + EPS) * g_ref[...]
    h = y.astype(jnp.bfloat16)
    wn = w_ref.shape[1] // n_split
    for c in range(n_split):
        o_ref[:, c * wn:(c + 1) * wn] = jnp.dot(
            h, w_ref[:, c * wn:(c + 1) * wn], preferred_element_type=jnp.float32
        ).astype(o_ref.dtype)


def norm_proj(x2d, g, w_bf16, tm=512):
    n, d = x2d.shape
    tm = min(tm, n)
    dout = w_bf16.shape[1]
    return pl.pallas_call(
        functools.partial(_norm_proj_kernel, n_split=max(1, dout // 1024)),
        grid=(n // tm,),
        in_specs=[
            pl.BlockSpec((tm, d), lambda i: (i, 0)),
            pl.BlockSpec((1, d), lambda i: (0, 0)),
            pl.BlockSpec((d, dout), lambda i: (0, 0)),
        ],
        out_specs=pl.BlockSpec((tm, dout), lambda i: (i, 0)),
        out_shape=jax.ShapeDtypeStruct((n, dout), jnp.bfloat16),
        compiler_params=_cparams(("parallel",)),
        name="norm_proj",
    )(x2d, g.reshape(1, d), w_bf16)


_LOG2E = 1.4426950408889634
_Q_STRIP = 128
_ONES_ROWS = 16


def _split3(x):
    bf = jnp.bfloat16
    h = x.astype(bf).astype(jnp.float32)
    r = x - h
    m = r.astype(bf).astype(jnp.float32)
    lo = (r - m).astype(bf).astype(jnp.float32)
    return h, m, lo


def _attn_kernel(beta_ref, qconst_ref, lq1_ref, lk1_ref, lq2_ref, lk2_ref, subg_ref,
                 q_ref, k_ref, v_ref, kext_ref, o_ref, vt_sc, acc_sc, s_sc, *, lam_init, tk, tq):
    f32, bf = jnp.float32, jnp.bfloat16
    t = q_ref.shape[0]
    nk = t // tk
    qs_ = _Q_STRIP
    ng = tq // qs_
    ncol = 2 * tq
    dv = v_ref.shape[1]
    beta = beta_ref[0:1, 0:1]

    def vt_body(c, carry):
        r0 = pl.multiple_of(c * tk, tk)
        vt_sc[0:dv, pl.ds(r0, tk)] = v_ref[pl.ds(r0, tk), :].astype(f32).T.astype(bf)
        return carry

    lax.fori_loop(0, nk, vt_body, 0)
    vt_sc[dv:dv + _ONES_ROWS, :] = jnp.ones((_ONES_ROWS, t), bf)

    lam = (jnp.exp(jnp.sum(lq1_ref[...] * lk1_ref[...], axis=-1, keepdims=True))
           - jnp.exp(jnp.sum(lq2_ref[...] * lk2_ref[...], axis=-1, keepdims=True)) + lam_init)
    lane = lax.broadcasted_iota(jnp.int32, (qs_, LANES), 1)
    rowq = lax.broadcasted_iota(jnp.int32, (qs_, LANES), 0)
    kr = lax.broadcasted_iota(jnp.int32, (tk, qs_), 0)
    qc = lax.broadcasted_iota(jnp.int32, (tk, qs_), 1)
    cscale = ATT_QK_DIM ** -0.5 * _LOG2E

    def group(gi, carry):
        g0 = pl.multiple_of(gi * tq, tq)
        tiles = []
        for u in range(ng):
            q0 = g0 + u * qs_
            qs = (q_ref[pl.ds(q0, qs_), :].astype(f32) * cscale).astype(bf).astype(f32)
            a_h, a_m, a_l = _split3(-beta * (q0 + rowq).astype(f32))
            qext = jnp.where(lane == 0, a_h, jnp.where(lane == 1, a_m,
                             jnp.where(lane == 2, a_l, qconst_ref[...])))
            q_aug = jnp.concatenate([
                jnp.concatenate([jnp.where(lane < ATT_QK_DIM, qs, 0.0), qext], axis=1),
                jnp.concatenate([jnp.where(lane >= ATT_QK_DIM, qs, 0.0), qext], axis=1)], axis=0)
            tiles.append(q_aug.T.astype(bf))
        w = jnp.concatenate(tiles, axis=1)
        jc = g0 // tk

        def scores(n, sidx, crossing):
            r0 = pl.multiple_of(n * tk, tk)
            k_aug = jnp.concatenate([k_ref[pl.ds(r0, tk), :], kext_ref[sidx, pl.ds(r0, tk), :]],
                                    axis=1)
            for u in range(ng):
                c0 = u * 2 * qs_
                s = jnp.dot(k_aug, w[:, c0:c0 + 2 * qs_], preferred_element_type=f32)
                if crossing:
                    x = (g0 + u * qs_ - r0 + qc - kr).astype(f32)
                    corr = 2.0 * beta * jnp.minimum(x, 0.0)
                    s = s + jnp.concatenate([corr, corr], axis=1)
                s_sc[:, c0:c0 + 2 * qs_] = s

        def step(n, m, sidx, crossing):
            r0 = pl.multiple_of((n - 1) * tk, tk)
            r1 = pl.multiple_of(n * tk, tk)
            vt = vt_sc[:, pl.ds(r0, tk)]
            if sidx is not None:
                k_aug = jnp.concatenate(
                    [k_ref[pl.ds(r1, tk), :], kext_ref[sidx, pl.ds(r1, tk), :]], axis=1)
            ms = []
            for u in range(ng):
                c0 = u * 2 * qs_
                s = s_sc[:, c0:c0 + 2 * qs_]
                m_old = m[:, c0:c0 + 2 * qs_]
                m_new = jnp.maximum(m_old, jnp.max(s, axis=0, keepdims=True))
                alpha = jnp.exp2(m_old - m_new)
                p = jnp.exp2(s - m_new).astype(bf)
                acc_sc[:, c0:c0 + 2 * qs_] = (acc_sc[:, c0:c0 + 2 * qs_] * alpha
                                              + jnp.dot(vt, p, preferred_element_type=f32))
                ms.append(m_new)
                if sidx is not None:
                    s = jnp.dot(k_aug, w[:, c0:c0 + 2 * qs_], preferred_element_type=f32)
                    if crossing:
                        x = (g0 + u * qs_ - r1 + qc - kr).astype(f32)
                        corr = 2.0 * beta * jnp.minimum(x, 0.0)
                        s = s + jnp.concatenate([corr, corr], axis=1)
                    s_sc[:, c0:c0 + 2 * qs_] = s
            return jnp.concatenate(ms, axis=1)

        acc_sc[...] = jnp.zeros(acc_sc.shape, f32)
        m = jnp.full((1, ncol), -1e30, f32)
        scores(0, 0, True)
        n1 = jnp.maximum(jc, 1)
        n2 = jc + tq // tk
        m = lax.fori_loop(1, n1, functools.partial(step, sidx=0, crossing=False), m)
        m = lax.fori_loop(n1, n2, functools.partial(step, sidx=0, crossing=True), m)
        m = lax.fori_loop(n2, nk, functools.partial(step, sidx=1, crossing=False), m)
        step(nk, m, None, False)

        for u in range(ng):
            c0 = u * 2 * qs_
            acc = acc_sc[:, c0:c0 + 2 * qs_]
            den = acc[dv:dv + 1, :]
            o = acc[0:dv, 0:qs_] / den[:, 0:qs_] - lam * (acc[0:dv, qs_:] / den[:, qs_:])
            y = o * lax.rsqrt(jnp.mean(o * o, axis=0, keepdims=True) + EPS) * subg_ref[...]
            o_ref[pl.ds(g0 + u * qs_, qs_), :] = (y * (1.0 - lam_init)).T.astype(o_ref.dtype)
        return carry

    lax.fori_loop(0, t // tq, group, 0)


def _attn_tables(t):
    f32 = jnp.float32
    slopes = 2.0 ** (-8.0 * (jnp.arange(ATT_HEADS, dtype=f32) + 1.0) / ATT_HEADS)
    beta = slopes * _LOG2E
    lane = jnp.arange(LANES)
    b128 = _split3(beta * float(_Q_STRIP))
    b1 = _split3(beta)
    qconst = jnp.zeros((ATT_HEADS, LANES), f32)
    for c in range(3):
        qconst = jnp.where(lane[None, :] == 3 + c, b128[c][:, None], qconst)
        qconst = jnp.where(lane[None, :] == 6 + c, b1[c][:, None], qconst)
    pos = jnp.arange(t)
    kh = (pos // _Q_STRIP).astype(f32)[:, None]
    kl = (pos % _Q_STRIP).astype(f32)[:, None]
    base = jnp.where(lane < 3, 1.0, jnp.where(lane < 6, kh, jnp.where(lane < 9, kl, 0.0)))
    kext = jnp.stack([base, -base]).astype(jnp.bfloat16)
    beta_t = jnp.broadcast_to(beta[:, None, None], (ATT_HEADS, 1, LANES))
    return beta_t, qconst.reshape(ATT_HEADS, 1, LANES), kext


def diff_attention(proj3d, lq1, lk1, lq2, lk2, subg, layer, tk=256, tq=1024):
    b, t, _ = proj3d.shape
    tk, tq = min(tk, t), min(tq, t)
    assert tq % tk == 0 and tq % _Q_STRIP == 0
    lam_init = 0.8 - 0.6 * math.exp(-0.3 * layer)
    beta_t, qconst, kext = _attn_tables(t)
    vec = lambda a: a.reshape(1, -1).astype(jnp.float32)
    small = lambda n: pl.BlockSpec((1, n), lambda bi, h: (0, 0))
    head = pl.BlockSpec((None, 1, LANES), lambda bi, h: (h, 0, 0))
    col = lambda blk: pl.BlockSpec((None, t, LANES), lambda bi, h: (bi, 0, blk + h))
    return pl.pallas_call(
        functools.partial(_attn_kernel, lam_init=lam_init, tk=tk, tq=tq),
        grid=(b, ATT_HEADS),
        in_specs=[
            head, head,
            small(ATT_QK_DIM), small(ATT_QK_DIM), small(ATT_QK_DIM), small(ATT_QK_DIM),
            pl.BlockSpec((ATT_V_DIM, 1), lambda bi, h: (0, 0)),
            col(_QBLK), col(_KBLK), col(_VBLK),
            pl.BlockSpec((2, t, LANES), lambda bi, h: (0, 0, 0)),
        ],
        out_specs=pl.BlockSpec((None, t, LANES), lambda bi, h: (bi, 0, h)),
        out_shape=jax.ShapeDtypeStruct((b, t, ATT_HEADS * ATT_V_DIM), jnp.bfloat16),
        scratch_shapes=[pltpu.VMEM((ATT_V_DIM + _ONES_ROWS, t), jnp.bfloat16),
                        pltpu.VMEM((ATT_V_DIM + _ONES_ROWS, 2 * tq), jnp.float32),
                        pltpu.VMEM((tk, 2 * tq), jnp.float32)],
        compiler_params=_cparams(("parallel", "parallel")),
        name="diff_attn",
    )(beta_t, qconst, vec(lq1), vec(lk1), vec(lq2), vec(lk2),
      subg.reshape(ATT_V_DIM, 1).astype(jnp.float32), proj3d, proj3d, proj3d, kext)


_HG_CHUNKS_PER_ITER = 2


def _scan_rows(g, rev):
    c = g.shape[0]
    row = lax.broadcasted_iota(jnp.int32, g.shape, 0)
    b = g
    s = 1
    while s < c:
        if rev:
            b = b + jnp.where(row < c - s, pltpu.roll(b, c - s, axis=0), 0.0)
        else:
            b = b + jnp.where(row >= s, pltpu.roll(b, s, axis=0), 0.0)
        s *= 2
    return b


def _level_ref(b, m, rev):
    c, w = b.shape
    r = m // 2 if rev else m // 2 - 1
    pieces = []
    if m >= SUBLANES:
        for j in range(c // m):
            pieces.append(jnp.broadcast_to(b[j * m + r:j * m + r + 1, :], (m, w)))
    else:
        sub = lax.broadcasted_iota(jnp.int32, (SUBLANES, w), 0)
        for j in range(c // SUBLANES):
            base = j * SUBLANES
            acc = jnp.broadcast_to(b[base + r:base + r + 1, :], (SUBLANES, w))
            for i in range(1, SUBLANES // m):
                cand = jnp.broadcast_to(b[base + i * m + r:base + i * m + r + 1, :], (SUBLANES, w))
                acc = jnp.where(sub >= i * m, cand, acc)
            pieces.append(acc)
    return jnp.concatenate(pieces, axis=0)


def _hgrn_masks(c, rev):
    row = lax.broadcasted_iota(jnp.int32, (c, LANES), 0)
    ti = lax.broadcasted_iota(jnp.int32, (c, c), 0)
    si = lax.broadcasted_iota(jnp.int32, (c, c), 1)
    qrows, pairs = {}, {}
    m = c
    while m >= 2:
        late = (row & (m - 1)) >= m // 2
        t_late = (ti & (m - 1)) >= m // 2
        s_late = (si & (m - 1)) >= m // 2
        same = (ti ^ si) < m
        if rev:
            qrows[m] = jnp.logical_not(late)
            pairs[m] = same & jnp.logical_not(t_late) & s_late
        else:
            qrows[m] = late
            pairs[m] = same & t_late & jnp.logical_not(s_late)
        m //= 2
    return qrows, pairs, ti == si


def _hgrn_chunk(q, f, v, st_ref, rev, masks):
    qrows, pairs, diag = masks
    c = q.shape[0]
    bf = jnp.bfloat16
    k_ = 1.0 - f
    b = _scan_rows(jnp.log(f), rev)
    a = jnp.where(diag,
                  lax.dot_general(q.astype(bf), k_.astype(bf), _NT,
                                  preferred_element_type=jnp.float32), 0.0)
    m = c
    while m >= 2:
        e = jnp.exp(-jnp.abs(b - _level_ref(b, m, rev)))
        z = (jnp.where(qrows[m], q, k_) * e).astype(bf)
        p = lax.dot_general(z, z, _NT, preferred_element_type=jnp.float32)
        a = a + jnp.where(pairs[m], p, 0.0)
        m //= 2
    st = st_ref[...]
    o = jnp.dot(a.astype(bf), v.astype(bf), preferred_element_type=jnp.float32)
    o = o + lax.dot_general((q * jnp.exp(b)).astype(bf), st.astype(bf), _NT,
                            preferred_element_type=jnp.float32)
    btot = b[0:1, :] if rev else b[c - 1:c, :]
    khat = (k_ * jnp.exp(btot - b)).astype(bf)
    st_ref[...] = st * jnp.exp(btot) + jnp.dot(v.T.astype(bf), khat,
                                               preferred_element_type=jnp.float32)
    return o


def _hgrn_kernel(lb_ref, ng_ref, q_ref, ff_ref, fb_ref, i_ref, g_ref, o_ref,
                 of_sc, ob_sc, stf_sc, stb_sc, *, layer, chunk):
    t = q_ref.shape[0]
    n = t // chunk
    lb_raw = lb_ref[...].astype(jnp.float32)
    ex = jnp.exp(lb_raw - jnp.max(lb_raw, axis=0, keepdims=True))
    sm = ex / jnp.sum(ex, axis=0, keepdims=True)
    lb = jnp.zeros_like(sm[0])
    for l_ in range(1, layer + 1):
        lb = lb + sm[l_]
    lb_f = lb[0:1, :]
    lb_b = lb[1:2, :]

    stf_sc[...] = jnp.zeros(stf_sc.shape, jnp.float32)
    stb_sc[...] = jnp.zeros(stb_sc.shape, jnp.float32)

    def load(ref, r0):
        return ref[pl.ds(r0, chunk), :].astype(jnp.float32)

    masks_f = _hgrn_masks(chunk, False)
    masks_b = _hgrn_masks(chunk, True)

    per_iter = _HG_CHUNKS_PER_ITER if n % _HG_CHUNKS_PER_ITER == 0 else 1

    def body(ci, carry):
        for u in range(per_iter):
            r0 = pl.multiple_of((ci * per_iter + u) * chunk, chunk)
            qr = load(q_ref, r0)
            f = lb_f + (1.0 - lb_f) * jax.nn.sigmoid(load(ff_ref, r0))
            of_sc[pl.ds(r0, chunk), :] = _hgrn_chunk(qr * jax.nn.sigmoid(qr), f,
                                                     load(i_ref, r0), stf_sc, False, masks_f)
            r1 = pl.multiple_of((n - 1 - ci * per_iter - u) * chunk, chunk)
            qr = load(q_ref, r1)
            f = lb_b + (1.0 - lb_b) * jax.nn.sigmoid(load(fb_ref, r1))
            ob_sc[pl.ds(r1, chunk), :] = _hgrn_chunk(qr * jax.nn.sigmoid(qr), f,
                                                     load(i_ref, r1), stb_sc, True, masks_b)
        return carry

    lax.fori_loop(0, n // per_iter, body, 0)

    def fin(ci, carry):
        r0 = pl.multiple_of(ci * chunk, chunk)
        o = of_sc[pl.ds(r0, chunk), :] + ob_sc[pl.ds(r0, chunk), :]
        y = o * lax.rsqrt(jnp.mean(o * o, axis=-1, keepdims=True) + EPS) * ng_ref[...]
        gr = load(g_ref, r0)
        o_ref[pl.ds(r0, chunk), :] = (y * (gr * jax.nn.sigmoid(gr))).astype(o_ref.dtype)
        return carry

    lax.fori_loop(0, n, fin, 0)


def hgrn_bidir(proj3d, hgrn_lb, norm_g, layer):
    b, t, _ = proj3d.shape
    depth = hgrn_lb.shape[0]
    col = lambda blk: pl.BlockSpec((None, t, LANES), lambda bi, h: (bi, 0, blk + h))
    return pl.pallas_call(
        functools.partial(_hgrn_kernel, layer=layer, chunk=HG_CHUNK),
        grid=(b, HG_HEADS),
        in_specs=[
            pl.BlockSpec((depth, 2, LANES), lambda bi, h: (0, 0, h)),
            pl.BlockSpec((1, HG_V), lambda bi, h: (0, 0)),
            col(_HQ), col(_HFF), col(_HFB), col(_HI), col(_HG),
        ],
        out_specs=pl.BlockSpec((None, t, LANES), lambda bi, h: (bi, 0, h)),
        out_shape=jax.ShapeDtypeStruct((b, t, HG_HEADS * HG_V), jnp.bfloat16),
        scratch_shapes=[
            pltpu.VMEM((t, HG_V), jnp.float32),
            pltpu.VMEM((t, HG_V), jnp.float32),
            pltpu.VMEM((HG_V, HG_K), jnp.float32),
            pltpu.VMEM((HG_V, HG_K), jnp.float32),
        ],
        compiler_params=_cparams(("parallel", "parallel")),
        name="hgrn_bidir",
    )(hgrn_lb, norm_g.reshape(1, HG_V).astype(jnp.float32),
      proj3d, proj3d, proj3d, proj3d, proj3d)


def _out_proj_kernel(a_ref, h_ref, wa_ref, wh_ref, x_ref, o_ref):
    y = jnp.dot(a_ref[...], wa_ref[...], preferred_element_type=jnp.float32)
    y = y + jnp.dot(h_ref[...], wh_ref[...], preferred_element_type=jnp.float32)
    o_ref[...] = x_ref[...] + y


def out_proj(att2d, hgo2d, w_out_bf16, x2d, tm=512):
    n, d = x2d.shape
    tm = min(tm, n)
    wa = att2d.shape[1]
    wh = hgo2d.shape[1]
    return pl.pallas_call(
        _out_proj_kernel,
        grid=(n // tm,),
        in_specs=[
            pl.BlockSpec((tm, wa), lambda i: (i, 0)),
            pl.BlockSpec((tm, wh), lambda i: (i, 0)),
            pl.BlockSpec((wa, d), lambda i: (0, 0)),
            pl.BlockSpec((wh, d), lambda i: (wa // wh, 0)),
            pl.BlockSpec((tm, d), lambda i: (i, 0)),
        ],
        out_specs=pl.BlockSpec((tm, d), lambda i: (i, 0)),
        out_shape=jax.ShapeDtypeStruct((n, d), jnp.float32),
        compiler_params=_cparams(("parallel",)),
        name="out_proj",
    )(att2d, hgo2d, w_out_bf16, w_out_bf16, x2d)


def _router_kernel(x_ref, g_ref, wr_ref, h_ref, aff_ref):
    x = x_ref[...]
    h = x * lax.rsqrt(jnp.mean(x * x, axis=-1, keepdims=True) + EPS) * g_ref[...]
    h_ref[...] = h.astype(h_ref.dtype)
    logits = jnp.dot(h, wr_ref[...], preferred_element_type=jnp.float32,
                     precision=lax.Precision.HIGHEST)
    ex = jnp.exp(logits - jnp.max(logits, axis=-1, keepdims=True))
    aff_ref[...] = ex / jnp.sum(ex, axis=-1, keepdims=True)


def router(x2d, g, w_router, tm=512):
    n, d = x2d.shape
    tm = min(tm, n)
    e = w_router.shape[1]
    return pl.pallas_call(
        _router_kernel,
        grid=(n // tm,),
        in_specs=[
            pl.BlockSpec((tm, d), lambda i: (i, 0)),
            pl.BlockSpec((1, d), lambda i: (0, 0)),
            pl.BlockSpec((d, e), lambda i: (0, 0)),
        ],
        out_specs=[
            pl.BlockSpec((tm, d), lambda i: (i, 0)),
            pl.BlockSpec((tm, e), lambda i: (i, 0)),
        ],
        out_shape=[
            jax.ShapeDtypeStruct((n, d), jnp.bfloat16),
            jax.ShapeDtypeStruct((n, e), jnp.float32),
        ],
        compiler_params=_cparams(("parallel",)),
        name="router",
    )(x2d, g.reshape(1, d), w_router)


def _expert_kernel(xs_ref, gate_ref, wg_ref, wu_ref, wd_ref, o_ref):
    xs = xs_ref[...]
    a = jnp.dot(xs, wg_ref[...], preferred_element_type=jnp.float32)
    u = jnp.dot(xs, wu_ref[...], preferred_element_type=jnp.float32)
    hid = (a * jax.nn.sigmoid(a) * u).astype(jnp.bfloat16)
    y = jnp.dot(hid, wd_ref[...], preferred_element_type=jnp.float32)
    o_ref[...] = y * gate_ref[...]


def expert_ffn(xs, gates, wg, wu, wd, tm=512):
    e, c, d = xs.shape
    tm = min(tm, c)
    f = wg.shape[2]
    return pl.pallas_call(
        _expert_kernel,
        grid=(e, c // tm),
        in_specs=[
            pl.BlockSpec((None, tm, d), lambda ei, i: (ei, i, 0)),
            pl.BlockSpec((None, tm, 1), lambda ei, i: (ei, i, 0)),
            pl.BlockSpec((None, d, f), lambda ei, i: (ei, 0, 0)),
            pl.BlockSpec((None, d, f), lambda ei, i: (ei, 0, 0)),
            pl.BlockSpec((None, f, d), lambda ei, i: (ei, 0, 0)),
        ],
        out_specs=pl.BlockSpec((None, tm, d), lambda ei, i: (ei, i, 0)),
        out_shape=jax.ShapeDtypeStruct((e, c, d), jnp.float32),
        compiler_params=_cparams(("parallel", "arbitrary")),
        name="expert_ffn",
    )(xs, gates.reshape(e, c, 1), wg, wu, wd)


def _rms_kernel(x_ref, g_ref, o_ref):
    x = x_ref[...]
    o_ref[...] = x * lax.rsqrt(jnp.mean(x * x, axis=-1, keepdims=True) + EPS) * g_ref[...]


def rms_norm_call(x2d, g, tm=1024):
    n, d = x2d.shape
    tm = min(tm, n)
    return pl.pallas_call(
        _rms_kernel,
        grid=(n // tm,),
        in_specs=[pl.BlockSpec((tm, d), lambda i: (i, 0)),
                  pl.BlockSpec((1, d), lambda i: (0, 0))],
        out_specs=pl.BlockSpec((tm, d), lambda i: (i, 0)),
        out_shape=jax.ShapeDtypeStruct((n, d), jnp.float32),
        compiler_params=_cparams(("parallel",)),
        name="final_norm",
    )(x2d, g.reshape(1, d))


def _trunk(x, p):
    b, t, d = x.shape
    n = b * t
    depth = p["w_in"].shape[0]
    x2d = x.reshape(n, d)
    cap = max(1, EC_CAPACITY * n // N_EXPERTS)
    for layer in range(depth):
        proj = norm_proj(x2d, p["norm_mix_g"][layer], p["w_in"][layer])
        proj3d = proj.reshape(b, t, -1)
        att = diff_attention(proj3d, p["lambda_q1"][layer], p["lambda_k1"][layer],
                             p["lambda_q2"][layer], p["lambda_k2"][layer],
                             p["diff_subln_g"][layer], layer)
        hgo = hgrn_bidir(proj3d, p["hgrn_lb"], p["hgrn_norm_g"][layer], layer)
        x2d = out_proj(att.reshape(n, -1), hgo.reshape(n, -1), p["w_out"][layer], x2d)
        h, aff = router(x2d, p["norm_ffn_g"][layer], p["w_router"][layer])
        gates, idx = lax.top_k(aff.T, cap)
        xs = h[idx]
        out = expert_ffn(xs, gates, p["w_gate"][layer], p["w_up"][layer], p["w_down"][layer])
        x2d = x2d.at[idx.reshape(-1)].add(out.reshape(-1, d))
    return rms_norm_call(x2d, p["final_norm_g"]).reshape(b, t, d)


def kernel(x_prompt, x_sample, norm_mix_g, w_in, lambda_q1, lambda_k1, lambda_q2, lambda_k2,
           diff_subln_g, hgrn_lb, hgrn_norm_g, w_out, norm_ffn_g, w_router, w_gate, w_up,
           w_down, final_norm_g):
    bf = jnp.bfloat16
    p = dict(norm_mix_g=norm_mix_g, w_in=w_in.astype(bf), lambda_q1=lambda_q1,
             lambda_k1=lambda_k1, lambda_q2=lambda_q2, lambda_k2=lambda_k2,
             diff_subln_g=diff_subln_g, hgrn_lb=hgrn_lb, hgrn_norm_g=hgrn_norm_g,
             w_out=w_out.astype(bf), norm_ffn_g=norm_ffn_g, w_router=w_router,
             w_gate=w_gate.astype(bf), w_up=w_up.astype(bf), w_down=w_down.astype(bf),
             final_norm_g=final_norm_g)
    return (_trunk(x_prompt, p), _trunk(x_sample, p))
```

```python
import functools
import math

import jax
import jax.numpy as jnp
from jax import lax
from jax.experimental import pallas as pl
from jax.experimental.pallas import tpu as pltpu

ATT_HEADS = 4
ATT_QK_DIM = 64
ATT_V_DIM = 128
HG_HEADS = 4
HG_K = 128
HG_V = 128
HG_CHUNK = 64
N_EXPERTS = 16
EC_CAPACITY = 2
EPS = 1e-6
LANES = 128
SUBLANES = 8

_QBLK, _KBLK, _VBLK = 0, 4, 8
_HQ, _HFF, _HFB, _HI, _HG = 12, 16, 20, 24, 28

_VMEM_LIMIT = 56 * 1024 * 1024

_NT = (((1,), (1,)), ((), ()))


def _cparams(sem):
    return pltpu.CompilerParams(dimension_semantics=sem, vmem_limit_bytes=_VMEM_LIMIT)


def _norm_proj_kernel(x_ref, g_ref, w_ref, o_ref, *, n_split):
    x = x_ref[...]
    y = x * lax.rsqrt(jnp.mean(x * x, axis=-1, keepdims=True) + EPS) * g_ref[...]
    h = y.astype(jnp.bfloat16)
    wn = w_ref.shape[1] // n_split
    for c in range(n_split):
        o_ref[:, c * wn:(c + 1) * wn] = jnp.dot(
            h, w_ref[:, c * wn:(c + 1) * wn], preferred_element_type=jnp.float32
        ).astype(o_ref.dtype)


def norm_proj(x2d, g, w_bf16, tm=512):
    n, d = x2d.shape
    tm = min(tm, n)
    dout = w_bf16.shape[1]
    return pl.pallas_call(
        functools.partial(_norm_proj_kernel, n_split=max(1, dout // 1024)),
        grid=(n // tm,),
        in_specs=[
            pl.BlockSpec((tm, d), lambda i: (i, 0)),
            pl.BlockSpec((1, d), lambda i: (0, 0)),
            pl.BlockSpec((d, dout), lambda i: (0, 0)),
        ],
        out_specs=pl.BlockSpec((tm, dout), lambda i: (i, 0)),
        out_shape=jax.ShapeDtypeStruct((n, dout), jnp.bfloat16),
        compiler_params=_cparams(("parallel",)),
        name="norm_proj",
    )(x2d, g.reshape(1, d), w_bf16)


_LOG2E = 1.4426950408889634
_Q_STRIP = 128
_ONES_ROWS = 16


def _split3(x):
    bf = jnp.bfloat16
    h = x.astype(bf).astype(jnp.float32)
    r = x - h
    m = r.astype(bf).astype(jnp.float32)
    lo = (r - m).astype(bf).astype(jnp.float32)
    return h, m, lo


def _attn_kernel(beta_ref, qext_ref, lq1_ref, lk1_ref, lq2_ref, lk2_ref, subg_ref,
                 q_ref, k_ref, v_ref, kext_ref, o_ref, vt_sc, acc_sc, s_sc, *, lam_init, tk, tq):
    f32, bf = jnp.float32, jnp.bfloat16
    t = q_ref.shape[0]
    nk = t // tk
    qs_ = _Q_STRIP
    ng = tq // qs_
    ncol = 2 * tq
    dv = v_ref.shape[1]
    beta = beta_ref[0:1, 0:1]

    def vt_body(c, carry):
        r0 = pl.multiple_of(c * tk, tk)
        vt_sc[0:dv, pl.ds(r0, tk)] = v_ref[pl.ds(r0, tk), :].astype(f32).T.astype(bf)
        return carry

    lax.fori_loop(0, nk, vt_body, 0)
    vt_sc[dv:dv + _ONES_ROWS, :] = jnp.ones((_ONES_ROWS, t), bf)

    lam = (jnp.exp(jnp.sum(lq1_ref[...] * lk1_ref[...], axis=-1, keepdims=True))
           - jnp.exp(jnp.sum(lq2_ref[...] * lk2_ref[...], axis=-1, keepdims=True)) + lam_init)
    lane = lax.broadcasted_iota(jnp.int32, (qs_, LANES), 1)
    kr = lax.broadcasted_iota(jnp.int32, (tk, qs_), 0)
    qc = lax.broadcasted_iota(jnp.int32, (tk, qs_), 1)
    cscale = ATT_QK_DIM ** -0.5 * _LOG2E

    def group(gi, carry):
        g0 = pl.multiple_of(gi * tq, tq)
        tiles = []
        for u in range(ng):
            q0 = g0 + u * qs_
            qs = (q_ref[pl.ds(q0, qs_), :].astype(f32) * cscale).astype(bf).astype(f32)
            qext = qext_ref[...]
            q_aug = jnp.concatenate([
                jnp.concatenate([jnp.where(lane < ATT_QK_DIM, qs, 0.0), qext], axis=1),
                jnp.concatenate([jnp.where(lane >= ATT_QK_DIM, qs, 0.0), qext], axis=1)], axis=0)
            tiles.append(q_aug.T.astype(bf))
        w = jnp.concatenate(tiles, axis=1)
        jc = g0 // tk

        def corr_tile(d):
            corr = 2.0 * beta * jnp.minimum((d + qc - kr).astype(f32), 0.0)
            return jnp.concatenate([corr, corr], axis=1)

        def shift(d, sidx):
            sgn = 1.0 if sidx == 0 else -1.0
            return (-sgn) * beta * d.astype(f32)

        def scores(n, sidx, crossing):
            r0 = pl.multiple_of(n * tk, tk)
            k_aug = jnp.concatenate([k_ref[pl.ds(r0, tk), :], kext_ref[sidx]], axis=1)
            for u in range(ng):
                c0 = u * 2 * qs_
                s = jnp.dot(k_aug, w[:, c0:c0 + 2 * qs_], preferred_element_type=f32)
                if crossing:
                    s = s + corr_tile(g0 + u * qs_ - r0)
                s_sc[:, c0:c0 + 2 * qs_] = s

        def step(n, mc, sidx, crossing):
            r0 = pl.multiple_of((n - 1) * tk, tk)
            r1 = pl.multiple_of(n * tk, tk)
            vt = vt_sc[:, pl.ds(r0, tk)]
            if sidx is not None:
                k_aug = jnp.concatenate([k_ref[pl.ds(r1, tk), :], kext_ref[sidx]], axis=1)
            m, csts = mc
            ms, cs = [], []
            for u in range(ng):
                c0 = u * 2 * qs_
                s = s_sc[:, c0:c0 + 2 * qs_]
                m_old = m[:, c0:c0 + 2 * qs_]
                cst = csts[:, c0:c0 + 2 * qs_]
                m_new = jnp.maximum(m_old, jnp.max(s, axis=0, keepdims=True) + cst)
                alpha = jnp.exp2(m_old - m_new)
                p = jnp.exp2(s - (m_new - cst)).astype(bf)
                acc_sc[:, c0:c0 + 2 * qs_] = (acc_sc[:, c0:c0 + 2 * qs_] * alpha
                                              + jnp.dot(vt, p, preferred_element_type=f32))
                ms.append(m_new)
                if sidx is not None:
                    s = jnp.dot(k_aug, w[:, c0:c0 + 2 * qs_], preferred_element_type=f32)
                    if crossing:
                        s = s + corr_tile(g0 + u * qs_ - r1)
                    s_sc[:, c0:c0 + 2 * qs_] = s
                    cs.append(jnp.broadcast_to(shift(g0 + u * qs_ - r1, sidx), (1, 2 * qs_)))
            return jnp.concatenate(ms, axis=1), (jnp.concatenate(cs, axis=1) if cs else csts)

        acc_sc[...] = jnp.zeros(acc_sc.shape, f32)
        m = jnp.full((1, ncol), -1e30, f32)
        scores(0, 0, True)
        n1 = jnp.maximum(jc, 1)
        n2 = jc + tq // tk
        cst0 = jnp.concatenate([jnp.broadcast_to(shift(g0 + u * qs_, 0), (1, 2 * qs_))
                                for u in range(ng)], axis=1)
        mc = (m, cst0)
        mc = lax.fori_loop(1, n1, functools.partial(step, sidx=0, crossing=False), mc)
        mc = lax.fori_loop(n1, n2, functools.partial(step, sidx=0, crossing=True), mc)
        mc = lax.fori_loop(n2, nk, functools.partial(step, sidx=1, crossing=False), mc)
        step(nk, mc, None, False)

        for u in range(ng):
            c0 = u * 2 * qs_
            acc = acc_sc[:, c0:c0 + 2 * qs_]
            den = acc[dv:dv + 1, :]
            o = acc[0:dv, 0:qs_] / den[:, 0:qs_] - lam * (acc[0:dv, qs_:] / den[:, qs_:])
            y = o * lax.rsqrt(jnp.mean(o * o, axis=0, keepdims=True) + EPS) * subg_ref[...]
            o_ref[pl.ds(g0 + u * qs_, qs_), :] = (y * (1.0 - lam_init)).T.astype(o_ref.dtype)
        return carry

    lax.fori_loop(0, t // tq, group, 0)


def _attn_tables(tk):
    f32 = jnp.float32
    slopes = 2.0 ** (-8.0 * (jnp.arange(ATT_HEADS, dtype=f32) + 1.0) / ATT_HEADS)
    beta = slopes * _LOG2E
    lane = jnp.arange(LANES)[None, None, :]
    b = _split3(beta)
    a = _split3(-beta[:, None] * jnp.arange(_Q_STRIP, dtype=f32)[None, :])
    qext = jnp.zeros((ATT_HEADS, _Q_STRIP, LANES), f32)
    for c in range(3):
        qext = jnp.where(lane == c, b[c][:, None, None], qext)
        qext = jnp.where(lane == 3 + c, a[c][:, :, None], qext)
    r = jnp.arange(tk, dtype=f32)[:, None]
    lane2 = jnp.arange(LANES)[None, :]
    base = jnp.where(lane2 < 3, r, jnp.where(lane2 < 6, 1.0, 0.0))
    kext = jnp.stack([base, -base]).astype(jnp.bfloat16)
    beta_t = jnp.broadcast_to(beta[:, None, None], (ATT_HEADS, 1, LANES))
    return beta_t, qext, kext


def diff_attention(proj3d, lq1, lk1, lq2, lk2, subg, layer, tk=256, tq=1024):
    b, t, _ = proj3d.shape
    tk, tq = min(tk, t), min(tq, t)
    assert tq % tk == 0 and tq % _Q_STRIP == 0
    lam_init = 0.8 - 0.6 * math.exp(-0.3 * layer)
    assert tk <= 256
    beta_t, qext, kext = _attn_tables(tk)
    vec = lambda a: a.reshape(1, -1).astype(jnp.float32)
    small = lambda n: pl.BlockSpec((1, n), lambda bi, h: (0, 0))
    head = pl.BlockSpec((None, 1, LANES), lambda bi, h: (h, 0, 0))
    col = lambda blk: pl.BlockSpec((None, t, LANES), lambda bi, h: (bi, 0, blk + h))
    return pl.pallas_call(
        functools.partial(_attn_kernel, lam_init=lam_init, tk=tk, tq=tq),
        grid=(b, ATT_HEADS),
        in_specs=[
            head, pl.BlockSpec((None, _Q_STRIP, LANES), lambda bi, h: (h, 0, 0)),
            small(ATT_QK_DIM), small(ATT_QK_DIM), small(ATT_QK_DIM), small(ATT_QK_DIM),
            pl.BlockSpec((ATT_V_DIM, 1), lambda bi, h: (0, 0)),
            col(_QBLK), col(_KBLK), col(_VBLK),
            pl.BlockSpec((2, tk, LANES), lambda bi, h: (0, 0, 0)),
        ],
        out_specs=pl.BlockSpec((None, t, LANES), lambda bi, h: (bi, 0, h)),
        out_shape=jax.ShapeDtypeStruct((b, t, ATT_HEADS * ATT_V_DIM), jnp.bfloat16),
        scratch_shapes=[pltpu.VMEM((ATT_V_DIM + _ONES_ROWS, t), jnp.bfloat16),
                        pltpu.VMEM((ATT_V_DIM + _ONES_ROWS, 2 * tq), jnp.float32),
                        pltpu.VMEM((tk, 2 * tq), jnp.float32)],
        compiler_params=_cparams(("parallel", "parallel")),
        name="diff_attn",
    )(beta_t, qext, vec(lq1), vec(lk1), vec(lq2), vec(lk2),
      subg.reshape(ATT_V_DIM, 1).astype(jnp.float32), proj3d, proj3d, proj3d, kext)


_HG_CHUNKS_PER_ITER = 2


def _scan_rows(g, rev):
    c = g.shape[0]
    row = lax.broadcasted_iota(jnp.int32, g.shape, 0)
    b = g
    s = 1
    while s < c:
        if rev:
            b = b + jnp.where(row < c - s, pltpu.roll(b, c - s, axis=0), 0.0)
        else:
            b = b + jnp.where(row >= s, pltpu.roll(b, s, axis=0), 0.0)
        s *= 2
    return b


def _level_ref(b, m, rev):
    c, w = b.shape
    r = m // 2 if rev else m // 2 - 1
    pieces = []
    if m >= SUBLANES:
        for j in range(c // m):
            pieces.append(jnp.broadcast_to(b[j * m + r:j * m + r + 1, :], (m, w)))
    else:
        sub = lax.broadcasted_iota(jnp.int32, (SUBLANES, w), 0)
        for j in range(c // SUBLANES):
            base = j * SUBLANES
            acc = jnp.broadcast_to(b[base + r:base + r + 1, :], (SUBLANES, w))
            for i in range(1, SUBLANES // m):
                cand = jnp.broadcast_to(b[base + i * m + r:base + i * m + r + 1, :], (SUBLANES, w))
                acc = jnp.where(sub >= i * m, cand, acc)
            pieces.append(acc)
    return jnp.concatenate(pieces, axis=0)


def _hgrn_masks(c, rev):
    row = lax.broadcasted_iota(jnp.int32, (c, LANES), 0)
    ti = lax.broadcasted_iota(jnp.int32, (c, c), 0)
    si = lax.broadcasted_iota(jnp.int32, (c, c), 1)
    qrows, pairs = {}, {}
    m = c
    while m >= 2:
        late = (row & (m - 1)) >= m // 2
        t_late = (ti & (m - 1)) >= m // 2
        s_late = (si & (m - 1)) >= m // 2
        same = (ti ^ si) < m
        if rev:
            qrows[m] = jnp.logical_not(late)
            pairs[m] = same & jnp.logical_not(t_late) & s_late
        else:
            qrows[m] = late
            pairs[m] = same & t_late & jnp.logical_not(s_late)
        m //= 2
    return qrows, pairs, ti == si


def _hgrn_chunk(q, f, v, st_ref, rev, masks):
    qrows, pairs, diag = masks
    c = q.shape[0]
    bf = jnp.bfloat16
    k_ = 1.0 - f
    b = _scan_rows(jnp.log(f), rev)
    a = jnp.where(diag,
                  lax.dot_general(q.astype(bf), k_.astype(bf), _NT,
                                  preferred_element_type=jnp.float32), 0.0)
    m = c
    while m >= 2:
        e = jnp.exp(-jnp.abs(b - _level_ref(b, m, rev)))
        z = (jnp.where(qrows[m], q, k_) * e).astype(bf)
        p = lax.dot_general(z, z, _NT, preferred_element_type=jnp.float32)
        a = a + jnp.where(pairs[m], p, 0.0)
        m //= 2
    st = st_ref[...]
    o = jnp.dot(a.astype(bf), v.astype(bf), preferred_element_type=jnp.float32)
    o = o + lax.dot_general((q * jnp.exp(b)).astype(bf), st.astype(bf), _NT,
                            preferred_element_type=jnp.float32)
    btot = b[0:1, :] if rev else b[c - 1:c, :]
    khat = (k_ * jnp.exp(btot - b)).astype(bf)
    st_ref[...] = st * jnp.exp(btot) + jnp.dot(v.T.astype(bf), khat,
                                               preferred_element_type=jnp.float32)
    return o


def _hgrn_kernel(lb_ref, ng_ref, q_ref, ff_ref, fb_ref, i_ref, g_ref, o_ref,
                 of_sc, ob_sc, stf_sc, stb_sc, *, layer, chunk):
    t = q_ref.shape[0]
    n = t // chunk
    lb_raw = lb_ref[...].astype(jnp.float32)
    ex = jnp.exp(lb_raw - jnp.max(lb_raw, axis=0, keepdims=True))
    sm = ex / jnp.sum(ex, axis=0, keepdims=True)
    lb = jnp.zeros_like(sm[0])
    for l_ in range(1, layer + 1):
        lb = lb + sm[l_]
    lb_f = lb[0:1, :]
    lb_b = lb[1:2, :]

    stf_sc[...] = jnp.zeros(stf_sc.shape, jnp.float32)
    stb_sc[...] = jnp.zeros(stb_sc.shape, jnp.float32)

    def load(ref, r0):
        return ref[pl.ds(r0, chunk), :].astype(jnp.float32)

    masks_f = _hgrn_masks(chunk, False)
    masks_b = _hgrn_masks(chunk, True)

    per_iter = _HG_CHUNKS_PER_ITER if n % _HG_CHUNKS_PER_ITER == 0 else 1

    def body(ci, carry):
        for u in range(per_iter):
            r0 = pl.multiple_of((ci * per_iter + u) * chunk, chunk)
            qr = load(q_ref, r0)
            f = lb_f + (1.0 - lb_f) * jax.nn.sigmoid(load(ff_ref, r0))
            of_sc[pl.ds(r0, chunk), :] = _hgrn_chunk(qr * jax.nn.sigmoid(qr), f,
                                                     load(i_ref, r0), stf_sc, False, masks_f)
            r1 = pl.multiple_of((n - 1 - ci * per_iter - u) * chunk, chunk)
            qr = load(q_ref, r1)
            f = lb_b + (1.0 - lb_b) * jax.nn.sigmoid(load(fb_ref, r1))
            ob_sc[pl.ds(r1, chunk), :] = _hgrn_chunk(qr * jax.nn.sigmoid(qr), f,
                                                     load(i_ref, r1), stb_sc, True, masks_b)
        return carry

    lax.fori_loop(0, n // per_iter, body, 0)

    def fin(ci, carry):
        r0 = pl.multiple_of(ci * chunk, chunk)
        o = of_sc[pl.ds(r0, chunk), :] + ob_sc[pl.ds(r0, chunk), :]
        y = o * lax.rsqrt(jnp.mean(o * o, axis=-1, keepdims=True) + EPS) * ng_ref[...]
        gr = load(g_ref, r0)
        o_ref[pl.ds(r0, chunk), :] = (y * (gr * jax.nn.sigmoid(gr))).astype(o_ref.dtype)
        return carry

    lax.fori_loop(0, n, fin, 0)


def hgrn_bidir(proj3d, hgrn_lb, norm_g, layer):
    b, t, _ = proj3d.shape
    depth = hgrn_lb.shape[0]
    col = lambda blk: pl.BlockSpec((None, t, LANES), lambda bi, h: (bi, 0, blk + h))
    return pl.pallas_call(
        functools.partial(_hgrn_kernel, layer=layer, chunk=HG_CHUNK),
        grid=(b, HG_HEADS),
        in_specs=[
            pl.BlockSpec((depth, 2, LANES), lambda bi, h: (0, 0, h)),
            pl.BlockSpec((1, HG_V), lambda bi, h: (0, 0)),
            col(_HQ), col(_HFF), col(_HFB), col(_HI), col(_HG),
        ],
        out_specs=pl.BlockSpec((None, t, LANES), lambda bi, h: (bi, 0, h)),
        out_shape=jax.ShapeDtypeStruct((b, t, HG_HEADS * HG_V), jnp.bfloat16),
        scratch_shapes=[
            pltpu.VMEM((t, HG_V), jnp.float32),
            pltpu.VMEM((t, HG_V), jnp.float32),
            pltpu.VMEM((HG_V, HG_K), jnp.float32),
            pltpu.VMEM((HG_V, HG_K), jnp.float32),
        ],
        compiler_params=_cparams(("parallel", "parallel")),
        name="hgrn_bidir",
    )(hgrn_lb, norm_g.reshape(1, HG_V).astype(jnp.float32),
      proj3d, proj3d, proj3d, proj3d, proj3d)


def _out_proj_kernel(a_ref, h_ref, wa_ref, wh_ref, x_ref, o_ref):
    y = jnp.dot(a_ref[...], wa_ref[...], preferred_element_type=jnp.float32)
    y = y + jnp.dot(h_ref[...], wh_ref[...], preferred_element_type=jnp.float32)
    o_ref[...] = x_ref[...] + y


def out_proj(att2d, hgo2d, w_out_bf16, x2d, tm=512):
    n, d = x2d.shape
    tm = min(tm, n)
    wa = att2d.shape[1]
    wh = hgo2d.shape[1]
    return pl.pallas_call(
        _out_proj_kernel,
        grid=(n // tm,),
        in_specs=[
            pl.BlockSpec((tm, wa), lambda i: (i, 0)),
            pl.BlockSpec((tm, wh), lambda i: (i, 0)),
            pl.BlockSpec((wa, d), lambda i: (0, 0)),
            pl.BlockSpec((wh, d), lambda i: (wa // wh, 0)),
            pl.BlockSpec((tm, d), lambda i: (i, 0)),
        ],
        out_specs=pl.BlockSpec((tm, d), lambda i: (i, 0)),
        out_shape=jax.ShapeDtypeStruct((n, d), jnp.float32),
        compiler_params=_cparams(("parallel",)),
        name="out_proj",
    )(att2d, hgo2d, w_out_bf16, w_out_bf16, x2d)


def _router_kernel(x_ref, g_ref, wr_ref, h_ref, aff_ref):
    x = x_ref[...]
    h = x * lax.rsqrt(jnp.mean(x * x, axis=-1, keepdims=True) + EPS) * g_ref[...]
    h_ref[...] = h.astype(h_ref.dtype)
    logits = jnp.dot(h, wr_ref[...], preferred_element_type=jnp.float32,
                     precision=lax.Precision.HIGHEST)
    ex = jnp.exp(logits - jnp.max(logits, axis=-1, keepdims=True))
    aff_ref[...] = ex / jnp.sum(ex, axis=-1, keepdims=True)


def router(x2d, g, w_router, tm=512):
    n, d = x2d.shape
    tm = min(tm, n)
    e = w_router.shape[1]
    return pl.pallas_call(
        _router_kernel,
        grid=(n // tm,),
        in_specs=[
            pl.BlockSpec((tm, d), lambda i: (i, 0)),
            pl.BlockSpec((1, d), lambda i: (0, 0)),
            pl.BlockSpec((d, e), lambda i: (0, 0)),
        ],
        out_specs=[
            pl.BlockSpec((tm, d), lambda i: (i, 0)),
            pl.BlockSpec((tm, e), lambda i: (i, 0)),
        ],
        out_shape=[
            jax.ShapeDtypeStruct((n, d), jnp.bfloat16),
            jax.ShapeDtypeStruct((n, e), jnp.float32),
        ],
        compiler_params=_cparams(("parallel",)),
        name="router",
    )(x2d, g.reshape(1, d), w_router)


def _expert_kernel(xs_ref, gate_ref, wg_ref, wu_ref, wd_ref, o_ref):
    xs = xs_ref[...]
    a = jnp.dot(xs, wg_ref[...], preferred_element_type=jnp.float32)
    u = jnp.dot(xs, wu_ref[...], preferred_element_type=jnp.float32)
    hid = (a * jax.nn.sigmoid(a) * u).astype(jnp.bfloat16)
    y = jnp.dot(hid, wd_ref[...], preferred_element_type=jnp.float32)
    o_ref[...] = y * gate_ref[...]


def expert_ffn(xs, gates, wg, wu, wd, tm=512):
    e, c, d = xs.shape
    tm = min(tm, c)
    f = wg.shape[2]
    return pl.pallas_call(
        _expert_kernel,
        grid=(e, c // tm),
        in_specs=[
            pl.BlockSpec((None, tm, d), lambda ei, i: (ei, i, 0)),
            pl.BlockSpec((None, tm, 1), lambda ei, i: (ei, i, 0)),
            pl.BlockSpec((None, d, f), lambda ei, i: (ei, 0, 0)),
            pl.BlockSpec((None, d, f), lambda ei, i: (ei, 0, 0)),
            pl.BlockSpec((None, f, d), lambda ei, i: (ei, 0, 0)),
        ],
        out_specs=pl.BlockSpec((None, tm, d), lambda ei, i: (ei, i, 0)),
        out_shape=jax.ShapeDtypeStruct((e, c, d), jnp.float32),
        compiler_params=_cparams(("parallel", "arbitrary")),
        name="expert_ffn",
    )(xs, gates.reshape(e, c, 1), wg, wu, wd)


def _rms_kernel(x_ref, g_ref, o_ref):
    x = x_ref[...]
    o_ref[...] = x * lax.rsqrt(jnp.mean(x * x, axis=-1, keepdims=True) + EPS) * g_ref[...]


def rms_norm_call(x2d, g, tm=1024):
    n, d = x2d.shape
    tm = min(tm, n)
    return pl.pallas_call(
        _rms_kernel,
        grid=(n // tm,),
        in_specs=[pl.BlockSpec((tm, d), lambda i: (i, 0)),
                  pl.BlockSpec((1, d), lambda i: (0, 0))],
        out_specs=pl.BlockSpec((tm, d), lambda i: (i, 0)),
        out_shape=jax.ShapeDtypeStruct((n, d), jnp.float32),
        compiler_params=_cparams(("parallel",)),
        name="final_norm",
    )(x2d, g.reshape(1, d))


def _trunk(x, p):
    b, t, d = x.shape
    n = b * t
    depth = p["w_in"].shape[0]
    x2d = x.reshape(n, d)
    cap = max(1, EC_CAPACITY * n // N_EXPERTS)
    for layer in range(depth):
        proj = norm_proj(x2d, p["norm_mix_g"][layer], p["w_in"][layer])
        proj3d = proj.reshape(b, t, -1)
        att = diff_attention(proj3d, p["lambda_q1"][layer], p["lambda_k1"][layer],
                             p["lambda_q2"][layer], p["lambda_k2"][layer],
                             p["diff_subln_g"][layer], layer)
        hgo = hgrn_bidir(proj3d, p["hgrn_lb"], p["hgrn_norm_g"][layer], layer)
        x2d = out_proj(att.reshape(n, -1), hgo.reshape(n, -1), p["w_out"][layer], x2d)
        h, aff = router(x2d, p["norm_ffn_g"][layer], p["w_router"][layer])
        gates, idx = lax.top_k(aff.T, cap)
        xs = h[idx]
        out = expert_ffn(xs, gates, p["w_gate"][layer], p["w_up"][layer], p["w_down"][layer])
        x2d = x2d.at[idx.reshape(-1)].add(out.reshape(-1, d))
    return rms_norm_call(x2d, p["final_norm_g"]).reshape(b, t, d)


def kernel(x_prompt, x_sample, norm_mix_g, w_in, lambda_q1, lambda_k1, lambda_q2, lambda_k2,
           diff_subln_g, hgrn_lb, hgrn_norm_g, w_out, norm_ffn_g, w_router, w_gate, w_up,
           w_down, final_norm_g):
    bf = jnp.bfloat16
    p = dict(norm_mix_g=norm_mix_g, w_in=w_in.astype(bf), lambda_q1=lambda_q1,
             lambda_k1=lambda_k1, lambda_q2=lambda_q2, lambda_k2=lambda_k2,
             diff_subln_g=diff_subln_g, hgrn_lb=hgrn_lb, hgrn_norm_g=hgrn_norm_g,
             w_out=w_out.astype(bf), norm_ffn_g=norm_ffn_g, w_router=w_router,
             w_gate=w_gate.astype(bf), w_up=w_up.astype(bf), w_down=w_down.astype(bf),
             final_norm_g=final_norm_g)
    return (_trunk(x_prompt, p), _trunk(x_sample, p))
```

```python
import functools
import math

import jax
import jax.numpy as jnp
from jax import lax
from jax.experimental import pallas as pl
from jax.experimental.pallas import tpu as pltpu

ATT_HEADS = 4
ATT_QK_DIM = 64
ATT_V_DIM = 128
HG_HEADS = 4
HG_K = 128
HG_V = 128
HG_CHUNK = 64
N_EXPERTS = 16
EC_CAPACITY = 2
EPS = 1e-6
LANES = 128
SUBLANES = 8

_QBLK, _KBLK, _VBLK = 0, 4, 8
_HQ, _HFF, _HFB, _HI, _HG = 12, 16, 20, 24, 28

_VMEM_LIMIT = 56 * 1024 * 1024

_NT = (((1,), (1,)), ((), ()))


def _cparams(sem):
    return pltpu.CompilerParams(dimension_semantics=sem, vmem_limit_bytes=_VMEM_LIMIT)


def _norm_proj_kernel(x_ref, g_ref, w_ref, o_ref, *, n_split):
    x = x_ref[...]
    y = x * lax.rsqrt(jnp.mean(x * x, axis=-1, keepdims=True) + EPS) * g_ref[...]
    h = y.astype(jnp.bfloat16)
    wn = w_ref.shape[1] // n_split
    for c in range(n_split):
        o_ref[:, c * wn:(c + 1) * wn] = jnp.dot(
            h, w_ref[:, c * wn:(c + 1) * wn], preferred_element_type=jnp.float32
        ).astype(o_ref.dtype)


def norm_proj(x2d, g, w_bf16, tm=512):
    n, d = x2d.shape
    tm = min(tm, n)
    dout = w_bf16.shape[1]
    return pl.pallas_call(
        functools.partial(_norm_proj_kernel, n_split=max(1, dout // 1024)),
        grid=(n // tm,),
        in_specs=[
            pl.BlockSpec((tm, d), lambda i: (i, 0)),
            pl.BlockSpec((1, d), lambda i: (0, 0)),
            pl.BlockSpec((d, dout), lambda i: (0, 0)),
        ],
        out_specs=pl.BlockSpec((tm, dout), lambda i: (i, 0)),
        out_shape=jax.ShapeDtypeStruct((n, dout), jnp.bfloat16),
        compiler_params=_cparams(("parallel",)),
        name="norm_proj",
    )(x2d, g.reshape(1, d), w_bf16)


_LOG2E = 1.4426950408889634
_Q_STRIP = 128
_ONES_ROWS = 16


def _split3(x):
    bf = jnp.bfloat16
    h = x.astype(bf).astype(jnp.float32)
    r = x - h
    m = r.astype(bf).astype(jnp.float32)
    lo = (r - m).astype(bf).astype(jnp.float32)
    return h, m, lo


def _attn_kernel(beta_ref, qext_ref, lq1_ref, lk1_ref, lq2_ref, lk2_ref, subg_ref,
                 q_ref, k_ref, v_ref, kext_ref, o_ref, vt_sc, acc_sc, s_sc, *, lam_init, tk, tq):
    f32, bf = jnp.float32, jnp.bfloat16
    t = q_ref.shape[0]
    nk = t // tk
    qs_ = _Q_STRIP
    ng = tq // qs_
    ncol = 2 * tq
    dv = v_ref.shape[1]
    beta = beta_ref[0:1, 0:1]

    def vt_body(c, carry):
        r0 = pl.multiple_of(c * tk, tk)
        vt_sc[0:dv, pl.ds(r0, tk)] = v_ref[pl.ds(r0, tk), :].astype(f32).T.astype(bf)
        return carry

    lax.fori_loop(0, nk, vt_body, 0)
    vt_sc[dv:dv + _ONES_ROWS, :] = jnp.ones((_ONES_ROWS, t), bf)

    lam = (jnp.exp(jnp.sum(lq1_ref[...] * lk1_ref[...], axis=-1, keepdims=True))
           - jnp.exp(jnp.sum(lq2_ref[...] * lk2_ref[...], axis=-1, keepdims=True)) + lam_init)
    lane = lax.broadcasted_iota(jnp.int32, (qs_, LANES), 1)
    kr = lax.broadcasted_iota(jnp.int32, (tk, qs_), 0)
    qc = lax.broadcasted_iota(jnp.int32, (tk, qs_), 1)
    cscale = ATT_QK_DIM ** -0.5 * _LOG2E

    def group(gi, carry):
        g0 = pl.multiple_of(gi * tq, tq)
        tiles = []
        for u in range(ng):
            q0 = g0 + u * qs_
            qs = (q_ref[pl.ds(q0, qs_), :].astype(f32) * cscale).astype(bf).astype(f32)
            qext = qext_ref[...]
            q_aug = jnp.concatenate([
                jnp.concatenate([jnp.where(lane < ATT_QK_DIM, qs, 0.0), qext], axis=1),
                jnp.concatenate([jnp.where(lane >= ATT_QK_DIM, qs, 0.0), qext], axis=1)], axis=0)
            tiles.append(q_aug.T.astype(bf))
        w = jnp.concatenate(tiles, axis=1)
        jc = g0 // tk

        def corr_tile(d):
            corr = 2.0 * beta * jnp.minimum((d + qc - kr).astype(f32), 0.0)
            return jnp.concatenate([corr, corr], axis=1)

        def shift(d, sidx):
            sgn = 1.0 if sidx == 0 else -1.0
            return (-sgn) * beta * d.astype(f32)

        def scores(n, sidx, crossing):
            r0 = pl.multiple_of(n * tk, tk)
            k_aug = jnp.concatenate([k_ref[pl.ds(r0, tk), :], kext_ref[sidx]], axis=1)
            for u in range(ng):
                c0 = u * 2 * qs_
                s = jnp.dot(k_aug, w[:, c0:c0 + 2 * qs_], preferred_element_type=f32)
                if crossing:
                    s = s + corr_tile(g0 + u * qs_ - r0)
                s_sc[:, c0:c0 + 2 * qs_] = s

        def step(n, mc, sidx, crossing):
            r0 = pl.multiple_of((n - 1) * tk, tk)
            r1 = pl.multiple_of(n * tk, tk)
            vt = vt_sc[:, pl.ds(r0, tk)]
            if sidx is not None:
                k_aug = jnp.concatenate([k_ref[pl.ds(r1, tk), :], kext_ref[sidx]], axis=1)
            m, csts = mc
            ms, cs = [], []
            for u in range(ng):
                c0 = u * 2 * qs_
                s = s_sc[:, c0:c0 + 2 * qs_]
                m_old = m[:, c0:c0 + 2 * qs_]
                cst = csts[:, c0:c0 + 2 * qs_]
                m_new = jnp.maximum(m_old, jnp.max(s, axis=0, keepdims=True) + cst)
                alpha = jnp.exp2(m_old - m_new)
                p = jnp.exp2(s - (m_new - cst)).astype(bf)
                acc_sc[:, c0:c0 + 2 * qs_] = (acc_sc[:, c0:c0 + 2 * qs_] * alpha
                                              + jnp.dot(vt, p, preferred_element_type=f32))
                ms.append(m_new)
                if sidx is not None:
                    s = jnp.dot(k_aug, w[:, c0:c0 + 2 * qs_], preferred_element_type=f32)
                    if crossing:
                        s = s + corr_tile(g0 + u * qs_ - r1)
                    s_sc[:, c0:c0 + 2 * qs_] = s
                    cs.append(jnp.broadcast_to(shift(g0 + u * qs_ - r1, sidx), (1, 2 * qs_)))
            return jnp.concatenate(ms, axis=1), (jnp.concatenate(cs, axis=1) if cs else csts)

        acc_sc[...] = jnp.zeros(acc_sc.shape, f32)
        m = jnp.full((1, ncol), -1e30, f32)
        scores(0, 0, True)
        n1 = jnp.maximum(jc, 1)
        n2 = jc + tq // tk
        cst0 = jnp.concatenate([jnp.broadcast_to(shift(g0 + u * qs_, 0), (1, 2 * qs_))
                                for u in range(ng)], axis=1)
        mc = (m, cst0)
        mc = lax.fori_loop(1, n1, functools.partial(step, sidx=0, crossing=False), mc)
        mc = lax.fori_loop(n1, n2, functools.partial(step, sidx=0, crossing=True), mc)
        mc = lax.fori_loop(n2, nk, functools.partial(step, sidx=1, crossing=False), mc)
        step(nk, mc, None, False)

        for u in range(ng):
            c0 = u * 2 * qs_
            acc = acc_sc[:, c0:c0 + 2 * qs_]
            den = acc[dv:dv + 1, :]
            o = acc[0:dv, 0:qs_] / den[:, 0:qs_] - lam * (acc[0:dv, qs_:] / den[:, qs_:])
            y = o * lax.rsqrt(jnp.mean(o * o, axis=0, keepdims=True) + EPS) * subg_ref[...]
            o_ref[pl.ds(g0 + u * qs_, qs_), :] = (y * (1.0 - lam_init)).T.astype(o_ref.dtype)
        return carry

    lax.fori_loop(0, t // tq, group, 0)


def _attn_tables(tk):
    f32 = jnp.float32
    slopes = 2.0 ** (-8.0 * (jnp.arange(ATT_HEADS, dtype=f32) + 1.0) / ATT_HEADS)
    beta = slopes * _LOG2E
    lane = jnp.arange(LANES)[None, None, :]
    b = _split3(beta)
    a = _split3(-beta[:, None] * jnp.arange(_Q_STRIP, dtype=f32)[None, :])
    qext = jnp.zeros((ATT_HEADS, _Q_STRIP, LANES), f32)
    for c in range(3):
        qext = jnp.where(lane == c, b[c][:, None, None], qext)
        qext = jnp.where(lane == 3 + c, a[c][:, :, None], qext)
    r = jnp.arange(tk, dtype=f32)[:, None]
    lane2 = jnp.arange(LANES)[None, :]
    base = jnp.where(lane2 < 3, r, jnp.where(lane2 < 6, 1.0, 0.0))
    kext = jnp.stack([base, -base]).astype(jnp.bfloat16)
    beta_t = jnp.broadcast_to(beta[:, None, None], (ATT_HEADS, 1, LANES))
    return beta_t, qext, kext


def diff_attention(proj3d, lq1, lk1, lq2, lk2, subg, layer, tk=256, tq=1024):
    b, t, _ = proj3d.shape
    tk, tq = min(tk, t), min(tq, t)
    assert tq % tk == 0 and tq % _Q_STRIP == 0
    lam_init = 0.8 - 0.6 * math.exp(-0.3 * layer)
    assert tk <= 256
    beta_t, qext, kext = _attn_tables(tk)
    vec = lambda a: a.reshape(1, -1).astype(jnp.float32)
    small = lambda n: pl.BlockSpec((1, n), lambda bi, h: (0, 0))
    head = pl.BlockSpec((None, 1, LANES), lambda bi, h: (h, 0, 0))
    col = lambda blk: pl.BlockSpec((None, t, LANES), lambda bi, h: (bi, 0, blk + h))
    return pl.pallas_call(
        functools.partial(_attn_kernel, lam_init=lam_init, tk=tk, tq=tq),
        grid=(b, ATT_HEADS),
        in_specs=[
            head, pl.BlockSpec((None, _Q_STRIP, LANES), lambda bi, h: (h, 0, 0)),
            small(ATT_QK_DIM), small(ATT_QK_DIM), small(ATT_QK_DIM), small(ATT_QK_DIM),
            pl.BlockSpec((ATT_V_DIM, 1), lambda bi, h: (0, 0)),
            col(_QBLK), col(_KBLK), col(_VBLK),
            pl.BlockSpec((2, tk, LANES), lambda bi, h: (0, 0, 0)),
        ],
        out_specs=pl.BlockSpec((None, t, LANES), lambda bi, h: (bi, 0, h)),
        out_shape=jax.ShapeDtypeStruct((b, t, ATT_HEADS * ATT_V_DIM), jnp.bfloat16),
        scratch_shapes=[pltpu.VMEM((ATT_V_DIM + _ONES_ROWS, t), jnp.bfloat16),
                        pltpu.VMEM((ATT_V_DIM + _ONES_ROWS, 2 * tq), jnp.float32),
                        pltpu.VMEM((tk, 2 * tq), jnp.float32)],
        compiler_params=_cparams(("parallel", "parallel")),
        name="diff_attn",
    )(beta_t, qext, vec(lq1), vec(lk1), vec(lq2), vec(lk2),
      subg.reshape(ATT_V_DIM, 1).astype(jnp.float32), proj3d, proj3d, proj3d, kext)


_HG_CHUNKS_PER_ITER = 2


def _scan_rows(g, rev):
    c = g.shape[0]
    row = lax.broadcasted_iota(jnp.int32, g.shape, 0)
    b = g
    s = 1
    while s < c:
        if rev:
            b = b + jnp.where(row < c - s, pltpu.roll(b, c - s, axis=0), 0.0)
        else:
            b = b + jnp.where(row >= s, pltpu.roll(b, s, axis=0), 0.0)
        s *= 2
    return b


def _level_ref(b, m, rev):
    c, w = b.shape
    r = m // 2 if rev else m // 2 - 1
    pieces = []
    if m >= SUBLANES:
        for j in range(c // m):
            pieces.append(jnp.broadcast_to(b[j * m + r:j * m + r + 1, :], (m, w)))
    else:
        sub = lax.broadcasted_iota(jnp.int32, (SUBLANES, w), 0)
        for j in range(c // SUBLANES):
            base = j * SUBLANES
            acc = jnp.broadcast_to(b[base + r:base + r + 1, :], (SUBLANES, w))
            for i in range(1, SUBLANES // m):
                cand = jnp.broadcast_to(b[base + i * m + r:base + i * m + r + 1, :], (SUBLANES, w))
                acc = jnp.where(sub >= i * m, cand, acc)
            pieces.append(acc)
    return jnp.concatenate(pieces, axis=0)


def _hgrn_masks(c, rev):
    row = lax.broadcasted_iota(jnp.int32, (c, LANES), 0)
    ti = lax.broadcasted_iota(jnp.int32, (c, c), 0)
    si = lax.broadcasted_iota(jnp.int32, (c, c), 1)
    qrows, pairs = {}, {}
    m = c
    while m >= 2:
        late = (row & (m - 1)) >= m // 2
        t_late = (ti & (m - 1)) >= m // 2
        s_late = (si & (m - 1)) >= m // 2
        same = (ti ^ si) < m
        if rev:
            qrows[m] = jnp.logical_not(late)
            pairs[m] = same & jnp.logical_not(t_late) & s_late
        else:
            qrows[m] = late
            pairs[m] = same & t_late & jnp.logical_not(s_late)
        m //= 2
    return qrows, pairs, ti == si


def _hgrn_chunk(q, f, v, st_ref, rev, masks):
    qrows, pairs, diag = masks
    c = q.shape[0]
    bf = jnp.bfloat16
    k_ = 1.0 - f
    b = _scan_rows(jnp.log(f), rev)
    a = jnp.where(diag,
                  lax.dot_general(q.astype(bf), k_.astype(bf), _NT,
                                  preferred_element_type=jnp.float32), 0.0)
    m = c
    while m >= 2:
        e = jnp.exp(-jnp.abs(b - _level_ref(b, m, rev)))
        z = (jnp.where(qrows[m], q, k_) * e).astype(bf)
        p = lax.dot_general(z, z, _NT, preferred_element_type=jnp.float32)
        a = a + jnp.where(pairs[m], p, 0.0)
        m //= 2
    st = st_ref[...]
    o = jnp.dot(a.astype(bf), v.astype(bf), preferred_element_type=jnp.float32)
    o = o + lax.dot_general((q * jnp.exp(b)).astype(bf), st.astype(bf), _NT,
                            preferred_element_type=jnp.float32)
    btot = b[0:1, :] if rev else b[c - 1:c, :]
    khat = (k_ * jnp.exp(btot - b)).astype(bf)
    st_ref[...] = st * jnp.exp(btot) + jnp.dot(v.T.astype(bf), khat,
                                               preferred_element_type=jnp.float32)
    return o


def _hgrn_kernel(lb_ref, ng_ref, q_ref, ff_ref, fb_ref, i_ref, g_ref, o_ref,
                 of_sc, ob_sc, stf_sc, stb_sc, *, layer, chunk):
    t = q_ref.shape[0]
    n = t // chunk
    lb_raw = lb_ref[...].astype(jnp.float32)
    ex = jnp.exp(lb_raw - jnp.max(lb_raw, axis=0, keepdims=True))
    sm = ex / jnp.sum(ex, axis=0, keepdims=True)
    lb = jnp.zeros_like(sm[0])
    for l_ in range(1, layer + 1):
        lb = lb + sm[l_]
    lb_f = lb[0:1, :]
    lb_b = lb[1:2, :]

    stf_sc[...] = jnp.zeros(stf_sc.shape, jnp.float32)
    stb_sc[...] = jnp.zeros(stb_sc.shape, jnp.float32)

    def load(ref, r0):
        return ref[pl.ds(r0, chunk), :].astype(jnp.float32)

    masks_f = _hgrn_masks(chunk, False)
    masks_b = _hgrn_masks(chunk, True)

    per_iter = _HG_CHUNKS_PER_ITER if n % _HG_CHUNKS_PER_ITER == 0 else 1

    def body(ci, carry):
        for u in range(per_iter):
            r0 = pl.multiple_of((ci * per_iter + u) * chunk, chunk)
            qr = load(q_ref, r0)
            f = lb_f + (1.0 - lb_f) * jax.nn.sigmoid(load(ff_ref, r0))
            of_sc[pl.ds(r0, chunk), :] = _hgrn_chunk(qr * jax.nn.sigmoid(qr), f,
                                                     load(i_ref, r0), stf_sc, False, masks_f)
            r1 = pl.multiple_of((n - 1 - ci * per_iter - u) * chunk, chunk)
            qr = load(q_ref, r1)
            f = lb_b + (1.0 - lb_b) * jax.nn.sigmoid(load(fb_ref, r1))
            ob_sc[pl.ds(r1, chunk), :] = _hgrn_chunk(qr * jax.nn.sigmoid(qr), f,
                                                     load(i_ref, r1), stb_sc, True, masks_b)
        return carry

    lax.fori_loop(0, n // per_iter, body, 0)

    def fin(ci, carry):
        r0 = pl.multiple_of(ci * chunk, chunk)
        o = of_sc[pl.ds(r0, chunk), :] + ob_sc[pl.ds(r0, chunk), :]
        y = o * lax.rsqrt(jnp.mean(o * o, axis=-1, keepdims=True) + EPS) * ng_ref[...]
        gr = load(g_ref, r0)
        o_ref[pl.ds(r0, chunk), :] = (y * (gr * jax.nn.sigmoid(gr))).astype(o_ref.dtype)
        return carry

    lax.fori_loop(0, n, fin, 0)


def hgrn_bidir(proj3d, hgrn_lb, norm_g, layer):
    b, t, _ = proj3d.shape
    depth = hgrn_lb.shape[0]
    col = lambda blk: pl.BlockSpec((None, t, LANES), lambda bi, h: (bi, 0, blk + h))
    return pl.pallas_call(
        functools.partial(_hgrn_kernel, layer=layer, chunk=HG_CHUNK),
        grid=(b, HG_HEADS),
        in_specs=[
            pl.BlockSpec((depth, 2, LANES), lambda bi, h: (0, 0, h)),
            pl.BlockSpec((1, HG_V), lambda bi, h: (0, 0)),
            col(_HQ), col(_HFF), col(_HFB), col(_HI), col(_HG),
        ],
        out_specs=pl.BlockSpec((None, t, LANES), lambda bi, h: (bi, 0, h)),
        out_shape=jax.ShapeDtypeStruct((b, t, HG_HEADS * HG_V), jnp.bfloat16),
        scratch_shapes=[
            pltpu.VMEM((t, HG_V), jnp.float32),
            pltpu.VMEM((t, HG_V), jnp.float32),
            pltpu.VMEM((HG_V, HG_K), jnp.float32),
            pltpu.VMEM((HG_V, HG_K), jnp.float32),
        ],
        compiler_params=_cparams(("parallel", "parallel")),
        name="hgrn_bidir",
    )(hgrn_lb, norm_g.reshape(1, HG_V).astype(jnp.float32),
      proj3d, proj3d, proj3d, proj3d, proj3d)


def _out_proj_kernel(a_ref, h_ref, wa_ref, wh_ref, x_ref, o_ref):
    y = jnp.dot(a_ref[...], wa_ref[...], preferred_element_type=jnp.float32)
    y = y + jnp.dot(h_ref[...], wh_ref[...], preferred_element_type=jnp.float32)
    o_ref[...] = x_ref[...] + y


def out_proj(att2d, hgo2d, w_out_bf16, x2d, tm=512):
    n, d = x2d.shape
    tm = min(tm, n)
    wa = att2d.shape[1]
    wh = hgo2d.shape[1]
    return pl.pallas_call(
        _out_proj_kernel,
        grid=(n // tm,),
        in_specs=[
            pl.BlockSpec((tm, wa), lambda i: (i, 0)),
            pl.BlockSpec((tm, wh), lambda i: (i, 0)),
            pl.BlockSpec((wa, d), lambda i: (0, 0)),
            pl.BlockSpec((wh, d), lambda i: (wa // wh, 0)),
            pl.BlockSpec((tm, d), lambda i: (i, 0)),
        ],
        out_specs=pl.BlockSpec((tm, d), lambda i: (i, 0)),
        out_shape=jax.ShapeDtypeStruct((n, d), jnp.float32),
        compiler_params=_cparams(("parallel",)),
        name="out_proj",
    )(att2d, hgo2d, w_out_bf16, w_out_bf16, x2d)


def _router_kernel(x_ref, g_ref, wrt_ref, h_ref, afft_ref):
    x = x_ref[...]
    h = x * lax.rsqrt(jnp.mean(x * x, axis=-1, keepdims=True) + EPS) * g_ref[...]
    h_ref[...] = h.astype(h_ref.dtype)
    logits = lax.dot_general(wrt_ref[...], h, _NT, preferred_element_type=jnp.float32,
                             precision=lax.Precision.HIGHEST)
    ex = jnp.exp(logits - jnp.max(logits, axis=0, keepdims=True))
    afft_ref[...] = ex / jnp.sum(ex, axis=0, keepdims=True)


def router(x2d, g, w_router, tm=512):
    n, d = x2d.shape
    tm = min(tm, n)
    e = w_router.shape[1]
    return pl.pallas_call(
        _router_kernel,
        grid=(n // tm,),
        in_specs=[
            pl.BlockSpec((tm, d), lambda i: (i, 0)),
            pl.BlockSpec((1, d), lambda i: (0, 0)),
            pl.BlockSpec((e, d), lambda i: (0, 0)),
        ],
        out_specs=[
            pl.BlockSpec((tm, d), lambda i: (i, 0)),
            pl.BlockSpec((e, tm), lambda i: (0, i)),
        ],
        out_shape=[
            jax.ShapeDtypeStruct((n, d), jnp.bfloat16),
            jax.ShapeDtypeStruct((e, n), jnp.float32),
        ],
        compiler_params=_cparams(("parallel",)),
        name="router",
    )(x2d, g.reshape(1, d), w_router.T)


_SEL_CHUNK = 1024


def _select_kernel(afft_ref, pos_ref, offs_ref, *, cap):
    f32, i32 = jnp.float32, jnp.int32
    e, n = afft_ref.shape
    chunk = min(_SEL_CHUNK, n)
    nt = n // LANES

    def bits_at(start, size):
        return lax.bitcast_convert_type(afft_ref[:, pl.ds(start, size)], i32)

    def count_ge(cand):
        def body(c, acc):
            x = bits_at(pl.multiple_of(c * chunk, chunk), chunk)
            return acc + jnp.where(x >= cand, 1.0, 0.0)
        acc = lax.fori_loop(0, n // chunk, body, jnp.zeros((e, chunk), f32))
        return jnp.sum(acc, axis=1, keepdims=True)

    def bit_body(it, prefix):
        cand = prefix | jnp.left_shift(jnp.int32(1), 30 - it)
        return jnp.where(count_ge(cand) >= cap, cand, prefix)

    thr = lax.fori_loop(0, 31, bit_body, jnp.zeros((e, 1), i32))
    need = cap - count_ge(thr + 1)

    tri = (lax.broadcasted_iota(i32, (LANES, LANES), 0)
           <= lax.broadcasted_iota(i32, (LANES, LANES), 1)).astype(jnp.bfloat16)
    tile_id = lax.broadcasted_iota(i32, (e, nt), 1)

    def tile_body(j, carry):
        c_gt, c_eq, offs_acc = carry
        x = bits_at(pl.multiple_of(j * LANES, LANES), LANES)
        gt = x > thr
        eq = x == thr
        both = jnp.concatenate([jnp.where(gt, 1.0, 0.0), jnp.where(eq, 1.0, 0.0)],
                               axis=0).astype(jnp.bfloat16)
        inc = jnp.dot(both, tri, preferred_element_type=f32)
        inc_gt = inc[:e] + c_gt
        inc_eq = inc[e:] + c_eq
        sel = gt | (eq & (inc_eq <= need))
        incl = inc_gt + jnp.minimum(inc_eq, need)
        pos_ref[:, pl.ds(pl.multiple_of(j * LANES, LANES), LANES)] = jnp.where(
            sel, incl - 1.0, -1.0).astype(i32)
        offs_acc = jnp.where(tile_id == j, c_gt + jnp.minimum(c_eq, need), offs_acc)
        return inc_gt[:, LANES - 1:LANES], inc_eq[:, LANES - 1:LANES], offs_acc

    zero = jnp.zeros((e, 1), f32)
    _, _, offs_acc = lax.fori_loop(0, nt, tile_body, (zero, zero, jnp.zeros((e, nt), f32)))
    offs_ref[...] = offs_acc.astype(i32)


def ec_select(afft, cap):
    e, n = afft.shape
    return pl.pallas_call(
        functools.partial(_select_kernel, cap=cap),
        out_shape=[jax.ShapeDtypeStruct((e, n), jnp.int32),
                   jax.ShapeDtypeStruct((e, n // LANES), jnp.int32)],
        compiler_params=pltpu.CompilerParams(vmem_limit_bytes=_VMEM_LIMIT),
        name="ec_select",
    )(afft)


def _expert_kernel(xs_ref, wg_ref, wu_ref, wd_ref, o_ref):
    @pl.when(pl.program_id(1) == pl.num_programs(1) - 1)
    def _():
        o_ref[...] = jnp.zeros(o_ref.shape, o_ref.dtype)

    @pl.when(pl.program_id(1) < pl.num_programs(1) - 1)
    def _():
        xs = xs_ref[...]
        a = jnp.dot(xs, wg_ref[...], preferred_element_type=jnp.float32)
        u = jnp.dot(xs, wu_ref[...], preferred_element_type=jnp.float32)
        hid = (a * jax.nn.sigmoid(a) * u).astype(jnp.bfloat16)
        o_ref[...] = jnp.dot(hid, wd_ref[...],
                             preferred_element_type=jnp.float32).astype(o_ref.dtype)


def expert_ffn(xs, wg, wu, wd, tm=512):
    e, c, d = xs.shape
    tm = min(tm, c)
    f = wg.shape[2]
    nt = c // tm
    return pl.pallas_call(
        _expert_kernel,
        grid=(e, nt + 1),
        in_specs=[
            pl.BlockSpec((None, tm, d), lambda ei, i: (ei, jnp.minimum(i, nt - 1), 0)),
            pl.BlockSpec((None, d, f), lambda ei, i: (ei, 0, 0)),
            pl.BlockSpec((None, d, f), lambda ei, i: (ei, 0, 0)),
            pl.BlockSpec((None, f, d), lambda ei, i: (ei, 0, 0)),
        ],
        out_specs=pl.BlockSpec((None, tm, d), lambda ei, i: (ei, i, 0)),
        out_shape=jax.ShapeDtypeStruct((e, c + tm, d), jnp.bfloat16),
        compiler_params=_cparams(("parallel", "arbitrary")),
        name="expert_ffn",
    )(xs, wg, wu, wd)


def _combine_kernel(offs_ref, x_ref, pos_ref, afft_ref, *rest):
    f32 = jnp.float32
    ne = pos_ref.shape[0]
    tiles, o_ref = rest[:2 * ne], rest[2 * ne]
    j = pl.program_id(0)
    pad = jnp.zeros((LANES - ne, LANES), f32)
    pos_t = jnp.concatenate([pos_ref[...].astype(f32), pad], axis=0).T
    gate_t = jnp.concatenate([afft_ref[...], pad], axis=0).T
    row_id = lax.broadcasted_iota(jnp.int32, (LANES, 2 * LANES), 1).astype(f32)
    y = x_ref[...]
    for e in range(ne):
        base = (offs_ref[e, j] // LANES) * LANES
        rel = pos_t[:, e:e + 1] - base.astype(f32)
        onehot = jnp.where(rel == row_id, 1.0, 0.0).astype(jnp.bfloat16)
        rows = jnp.concatenate([tiles[2 * e][...], tiles[2 * e + 1][...]], axis=0)
        y = y + jnp.dot(onehot, rows, preferred_element_type=f32) * gate_t[:, e:e + 1]
    o_ref[...] = y


def ec_combine(x2d, pos, afft, offs, out):
    n, d = x2d.shape
    ne, crows, _ = out.shape
    nblk = crows // LANES

    def tile_spec(e, k):
        return pl.BlockSpec(
            (None, LANES, d),
            lambda j, offs_ref: (e, jnp.minimum(offs_ref[e, j] // LANES + k, nblk - 1), 0))

    grid_spec = pltpu.PrefetchScalarGridSpec(
        num_scalar_prefetch=1,
        grid=(n // LANES,),
        in_specs=[
            pl.BlockSpec((LANES, d), lambda j, offs_ref: (j, 0)),
            pl.BlockSpec((ne, LANES), lambda j, offs_ref: (0, j)),
            pl.BlockSpec((ne, LANES), lambda j, offs_ref: (0, j)),
        ] + [tile_spec(e, k) for e in range(ne) for k in range(2)],
        out_specs=pl.BlockSpec((LANES, d), lambda j, offs_ref: (j, 0)),
    )
    return pl.pallas_call(
        _combine_kernel,
        grid_spec=grid_spec,
        out_shape=jax.ShapeDtypeStruct((n, d), jnp.float32),
        compiler_params=_cparams(("arbitrary",)),
        name="ec_combine",
    )(offs, x2d, pos, afft, *([out] * (2 * ne)))


def _rms_kernel(x_ref, g_ref, o_ref):
    x = x_ref[...]
    o_ref[...] = x * lax.rsqrt(jnp.mean(x * x, axis=-1, keepdims=True) + EPS) * g_ref[...]


def rms_norm_call(x2d, g, tm=1024):
    n, d = x2d.shape
    tm = min(tm, n)
    return pl.pallas_call(
        _rms_kernel,
        grid=(n // tm,),
        in_specs=[pl.BlockSpec((tm, d), lambda i: (i, 0)),
                  pl.BlockSpec((1, d), lambda i: (0, 0))],
        out_specs=pl.BlockSpec((tm, d), lambda i: (i, 0)),
        out_shape=jax.ShapeDtypeStruct((n, d), jnp.float32),
        compiler_params=_cparams(("parallel",)),
        name="final_norm",
    )(x2d, g.reshape(1, d))


def _trunk(x, p):
    b, t, d = x.shape
    n = b * t
    depth = p["w_in"].shape[0]
    x2d = x.reshape(n, d)
    cap = max(1, EC_CAPACITY * n // N_EXPERTS)
    for layer in range(depth):
        proj = norm_proj(x2d, p["norm_mix_g"][layer], p["w_in"][layer])
        proj3d = proj.reshape(b, t, -1)
        att = diff_attention(proj3d, p["lambda_q1"][layer], p["lambda_k1"][layer],
                             p["lambda_q2"][layer], p["lambda_k2"][layer],
                             p["diff_subln_g"][layer], layer)
        hgo = hgrn_bidir(proj3d, p["hgrn_lb"], p["hgrn_norm_g"][layer], layer)
        x2d = out_proj(att.reshape(n, -1), hgo.reshape(n, -1), p["w_out"][layer], x2d)
        h, afft = router(x2d, p["norm_ffn_g"][layer], p["w_router"][layer])
        pos, offs = ec_select(afft, cap)
        ne = afft.shape[0]
        idx = jnp.zeros((ne, cap), jnp.int32).at[
            jnp.arange(ne)[:, None], jnp.where(pos >= 0, pos, cap)].set(
                jnp.broadcast_to(jnp.arange(n, dtype=jnp.int32)[None, :], pos.shape), mode="drop")
        out = expert_ffn(h[idx], p["w_gate"][layer], p["w_up"][layer], p["w_down"][layer])
        x2d = ec_combine(x2d, pos, afft, offs, out)
    return rms_norm_call(x2d, p["final_norm_g"]).reshape(b, t, d)


def kernel(x_prompt, x_sample, norm_mix_g, w_in, lambda_q1, lambda_k1, lambda_q2, lambda_k2,
           diff_subln_g, hgrn_lb, hgrn_norm_g, w_out, norm_ffn_g, w_router, w_gate, w_up,
           w_down, final_norm_g):
    bf = jnp.bfloat16
    p = dict(norm_mix_g=norm_mix_g, w_in=w_in.astype(bf), lambda_q1=lambda_q1,
             lambda_k1=lambda_k1, lambda_q2=lambda_q2, lambda_k2=lambda_k2,
             diff_subln_g=diff_subln_g, hgrn_lb=hgrn_lb, hgrn_norm_g=hgrn_norm_g,
             w_out=w_out.astype(bf), norm_ffn_g=norm_ffn_g, w_router=w_router,
             w_gate=w_gate.astype(bf), w_up=w_up.astype(bf), w_down=w_down.astype(bf),
             final_norm_g=final_norm_g)
    return (_trunk(x_prompt, p), _trunk(x_sample, p))
```

```python
import functools
import math

import jax
import jax.numpy as jnp
from jax import lax
from jax.experimental import pallas as pl
from jax.experimental.pallas import tpu as pltpu
from jax.experimental.pallas import tpu_sc as plsc

ATT_HEADS = 4
ATT_QK_DIM = 64
ATT_V_DIM = 128
HG_HEADS = 4
HG_K = 128
HG_V = 128
HG_CHUNK = 64
N_EXPERTS = 16
EC_CAPACITY = 2
EPS = 1e-6
LANES = 128
SUBLANES = 8

_QBLK, _KBLK, _VBLK = 0, 4, 8
_HQ, _HFF, _HFB, _HI, _HG = 12, 16, 20, 24, 28

_VMEM_LIMIT = 56 * 1024 * 1024

_NT = (((1,), (1,)), ((), ()))


def _cparams(sem):
    return pltpu.CompilerParams(dimension_semantics=sem, vmem_limit_bytes=_VMEM_LIMIT)


def _norm_proj_kernel(x_ref, g_ref, w_ref, o_ref, *, n_split):
    x = x_ref[...]
    y = x * lax.rsqrt(jnp.mean(x * x, axis=-1, keepdims=True) + EPS) * g_ref[...]
    h = y.astype(jnp.bfloat16)
    wn = w_ref.shape[1] // n_split
    for c in range(n_split):
        o_ref[:, c * wn:(c + 1) * wn] = jnp.dot(
            h, w_ref[:, c * wn:(c + 1) * wn], preferred_element_type=jnp.float32
        ).astype(o_ref.dtype)


def norm_proj(x2d, g, w_bf16, tm=512):
    n, d = x2d.shape
    tm = min(tm, n)
    dout = w_bf16.shape[1]
    return pl.pallas_call(
        functools.partial(_norm_proj_kernel, n_split=max(1, dout // 1024)),
        grid=(n // tm,),
        in_specs=[
            pl.BlockSpec((tm, d), lambda i: (i, 0)),
            pl.BlockSpec((1, d), lambda i: (0, 0)),
            pl.BlockSpec((d, dout), lambda i: (0, 0)),
        ],
        out_specs=pl.BlockSpec((tm, dout), lambda i: (i, 0)),
        out_shape=jax.ShapeDtypeStruct((n, dout), jnp.bfloat16),
        compiler_params=_cparams(("parallel",)),
        name="norm_proj",
    )(x2d, g.reshape(1, d), w_bf16)


_LOG2E = 1.4426950408889634
_Q_STRIP = 128
_ONES_ROWS = 16


def _split3(x):
    bf = jnp.bfloat16
    h = x.astype(bf).astype(jnp.float32)
    r = x - h
    m = r.astype(bf).astype(jnp.float32)
    lo = (r - m).astype(bf).astype(jnp.float32)
    return h, m, lo


def _attn_kernel(beta_ref, qext_ref, lq1_ref, lk1_ref, lq2_ref, lk2_ref, subg_ref,
                 q_ref, k_ref, v_ref, kext_ref, o_ref, vt_sc, acc_sc, s_sc, *, lam_init, tk, tq):
    f32, bf = jnp.float32, jnp.bfloat16
    t = q_ref.shape[0]
    nk = t // tk
    qs_ = _Q_STRIP
    ng = tq // qs_
    ncol = 2 * tq
    dv = v_ref.shape[1]
    beta = beta_ref[0:1, 0:1]

    def vt_body(c, carry):
        r0 = pl.multiple_of(c * tk, tk)
        vt_sc[0:dv, pl.ds(r0, tk)] = v_ref[pl.ds(r0, tk), :].astype(f32).T.astype(bf)
        return carry

    lax.fori_loop(0, nk, vt_body, 0)
    vt_sc[dv:dv + _ONES_ROWS, :] = jnp.ones((_ONES_ROWS, t), bf)

    lam = (jnp.exp(jnp.sum(lq1_ref[...] * lk1_ref[...], axis=-1, keepdims=True))
           - jnp.exp(jnp.sum(lq2_ref[...] * lk2_ref[...], axis=-1, keepdims=True)) + lam_init)
    lane = lax.broadcasted_iota(jnp.int32, (qs_, LANES), 1)
    kr = lax.broadcasted_iota(jnp.int32, (tk, qs_), 0)
    qc = lax.broadcasted_iota(jnp.int32, (tk, qs_), 1)
    cscale = ATT_QK_DIM ** -0.5 * _LOG2E

    def group(gi, carry):
        g0 = pl.multiple_of(gi * tq, tq)
        tiles = []
        for u in range(ng):
            q0 = g0 + u * qs_
            qs = (q_ref[pl.ds(q0, qs_), :].astype(f32) * cscale).astype(bf).astype(f32)
            qext = qext_ref[...]
            q_aug = jnp.concatenate([
                jnp.concatenate([jnp.where(lane < ATT_QK_DIM, qs, 0.0), qext], axis=1),
                jnp.concatenate([jnp.where(lane >= ATT_QK_DIM, qs, 0.0), qext], axis=1)], axis=0)
            tiles.append(q_aug.T.astype(bf))
        w = jnp.concatenate(tiles, axis=1)
        jc = g0 // tk

        def corr_tile(d):
            corr = 2.0 * beta * jnp.minimum((d + qc - kr).astype(f32), 0.0)
            return jnp.concatenate([corr, corr], axis=1)

        def shift(d, sidx):
            sgn = 1.0 if sidx == 0 else -1.0
            return (-sgn) * beta * d.astype(f32)

        def scores(n, sidx, crossing):
            r0 = pl.multiple_of(n * tk, tk)
            k_aug = jnp.concatenate([k_ref[pl.ds(r0, tk), :], kext_ref[sidx]], axis=1)
            for u in range(ng):
                c0 = u * 2 * qs_
                s = jnp.dot(k_aug, w[:, c0:c0 + 2 * qs_], preferred_element_type=f32)
                if crossing:
                    s = s + corr_tile(g0 + u * qs_ - r0)
                s_sc[:, c0:c0 + 2 * qs_] = s

        def step(n, mc, sidx, crossing):
            r0 = pl.multiple_of((n - 1) * tk, tk)
            r1 = pl.multiple_of(n * tk, tk)
            vt = vt_sc[:, pl.ds(r0, tk)]
            if sidx is not None:
                k_aug = jnp.concatenate([k_ref[pl.ds(r1, tk), :], kext_ref[sidx]], axis=1)
            m, csts = mc
            ms, cs = [], []
            for u in range(ng):
                c0 = u * 2 * qs_
                s = s_sc[:, c0:c0 + 2 * qs_]
                m_old = m[:, c0:c0 + 2 * qs_]
                cst = csts[:, c0:c0 + 2 * qs_]
                m_new = jnp.maximum(m_old, jnp.max(s, axis=0, keepdims=True) + cst)
                alpha = jnp.exp2(m_old - m_new)
                p = jnp.exp2(s - (m_new - cst)).astype(bf)
                acc_sc[:, c0:c0 + 2 * qs_] = (acc_sc[:, c0:c0 + 2 * qs_] * alpha
                                              + jnp.dot(vt, p, preferred_element_type=f32))
                ms.append(m_new)
                if sidx is not None:
                    s = jnp.dot(k_aug, w[:, c0:c0 + 2 * qs_], preferred_element_type=f32)
                    if crossing:
                        s = s + corr_tile(g0 + u * qs_ - r1)
                    s_sc[:, c0:c0 + 2 * qs_] = s
                    cs.append(jnp.broadcast_to(shift(g0 + u * qs_ - r1, sidx), (1, 2 * qs_)))
            return jnp.concatenate(ms, axis=1), (jnp.concatenate(cs, axis=1) if cs else csts)

        acc_sc[...] = jnp.zeros(acc_sc.shape, f32)
        m = jnp.full((1, ncol), -1e30, f32)
        scores(0, 0, True)
        n1 = jnp.maximum(jc, 1)
        n2 = jc + tq // tk
        cst0 = jnp.concatenate([jnp.broadcast_to(shift(g0 + u * qs_, 0), (1, 2 * qs_))
                                for u in range(ng)], axis=1)
        mc = (m, cst0)
        mc = lax.fori_loop(1, n1, functools.partial(step, sidx=0, crossing=False), mc)
        mc = lax.fori_loop(n1, n2, functools.partial(step, sidx=0, crossing=True), mc)
        mc = lax.fori_loop(n2, nk, functools.partial(step, sidx=1, crossing=False), mc)
        step(nk, mc, None, False)

        for u in range(ng):
            c0 = u * 2 * qs_
            acc = acc_sc[:, c0:c0 + 2 * qs_]
            den = acc[dv:dv + 1, :]
            o = acc[0:dv, 0:qs_] / den[:, 0:qs_] - lam * (acc[0:dv, qs_:] / den[:, qs_:])
            y = o * lax.rsqrt(jnp.mean(o * o, axis=0, keepdims=True) + EPS) * subg_ref[...]
            o_ref[pl.ds(g0 + u * qs_, qs_), :] = (y * (1.0 - lam_init)).T.astype(o_ref.dtype)
        return carry

    lax.fori_loop(0, t // tq, group, 0)


def _attn_tables(tk):
    f32 = jnp.float32
    slopes = 2.0 ** (-8.0 * (jnp.arange(ATT_HEADS, dtype=f32) + 1.0) / ATT_HEADS)
    beta = slopes * _LOG2E
    lane = jnp.arange(LANES)[None, None, :]
    b = _split3(beta)
    a = _split3(-beta[:, None] * jnp.arange(_Q_STRIP, dtype=f32)[None, :])
    qext = jnp.zeros((ATT_HEADS, _Q_STRIP, LANES), f32)
    for c in range(3):
        qext = jnp.where(lane == c, b[c][:, None, None], qext)
        qext = jnp.where(lane == 3 + c, a[c][:, :, None], qext)
    r = jnp.arange(tk, dtype=f32)[:, None]
    lane2 = jnp.arange(LANES)[None, :]
    base = jnp.where(lane2 < 3, r, jnp.where(lane2 < 6, 1.0, 0.0))
    kext = jnp.stack([base, -base]).astype(jnp.bfloat16)
    beta_t = jnp.broadcast_to(beta[:, None, None], (ATT_HEADS, 1, LANES))
    return beta_t, qext, kext


def diff_attention(proj3d, lq1, lk1, lq2, lk2, subg, layer, tk=256, tq=1024):
    b, t, _ = proj3d.shape
    tk, tq = min(tk, t), min(tq, t)
    assert tq % tk == 0 and tq % _Q_STRIP == 0
    lam_init = 0.8 - 0.6 * math.exp(-0.3 * layer)
    assert tk <= 256
    beta_t, qext, kext = _attn_tables(tk)
    vec = lambda a: a.reshape(1, -1).astype(jnp.float32)
    small = lambda n: pl.BlockSpec((1, n), lambda bi, h: (0, 0))
    head = pl.BlockSpec((None, 1, LANES), lambda bi, h: (h, 0, 0))
    col = lambda blk: pl.BlockSpec((None, t, LANES), lambda bi, h: (bi, 0, blk + h))
    return pl.pallas_call(
        functools.partial(_attn_kernel, lam_init=lam_init, tk=tk, tq=tq),
        grid=(b, ATT_HEADS),
        in_specs=[
            head, pl.BlockSpec((None, _Q_STRIP, LANES), lambda bi, h: (h, 0, 0)),
            small(ATT_QK_DIM), small(ATT_QK_DIM), small(ATT_QK_DIM), small(ATT_QK_DIM),
            pl.BlockSpec((ATT_V_DIM, 1), lambda bi, h: (0, 0)),
            col(_QBLK), col(_KBLK), col(_VBLK),
            pl.BlockSpec((2, tk, LANES), lambda bi, h: (0, 0, 0)),
        ],
        out_specs=pl.BlockSpec((None, t, LANES), lambda bi, h: (bi, 0, h)),
        out_shape=jax.ShapeDtypeStruct((b, t, ATT_HEADS * ATT_V_DIM), jnp.bfloat16),
        scratch_shapes=[pltpu.VMEM((ATT_V_DIM + _ONES_ROWS, t), jnp.bfloat16),
                        pltpu.VMEM((ATT_V_DIM + _ONES_ROWS, 2 * tq), jnp.float32),
                        pltpu.VMEM((tk, 2 * tq), jnp.float32)],
        compiler_params=_cparams(("parallel", "parallel")),
        name="diff_attn",
    )(beta_t, qext, vec(lq1), vec(lk1), vec(lq2), vec(lk2),
      subg.reshape(ATT_V_DIM, 1).astype(jnp.float32), proj3d, proj3d, proj3d, kext)


_HG_CHUNKS_PER_ITER = 2


def _scan_rows(g, rev):
    c = g.shape[0]
    row = lax.broadcasted_iota(jnp.int32, g.shape, 0)
    b = g
    s = 1
    while s < c:
        if rev:
            b = b + jnp.where(row < c - s, pltpu.roll(b, c - s, axis=0), 0.0)
        else:
            b = b + jnp.where(row >= s, pltpu.roll(b, s, axis=0), 0.0)
        s *= 2
    return b


def _level_ref(b, m, rev):
    c, w = b.shape
    r = m // 2 if rev else m // 2 - 1
    pieces = []
    if m >= SUBLANES:
        for j in range(c // m):
            pieces.append(jnp.broadcast_to(b[j * m + r:j * m + r + 1, :], (m, w)))
    else:
        sub = lax.broadcasted_iota(jnp.int32, (SUBLANES, w), 0)
        for j in range(c // SUBLANES):
            base = j * SUBLANES
            acc = jnp.broadcast_to(b[base + r:base + r + 1, :], (SUBLANES, w))
            for i in range(1, SUBLANES // m):
                cand = jnp.broadcast_to(b[base + i * m + r:base + i * m + r + 1, :], (SUBLANES, w))
                acc = jnp.where(sub >= i * m, cand, acc)
            pieces.append(acc)
    return jnp.concatenate(pieces, axis=0)


def _hgrn_masks(c, rev):
    row = lax.broadcasted_iota(jnp.int32, (c, LANES), 0)
    ti = lax.broadcasted_iota(jnp.int32, (c, c), 0)
    si = lax.broadcasted_iota(jnp.int32, (c, c), 1)
    qrows, pairs = {}, {}
    m = c
    while m >= 2:
        late = (row & (m - 1)) >= m // 2
        t_late = (ti & (m - 1)) >= m // 2
        s_late = (si & (m - 1)) >= m // 2
        same = (ti ^ si) < m
        if rev:
            qrows[m] = jnp.logical_not(late)
            pairs[m] = same & jnp.logical_not(t_late) & s_late
        else:
            qrows[m] = late
            pairs[m] = same & t_late & jnp.logical_not(s_late)
        m //= 2
    return qrows, pairs, ti == si


def _hgrn_chunk(q, f, v, st_ref, rev, masks):
    qrows, pairs, diag = masks
    c = q.shape[0]
    bf = jnp.bfloat16
    k_ = 1.0 - f
    b = _scan_rows(jnp.log(f), rev)
    a = jnp.where(diag,
                  lax.dot_general(q.astype(bf), k_.astype(bf), _NT,
                                  preferred_element_type=jnp.float32), 0.0)
    m = c
    while m >= 2:
        e = jnp.exp(-jnp.abs(b - _level_ref(b, m, rev)))
        z = (jnp.where(qrows[m], q, k_) * e).astype(bf)
        p = lax.dot_general(z, z, _NT, preferred_element_type=jnp.float32)
        a = a + jnp.where(pairs[m], p, 0.0)
        m //= 2
    st = st_ref[...]
    o = jnp.dot(a.astype(bf), v.astype(bf), preferred_element_type=jnp.float32)
    o = o + lax.dot_general((q * jnp.exp(b)).astype(bf), st.astype(bf), _NT,
                            preferred_element_type=jnp.float32)
    btot = b[0:1, :] if rev else b[c - 1:c, :]
    khat = (k_ * jnp.exp(btot - b)).astype(bf)
    st_ref[...] = st * jnp.exp(btot) + jnp.dot(v.T.astype(bf), khat,
                                               preferred_element_type=jnp.float32)
    return o


def _hgrn_kernel(lb_ref, ng_ref, q_ref, ff_ref, fb_ref, i_ref, g_ref, o_ref,
                 of_sc, ob_sc, stf_sc, stb_sc, *, layer, chunk):
    t = q_ref.shape[0]
    n = t // chunk
    lb_raw = lb_ref[...].astype(jnp.float32)
    ex = jnp.exp(lb_raw - jnp.max(lb_raw, axis=0, keepdims=True))
    sm = ex / jnp.sum(ex, axis=0, keepdims=True)
    lb = jnp.zeros_like(sm[0])
    for l_ in range(1, layer + 1):
        lb = lb + sm[l_]
    lb_f = lb[0:1, :]
    lb_b = lb[1:2, :]

    stf_sc[...] = jnp.zeros(stf_sc.shape, jnp.float32)
    stb_sc[...] = jnp.zeros(stb_sc.shape, jnp.float32)

    def load(ref, r0):
        return ref[pl.ds(r0, chunk), :].astype(jnp.float32)

    masks_f = _hgrn_masks(chunk, False)
    masks_b = _hgrn_masks(chunk, True)

    per_iter = _HG_CHUNKS_PER_ITER if n % _HG_CHUNKS_PER_ITER == 0 else 1

    def body(ci, carry):
        for u in range(per_iter):
            r0 = pl.multiple_of((ci * per_iter + u) * chunk, chunk)
            qr = load(q_ref, r0)
            f = lb_f + (1.0 - lb_f) * jax.nn.sigmoid(load(ff_ref, r0))
            of_sc[pl.ds(r0, chunk), :] = _hgrn_chunk(qr * jax.nn.sigmoid(qr), f,
                                                     load(i_ref, r0), stf_sc, False, masks_f)
            r1 = pl.multiple_of((n - 1 - ci * per_iter - u) * chunk, chunk)
            qr = load(q_ref, r1)
            f = lb_b + (1.0 - lb_b) * jax.nn.sigmoid(load(fb_ref, r1))
            ob_sc[pl.ds(r1, chunk), :] = _hgrn_chunk(qr * jax.nn.sigmoid(qr), f,
                                                     load(i_ref, r1), stb_sc, True, masks_b)
        return carry

    lax.fori_loop(0, n // per_iter, body, 0)

    def fin(ci, carry):
        r0 = pl.multiple_of(ci * chunk, chunk)
        o = of_sc[pl.ds(r0, chunk), :] + ob_sc[pl.ds(r0, chunk), :]
        y = o * lax.rsqrt(jnp.mean(o * o, axis=-1, keepdims=True) + EPS) * ng_ref[...]
        gr = load(g_ref, r0)
        o_ref[pl.ds(r0, chunk), :] = (y * (gr * jax.nn.sigmoid(gr))).astype(o_ref.dtype)
        return carry

    lax.fori_loop(0, n, fin, 0)


def hgrn_bidir(proj3d, hgrn_lb, norm_g, layer):
    b, t, _ = proj3d.shape
    depth = hgrn_lb.shape[0]
    col = lambda blk: pl.BlockSpec((None, t, LANES), lambda bi, h: (bi, 0, blk + h))
    return pl.pallas_call(
        functools.partial(_hgrn_kernel, layer=layer, chunk=HG_CHUNK),
        grid=(b, HG_HEADS),
        in_specs=[
            pl.BlockSpec((depth, 2, LANES), lambda bi, h: (0, 0, h)),
            pl.BlockSpec((1, HG_V), lambda bi, h: (0, 0)),
            col(_HQ), col(_HFF), col(_HFB), col(_HI), col(_HG),
        ],
        out_specs=pl.BlockSpec((None, t, LANES), lambda bi, h: (bi, 0, h)),
        out_shape=jax.ShapeDtypeStruct((b, t, HG_HEADS * HG_V), jnp.bfloat16),
        scratch_shapes=[
            pltpu.VMEM((t, HG_V), jnp.float32),
            pltpu.VMEM((t, HG_V), jnp.float32),
            pltpu.VMEM((HG_V, HG_K), jnp.float32),
            pltpu.VMEM((HG_V, HG_K), jnp.float32),
        ],
        compiler_params=_cparams(("parallel", "parallel")),
        name="hgrn_bidir",
    )(hgrn_lb, norm_g.reshape(1, HG_V).astype(jnp.float32),
      proj3d, proj3d, proj3d, proj3d, proj3d)


def _out_proj_kernel(a_ref, h_ref, wa_ref, wh_ref, x_ref, o_ref):
    y = jnp.dot(a_ref[...], wa_ref[...], preferred_element_type=jnp.float32)
    y = y + jnp.dot(h_ref[...], wh_ref[...], preferred_element_type=jnp.float32)
    o_ref[...] = x_ref[...] + y


def out_proj(att2d, hgo2d, w_out_bf16, x2d, tm=512):
    n, d = x2d.shape
    tm = min(tm, n)
    wa = att2d.shape[1]
    wh = hgo2d.shape[1]
    return pl.pallas_call(
        _out_proj_kernel,
        grid=(n // tm,),
        in_specs=[
            pl.BlockSpec((tm, wa), lambda i: (i, 0)),
            pl.BlockSpec((tm, wh), lambda i: (i, 0)),
            pl.BlockSpec((wa, d), lambda i: (0, 0)),
            pl.BlockSpec((wh, d), lambda i: (wa // wh, 0)),
            pl.BlockSpec((tm, d), lambda i: (i, 0)),
        ],
        out_specs=pl.BlockSpec((tm, d), lambda i: (i, 0)),
        out_shape=jax.ShapeDtypeStruct((n, d), jnp.float32),
        compiler_params=_cparams(("parallel",)),
        name="out_proj",
    )(att2d, hgo2d, w_out_bf16, w_out_bf16, x2d)


def _router_kernel(x_ref, g_ref, wrt_ref, h_ref, afft_ref):
    x = x_ref[...]
    h = x * lax.rsqrt(jnp.mean(x * x, axis=-1, keepdims=True) + EPS) * g_ref[...]
    h_ref[...] = h.astype(h_ref.dtype)
    logits = lax.dot_general(wrt_ref[...], h, _NT, preferred_element_type=jnp.float32,
                             precision=lax.Precision.HIGHEST)
    ex = jnp.exp(logits - jnp.max(logits, axis=0, keepdims=True))
    afft_ref[...] = ex / jnp.sum(ex, axis=0, keepdims=True)


def router(x2d, g, w_router, tm=512):
    n, d = x2d.shape
    tm = min(tm, n)
    e = w_router.shape[1]
    return pl.pallas_call(
        _router_kernel,
        grid=(n // tm,),
        in_specs=[
            pl.BlockSpec((tm, d), lambda i: (i, 0)),
            pl.BlockSpec((1, d), lambda i: (0, 0)),
            pl.BlockSpec((e, d), lambda i: (0, 0)),
        ],
        out_specs=[
            pl.BlockSpec((tm, d), lambda i: (i, 0)),
            pl.BlockSpec((e, tm), lambda i: (0, i)),
        ],
        out_shape=[
            jax.ShapeDtypeStruct((n, d), jnp.bfloat16),
            jax.ShapeDtypeStruct((e, n), jnp.float32),
        ],
        compiler_params=_cparams(("parallel",)),
        name="router",
    )(x2d, g.reshape(1, d), w_router.T)


_SEL_CHUNK = 1024


def _select_kernel(afft_ref, pos_ref, offs_ref, *, cap):
    f32, i32 = jnp.float32, jnp.int32
    e, n = afft_ref.shape
    chunk = min(_SEL_CHUNK, n)
    nt = n // LANES

    def bits_at(start, size):
        return lax.bitcast_convert_type(afft_ref[:, pl.ds(start, size)], i32)

    def count_ge(cand):
        def body(c, acc):
            x = bits_at(pl.multiple_of(c * chunk, chunk), chunk)
            return acc + jnp.where(x >= cand, 1.0, 0.0)
        acc = lax.fori_loop(0, n // chunk, body, jnp.zeros((e, chunk), f32))
        return jnp.sum(acc, axis=1, keepdims=True)

    def bit_body(it, prefix):
        cand = prefix | jnp.left_shift(jnp.int32(1), 30 - it)
        return jnp.where(count_ge(cand) >= cap, cand, prefix)

    thr = lax.fori_loop(0, 31, bit_body, jnp.zeros((e, 1), i32))
    need = cap - count_ge(thr + 1)

    tri = (lax.broadcasted_iota(i32, (LANES, LANES), 0)
           <= lax.broadcasted_iota(i32, (LANES, LANES), 1)).astype(jnp.bfloat16)
    tile_id = lax.broadcasted_iota(i32, (e, nt), 1)

    def tile_body(j, carry):
        c_gt, c_eq, offs_acc = carry
        x = bits_at(pl.multiple_of(j * LANES, LANES), LANES)
        gt = x > thr
        eq = x == thr
        both = jnp.concatenate([jnp.where(gt, 1.0, 0.0), jnp.where(eq, 1.0, 0.0)],
                               axis=0).astype(jnp.bfloat16)
        inc = jnp.dot(both, tri, preferred_element_type=f32)
        inc_gt = inc[:e] + c_gt
        inc_eq = inc[e:] + c_eq
        sel = gt | (eq & (inc_eq <= need))
        incl = inc_gt + jnp.minimum(inc_eq, need)
        pos_ref[:, pl.ds(pl.multiple_of(j * LANES, LANES), LANES)] = jnp.where(
            sel, incl - 1.0, -1.0).astype(i32)
        offs_acc = jnp.where(tile_id == j, c_gt + jnp.minimum(c_eq, need), offs_acc)
        return inc_gt[:, LANES - 1:LANES], inc_eq[:, LANES - 1:LANES], offs_acc

    zero = jnp.zeros((e, 1), f32)
    _, _, offs_acc = lax.fori_loop(0, nt, tile_body, (zero, zero, jnp.zeros((e, nt), f32)))
    offs_ref[...] = offs_acc.astype(i32)


def ec_select(afft, cap):
    e, n = afft.shape
    return pl.pallas_call(
        functools.partial(_select_kernel, cap=cap),
        out_shape=[jax.ShapeDtypeStruct((e, n), jnp.int32),
                   jax.ShapeDtypeStruct((e, n // LANES), jnp.int32)],
        compiler_params=pltpu.CompilerParams(vmem_limit_bytes=_VMEM_LIMIT),
        name="ec_select",
    )(afft)


_SC_LANES = 16
_SC_CORES = 2
_INV_CHUNK = 8192


def ec_invert(pos, cap):
    ne, n = pos.shape
    ch = min(_INV_CHUNK, n)
    mesh = plsc.VectorSubcoreMesh(core_axis_name="c", subcore_axis_name="s")

    @functools.partial(
        pl.kernel, mesh=mesh,
        out_type=jax.ShapeDtypeStruct((ne * cap,), jnp.int32),
        scratch_types=[pltpu.VMEM((ch,), jnp.int32), pltpu.VMEM((cap,), jnp.int32)],
        compiler_params=pltpu.CompilerParams(needs_layout_passes=False))
    def invert(pos_hbm, idx_hbm, pos_v, idx_v):
        wid = lax.axis_index("s") * _SC_CORES + lax.axis_index("c")

        @pl.when(wid < ne)
        def _():
            @pl.loop(0, n // ch)
            def _(c):
                pltpu.sync_copy(pos_hbm.at[pl.ds(wid * n + c * ch, ch)], pos_v)

                @pl.loop(0, ch, step=_SC_LANES)
                def _(i):
                    p = pos_v[pl.ds(i, _SC_LANES)]
                    tok = lax.iota(jnp.int32, _SC_LANES) + (c * ch + i)
                    plsc.store_scatter(idx_v, [p], tok, mask=p >= 0)

            pltpu.sync_copy(idx_v, idx_hbm.at[pl.ds(wid * cap, cap)])

    return invert(pos.reshape(ne * n)).reshape(ne, cap)


def _expert_kernel(xs_ref, wg_ref, wu_ref, wd_ref, o_ref):
    @pl.when(pl.program_id(1) == pl.num_programs(1) - 1)
    def _():
        o_ref[...] = jnp.zeros(o_ref.shape, o_ref.dtype)

    @pl.when(pl.program_id(1) < pl.num_programs(1) - 1)
    def _():
        xs = xs_ref[...]
        a = jnp.dot(xs, wg_ref[...], preferred_element_type=jnp.float32)
        u = jnp.dot(xs, wu_ref[...], preferred_element_type=jnp.float32)
        hid = (a * jax.nn.sigmoid(a) * u).astype(jnp.bfloat16)
        o_ref[...] = jnp.dot(hid, wd_ref[...],
                             preferred_element_type=jnp.float32).astype(o_ref.dtype)


def expert_ffn(xs, wg, wu, wd, tm=512):
    e, c, d = xs.shape
    tm = min(tm, c)
    f = wg.shape[2]
    nt = c // tm
    return pl.pallas_call(
        _expert_kernel,
        grid=(e, nt + 1),
        in_specs=[
            pl.BlockSpec((None, tm, d), lambda ei, i: (ei, jnp.minimum(i, nt - 1), 0)),
            pl.BlockSpec((None, d, f), lambda ei, i: (ei, 0, 0)),
            pl.BlockSpec((None, d, f), lambda ei, i: (ei, 0, 0)),
            pl.BlockSpec((None, f, d), lambda ei, i: (ei, 0, 0)),
        ],
        out_specs=pl.BlockSpec((None, tm, d), lambda ei, i: (ei, i, 0)),
        out_shape=jax.ShapeDtypeStruct((e, c + tm, d), jnp.bfloat16),
        compiler_params=_cparams(("parallel", "arbitrary")),
        name="expert_ffn",
    )(xs, wg, wu, wd)


def _combine_kernel(offs_ref, x_ref, pos_ref, afft_ref, *rest):
    f32 = jnp.float32
    ne = pos_ref.shape[0]
    tiles, o_ref = rest[:2 * ne], rest[2 * ne]
    j = pl.program_id(0)
    pad = jnp.zeros((LANES - ne, LANES), f32)
    pos_t = jnp.concatenate([pos_ref[...].astype(f32), pad], axis=0).T
    gate_t = jnp.concatenate([afft_ref[...], pad], axis=0).T
    row_id = lax.broadcasted_iota(jnp.int32, (LANES, 2 * LANES), 1).astype(f32)
    y = x_ref[...]
    for e in range(ne):
        base = (offs_ref[e, j] // LANES) * LANES
        rel = pos_t[:, e:e + 1] - base.astype(f32)
        onehot = jnp.where(rel == row_id, 1.0, 0.0).astype(jnp.bfloat16)
        rows = jnp.concatenate([tiles[2 * e][...], tiles[2 * e + 1][...]], axis=0)
        y = y + jnp.dot(onehot, rows, preferred_element_type=f32) * gate_t[:, e:e + 1]
    o_ref[...] = y


def ec_combine(x2d, pos, afft, offs, out):
    n, d = x2d.shape
    ne, crows, _ = out.shape
    nblk = crows // LANES

    def tile_spec(e, k):
        return pl.BlockSpec(
            (None, LANES, d),
            lambda j, offs_ref: (e, jnp.minimum(offs_ref[e, j] // LANES + k, nblk - 1), 0))

    grid_spec = pltpu.PrefetchScalarGridSpec(
        num_scalar_prefetch=1,
        grid=(n // LANES,),
        in_specs=[
            pl.BlockSpec((LANES, d), lambda j, offs_ref: (j, 0)),
            pl.BlockSpec((ne, LANES), lambda j, offs_ref: (0, j)),
            pl.BlockSpec((ne, LANES), lambda j, offs_ref: (0, j)),
        ] + [tile_spec(e, k) for e in range(ne) for k in range(2)],
        out_specs=pl.BlockSpec((LANES, d), lambda j, offs_ref: (j, 0)),
    )
    return pl.pallas_call(
        _combine_kernel,
        grid_spec=grid_spec,
        out_shape=jax.ShapeDtypeStruct((n, d), jnp.float32),
        compiler_params=_cparams(("arbitrary",)),
        name="ec_combine",
    )(offs, x2d, pos, afft, *([out] * (2 * ne)))


def _rms_kernel(x_ref, g_ref, o_ref):
    x = x_ref[...]
    o_ref[...] = x * lax.rsqrt(jnp.mean(x * x, axis=-1, keepdims=True) + EPS) * g_ref[...]


def rms_norm_call(x2d, g, tm=1024):
    n, d = x2d.shape
    tm = min(tm, n)
    return pl.pallas_call(
        _rms_kernel,
        grid=(n // tm,),
        in_specs=[pl.BlockSpec((tm, d), lambda i: (i, 0)),
                  pl.BlockSpec((1, d), lambda i: (0, 0))],
        out_specs=pl.BlockSpec((tm, d), lambda i: (i, 0)),
        out_shape=jax.ShapeDtypeStruct((n, d), jnp.float32),
        compiler_params=_cparams(("parallel",)),
        name="final_norm",
    )(x2d, g.reshape(1, d))


def _trunk(x, p):
    b, t, d = x.shape
    n = b * t
    depth = p["w_in"].shape[0]
    x2d = x.reshape(n, d)
    cap = max(1, EC_CAPACITY * n // N_EXPERTS)
    for layer in range(depth):
        proj = norm_proj(x2d, p["norm_mix_g"][layer], p["w_in"][layer])
        proj3d = proj.reshape(b, t, -1)
        att = diff_attention(proj3d, p["lambda_q1"][layer], p["lambda_k1"][layer],
                             p["lambda_q2"][layer], p["lambda_k2"][layer],
                             p["diff_subln_g"][layer], layer)
        hgo = hgrn_bidir(proj3d, p["hgrn_lb"], p["hgrn_norm_g"][layer], layer)
        x2d = out_proj(att.reshape(n, -1), hgo.reshape(n, -1), p["w_out"][layer], x2d)
        h, afft = router(x2d, p["norm_ffn_g"][layer], p["w_router"][layer])
        pos, offs = ec_select(afft, cap)
        idx = ec_invert(pos, cap)
        out = expert_ffn(h[idx], p["w_gate"][layer], p["w_up"][layer], p["w_down"][layer])
        x2d = ec_combine(x2d, pos, afft, offs, out)
    return rms_norm_call(x2d, p["final_norm_g"]).reshape(b, t, d)


def kernel(x_prompt, x_sample, norm_mix_g, w_in, lambda_q1, lambda_k1, lambda_q2, lambda_k2,
           diff_subln_g, hgrn_lb, hgrn_norm_g, w_out, norm_ffn_g, w_router, w_gate, w_up,
           w_down, final_norm_g):
    bf = jnp.bfloat16
    p = dict(norm_mix_g=norm_mix_g, w_in=w_in.astype(bf), lambda_q1=lambda_q1,
             lambda_k1=lambda_k1, lambda_q2=lambda_q2, lambda_k2=lambda_k2,
             diff_subln_g=diff_subln_g, hgrn_lb=hgrn_lb, hgrn_norm_g=hgrn_norm_g,
             w_out=w_out.astype(bf), norm_ffn_g=norm_ffn_g, w_router=w_router,
             w_gate=w_gate.astype(bf), w_up=w_up.astype(bf), w_down=w_down.astype(bf),
             final_norm_g=final_norm_g)
    return (_trunk(x_prompt, p), _trunk(x_sample, p))
```

```python
import functools
import math

import jax
import jax.numpy as jnp
from jax import lax
from jax.experimental import pallas as pl
from jax.experimental.pallas import tpu as pltpu
from jax.experimental.pallas import tpu_sc as plsc

ATT_HEADS = 4
ATT_QK_DIM = 64
ATT_V_DIM = 128
HG_HEADS = 4
HG_K = 128
HG_V = 128
HG_CHUNK = 64
N_EXPERTS = 16
EC_CAPACITY = 2
EPS = 1e-6
LANES = 128
SUBLANES = 8

_QBLK, _KBLK, _VBLK = 0, 4, 8
_HQ, _HFF, _HFB, _HI, _HG = 12, 16, 20, 24, 28

_VMEM_LIMIT = 56 * 1024 * 1024

_NT = (((1,), (1,)), ((), ()))


def _cparams(sem):
    return pltpu.CompilerParams(dimension_semantics=sem, vmem_limit_bytes=_VMEM_LIMIT)


def _norm_proj_kernel(x_ref, g_ref, w_ref, o_ref, *, n_split):
    x = x_ref[...]
    y = x * lax.rsqrt(jnp.mean(x * x, axis=-1, keepdims=True) + EPS) * g_ref[...]
    h = y.astype(jnp.bfloat16)
    wn = w_ref.shape[1] // n_split
    for c in range(n_split):
        o_ref[:, c * wn:(c + 1) * wn] = jnp.dot(
            h, w_ref[:, c * wn:(c + 1) * wn], preferred_element_type=jnp.float32
        ).astype(o_ref.dtype)


def norm_proj(x2d, g, w_bf16, tm=512):
    n, d = x2d.shape
    tm = min(tm, n)
    dout = w_bf16.shape[1]
    return pl.pallas_call(
        functools.partial(_norm_proj_kernel, n_split=max(1, dout // 1024)),
        grid=(n // tm,),
        in_specs=[
            pl.BlockSpec((tm, d), lambda i: (i, 0)),
            pl.BlockSpec((1, d), lambda i: (0, 0)),
            pl.BlockSpec((d, dout), lambda i: (0, 0)),
        ],
        out_specs=pl.BlockSpec((tm, dout), lambda i: (i, 0)),
        out_shape=jax.ShapeDtypeStruct((n, dout), jnp.bfloat16),
        compiler_params=_cparams(("parallel",)),
        name="norm_proj",
    )(x2d, g.reshape(1, d), w_bf16)


_LOG2E = 1.4426950408889634
_Q_STRIP = 128
_ONES_ROWS = 16


def _split3(x):
    bf = jnp.bfloat16
    h = x.astype(bf).astype(jnp.float32)
    r = x - h
    m = r.astype(bf).astype(jnp.float32)
    lo = (r - m).astype(bf).astype(jnp.float32)
    return h, m, lo


def _attn_kernel(beta_ref, qext_ref, lq1_ref, lk1_ref, lq2_ref, lk2_ref, subg_ref,
                 q_ref, k_ref, v_ref, kext_ref, o_ref, vt_sc, acc_sc, s_sc, *, lam_init, tk, tq):
    f32, bf = jnp.float32, jnp.bfloat16
    t = q_ref.shape[0]
    nk = t // tk
    qs_ = _Q_STRIP
    ng = tq // qs_
    ncol = 2 * tq
    dv = v_ref.shape[1]
    beta = beta_ref[0:1, 0:1]

    def vt_body(c, carry):
        r0 = pl.multiple_of(c * tk, tk)
        vt_sc[0:dv, pl.ds(r0, tk)] = v_ref[pl.ds(r0, tk), :].astype(f32).T.astype(bf)
        return carry

    lax.fori_loop(0, nk, vt_body, 0)
    vt_sc[dv:dv + _ONES_ROWS, :] = jnp.ones((_ONES_ROWS, t), bf)

    lam = (jnp.exp(jnp.sum(lq1_ref[...] * lk1_ref[...], axis=-1, keepdims=True))
           - jnp.exp(jnp.sum(lq2_ref[...] * lk2_ref[...], axis=-1, keepdims=True)) + lam_init)
    lane = lax.broadcasted_iota(jnp.int32, (qs_, LANES), 1)
    kr = lax.broadcasted_iota(jnp.int32, (tk, qs_), 0)
    qc = lax.broadcasted_iota(jnp.int32, (tk, qs_), 1)
    cscale = ATT_QK_DIM ** -0.5 * _LOG2E

    def group(gi, carry):
        g0 = pl.multiple_of(gi * tq, tq)
        tiles = []
        for u in range(ng):
            q0 = g0 + u * qs_
            qs = (q_ref[pl.ds(q0, qs_), :].astype(f32) * cscale).astype(bf).astype(f32)
            qext = qext_ref[...]
            q_aug = jnp.concatenate([
                jnp.concatenate([jnp.where(lane < ATT_QK_DIM, qs, 0.0), qext], axis=1),
                jnp.concatenate([jnp.where(lane >= ATT_QK_DIM, qs, 0.0), qext], axis=1)], axis=0)
            tiles.append(q_aug.T.astype(bf))
        w = jnp.concatenate(tiles, axis=1)
        jc = g0 // tk

        def corr_tile(d):
            corr = 2.0 * beta * jnp.minimum((d + qc - kr).astype(f32), 0.0)
            return jnp.concatenate([corr, corr], axis=1)

        def shift(d, sidx):
            sgn = 1.0 if sidx == 0 else -1.0
            return (-sgn) * beta * d.astype(f32)

        def scores(n, sidx, crossing):
            r0 = pl.multiple_of(n * tk, tk)
            k_aug = jnp.concatenate([k_ref[pl.ds(r0, tk), :], kext_ref[sidx]], axis=1)
            for u in range(ng):
                c0 = u * 2 * qs_
                s = jnp.dot(k_aug, w[:, c0:c0 + 2 * qs_], preferred_element_type=f32)
                if crossing:
                    s = s + corr_tile(g0 + u * qs_ - r0)
                s_sc[:, c0:c0 + 2 * qs_] = s

        def step(n, mc, sidx, crossing):
            r0 = pl.multiple_of((n - 1) * tk, tk)
            r1 = pl.multiple_of(n * tk, tk)
            vt = vt_sc[:, pl.ds(r0, tk)]
            if sidx is not None:
                k_aug = jnp.concatenate([k_ref[pl.ds(r1, tk), :], kext_ref[sidx]], axis=1)
            m, csts = mc
            ms, cs = [], []
            for u in range(ng):
                c0 = u * 2 * qs_
                s = s_sc[:, c0:c0 + 2 * qs_]
                m_old = m[:, c0:c0 + 2 * qs_]
                cst = csts[:, c0:c0 + 2 * qs_]
                m_new = jnp.maximum(m_old, jnp.max(s, axis=0, keepdims=True) + cst)
                alpha = jnp.exp2(m_old - m_new)
                p = jnp.exp2(s - (m_new - cst)).astype(bf)
                acc_sc[:, c0:c0 + 2 * qs_] = (acc_sc[:, c0:c0 + 2 * qs_] * alpha
                                              + jnp.dot(vt, p, preferred_element_type=f32))
                ms.append(m_new)
                if sidx is not None:
                    s = jnp.dot(k_aug, w[:, c0:c0 + 2 * qs_], preferred_element_type=f32)
                    if crossing:
                        s = s + corr_tile(g0 + u * qs_ - r1)
                    s_sc[:, c0:c0 + 2 * qs_] = s
                    cs.append(jnp.broadcast_to(shift(g0 + u * qs_ - r1, sidx), (1, 2 * qs_)))
            return jnp.concatenate(ms, axis=1), (jnp.concatenate(cs, axis=1) if cs else csts)

        acc_sc[...] = jnp.zeros(acc_sc.shape, f32)
        m = jnp.full((1, ncol), -1e30, f32)
        scores(0, 0, True)
        n1 = jnp.maximum(jc, 1)
        n2 = jc + tq // tk
        cst0 = jnp.concatenate([jnp.broadcast_to(shift(g0 + u * qs_, 0), (1, 2 * qs_))
                                for u in range(ng)], axis=1)
        mc = (m, cst0)
        mc = lax.fori_loop(1, n1, functools.partial(step, sidx=0, crossing=False), mc)
        mc = lax.fori_loop(n1, n2, functools.partial(step, sidx=0, crossing=True), mc)
        mc = lax.fori_loop(n2, nk, functools.partial(step, sidx=1, crossing=False), mc)
        step(nk, mc, None, False)

        for u in range(ng):
            c0 = u * 2 * qs_
            acc = acc_sc[:, c0:c0 + 2 * qs_]
            den = acc[dv:dv + 1, :]
            o = acc[0:dv, 0:qs_] / den[:, 0:qs_] - lam * (acc[0:dv, qs_:] / den[:, qs_:])
            y = o * lax.rsqrt(jnp.mean(o * o, axis=0, keepdims=True) + EPS) * subg_ref[...]
            o_ref[pl.ds(g0 + u * qs_, qs_), :] = (y * (1.0 - lam_init)).T.astype(o_ref.dtype)
        return carry

    lax.fori_loop(0, t // tq, group, 0)


def _attn_tables(tk):
    f32 = jnp.float32
    slopes = 2.0 ** (-8.0 * (jnp.arange(ATT_HEADS, dtype=f32) + 1.0) / ATT_HEADS)
    beta = slopes * _LOG2E
    lane = jnp.arange(LANES)[None, None, :]
    b = _split3(beta)
    a = _split3(-beta[:, None] * jnp.arange(_Q_STRIP, dtype=f32)[None, :])
    qext = jnp.zeros((ATT_HEADS, _Q_STRIP, LANES), f32)
    for c in range(3):
        qext = jnp.where(lane == c, b[c][:, None, None], qext)
        qext = jnp.where(lane == 3 + c, a[c][:, :, None], qext)
    r = jnp.arange(tk, dtype=f32)[:, None]
    lane2 = jnp.arange(LANES)[None, :]
    base = jnp.where(lane2 < 3, r, jnp.where(lane2 < 6, 1.0, 0.0))
    kext = jnp.stack([base, -base]).astype(jnp.bfloat16)
    beta_t = jnp.broadcast_to(beta[:, None, None], (ATT_HEADS, 1, LANES))
    return beta_t, qext, kext


def diff_attention(proj3d, lq1, lk1, lq2, lk2, subg, layer, tk=256, tq=2048):
    b, t, _ = proj3d.shape
    tk, tq = min(tk, t), min(tq, t)
    assert tq % tk == 0 and tq % _Q_STRIP == 0
    lam_init = 0.8 - 0.6 * math.exp(-0.3 * layer)
    assert tk <= 256
    beta_t, qext, kext = _attn_tables(tk)
    vec = lambda a: a.reshape(1, -1).astype(jnp.float32)
    small = lambda n: pl.BlockSpec((1, n), lambda bi, h: (0, 0))
    head = pl.BlockSpec((None, 1, LANES), lambda bi, h: (h, 0, 0))
    col = lambda blk: pl.BlockSpec((None, t, LANES), lambda bi, h: (bi, 0, blk + h))
    return pl.pallas_call(
        functools.partial(_attn_kernel, lam_init=lam_init, tk=tk, tq=tq),
        grid=(b, ATT_HEADS),
        in_specs=[
            head, pl.BlockSpec((None, _Q_STRIP, LANES), lambda bi, h: (h, 0, 0)),
            small(ATT_QK_DIM), small(ATT_QK_DIM), small(ATT_QK_DIM), small(ATT_QK_DIM),
            pl.BlockSpec((ATT_V_DIM, 1), lambda bi, h: (0, 0)),
            col(_QBLK), col(_KBLK), col(_VBLK),
            pl.BlockSpec((2, tk, LANES), lambda bi, h: (0, 0, 0)),
        ],
        out_specs=pl.BlockSpec((None, t, LANES), lambda bi, h: (bi, 0, h)),
        out_shape=jax.ShapeDtypeStruct((b, t, ATT_HEADS * ATT_V_DIM), jnp.bfloat16),
        scratch_shapes=[pltpu.VMEM((ATT_V_DIM + _ONES_ROWS, t), jnp.bfloat16),
                        pltpu.VMEM((ATT_V_DIM + _ONES_ROWS, 2 * tq), jnp.float32),
                        pltpu.VMEM((tk, 2 * tq), jnp.float32)],
        compiler_params=_cparams(("parallel", "parallel")),
        name="diff_attn",
    )(beta_t, qext, vec(lq1), vec(lk1), vec(lq2), vec(lk2),
      subg.reshape(ATT_V_DIM, 1).astype(jnp.float32), proj3d, proj3d, proj3d, kext)


_HG_CHUNKS_PER_ITER = 4
_HG_FIN_ROWS = 256


def _scan_rows(g, rev):
    c = g.shape[0]
    row = lax.broadcasted_iota(jnp.int32, g.shape, 0)
    b = g
    s = 1
    while s < c:
        if rev:
            b = b + jnp.where(row < c - s, pltpu.roll(b, c - s, axis=0), 0.0)
        else:
            b = b + jnp.where(row >= s, pltpu.roll(b, s, axis=0), 0.0)
        s *= 2
    return b


def _level_ref(b, m, rev):
    c, w = b.shape
    r = m // 2 if rev else m // 2 - 1
    pieces = []
    if m >= SUBLANES:
        for j in range(c // m):
            pieces.append(jnp.broadcast_to(b[j * m + r:j * m + r + 1, :], (m, w)))
    else:
        sub = lax.broadcasted_iota(jnp.int32, (SUBLANES, w), 0)
        for j in range(c // SUBLANES):
            base = j * SUBLANES
            acc = jnp.broadcast_to(b[base + r:base + r + 1, :], (SUBLANES, w))
            for i in range(1, SUBLANES // m):
                cand = jnp.broadcast_to(b[base + i * m + r:base + i * m + r + 1, :], (SUBLANES, w))
                acc = jnp.where(sub >= i * m, cand, acc)
            pieces.append(acc)
    return jnp.concatenate(pieces, axis=0)


def _neg_abs(x):
    bits = lax.bitcast_convert_type(x, jnp.uint32) | jnp.uint32(0x80000000)
    return lax.bitcast_convert_type(bits, jnp.float32)


def _hgrn_masks(c, rev):
    row = lax.broadcasted_iota(jnp.int32, (c, LANES), 0)
    ti = lax.broadcasted_iota(jnp.int32, (c, c), 0)
    si = lax.broadcasted_iota(jnp.int32, (c, c), 1)
    qrows, pairs = {}, {}
    m = c
    while m >= 2:
        late = (row & (m - 1)) >= m // 2
        t_late = (ti & (m - 1)) >= m // 2
        s_late = (si & (m - 1)) >= m // 2
        same = (ti ^ si) < m
        if rev:
            qrows[m] = jnp.logical_not(late)
            pairs[m] = same & jnp.logical_not(t_late) & s_late
        else:
            qrows[m] = late
            pairs[m] = same & t_late & jnp.logical_not(s_late)
        m //= 2
    return qrows, pairs, ti == si


def _hgrn_chunk(q, f, v, st_ref, rev, masks):
    qrows, pairs, diag = masks
    c = q.shape[0]
    bf = jnp.bfloat16
    k_ = 1.0 - f
    b = _scan_rows(jnp.log(f) * _LOG2E, rev)
    a = jnp.where(diag,
                  lax.dot_general(q.astype(bf), k_.astype(bf), _NT,
                                  preferred_element_type=jnp.float32), 0.0)
    m = c
    while m >= 2:
        if m == 2:
            cref = jnp.where(qrows[m], pltpu.roll(b, c - 1 if rev else 1, axis=0), b)
        else:
            cref = _level_ref(b, m, rev)
        e = jnp.exp2(_neg_abs(b - cref))
        z = (jnp.where(qrows[m], q, k_) * e).astype(bf)
        p = lax.dot_general(z, z, _NT, preferred_element_type=jnp.float32)
        a = jnp.where(pairs[m], p, a)
        m //= 2
    st = st_ref[...]
    o = jnp.dot(a.astype(bf), v.astype(bf), preferred_element_type=jnp.float32)
    o = o + lax.dot_general((q * jnp.exp2(b)).astype(bf), st.astype(bf), _NT,
                            preferred_element_type=jnp.float32)
    btot = b[0:1, :] if rev else b[c - 1:c, :]
    khat = (k_ * jnp.exp2(btot - b)).astype(bf)
    st_ref[...] = st * jnp.exp2(btot) + jnp.dot(v.T.astype(bf), khat,
                                                preferred_element_type=jnp.float32)
    return o


def _hgrn_kernel(lb_ref, ng_ref, q_ref, ff_ref, fb_ref, i_ref, g_ref, o_ref,
                 of_sc, ob_sc, stf_sc, stb_sc, *, layer, chunk):
    t = q_ref.shape[0]
    n = t // chunk
    lb_raw = lb_ref[...].astype(jnp.float32)
    ex = jnp.exp(lb_raw - jnp.max(lb_raw, axis=0, keepdims=True))
    sm = ex / jnp.sum(ex, axis=0, keepdims=True)
    lb = jnp.zeros_like(sm[0])
    for l_ in range(1, layer + 1):
        lb = lb + sm[l_]
    lb_f = lb[0:1, :]
    lb_b = lb[1:2, :]

    stf_sc[...] = jnp.zeros(stf_sc.shape, jnp.float32)
    stb_sc[...] = jnp.zeros(stb_sc.shape, jnp.float32)

    def load(ref, r0):
        return ref[pl.ds(r0, chunk), :].astype(jnp.float32)

    masks_f = _hgrn_masks(chunk, False)
    masks_b = _hgrn_masks(chunk, True)

    per_iter = _HG_CHUNKS_PER_ITER if n % _HG_CHUNKS_PER_ITER == 0 else 1

    def body(ci, carry):
        for u in range(per_iter):
            r0 = pl.multiple_of((ci * per_iter + u) * chunk, chunk)
            qr = load(q_ref, r0)
            f = lb_f + (1.0 - lb_f) * jax.nn.sigmoid(load(ff_ref, r0))
            of_sc[pl.ds(r0, chunk), :] = _hgrn_chunk(qr * jax.nn.sigmoid(qr), f,
                                                     load(i_ref, r0), stf_sc, False, masks_f)
            r1 = pl.multiple_of((n - 1 - ci * per_iter - u) * chunk, chunk)
            qr = load(q_ref, r1)
            f = lb_b + (1.0 - lb_b) * jax.nn.sigmoid(load(fb_ref, r1))
            ob_sc[pl.ds(r1, chunk), :] = _hgrn_chunk(qr * jax.nn.sigmoid(qr), f,
                                                     load(i_ref, r1), stb_sc, True, masks_b)
        return carry

    lax.fori_loop(0, n // per_iter, body, 0)

    rows = math.gcd(t, _HG_FIN_ROWS)

    def fin(ci, carry):
        r0 = pl.multiple_of(ci * rows, rows)
        o = of_sc[pl.ds(r0, rows), :] + ob_sc[pl.ds(r0, rows), :]
        y = o * lax.rsqrt(jnp.mean(o * o, axis=-1, keepdims=True) + EPS) * ng_ref[...]
        gr = g_ref[pl.ds(r0, rows), :].astype(jnp.float32)
        o_ref[pl.ds(r0, rows), :] = (y * (gr * jax.nn.sigmoid(gr))).astype(o_ref.dtype)
        return carry

    lax.fori_loop(0, t // rows, fin, 0)


def hgrn_bidir(proj3d, hgrn_lb, norm_g, layer):
    b, t, _ = proj3d.shape
    depth = hgrn_lb.shape[0]
    col = lambda blk: pl.BlockSpec((None, t, LANES), lambda bi, h: (bi, 0, blk + h))
    return pl.pallas_call(
        functools.partial(_hgrn_kernel, layer=layer, chunk=HG_CHUNK),
        grid=(b, HG_HEADS),
        in_specs=[
            pl.BlockSpec((depth, 2, LANES), lambda bi, h: (0, 0, h)),
            pl.BlockSpec((1, HG_V), lambda bi, h: (0, 0)),
            col(_HQ), col(_HFF), col(_HFB), col(_HI), col(_HG),
        ],
        out_specs=pl.BlockSpec((None, t, LANES), lambda bi, h: (bi, 0, h)),
        out_shape=jax.ShapeDtypeStruct((b, t, HG_HEADS * HG_V), jnp.bfloat16),
        scratch_shapes=[
            pltpu.VMEM((t, HG_V), jnp.float32),
            pltpu.VMEM((t, HG_V), jnp.float32),
            pltpu.VMEM((HG_V, HG_K), jnp.float32),
            pltpu.VMEM((HG_V, HG_K), jnp.float32),
        ],
        compiler_params=_cparams(("parallel", "parallel")),
        name="hgrn_bidir",
    )(hgrn_lb, norm_g.reshape(1, HG_V).astype(jnp.float32),
      proj3d, proj3d, proj3d, proj3d, proj3d)


def _out_proj_kernel(a_ref, h_ref, wa_ref, wh_ref, x_ref, o_ref):
    y = jnp.dot(a_ref[...], wa_ref[...], preferred_element_type=jnp.float32)
    y = y + jnp.dot(h_ref[...], wh_ref[...], preferred_element_type=jnp.float32)
    o_ref[...] = x_ref[...] + y


def out_proj(att2d, hgo2d, w_out_bf16, x2d, tm=512):
    n, d = x2d.shape
    tm = min(tm, n)
    wa = att2d.shape[1]
    wh = hgo2d.shape[1]
    return pl.pallas_call(
        _out_proj_kernel,
        grid=(n // tm,),
        in_specs=[
            pl.BlockSpec((tm, wa), lambda i: (i, 0)),
            pl.BlockSpec((tm, wh), lambda i: (i, 0)),
            pl.BlockSpec((wa, d), lambda i: (0, 0)),
            pl.BlockSpec((wh, d), lambda i: (wa // wh, 0)),
            pl.BlockSpec((tm, d), lambda i: (i, 0)),
        ],
        out_specs=pl.BlockSpec((tm, d), lambda i: (i, 0)),
        out_shape=jax.ShapeDtypeStruct((n, d), jnp.float32),
        compiler_params=_cparams(("parallel",)),
        name="out_proj",
    )(att2d, hgo2d, w_out_bf16, w_out_bf16, x2d)


def _router_kernel(x_ref, g_ref, wrt_ref, h_ref, afft_ref):
    x = x_ref[...]
    h = x * lax.rsqrt(jnp.mean(x * x, axis=-1, keepdims=True) + EPS) * g_ref[...]
    h_ref[...] = h.astype(h_ref.dtype)
    logits = lax.dot_general(wrt_ref[...], h, _NT, preferred_element_type=jnp.float32,
                             precision=lax.Precision.HIGHEST)
    ex = jnp.exp(logits - jnp.max(logits, axis=0, keepdims=True))
    afft_ref[...] = ex / jnp.sum(ex, axis=0, keepdims=True)


def router(x2d, g, w_router, tm=512):
    n, d = x2d.shape
    tm = min(tm, n)
    e = w_router.shape[1]
    return pl.pallas_call(
        _router_kernel,
        grid=(n // tm,),
        in_specs=[
            pl.BlockSpec((tm, d), lambda i: (i, 0)),
            pl.BlockSpec((1, d), lambda i: (0, 0)),
            pl.BlockSpec((e, d), lambda i: (0, 0)),
        ],
        out_specs=[
            pl.BlockSpec((tm, d), lambda i: (i, 0)),
            pl.BlockSpec((e, tm), lambda i: (0, i)),
        ],
        out_shape=[
            jax.ShapeDtypeStruct((n, d), jnp.bfloat16),
            jax.ShapeDtypeStruct((e, n), jnp.float32),
        ],
        compiler_params=_cparams(("parallel",)),
        name="router",
    )(x2d, g.reshape(1, d), w_router.T)


_SEL_CHUNK = 1024


def _select_kernel(afft_ref, pos_ref, offs_ref, *, cap):
    f32, i32 = jnp.float32, jnp.int32
    e, n = afft_ref.shape
    chunk = min(_SEL_CHUNK, n)
    nt = n // LANES

    def bits_at(start, size):
        return lax.bitcast_convert_type(afft_ref[:, pl.ds(start, size)], i32)

    def count_ge(cand):
        def body(c, acc):
            x = bits_at(pl.multiple_of(c * chunk, chunk), chunk)
            return acc + jnp.where(x >= cand, 1.0, 0.0)
        acc = lax.fori_loop(0, n // chunk, body, jnp.zeros((e, chunk), f32))
        return jnp.sum(acc, axis=1, keepdims=True)

    def bit_body(it, prefix):
        cand = prefix | jnp.left_shift(jnp.int32(1), 30 - it)
        return jnp.where(count_ge(cand) >= cap, cand, prefix)

    thr = lax.fori_loop(0, 31, bit_body, jnp.zeros((e, 1), i32))
    need = cap - count_ge(thr + 1)

    tri = (lax.broadcasted_iota(i32, (LANES, LANES), 0)
           <= lax.broadcasted_iota(i32, (LANES, LANES), 1)).astype(jnp.bfloat16)
    tile_id = lax.broadcasted_iota(i32, (e, nt), 1)

    def tile_body(j, carry):
        c_gt, c_eq, offs_acc = carry
        x = bits_at(pl.multiple_of(j * LANES, LANES), LANES)
        gt = x > thr
        eq = x == thr
        both = jnp.concatenate([jnp.where(gt, 1.0, 0.0), jnp.where(eq, 1.0, 0.0)],
                               axis=0).astype(jnp.bfloat16)
        inc = jnp.dot(both, tri, preferred_element_type=f32)
        inc_gt = inc[:e] + c_gt
        inc_eq = inc[e:] + c_eq
        sel = gt | (eq & (inc_eq <= need))
        incl = inc_gt + jnp.minimum(inc_eq, need)
        pos_ref[:, pl.ds(pl.multiple_of(j * LANES, LANES), LANES)] = jnp.where(
            sel, incl - 1.0, -1.0).astype(i32)
        offs_acc = jnp.where(tile_id == j, c_gt + jnp.minimum(c_eq, need), offs_acc)
        return inc_gt[:, LANES - 1:LANES], inc_eq[:, LANES - 1:LANES], offs_acc

    zero = jnp.zeros((e, 1), f32)
    _, _, offs_acc = lax.fori_loop(0, nt, tile_body, (zero, zero, jnp.zeros((e, nt), f32)))
    offs_ref[...] = offs_acc.astype(i32)


def ec_select(afft, cap):
    e, n = afft.shape
    return pl.pallas_call(
        functools.partial(_select_kernel, cap=cap),
        out_shape=[jax.ShapeDtypeStruct((e, n), jnp.int32),
                   jax.ShapeDtypeStruct((e, n // LANES), jnp.int32)],
        compiler_params=pltpu.CompilerParams(vmem_limit_bytes=_VMEM_LIMIT),
        name="ec_select",
    )(afft)


_SC_LANES = 16
_SC_CORES = 2
_INV_CHUNK = 8192


def ec_invert(pos, cap):
    ne, n = pos.shape
    ch = min(_INV_CHUNK, n)
    mesh = plsc.VectorSubcoreMesh(core_axis_name="c", subcore_axis_name="s")

    @functools.partial(
        pl.kernel, mesh=mesh,
        out_type=jax.ShapeDtypeStruct((ne * cap,), jnp.int32),
        scratch_types=[pltpu.VMEM((ch,), jnp.int32), pltpu.VMEM((cap,), jnp.int32)],
        compiler_params=pltpu.CompilerParams(needs_layout_passes=False))
    def invert(pos_hbm, idx_hbm, pos_v, idx_v):
        wid = lax.axis_index("s") * _SC_CORES + lax.axis_index("c")

        @pl.when(wid < ne)
        def _():
            @pl.loop(0, n // ch)
            def _(c):
                pltpu.sync_copy(pos_hbm.at[pl.ds(wid * n + c * ch, ch)], pos_v)

                @pl.loop(0, ch, step=_SC_LANES)
                def _(i):
                    p = pos_v[pl.ds(i, _SC_LANES)]
                    tok = lax.iota(jnp.int32, _SC_LANES) + (c * ch + i)
                    plsc.store_scatter(idx_v, [p], tok, mask=p >= 0)

            pltpu.sync_copy(idx_v, idx_hbm.at[pl.ds(wid * cap, cap)])

    return invert(pos.reshape(ne * n)).reshape(ne, cap)


_GATHER_WIN = 128


def ec_gather(h, idx):
    n, d = h.shape
    ne, cap = idx.shape
    m = ne * cap
    words = d // 2
    half = words // 2
    table = lax.bitcast_convert_type(h.reshape(n, words, 2), jnp.uint32).reshape(2 * n, half)
    idx2 = (2 * idx.reshape(m, 1) + jnp.arange(2, dtype=jnp.int32)[None, :]).reshape(1, 2 * m)
    steps = 2 * m // (_GATHER_WIN * _SC_CORES)
    mesh = plsc.VectorSubcoreMesh(core_axis_name="c", subcore_axis_name="s")

    @functools.partial(
        pl.kernel, mesh=mesh,
        out_type=jax.ShapeDtypeStruct((2 * m, half), jnp.uint32),
        scratch_types=[])
    def gather(x_hbm, i_hbm, o_hbm):
        def body(i_vmem, o_vmem):
            pltpu.sync_copy(x_hbm.at[i_vmem.at[0]], o_vmem)

        pltpu.emit_pipeline(
            body,
            grid=(_SC_CORES, steps),
            in_specs=[pl.BlockSpec((1, _GATHER_WIN), index_map=lambda a, i: (0, a * steps + i))],
            out_specs=[pl.BlockSpec((_GATHER_WIN, half), index_map=lambda a, i: (a * steps + i, 0))],
            core_axis_name=("c", "s"),
            dimension_semantics=(pltpu.PARALLEL, pltpu.PARALLEL),
        )(i_hbm, o_hbm)

    rows = gather(table, idx2)
    return lax.bitcast_convert_type(rows.reshape(m, words), jnp.bfloat16).reshape(ne, cap, d)


def _expert_kernel(xs_ref, wg_ref, wu_ref, wd_ref, o_ref):
    @pl.when(pl.program_id(1) == pl.num_programs(1) - 1)
    def _():
        o_ref[...] = jnp.zeros(o_ref.shape, o_ref.dtype)

    @pl.when(pl.program_id(1) < pl.num_programs(1) - 1)
    def _():
        xs = xs_ref[...]
        a = jnp.dot(xs, wg_ref[...], preferred_element_type=jnp.float32)
        u = jnp.dot(xs, wu_ref[...], preferred_element_type=jnp.float32)
        hid = (a * jax.nn.sigmoid(a) * u).astype(jnp.bfloat16)
        o_ref[...] = jnp.dot(hid, wd_ref[...],
                             preferred_element_type=jnp.float32).astype(o_ref.dtype)


def expert_ffn(xs, wg, wu, wd, tm=512):
    e, c, d = xs.shape
    tm = min(tm, c)
    f = wg.shape[2]
    nt = c // tm
    return pl.pallas_call(
        _expert_kernel,
        grid=(e, nt + 1),
        in_specs=[
            pl.BlockSpec((None, tm, d), lambda ei, i: (ei, jnp.minimum(i, nt - 1), 0)),
            pl.BlockSpec((None, d, f), lambda ei, i: (ei, 0, 0)),
            pl.BlockSpec((None, d, f), lambda ei, i: (ei, 0, 0)),
            pl.BlockSpec((None, f, d), lambda ei, i: (ei, 0, 0)),
        ],
        out_specs=pl.BlockSpec((None, tm, d), lambda ei, i: (ei, i, 0)),
        out_shape=jax.ShapeDtypeStruct((e, c + tm, d), jnp.bfloat16),
        compiler_params=_cparams(("parallel", "arbitrary")),
        name="expert_ffn",
    )(xs, wg, wu, wd)


def _combine_kernel(offs_ref, x_ref, pos_ref, afft_ref, *rest):
    f32 = jnp.float32
    ne = pos_ref.shape[0]
    tiles, o_ref = rest[:2 * ne], rest[2 * ne]
    j = pl.program_id(0)
    pad = jnp.zeros((LANES - ne, LANES), f32)
    pos_t = jnp.concatenate([pos_ref[...].astype(f32), pad], axis=0).T
    gate_t = jnp.concatenate([afft_ref[...], pad], axis=0).T
    row_id = lax.broadcasted_iota(jnp.int32, (LANES, 2 * LANES), 1).astype(f32)
    y = x_ref[...]
    for e in range(ne):
        base = (offs_ref[e, j] // LANES) * LANES
        rel = pos_t[:, e:e + 1] - base.astype(f32)
        onehot = jnp.where(rel == row_id, 1.0, 0.0).astype(jnp.bfloat16)
        rows = jnp.concatenate([tiles[2 * e][...], tiles[2 * e + 1][...]], axis=0)
        y = y + jnp.dot(onehot, rows, preferred_element_type=f32) * gate_t[:, e:e + 1]
    o_ref[...] = y


def ec_combine(x2d, pos, afft, offs, out):
    n, d = x2d.shape
    ne, crows, _ = out.shape
    nblk = crows // LANES

    def tile_spec(e, k):
        return pl.BlockSpec(
            (None, LANES, d),
            lambda j, offs_ref: (e, jnp.minimum(offs_ref[e, j] // LANES + k, nblk - 1), 0))

    grid_spec = pltpu.PrefetchScalarGridSpec(
        num_scalar_prefetch=1,
        grid=(n // LANES,),
        in_specs=[
            pl.BlockSpec((LANES, d), lambda j, offs_ref: (j, 0)),
            pl.BlockSpec((ne, LANES), lambda j, offs_ref: (0, j)),
            pl.BlockSpec((ne, LANES), lambda j, offs_ref: (0, j)),
        ] + [tile_spec(e, k) for e in range(ne) for k in range(2)],
        out_specs=pl.BlockSpec((LANES, d), lambda j, offs_ref: (j, 0)),
    )
    return pl.pallas_call(
        _combine_kernel,
        grid_spec=grid_spec,
        out_shape=jax.ShapeDtypeStruct((n, d), jnp.float32),
        compiler_params=_cparams(("arbitrary",)),
        name="ec_combine",
    )(offs, x2d, pos, afft, *([out] * (2 * ne)))


def _rms_kernel(x_ref, g_ref, o_ref):
    x = x_ref[...]
    o_ref[...] = x * lax.rsqrt(jnp.mean(x * x, axis=-1, keepdims=True) + EPS) * g_ref[...]


def rms_norm_call(x2d, g, tm=1024):
    n, d = x2d.shape
    tm = min(tm, n)
    return pl.pallas_call(
        _rms_kernel,
        grid=(n // tm,),
        in_specs=[pl.BlockSpec((tm, d), lambda i: (i, 0)),
                  pl.BlockSpec((1, d), lambda i: (0, 0))],
        out_specs=pl.BlockSpec((tm, d), lambda i: (i, 0)),
        out_shape=jax.ShapeDtypeStruct((n, d), jnp.float32),
        compiler_params=_cparams(("parallel",)),
        name="final_norm",
    )(x2d, g.reshape(1, d))


def _trunk(x, p):
    b, t, d = x.shape
    n = b * t
    depth = p["w_in"].shape[0]
    x2d = x.reshape(n, d)
    cap = max(1, EC_CAPACITY * n // N_EXPERTS)
    for layer in range(depth):
        proj = norm_proj(x2d, p["norm_mix_g"][layer], p["w_in"][layer])
        proj3d = proj.reshape(b, t, -1)
        att = diff_attention(proj3d, p["lambda_q1"][layer], p["lambda_k1"][layer],
                             p["lambda_q2"][layer], p["lambda_k2"][layer],
                             p["diff_subln_g"][layer], layer)
        hgo = hgrn_bidir(proj3d, p["hgrn_lb"], p["hgrn_norm_g"][layer], layer)
        x2d = out_proj(att.reshape(n, -1), hgo.reshape(n, -1), p["w_out"][layer], x2d)
        h, afft = router(x2d, p["norm_ffn_g"][layer], p["w_router"][layer])
        pos, offs = ec_select(afft, cap)
        idx = ec_invert(pos, cap)
        out = expert_ffn(ec_gather(h, idx), p["w_gate"][layer], p["w_up"][layer],
                         p["w_down"][layer])
        x2d = ec_combine(x2d, pos, afft, offs, out)
    return rms_norm_call(x2d, p["final_norm_g"]).reshape(b, t, d)


def kernel(x_prompt, x_sample, norm_mix_g, w_in, lambda_q1, lambda_k1, lambda_q2, lambda_k2,
           diff_subln_g, hgrn_lb, hgrn_norm_g, w_out, norm_ffn_g, w_router, w_gate, w_up,
           w_down, final_norm_g):
    bf = jnp.bfloat16
    p = dict(norm_mix_g=norm_mix_g, w_in=w_in.astype(bf), lambda_q1=lambda_q1,
             lambda_k1=lambda_k1, lambda_q2=lambda_q2, lambda_k2=lambda_k2,
             diff_subln_g=diff_subln_g, hgrn_lb=hgrn_lb, hgrn_norm_g=hgrn_norm_g,
             w_out=w_out.astype(bf), norm_ffn_g=norm_ffn_g, w_router=w_router,
             w_gate=w_gate.astype(bf), w_up=w_up.astype(bf), w_down=w_down.astype(bf),
             final_norm_g=final_norm_g)
    return (_trunk(x_prompt, p), _trunk(x_sample, p))
```

```python
import functools
import math

import jax
import jax.numpy as jnp
from jax import lax
from jax.experimental import pallas as pl
from jax.experimental.pallas import tpu as pltpu
from jax.experimental.pallas import tpu_sc as plsc

ATT_HEADS = 4
ATT_QK_DIM = 64
ATT_V_DIM = 128
HG_HEADS = 4
HG_K = 128
HG_V = 128
HG_CHUNK = 64
N_EXPERTS = 16
EC_CAPACITY = 2
EPS = 1e-6
LANES = 128
SUBLANES = 8

_QBLK, _KBLK, _VBLK = 0, 4, 8
_HQ, _HFF, _HFB, _HI, _HG = 12, 16, 20, 24, 28

_VMEM_LIMIT = 56 * 1024 * 1024

_NT = (((1,), (1,)), ((), ()))


def _cparams(sem):
    return pltpu.CompilerParams(dimension_semantics=sem, vmem_limit_bytes=_VMEM_LIMIT)


def _norm_proj_kernel(x_ref, g_ref, w_ref, o_ref, *, n_split):
    x = x_ref[...]
    y = x * lax.rsqrt(jnp.mean(x * x, axis=-1, keepdims=True) + EPS) * g_ref[...]
    h = y.astype(jnp.bfloat16)
    wn = w_ref.shape[1] // n_split
    for c in range(n_split):
        o_ref[:, c * wn:(c + 1) * wn] = jnp.dot(
            h, w_ref[:, c * wn:(c + 1) * wn], preferred_element_type=jnp.float32
        ).astype(o_ref.dtype)


def norm_proj(x2d, g, w_bf16, tm=512):
    n, d = x2d.shape
    tm = min(tm, n)
    dout = w_bf16.shape[1]
    return pl.pallas_call(
        functools.partial(_norm_proj_kernel, n_split=max(1, dout // 1024)),
        grid=(n // tm,),
        in_specs=[
            pl.BlockSpec((tm, d), lambda i: (i, 0)),
            pl.BlockSpec((1, d), lambda i: (0, 0)),
            pl.BlockSpec((d, dout), lambda i: (0, 0)),
        ],
        out_specs=pl.BlockSpec((tm, dout), lambda i: (i, 0)),
        out_shape=jax.ShapeDtypeStruct((n, dout), jnp.bfloat16),
        compiler_params=_cparams(("parallel",)),
        name="norm_proj",
    )(x2d, g.reshape(1, d), w_bf16)


_LOG2E = 1.4426950408889634
_Q_STRIP = 128
_ONES_ROWS = 16


def _split3(x):
    bf = jnp.bfloat16
    h = x.astype(bf).astype(jnp.float32)
    r = x - h
    m = r.astype(bf).astype(jnp.float32)
    lo = (r - m).astype(bf).astype(jnp.float32)
    return h, m, lo


def _attn_kernel(beta_ref, qext_ref, lq1_ref, lk1_ref, lq2_ref, lk2_ref, subg_ref,
                 q_ref, k_ref, v_ref, kext_ref, o_ref, vt_sc, acc_sc, s_sc, *, lam_init, tk, tq):
    f32, bf = jnp.float32, jnp.bfloat16
    t = q_ref.shape[0]
    nk = t // tk
    qs_ = _Q_STRIP
    ng = tq // qs_
    ncol = 2 * tq
    dv = v_ref.shape[1]
    beta = beta_ref[0:1, 0:1]

    def vt_body(c, carry):
        r0 = pl.multiple_of(c * tk, tk)
        vt_sc[0:dv, pl.ds(r0, tk)] = v_ref[pl.ds(r0, tk), :].astype(f32).T.astype(bf)
        return carry

    lax.fori_loop(0, nk, vt_body, 0)
    vt_sc[dv:dv + _ONES_ROWS, :] = jnp.ones((_ONES_ROWS, t), bf)

    lam = (jnp.exp(jnp.sum(lq1_ref[...] * lk1_ref[...], axis=-1, keepdims=True))
           - jnp.exp(jnp.sum(lq2_ref[...] * lk2_ref[...], axis=-1, keepdims=True)) + lam_init)
    lane = lax.broadcasted_iota(jnp.int32, (qs_, LANES), 1)
    kr = lax.broadcasted_iota(jnp.int32, (tk, qs_), 0)
    qc = lax.broadcasted_iota(jnp.int32, (tk, qs_), 1)
    cscale = ATT_QK_DIM ** -0.5 * _LOG2E

    def group(gi, carry):
        g0 = pl.multiple_of(gi * tq, tq)
        tiles = []
        for u in range(ng):
            q0 = g0 + u * qs_
            qs = (q_ref[pl.ds(q0, qs_), :].astype(f32) * cscale).astype(bf).astype(f32)
            qext = qext_ref[...]
            q_aug = jnp.concatenate([
                jnp.concatenate([jnp.where(lane < ATT_QK_DIM, qs, 0.0), qext], axis=1),
                jnp.concatenate([jnp.where(lane >= ATT_QK_DIM, qs, 0.0), qext], axis=1)], axis=0)
            tiles.append(q_aug.T.astype(bf))
        w = jnp.concatenate(tiles, axis=1)
        jc = g0 // tk

        def corr_tile(d):
            corr = 2.0 * beta * jnp.minimum((d + qc - kr).astype(f32), 0.0)
            return jnp.concatenate([corr, corr], axis=1)

        def shift(d, sidx):
            sgn = 1.0 if sidx == 0 else -1.0
            return (-sgn) * beta * d.astype(f32)

        def scores(n, sidx, crossing):
            r0 = pl.multiple_of(n * tk, tk)
            k_aug = jnp.concatenate([k_ref[pl.ds(r0, tk), :], kext_ref[sidx]], axis=1)
            for u in range(ng):
                c0 = u * 2 * qs_
                s = jnp.dot(k_aug, w[:, c0:c0 + 2 * qs_], preferred_element_type=f32)
                if crossing:
                    s = s + corr_tile(g0 + u * qs_ - r0)
                s_sc[:, c0:c0 + 2 * qs_] = s

        def step(n, mc, sidx, crossing):
            r0 = pl.multiple_of((n - 1) * tk, tk)
            r1 = pl.multiple_of(n * tk, tk)
            vt = vt_sc[:, pl.ds(r0, tk)]
            if sidx is not None:
                k_aug = jnp.concatenate([k_ref[pl.ds(r1, tk), :], kext_ref[sidx]], axis=1)
            m, csts = mc
            ms, cs = [], []
            for u in range(ng):
                c0 = u * 2 * qs_
                s = s_sc[:, c0:c0 + 2 * qs_]
                m_old = m[:, c0:c0 + 2 * qs_]
                cst = csts[:, c0:c0 + 2 * qs_]
                m_new = jnp.maximum(m_old, jnp.max(s, axis=0, keepdims=True) + cst)
                alpha = jnp.exp2(m_old - m_new)
                p = jnp.exp2(s - (m_new - cst)).astype(bf)
                acc_sc[:, c0:c0 + 2 * qs_] = (acc_sc[:, c0:c0 + 2 * qs_] * alpha
                                              + jnp.dot(vt, p, preferred_element_type=f32))
                ms.append(m_new)
                if sidx is not None:
                    s = jnp.dot(k_aug, w[:, c0:c0 + 2 * qs_], preferred_element_type=f32)
                    if crossing:
                        s = s + corr_tile(g0 + u * qs_ - r1)
                    s_sc[:, c0:c0 + 2 * qs_] = s
                    cs.append(jnp.broadcast_to(shift(g0 + u * qs_ - r1, sidx), (1, 2 * qs_)))
            return jnp.concatenate(ms, axis=1), (jnp.concatenate(cs, axis=1) if cs else csts)

        acc_sc[...] = jnp.zeros(acc_sc.shape, f32)
        m = jnp.full((1, ncol), -1e30, f32)
        scores(0, 0, True)
        n1 = jnp.maximum(jc, 1)
        n2 = jc + tq // tk
        cst0 = jnp.concatenate([jnp.broadcast_to(shift(g0 + u * qs_, 0), (1, 2 * qs_))
                                for u in range(ng)], axis=1)
        mc = (m, cst0)
        mc = lax.fori_loop(1, n1, functools.partial(step, sidx=0, crossing=False), mc)
        mc = lax.fori_loop(n1, n2, functools.partial(step, sidx=0, crossing=True), mc)
        mc = lax.fori_loop(n2, nk, functools.partial(step, sidx=1, crossing=False), mc)
        step(nk, mc, None, False)

        for u in range(ng):
            c0 = u * 2 * qs_
            acc = acc_sc[:, c0:c0 + 2 * qs_]
            den = acc[dv:dv + 1, :]
            o = acc[0:dv, 0:qs_] / den[:, 0:qs_] - lam * (acc[0:dv, qs_:] / den[:, qs_:])
            y = o * lax.rsqrt(jnp.mean(o * o, axis=0, keepdims=True) + EPS) * subg_ref[...]
            o_ref[pl.ds(g0 + u * qs_, qs_), :] = (y * (1.0 - lam_init)).T.astype(o_ref.dtype)
        return carry

    lax.fori_loop(0, t // tq, group, 0)


def _attn_tables(tk):
    f32 = jnp.float32
    slopes = 2.0 ** (-8.0 * (jnp.arange(ATT_HEADS, dtype=f32) + 1.0) / ATT_HEADS)
    beta = slopes * _LOG2E
    lane = jnp.arange(LANES)[None, None, :]
    b = _split3(beta)
    a = _split3(-beta[:, None] * jnp.arange(_Q_STRIP, dtype=f32)[None, :])
    qext = jnp.zeros((ATT_HEADS, _Q_STRIP, LANES), f32)
    for c in range(3):
        qext = jnp.where(lane == c, b[c][:, None, None], qext)
        qext = jnp.where(lane == 3 + c, a[c][:, :, None], qext)
    r = jnp.arange(tk, dtype=f32)[:, None]
    lane2 = jnp.arange(LANES)[None, :]
    base = jnp.where(lane2 < 3, r, jnp.where(lane2 < 6, 1.0, 0.0))
    kext = jnp.stack([base, -base]).astype(jnp.bfloat16)
    beta_t = jnp.broadcast_to(beta[:, None, None], (ATT_HEADS, 1, LANES))
    return beta_t, qext, kext


def diff_attention(proj3d, lq1, lk1, lq2, lk2, subg, layer, tk=256, tq=2048):
    b, t, _ = proj3d.shape
    tk, tq = min(tk, t), min(tq, t)
    assert tq % tk == 0 and tq % _Q_STRIP == 0
    lam_init = 0.8 - 0.6 * math.exp(-0.3 * layer)
    assert tk <= 256
    beta_t, qext, kext = _attn_tables(tk)
    vec = lambda a: a.reshape(1, -1).astype(jnp.float32)
    small = lambda n: pl.BlockSpec((1, n), lambda bi, h: (0, 0))
    head = pl.BlockSpec((None, 1, LANES), lambda bi, h: (h, 0, 0))
    col = lambda blk: pl.BlockSpec((None, t, LANES), lambda bi, h: (bi, 0, blk + h))
    return pl.pallas_call(
        functools.partial(_attn_kernel, lam_init=lam_init, tk=tk, tq=tq),
        grid=(b, ATT_HEADS),
        in_specs=[
            head, pl.BlockSpec((None, _Q_STRIP, LANES), lambda bi, h: (h, 0, 0)),
            small(ATT_QK_DIM), small(ATT_QK_DIM), small(ATT_QK_DIM), small(ATT_QK_DIM),
            pl.BlockSpec((ATT_V_DIM, 1), lambda bi, h: (0, 0)),
            col(_QBLK), col(_KBLK), col(_VBLK),
            pl.BlockSpec((2, tk, LANES), lambda bi, h: (0, 0, 0)),
        ],
        out_specs=pl.BlockSpec((None, t, LANES), lambda bi, h: (bi, 0, h)),
        out_shape=jax.ShapeDtypeStruct((b, t, ATT_HEADS * ATT_V_DIM), jnp.bfloat16),
        scratch_shapes=[pltpu.VMEM((ATT_V_DIM + _ONES_ROWS, t), jnp.bfloat16),
                        pltpu.VMEM((ATT_V_DIM + _ONES_ROWS, 2 * tq), jnp.float32),
                        pltpu.VMEM((tk, 2 * tq), jnp.float32)],
        compiler_params=_cparams(("parallel", "parallel")),
        name="diff_attn",
    )(beta_t, qext, vec(lq1), vec(lk1), vec(lq2), vec(lk2),
      subg.reshape(ATT_V_DIM, 1).astype(jnp.float32), proj3d, proj3d, proj3d, kext)


_HG_CHUNKS_PER_ITER = 4
_HG_FIN_ROWS = 256


def _scan_rows(g, rev):
    c = g.shape[0]
    row = lax.broadcasted_iota(jnp.int32, g.shape, 0)
    b = g
    s = 1
    while s < c:
        if rev:
            b = b + jnp.where(row < c - s, pltpu.roll(b, c - s, axis=0), 0.0)
        else:
            b = b + jnp.where(row >= s, pltpu.roll(b, s, axis=0), 0.0)
        s *= 2
    return b


def _level_ref(b, m, rev):
    c, w = b.shape
    r = m // 2 if rev else m // 2 - 1
    pieces = []
    if m >= SUBLANES:
        for j in range(c // m):
            pieces.append(jnp.broadcast_to(b[j * m + r:j * m + r + 1, :], (m, w)))
    else:
        sub = lax.broadcasted_iota(jnp.int32, (SUBLANES, w), 0)
        for j in range(c // SUBLANES):
            base = j * SUBLANES
            acc = jnp.broadcast_to(b[base + r:base + r + 1, :], (SUBLANES, w))
            for i in range(1, SUBLANES // m):
                cand = jnp.broadcast_to(b[base + i * m + r:base + i * m + r + 1, :], (SUBLANES, w))
                acc = jnp.where(sub >= i * m, cand, acc)
            pieces.append(acc)
    return jnp.concatenate(pieces, axis=0)


def _neg_abs(x):
    bits = lax.bitcast_convert_type(x, jnp.uint32) | jnp.uint32(0x80000000)
    return lax.bitcast_convert_type(bits, jnp.float32)


def _hgrn_masks(c, rev):
    row = lax.broadcasted_iota(jnp.int32, (c, LANES), 0)
    ti = lax.broadcasted_iota(jnp.int32, (c, c), 0)
    si = lax.broadcasted_iota(jnp.int32, (c, c), 1)
    qrows, pairs = {}, {}
    m = c
    while m >= 2:
        late = (row & (m - 1)) >= m // 2
        t_late = (ti & (m - 1)) >= m // 2
        s_late = (si & (m - 1)) >= m // 2
        same = (ti ^ si) < m
        if rev:
            qrows[m] = jnp.logical_not(late)
            pairs[m] = same & jnp.logical_not(t_late) & s_late
        else:
            qrows[m] = late
            pairs[m] = same & t_late & jnp.logical_not(s_late)
        m //= 2
    return qrows, pairs, ti == si


def _hgrn_chunk(q, f, v, st_ref, rev, masks):
    qrows, pairs, diag = masks
    c = q.shape[0]
    bf = jnp.bfloat16
    k_ = 1.0 - f
    b = _scan_rows(jnp.log(f) * _LOG2E, rev)
    a = jnp.where(diag,
                  lax.dot_general(q.astype(bf), k_.astype(bf), _NT,
                                  preferred_element_type=jnp.float32), 0.0)
    m = c
    while m >= 2:
        if m == 2:
            cref = jnp.where(qrows[m], pltpu.roll(b, c - 1 if rev else 1, axis=0), b)
        else:
            cref = _level_ref(b, m, rev)
        e = jnp.exp2(_neg_abs(b - cref))
        z = (jnp.where(qrows[m], q, k_) * e).astype(bf)
        p = lax.dot_general(z, z, _NT, preferred_element_type=jnp.float32)
        a = jnp.where(pairs[m], p, a)
        m //= 2
    st = st_ref[...]
    o = jnp.dot(a.astype(bf), v.astype(bf), preferred_element_type=jnp.float32)
    o = o + lax.dot_general((q * jnp.exp2(b)).astype(bf), st.astype(bf), _NT,
                            preferred_element_type=jnp.float32)
    btot = b[0:1, :] if rev else b[c - 1:c, :]
    khat = (k_ * jnp.exp2(btot - b)).astype(bf)
    st_ref[...] = st * jnp.exp2(btot) + jnp.dot(v.T.astype(bf), khat,
                                                preferred_element_type=jnp.float32)
    return o


def _hgrn_kernel(lb_ref, ng_ref, q_ref, ff_ref, fb_ref, i_ref, g_ref, o_ref,
                 of_sc, ob_sc, stf_sc, stb_sc, *, layer, chunk):
    t = q_ref.shape[0]
    n = t // chunk
    lb_raw = lb_ref[...].astype(jnp.float32)
    ex = jnp.exp(lb_raw - jnp.max(lb_raw, axis=0, keepdims=True))
    sm = ex / jnp.sum(ex, axis=0, keepdims=True)
    lb = jnp.zeros_like(sm[0])
    for l_ in range(1, layer + 1):
        lb = lb + sm[l_]
    lb_f = lb[0:1, :]
    lb_b = lb[1:2, :]

    stf_sc[...] = jnp.zeros(stf_sc.shape, jnp.float32)
    stb_sc[...] = jnp.zeros(stb_sc.shape, jnp.float32)

    def load(ref, r0):
        return ref[pl.ds(r0, chunk), :].astype(jnp.float32)

    masks_f = _hgrn_masks(chunk, False)
    masks_b = _hgrn_masks(chunk, True)

    per_iter = _HG_CHUNKS_PER_ITER if n % _HG_CHUNKS_PER_ITER == 0 else 1

    def body(ci, carry):
        for u in range(per_iter):
            r0 = pl.multiple_of((ci * per_iter + u) * chunk, chunk)
            qr = load(q_ref, r0)
            f = lb_f + (1.0 - lb_f) * jax.nn.sigmoid(load(ff_ref, r0))
            of_sc[pl.ds(r0, chunk), :] = _hgrn_chunk(qr * jax.nn.sigmoid(qr), f,
                                                     load(i_ref, r0), stf_sc, False, masks_f)
            r1 = pl.multiple_of((n - 1 - ci * per_iter - u) * chunk, chunk)
            qr = load(q_ref, r1)
            f = lb_b + (1.0 - lb_b) * jax.nn.sigmoid(load(fb_ref, r1))
            ob_sc[pl.ds(r1, chunk), :] = _hgrn_chunk(qr * jax.nn.sigmoid(qr), f,
                                                     load(i_ref, r1), stb_sc, True, masks_b)
        return carry

    lax.fori_loop(0, n // per_iter, body, 0)

    rows = math.gcd(t, _HG_FIN_ROWS)

    def fin(ci, carry):
        r0 = pl.multiple_of(ci * rows, rows)
        o = of_sc[pl.ds(r0, rows), :] + ob_sc[pl.ds(r0, rows), :]
        y = o * lax.rsqrt(jnp.mean(o * o, axis=-1, keepdims=True) + EPS) * ng_ref[...]
        gr = g_ref[pl.ds(r0, rows), :].astype(jnp.float32)
        o_ref[pl.ds(r0, rows), :] = (y * (gr * jax.nn.sigmoid(gr))).astype(o_ref.dtype)
        return carry

    lax.fori_loop(0, t // rows, fin, 0)


def hgrn_bidir(proj3d, hgrn_lb, norm_g, layer):
    b, t, _ = proj3d.shape
    depth = hgrn_lb.shape[0]
    col = lambda blk: pl.BlockSpec((None, t, LANES), lambda bi, h: (bi, 0, blk + h))
    return pl.pallas_call(
        functools.partial(_hgrn_kernel, layer=layer, chunk=HG_CHUNK),
        grid=(b, HG_HEADS),
        in_specs=[
            pl.BlockSpec((depth, 2, LANES), lambda bi, h: (0, 0, h)),
            pl.BlockSpec((1, HG_V), lambda bi, h: (0, 0)),
            col(_HQ), col(_HFF), col(_HFB), col(_HI), col(_HG),
        ],
        out_specs=pl.BlockSpec((None, t, LANES), lambda bi, h: (bi, 0, h)),
        out_shape=jax.ShapeDtypeStruct((b, t, HG_HEADS * HG_V), jnp.bfloat16),
        scratch_shapes=[
            pltpu.VMEM((t, HG_V), jnp.float32),
            pltpu.VMEM((t, HG_V), jnp.float32),
            pltpu.VMEM((HG_V, HG_K), jnp.float32),
            pltpu.VMEM((HG_V, HG_K), jnp.float32),
        ],
        compiler_params=_cparams(("parallel", "parallel")),
        name="hgrn_bidir",
    )(hgrn_lb, norm_g.reshape(1, HG_V).astype(jnp.float32),
      proj3d, proj3d, proj3d, proj3d, proj3d)


def _out_proj_kernel(a_ref, h_ref, wa_ref, wh_ref, x_ref, o_ref):
    y = jnp.dot(a_ref[...], wa_ref[...], preferred_element_type=jnp.float32)
    y = y + jnp.dot(h_ref[...], wh_ref[...], preferred_element_type=jnp.float32)
    o_ref[...] = x_ref[...] + y


def out_proj(att2d, hgo2d, w_out_bf16, x2d, tm=512):
    n, d = x2d.shape
    tm = min(tm, n)
    wa = att2d.shape[1]
    wh = hgo2d.shape[1]
    return pl.pallas_call(
        _out_proj_kernel,
        grid=(n // tm,),
        in_specs=[
            pl.BlockSpec((tm, wa), lambda i: (i, 0)),
            pl.BlockSpec((tm, wh), lambda i: (i, 0)),
            pl.BlockSpec((wa, d), lambda i: (0, 0)),
            pl.BlockSpec((wh, d), lambda i: (wa // wh, 0)),
            pl.BlockSpec((tm, d), lambda i: (i, 0)),
        ],
        out_specs=pl.BlockSpec((tm, d), lambda i: (i, 0)),
        out_shape=jax.ShapeDtypeStruct((n, d), jnp.float32),
        compiler_params=_cparams(("parallel",)),
        name="out_proj",
    )(att2d, hgo2d, w_out_bf16, w_out_bf16, x2d)


def _router_kernel(x_ref, g_ref, wrt_ref, h_ref, afft_ref):
    x = x_ref[...]
    h = x * lax.rsqrt(jnp.mean(x * x, axis=-1, keepdims=True) + EPS) * g_ref[...]
    h_ref[...] = h.astype(h_ref.dtype)
    logits = lax.dot_general(wrt_ref[...], h, _NT, preferred_element_type=jnp.float32,
                             precision=lax.Precision.HIGHEST)
    ex = jnp.exp(logits - jnp.max(logits, axis=0, keepdims=True))
    afft_ref[...] = ex / jnp.sum(ex, axis=0, keepdims=True)


def router(x2d, g, w_router, tm=512):
    n, d = x2d.shape
    tm = min(tm, n)
    e = w_router.shape[1]
    return pl.pallas_call(
        _router_kernel,
        grid=(n // tm,),
        in_specs=[
            pl.BlockSpec((tm, d), lambda i: (i, 0)),
            pl.BlockSpec((1, d), lambda i: (0, 0)),
            pl.BlockSpec((e, d), lambda i: (0, 0)),
        ],
        out_specs=[
            pl.BlockSpec((tm, d), lambda i: (i, 0)),
            pl.BlockSpec((e, tm), lambda i: (0, i)),
        ],
        out_shape=[
            jax.ShapeDtypeStruct((n, d), jnp.bfloat16),
            jax.ShapeDtypeStruct((e, n), jnp.float32),
        ],
        compiler_params=_cparams(("parallel",)),
        name="router",
    )(x2d, g.reshape(1, d), w_router.T)


_SEL_CHUNK = 1024


def _select_kernel(afft_ref, pos_ref, offs_ref, *, cap):
    f32, i32 = jnp.float32, jnp.int32
    e, n = afft_ref.shape
    chunk = min(_SEL_CHUNK, n)
    nt = n // LANES

    def bits_at(start, size):
        return lax.bitcast_convert_type(afft_ref[:, pl.ds(start, size)], i32)

    def count_ge(cand):
        def body(c, acc):
            x = bits_at(pl.multiple_of(c * chunk, chunk), chunk)
            return acc + jnp.where(x >= cand, 1.0, 0.0)
        acc = lax.fori_loop(0, n // chunk, body, jnp.zeros((e, chunk), f32))
        return jnp.sum(acc, axis=1, keepdims=True)

    def bit_body(it, prefix):
        cand = prefix | jnp.left_shift(jnp.int32(1), 30 - it)
        return jnp.where(count_ge(cand) >= cap, cand, prefix)

    thr = lax.fori_loop(0, 31, bit_body, jnp.zeros((e, 1), i32))
    need = cap - count_ge(thr + 1)

    tri = (lax.broadcasted_iota(i32, (LANES, LANES), 0)
           <= lax.broadcasted_iota(i32, (LANES, LANES), 1)).astype(jnp.bfloat16)
    tile_id = lax.broadcasted_iota(i32, (e, nt), 1)

    def tile_body(j, carry):
        c_gt, c_eq, offs_acc = carry
        x = bits_at(pl.multiple_of(j * LANES, LANES), LANES)
        gt = x > thr
        eq = x == thr
        both = jnp.concatenate([jnp.where(gt, 1.0, 0.0), jnp.where(eq, 1.0, 0.0)],
                               axis=0).astype(jnp.bfloat16)
        inc = jnp.dot(both, tri, preferred_element_type=f32)
        inc_gt = inc[:e] + c_gt
        inc_eq = inc[e:] + c_eq
        sel = gt | (eq & (inc_eq <= need))
        incl = inc_gt + jnp.minimum(inc_eq, need)
        pos_ref[:, pl.ds(pl.multiple_of(j * LANES, LANES), LANES)] = jnp.where(
            sel, incl - 1.0, -1.0).astype(i32)
        offs_acc = jnp.where(tile_id == j, c_gt + jnp.minimum(c_eq, need), offs_acc)
        return inc_gt[:, LANES - 1:LANES], inc_eq[:, LANES - 1:LANES], offs_acc

    zero = jnp.zeros((e, 1), f32)
    _, _, offs_acc = lax.fori_loop(0, nt, tile_body, (zero, zero, jnp.zeros((e, nt), f32)))
    offs_ref[...] = offs_acc.astype(i32)


def ec_select(afft, cap):
    e, n = afft.shape
    return pl.pallas_call(
        functools.partial(_select_kernel, cap=cap),
        out_shape=[jax.ShapeDtypeStruct((e, n), jnp.int32),
                   jax.ShapeDtypeStruct((e, n // LANES), jnp.int32)],
        compiler_params=pltpu.CompilerParams(vmem_limit_bytes=_VMEM_LIMIT),
        name="ec_select",
    )(afft)


_SC_LANES = 16
_SC_CORES = 2
_INV_CHUNK = 8192


def ec_invert(pos, cap):
    ne, n = pos.shape
    ch = min(_INV_CHUNK, n)
    mesh = plsc.VectorSubcoreMesh(core_axis_name="c", subcore_axis_name="s")

    @functools.partial(
        pl.kernel, mesh=mesh,
        out_type=jax.ShapeDtypeStruct((ne * cap,), jnp.int32),
        scratch_types=[pltpu.VMEM((ch,), jnp.int32), pltpu.VMEM((cap,), jnp.int32)],
        compiler_params=pltpu.CompilerParams(needs_layout_passes=False))
    def invert(pos_hbm, idx_hbm, pos_v, idx_v):
        wid = lax.axis_index("s") * _SC_CORES + lax.axis_index("c")

        @pl.when(wid < ne)
        def _():
            @pl.loop(0, n // ch)
            def _(c):
                pltpu.sync_copy(pos_hbm.at[pl.ds(wid * n + c * ch, ch)], pos_v)

                @pl.loop(0, ch, step=_SC_LANES)
                def _(i):
                    p = pos_v[pl.ds(i, _SC_LANES)]
                    tok = lax.iota(jnp.int32, _SC_LANES) + (c * ch + i)
                    plsc.store_scatter(idx_v, [p], tok, mask=p >= 0)

            pltpu.sync_copy(idx_v, idx_hbm.at[pl.ds(wid * cap, cap)])

    return invert(pos.reshape(ne * n)).reshape(ne, cap)


def _expert_kernel(xs_ref, wg_ref, wu_ref, wd_ref, o_ref):
    @pl.when(pl.program_id(1) == pl.num_programs(1) - 1)
    def _():
        o_ref[...] = jnp.zeros(o_ref.shape, o_ref.dtype)

    @pl.when(pl.program_id(1) < pl.num_programs(1) - 1)
    def _():
        xs = xs_ref[...]
        a = jnp.dot(xs, wg_ref[...], preferred_element_type=jnp.float32)
        u = jnp.dot(xs, wu_ref[...], preferred_element_type=jnp.float32)
        hid = (a * jax.nn.sigmoid(a) * u).astype(jnp.bfloat16)
        o_ref[...] = jnp.dot(hid, wd_ref[...],
                             preferred_element_type=jnp.float32).astype(o_ref.dtype)


def expert_ffn(xs, wg, wu, wd, tm=512):
    e, c, d = xs.shape
    tm = min(tm, c)
    f = wg.shape[2]
    nt = c // tm
    return pl.pallas_call(
        _expert_kernel,
        grid=(e, nt + 1),
        in_specs=[
            pl.BlockSpec((None, tm, d), lambda ei, i: (ei, jnp.minimum(i, nt - 1), 0)),
            pl.BlockSpec((None, d, f), lambda ei, i: (ei, 0, 0)),
            pl.BlockSpec((None, d, f), lambda ei, i: (ei, 0, 0)),
            pl.BlockSpec((None, f, d), lambda ei, i: (ei, 0, 0)),
        ],
        out_specs=pl.BlockSpec((None, tm, d), lambda ei, i: (ei, i, 0)),
        out_shape=jax.ShapeDtypeStruct((e, c + tm, d), jnp.bfloat16),
        compiler_params=_cparams(("parallel", "arbitrary")),
        name="expert_ffn",
    )(xs, wg, wu, wd)


def _combine_kernel(offs_ref, x_ref, pos_ref, afft_ref, *rest):
    f32 = jnp.float32
    ne = pos_ref.shape[0]
    tiles, o_ref = rest[:2 * ne], rest[2 * ne]
    j = pl.program_id(0)
    pad = jnp.zeros((LANES - ne, LANES), f32)
    pos_t = jnp.concatenate([pos_ref[...].astype(f32), pad], axis=0).T
    gate_t = jnp.concatenate([afft_ref[...], pad], axis=0).T
    row_id = lax.broadcasted_iota(jnp.int32, (LANES, 2 * LANES), 1).astype(f32)
    y = x_ref[...]
    for e in range(ne):
        base = (offs_ref[e, j] // LANES) * LANES
        rel = pos_t[:, e:e + 1] - base.astype(f32)
        onehot = jnp.where(rel == row_id, 1.0, 0.0).astype(jnp.bfloat16)
        rows = jnp.concatenate([tiles[2 * e][...], tiles[2 * e + 1][...]], axis=0)
        y = y + jnp.dot(onehot, rows, preferred_element_type=f32) * gate_t[:, e:e + 1]
    o_ref[...] = y


def ec_combine(x2d, pos, afft, offs, out):
    n, d = x2d.shape
    ne, crows, _ = out.shape
    nblk = crows // LANES

    def tile_spec(e, k):
        return pl.BlockSpec(
            (None, LANES, d),
            lambda j, offs_ref: (e, jnp.minimum(offs_ref[e, j] // LANES + k, nblk - 1), 0))

    grid_spec = pltpu.PrefetchScalarGridSpec(
        num_scalar_prefetch=1,
        grid=(n // LANES,),
        in_specs=[
            pl.BlockSpec((LANES, d), lambda j, offs_ref: (j, 0)),
            pl.BlockSpec((ne, LANES), lambda j, offs_ref: (0, j)),
            pl.BlockSpec((ne, LANES), lambda j, offs_ref: (0, j)),
        ] + [tile_spec(e, k) for e in range(ne) for k in range(2)],
        out_specs=pl.BlockSpec((LANES, d), lambda j, offs_ref: (j, 0)),
    )
    return pl.pallas_call(
        _combine_kernel,
        grid_spec=grid_spec,
        out_shape=jax.ShapeDtypeStruct((n, d), jnp.float32),
        compiler_params=_cparams(("arbitrary",)),
        name="ec_combine",
    )(offs, x2d, pos, afft, *([out] * (2 * ne)))


def _rms_kernel(x_ref, g_ref, o_ref):
    x = x_ref[...]
    o_ref[...] = x * lax.rsqrt(jnp.mean(x * x, axis=-1, keepdims=True) + EPS) * g_ref[...]


def rms_norm_call(x2d, g, tm=1024):
    n, d = x2d.shape
    tm = min(tm, n)
    return pl.pallas_call(
        _rms_kernel,
        grid=(n // tm,),
        in_specs=[pl.BlockSpec((tm, d), lambda i: (i, 0)),
                  pl.BlockSpec((1, d), lambda i: (0, 0))],
        out_specs=pl.BlockSpec((tm, d), lambda i: (i, 0)),
        out_shape=jax.ShapeDtypeStruct((n, d), jnp.float32),
        compiler_params=_cparams(("parallel",)),
        name="final_norm",
    )(x2d, g.reshape(1, d))


def _trunk(x, p):
    b, t, d = x.shape
    n = b * t
    depth = p["w_in"].shape[0]
    x2d = x.reshape(n, d)
    cap = max(1, EC_CAPACITY * n // N_EXPERTS)
    for layer in range(depth):
        proj = norm_proj(x2d, p["norm_mix_g"][layer], p["w_in"][layer])
        proj3d = proj.reshape(b, t, -1)
        att = diff_attention(proj3d, p["lambda_q1"][layer], p["lambda_k1"][layer],
                             p["lambda_q2"][layer], p["lambda_k2"][layer],
                             p["diff_subln_g"][layer], layer)
        hgo = hgrn_bidir(proj3d, p["hgrn_lb"], p["hgrn_norm_g"][layer], layer)
        x2d = out_proj(att.reshape(n, -1), hgo.reshape(n, -1), p["w_out"][layer], x2d)
        h, afft = router(x2d, p["norm_ffn_g"][layer], p["w_router"][layer])
        pos, offs = ec_select(afft, cap)
        idx = ec_invert(pos, cap)
        out = expert_ffn(h[idx], p["w_gate"][layer], p["w_up"][layer], p["w_down"][layer])
        x2d = ec_combine(x2d, pos, afft, offs, out)
    return rms_norm_call(x2d, p["final_norm_g"]).reshape(b, t, d)


def kernel(x_prompt, x_sample, norm_mix_g, w_in, lambda_q1, lambda_k1, lambda_q2, lambda_k2,
           diff_subln_g, hgrn_lb, hgrn_norm_g, w_out, norm_ffn_g, w_router, w_gate, w_up,
           w_down, final_norm_g):
    bf = jnp.bfloat16
    p = dict(norm_mix_g=norm_mix_g, w_in=w_in.astype(bf), lambda_q1=lambda_q1,
             lambda_k1=lambda_k1, lambda_q2=lambda_q2, lambda_k2=lambda_k2,
             diff_subln_g=diff_subln_g, hgrn_lb=hgrn_lb, hgrn_norm_g=hgrn_norm_g,
             w_out=w_out.astype(bf), norm_ffn_g=norm_ffn_g, w_router=w_router,
             w_gate=w_gate.astype(bf), w_up=w_up.astype(bf), w_down=w_down.astype(bf),
             final_norm_g=final_norm_g)
    return (_trunk(x_prompt, p), _trunk(x_sample, p))
```

```python
import functools
import math

import jax
import jax.numpy as jnp
from jax import lax
from jax.experimental import pallas as pl
from jax.experimental.pallas import tpu as pltpu
from jax.experimental.pallas import tpu_sc as plsc

ATT_HEADS = 4
ATT_QK_DIM = 64
ATT_V_DIM = 128
HG_HEADS = 4
HG_K = 128
HG_V = 128
HG_CHUNK = 64
N_EXPERTS = 16
EC_CAPACITY = 2
EPS = 1e-6
LANES = 128
SUBLANES = 8

_QBLK, _KBLK, _VBLK = 0, 4, 8
_HQ, _HFF, _HFB, _HI, _HG = 12, 16, 20, 24, 28

_VMEM_LIMIT = 56 * 1024 * 1024

_NT = (((1,), (1,)), ((), ()))


def _cparams(sem):
    return pltpu.CompilerParams(dimension_semantics=sem, vmem_limit_bytes=_VMEM_LIMIT)


def _norm_proj_kernel(x_ref, g_ref, w_ref, o_ref, *, n_split):
    x = x_ref[...]
    y = x * lax.rsqrt(jnp.mean(x * x, axis=-1, keepdims=True) + EPS) * g_ref[...]
    h = y.astype(jnp.bfloat16)
    wn = w_ref.shape[1] // n_split
    for c in range(n_split):
        o_ref[:, c * wn:(c + 1) * wn] = jnp.dot(
            h, w_ref[:, c * wn:(c + 1) * wn], preferred_element_type=jnp.float32
        ).astype(o_ref.dtype)


def norm_proj(x2d, g, w_bf16, tm=512):
    n, d = x2d.shape
    tm = min(tm, n)
    dout = w_bf16.shape[1]
    return pl.pallas_call(
        functools.partial(_norm_proj_kernel, n_split=max(1, dout // 1024)),
        grid=(n // tm,),
        in_specs=[
            pl.BlockSpec((tm, d), lambda i: (i, 0)),
            pl.BlockSpec((1, d), lambda i: (0, 0)),
            pl.BlockSpec((d, dout), lambda i: (0, 0)),
        ],
        out_specs=pl.BlockSpec((tm, dout), lambda i: (i, 0)),
        out_shape=jax.ShapeDtypeStruct((n, dout), jnp.bfloat16),
        compiler_params=_cparams(("parallel",)),
        name="norm_proj",
    )(x2d, g.reshape(1, d), w_bf16)


_LOG2E = 1.4426950408889634
_Q_STRIP = 128
_ONES_ROWS = 16


def _split3(x):
    bf = jnp.bfloat16
    h = x.astype(bf).astype(jnp.float32)
    r = x - h
    m = r.astype(bf).astype(jnp.float32)
    lo = (r - m).astype(bf).astype(jnp.float32)
    return h, m, lo


def _attn_kernel(beta_ref, qext_ref, lq1_ref, lk1_ref, lq2_ref, lk2_ref, subg_ref,
                 q_ref, k_ref, v_ref, kext_ref, o_ref, vt_sc, acc_sc, s_sc, *, lam_init, tk, tq):
    f32, bf = jnp.float32, jnp.bfloat16
    t = q_ref.shape[0]
    nk = t // tk
    qs_ = _Q_STRIP
    ng = tq // qs_
    ncol = 2 * tq
    dv = v_ref.shape[1]
    beta = beta_ref[0:1, 0:1]

    def vt_body(c, carry):
        r0 = pl.multiple_of(c * tk, tk)
        vt_sc[0:dv, pl.ds(r0, tk)] = v_ref[pl.ds(r0, tk), :].astype(f32).T.astype(bf)
        return carry

    lax.fori_loop(0, nk, vt_body, 0)
    vt_sc[dv:dv + _ONES_ROWS, :] = jnp.ones((_ONES_ROWS, t), bf)

    lam = (jnp.exp(jnp.sum(lq1_ref[...] * lk1_ref[...], axis=-1, keepdims=True))
           - jnp.exp(jnp.sum(lq2_ref[...] * lk2_ref[...], axis=-1, keepdims=True)) + lam_init)
    lane = lax.broadcasted_iota(jnp.int32, (qs_, LANES), 1)
    kr = lax.broadcasted_iota(jnp.int32, (tk, qs_), 0)
    qc = lax.broadcasted_iota(jnp.int32, (tk, qs_), 1)
    cscale = ATT_QK_DIM ** -0.5 * _LOG2E

    def group(gi, carry):
        g0 = pl.multiple_of(gi * tq, tq)
        tiles = []
        for u in range(ng):
            q0 = g0 + u * qs_
            qs = (q_ref[pl.ds(q0, qs_), :].astype(f32) * cscale).astype(bf).astype(f32)
            qext = qext_ref[...]
            q_aug = jnp.concatenate([
                jnp.concatenate([jnp.where(lane < ATT_QK_DIM, qs, 0.0), qext], axis=1),
                jnp.concatenate([jnp.where(lane >= ATT_QK_DIM, qs, 0.0), qext], axis=1)], axis=0)
            tiles.append(q_aug.T.astype(bf))
        w = jnp.concatenate(tiles, axis=1)
        jc = g0 // tk

        def corr_tile(d):
            corr = 2.0 * beta * jnp.minimum((d + qc - kr).astype(f32), 0.0)
            return jnp.concatenate([corr, corr], axis=1)

        def shift(d, sidx):
            sgn = 1.0 if sidx == 0 else -1.0
            return (-sgn) * beta * d.astype(f32)

        def scores(n, sidx, crossing):
            r0 = pl.multiple_of(n * tk, tk)
            k_aug = jnp.concatenate([k_ref[pl.ds(r0, tk), :], kext_ref[sidx]], axis=1)
            for u in range(ng):
                c0 = u * 2 * qs_
                s = jnp.dot(k_aug, w[:, c0:c0 + 2 * qs_], preferred_element_type=f32)
                if crossing:
                    s = s + corr_tile(g0 + u * qs_ - r0)
                s_sc[:, c0:c0 + 2 * qs_] = s

        def step(n, mc, sidx, crossing):
            r0 = pl.multiple_of((n - 1) * tk, tk)
            r1 = pl.multiple_of(n * tk, tk)
            vt = vt_sc[:, pl.ds(r0, tk)]
            if sidx is not None:
                k_aug = jnp.concatenate([k_ref[pl.ds(r1, tk), :], kext_ref[sidx]], axis=1)
            m, csts = mc
            ms, cs = [], []
            for u in range(ng):
                c0 = u * 2 * qs_
                s = s_sc[:, c0:c0 + 2 * qs_]
                m_old = m[:, c0:c0 + 2 * qs_]
                cst = csts[:, c0:c0 + 2 * qs_]
                m_new = jnp.maximum(m_old, jnp.max(s, axis=0, keepdims=True) + cst)
                alpha = jnp.exp2(m_old - m_new)
                p = jnp.exp2(s - (m_new - cst)).astype(bf)
                acc_sc[:, c0:c0 + 2 * qs_] = (acc_sc[:, c0:c0 + 2 * qs_] * alpha
                                              + jnp.dot(vt, p, preferred_element_type=f32))
                ms.append(m_new)
                if sidx is not None:
                    s = jnp.dot(k_aug, w[:, c0:c0 + 2 * qs_], preferred_element_type=f32)
                    if crossing:
                        s = s + corr_tile(g0 + u * qs_ - r1)
                    s_sc[:, c0:c0 + 2 * qs_] = s
                    cs.append(jnp.broadcast_to(shift(g0 + u * qs_ - r1, sidx), (1, 2 * qs_)))
            return jnp.concatenate(ms, axis=1), (jnp.concatenate(cs, axis=1) if cs else csts)

        acc_sc[...] = jnp.zeros(acc_sc.shape, f32)
        m = jnp.full((1, ncol), -1e30, f32)
        scores(0, 0, True)
        n1 = jnp.maximum(jc, 1)
        n2 = jc + tq // tk
        cst0 = jnp.concatenate([jnp.broadcast_to(shift(g0 + u * qs_, 0), (1, 2 * qs_))
                                for u in range(ng)], axis=1)
        mc = (m, cst0)
        mc = lax.fori_loop(1, n1, functools.partial(step, sidx=0, crossing=False), mc)
        mc = lax.fori_loop(n1, n2, functools.partial(step, sidx=0, crossing=True), mc)
        mc = lax.fori_loop(n2, nk, functools.partial(step, sidx=1, crossing=False), mc)
        step(nk, mc, None, False)

        for u in range(ng):
            c0 = u * 2 * qs_
            acc = acc_sc[:, c0:c0 + 2 * qs_]
            den = acc[dv:dv + 1, :]
            o = acc[0:dv, 0:qs_] / den[:, 0:qs_] - lam * (acc[0:dv, qs_:] / den[:, qs_:])
            y = o * lax.rsqrt(jnp.mean(o * o, axis=0, keepdims=True) + EPS) * subg_ref[...]
            o_ref[pl.ds(g0 + u * qs_, qs_), :] = (y * (1.0 - lam_init)).T.astype(o_ref.dtype)
        return carry

    lax.fori_loop(0, t // tq, group, 0)


def _attn_tables(tk):
    f32 = jnp.float32
    slopes = 2.0 ** (-8.0 * (jnp.arange(ATT_HEADS, dtype=f32) + 1.0) / ATT_HEADS)
    beta = slopes * _LOG2E
    lane = jnp.arange(LANES)[None, None, :]
    b = _split3(beta)
    a = _split3(-beta[:, None] * jnp.arange(_Q_STRIP, dtype=f32)[None, :])
    qext = jnp.zeros((ATT_HEADS, _Q_STRIP, LANES), f32)
    for c in range(3):
        qext = jnp.where(lane == c, b[c][:, None, None], qext)
        qext = jnp.where(lane == 3 + c, a[c][:, :, None], qext)
    r = jnp.arange(tk, dtype=f32)[:, None]
    lane2 = jnp.arange(LANES)[None, :]
    base = jnp.where(lane2 < 3, r, jnp.where(lane2 < 6, 1.0, 0.0))
    kext = jnp.stack([base, -base]).astype(jnp.bfloat16)
    beta_t = jnp.broadcast_to(beta[:, None, None], (ATT_HEADS, 1, LANES))
    return beta_t, qext, kext


def diff_attention(proj3d, lq1, lk1, lq2, lk2, subg, layer, tk=256, tq=2048):
    b, t, _ = proj3d.shape
    tk, tq = min(tk, t), min(tq, t)
    assert tq % tk == 0 and tq % _Q_STRIP == 0
    lam_init = 0.8 - 0.6 * math.exp(-0.3 * layer)
    assert tk <= 256
    beta_t, qext, kext = _attn_tables(tk)
    vec = lambda a: a.reshape(1, -1).astype(jnp.float32)
    small = lambda n: pl.BlockSpec((1, n), lambda bi, h: (0, 0))
    head = pl.BlockSpec((None, 1, LANES), lambda bi, h: (h, 0, 0))
    col = lambda blk: pl.BlockSpec((None, t, LANES), lambda bi, h: (bi, 0, blk + h))
    return pl.pallas_call(
        functools.partial(_attn_kernel, lam_init=lam_init, tk=tk, tq=tq),
        grid=(b, ATT_HEADS),
        in_specs=[
            head, pl.BlockSpec((None, _Q_STRIP, LANES), lambda bi, h: (h, 0, 0)),
            small(ATT_QK_DIM), small(ATT_QK_DIM), small(ATT_QK_DIM), small(ATT_QK_DIM),
            pl.BlockSpec((ATT_V_DIM, 1), lambda bi, h: (0, 0)),
            col(_QBLK), col(_KBLK), col(_VBLK),
            pl.BlockSpec((2, tk, LANES), lambda bi, h: (0, 0, 0)),
        ],
        out_specs=pl.BlockSpec((None, t, LANES), lambda bi, h: (bi, 0, h)),
        out_shape=jax.ShapeDtypeStruct((b, t, ATT_HEADS * ATT_V_DIM), jnp.bfloat16),
        scratch_shapes=[pltpu.VMEM((ATT_V_DIM + _ONES_ROWS, t), jnp.bfloat16),
                        pltpu.VMEM((ATT_V_DIM + _ONES_ROWS, 2 * tq), jnp.float32),
                        pltpu.VMEM((tk, 2 * tq), jnp.float32)],
        compiler_params=_cparams(("parallel", "parallel")),
        name="diff_attn",
    )(beta_t, qext, vec(lq1), vec(lk1), vec(lq2), vec(lk2),
      subg.reshape(ATT_V_DIM, 1).astype(jnp.float32), proj3d, proj3d, proj3d, kext)


_HG_CHUNKS_PER_ITER = 4
_HG_FIN_ROWS = 256


def _scan_rows(g, rev):
    c = g.shape[0]
    row = lax.broadcasted_iota(jnp.int32, g.shape, 0)
    b = g
    s = 1
    while s < c:
        if rev:
            b = b + jnp.where(row < c - s, pltpu.roll(b, c - s, axis=0), 0.0)
        else:
            b = b + jnp.where(row >= s, pltpu.roll(b, s, axis=0), 0.0)
        s *= 2
    return b


def _level_ref(b, m, rev):
    c, w = b.shape
    r = m // 2 if rev else m // 2 - 1
    pieces = []
    if m >= SUBLANES:
        for j in range(c // m):
            pieces.append(jnp.broadcast_to(b[j * m + r:j * m + r + 1, :], (m, w)))
    else:
        sub = lax.broadcasted_iota(jnp.int32, (SUBLANES, w), 0)
        for j in range(c // SUBLANES):
            base = j * SUBLANES
            acc = jnp.broadcast_to(b[base + r:base + r + 1, :], (SUBLANES, w))
            for i in range(1, SUBLANES // m):
                cand = jnp.broadcast_to(b[base + i * m + r:base + i * m + r + 1, :], (SUBLANES, w))
                acc = jnp.where(sub >= i * m, cand, acc)
            pieces.append(acc)
    return jnp.concatenate(pieces, axis=0)


def _neg_abs(x):
    bits = lax.bitcast_convert_type(x, jnp.uint32) | jnp.uint32(0x80000000)
    return lax.bitcast_convert_type(bits, jnp.float32)


def _hgrn_masks(c, rev):
    row = lax.broadcasted_iota(jnp.int32, (c, LANES), 0)
    ti = lax.broadcasted_iota(jnp.int32, (c, c), 0)
    si = lax.broadcasted_iota(jnp.int32, (c, c), 1)
    qrows, pairs = {}, {}
    m = c
    while m >= 2:
        late = (row & (m - 1)) >= m // 2
        t_late = (ti & (m - 1)) >= m // 2
        s_late = (si & (m - 1)) >= m // 2
        same = (ti ^ si) < m
        if rev:
            qrows[m] = jnp.logical_not(late)
            pairs[m] = same & jnp.logical_not(t_late) & s_late
        else:
            qrows[m] = late
            pairs[m] = same & t_late & jnp.logical_not(s_late)
        m //= 2
    return qrows, pairs, ti == si


def _hgrn_chunk(q, f, v, st_ref, rev, masks):
    qrows, pairs, diag = masks
    c = q.shape[0]
    bf = jnp.bfloat16
    k_ = 1.0 - f
    b = _scan_rows(jnp.log(f) * _LOG2E, rev)
    a = jnp.where(diag,
                  lax.dot_general(q.astype(bf), k_.astype(bf), _NT,
                                  preferred_element_type=jnp.float32), 0.0)
    m = c
    while m >= 2:
        if m == 2:
            cref = jnp.where(qrows[m], pltpu.roll(b, c - 1 if rev else 1, axis=0), b)
        else:
            cref = _level_ref(b, m, rev)
        e = jnp.exp2(_neg_abs(b - cref))
        z = (jnp.where(qrows[m], q, k_) * e).astype(bf)
        p = lax.dot_general(z, z, _NT, preferred_element_type=jnp.float32)
        a = jnp.where(pairs[m], p, a)
        m //= 2
    st = st_ref[...]
    o = jnp.dot(a.astype(bf), v.astype(bf), preferred_element_type=jnp.float32)
    o = o + lax.dot_general((q * jnp.exp2(b)).astype(bf), st.astype(bf), _NT,
                            preferred_element_type=jnp.float32)
    btot = b[0:1, :] if rev else b[c - 1:c, :]
    khat = (k_ * jnp.exp2(btot - b)).astype(bf)
    st_ref[...] = st * jnp.exp2(btot) + jnp.dot(v.T.astype(bf), khat,
                                                preferred_element_type=jnp.float32)
    return o


def _hgrn_kernel(lb_ref, ng_ref, q_ref, ff_ref, fb_ref, i_ref, g_ref, o_ref,
                 of_sc, ob_sc, stf_sc, stb_sc, *, layer, chunk):
    t = q_ref.shape[0]
    n = t // chunk
    lb_raw = lb_ref[...].astype(jnp.float32)
    ex = jnp.exp(lb_raw - jnp.max(lb_raw, axis=0, keepdims=True))
    sm = ex / jnp.sum(ex, axis=0, keepdims=True)
    lb = jnp.zeros_like(sm[0])
    for l_ in range(1, layer + 1):
        lb = lb + sm[l_]
    lb_f = lb[0:1, :]
    lb_b = lb[1:2, :]

    stf_sc[...] = jnp.zeros(stf_sc.shape, jnp.float32)
    stb_sc[...] = jnp.zeros(stb_sc.shape, jnp.float32)

    def load(ref, r0):
        return ref[pl.ds(r0, chunk), :].astype(jnp.float32)

    masks_f = _hgrn_masks(chunk, False)
    masks_b = _hgrn_masks(chunk, True)

    per_iter = _HG_CHUNKS_PER_ITER if n % _HG_CHUNKS_PER_ITER == 0 else 1

    def body(ci, carry):
        for u in range(per_iter):
            r0 = pl.multiple_of((ci * per_iter + u) * chunk, chunk)
            qr = load(q_ref, r0)
            f = lb_f + (1.0 - lb_f) * jax.nn.sigmoid(load(ff_ref, r0))
            of_sc[pl.ds(r0, chunk), :] = _hgrn_chunk(qr * jax.nn.sigmoid(qr), f,
                                                     load(i_ref, r0), stf_sc, False, masks_f)
            r1 = pl.multiple_of((n - 1 - ci * per_iter - u) * chunk, chunk)
            qr = load(q_ref, r1)
            f = lb_b + (1.0 - lb_b) * jax.nn.sigmoid(load(fb_ref, r1))
            ob_sc[pl.ds(r1, chunk), :] = _hgrn_chunk(qr * jax.nn.sigmoid(qr), f,
                                                     load(i_ref, r1), stb_sc, True, masks_b)
        return carry

    lax.fori_loop(0, n // per_iter, body, 0)

    rows = math.gcd(t, _HG_FIN_ROWS)

    def fin(ci, carry):
        r0 = pl.multiple_of(ci * rows, rows)
        o = of_sc[pl.ds(r0, rows), :] + ob_sc[pl.ds(r0, rows), :]
        y = o * lax.rsqrt(jnp.mean(o * o, axis=-1, keepdims=True) + EPS) * ng_ref[...]
        gr = g_ref[pl.ds(r0, rows), :].astype(jnp.float32)
        o_ref[pl.ds(r0, rows), :] = (y * (gr * jax.nn.sigmoid(gr))).astype(o_ref.dtype)
        return carry

    lax.fori_loop(0, t // rows, fin, 0)


def hgrn_bidir(proj3d, hgrn_lb, norm_g, layer):
    b, t, _ = proj3d.shape
    depth = hgrn_lb.shape[0]
    col = lambda blk: pl.BlockSpec((None, t, LANES), lambda bi, h: (bi, 0, blk + h))
    return pl.pallas_call(
        functools.partial(_hgrn_kernel, layer=layer, chunk=HG_CHUNK),
        grid=(b, HG_HEADS),
        in_specs=[
            pl.BlockSpec((depth, 2, LANES), lambda bi, h: (0, 0, h)),
            pl.BlockSpec((1, HG_V), lambda bi, h: (0, 0)),
            col(_HQ), col(_HFF), col(_HFB), col(_HI), col(_HG),
        ],
        out_specs=pl.BlockSpec((None, t, LANES), lambda bi, h: (bi, 0, h)),
        out_shape=jax.ShapeDtypeStruct((b, t, HG_HEADS * HG_V), jnp.bfloat16),
        scratch_shapes=[
            pltpu.VMEM((t, HG_V), jnp.float32),
            pltpu.VMEM((t, HG_V), jnp.float32),
            pltpu.VMEM((HG_V, HG_K), jnp.float32),
            pltpu.VMEM((HG_V, HG_K), jnp.float32),
        ],
        compiler_params=_cparams(("parallel", "parallel")),
        name="hgrn_bidir",
    )(hgrn_lb, norm_g.reshape(1, HG_V).astype(jnp.float32),
      proj3d, proj3d, proj3d, proj3d, proj3d)


def _out_proj_kernel(a_ref, h_ref, wa_ref, wh_ref, x_ref, o_ref):
    y = jnp.dot(a_ref[...], wa_ref[...], preferred_element_type=jnp.float32)
    y = y + jnp.dot(h_ref[...], wh_ref[...], preferred_element_type=jnp.float32)
    o_ref[...] = x_ref[...] + y


def out_proj(att2d, hgo2d, w_out_bf16, x2d, tm=512):
    n, d = x2d.shape
    tm = min(tm, n)
    wa = att2d.shape[1]
    wh = hgo2d.shape[1]
    return pl.pallas_call(
        _out_proj_kernel,
        grid=(n // tm,),
        in_specs=[
            pl.BlockSpec((tm, wa), lambda i: (i, 0)),
            pl.BlockSpec((tm, wh), lambda i: (i, 0)),
            pl.BlockSpec((wa, d), lambda i: (0, 0)),
            pl.BlockSpec((wh, d), lambda i: (wa // wh, 0)),
            pl.BlockSpec((tm, d), lambda i: (i, 0)),
        ],
        out_specs=pl.BlockSpec((tm, d), lambda i: (i, 0)),
        out_shape=jax.ShapeDtypeStruct((n, d), jnp.float32),
        compiler_params=_cparams(("parallel",)),
        name="out_proj",
    )(att2d, hgo2d, w_out_bf16, w_out_bf16, x2d)


def _router_kernel(x_ref, g_ref, wrt_ref, ha_ref, hb_ref, afft_ref):
    x = x_ref[...]
    h = x * lax.rsqrt(jnp.mean(x * x, axis=-1, keepdims=True) + EPS) * g_ref[...]
    bits = lax.bitcast_convert_type(h.astype(jnp.bfloat16).astype(jnp.float32), jnp.uint32)
    q = ha_ref.shape[1]
    ha_ref[...] = (bits[:, 0:q] >> 16) | bits[:, q:2 * q]
    hb_ref[...] = (bits[:, 2 * q:3 * q] >> 16) | bits[:, 3 * q:4 * q]
    logits = lax.dot_general(wrt_ref[...], h, _NT, preferred_element_type=jnp.float32,
                             precision=lax.Precision.HIGHEST)
    ex = jnp.exp(logits - jnp.max(logits, axis=0, keepdims=True))
    afft_ref[...] = ex / jnp.sum(ex, axis=0, keepdims=True)


def router(x2d, g, w_router, tm=512):
    n, d = x2d.shape
    tm = min(tm, n)
    e = w_router.shape[1]
    return pl.pallas_call(
        _router_kernel,
        grid=(n // tm,),
        in_specs=[
            pl.BlockSpec((tm, d), lambda i: (i, 0)),
            pl.BlockSpec((1, d), lambda i: (0, 0)),
            pl.BlockSpec((e, d), lambda i: (0, 0)),
        ],
        out_specs=[
            pl.BlockSpec((tm, d // 4), lambda i: (i, 0)),
            pl.BlockSpec((tm, d // 4), lambda i: (i, 0)),
            pl.BlockSpec((e, tm), lambda i: (0, i)),
        ],
        out_shape=[
            jax.ShapeDtypeStruct((n, d // 4), jnp.uint32),
            jax.ShapeDtypeStruct((n, d // 4), jnp.uint32),
            jax.ShapeDtypeStruct((e, n), jnp.float32),
        ],
        compiler_params=_cparams(("parallel",)),
        name="router",
    )(x2d, g.reshape(1, d), w_router.T)


_SEL_CHUNK = 1024


def _select_kernel(afft_ref, pos_ref, offs_ref, *, cap):
    f32, i32 = jnp.float32, jnp.int32
    e, n = afft_ref.shape
    chunk = min(_SEL_CHUNK, n)
    nt = n // LANES

    def bits_at(start, size):
        return lax.bitcast_convert_type(afft_ref[:, pl.ds(start, size)], i32)

    def count_ge(cand):
        def body(c, acc):
            x = bits_at(pl.multiple_of(c * chunk, chunk), chunk)
            return acc + jnp.where(x >= cand, 1.0, 0.0)
        acc = lax.fori_loop(0, n // chunk, body, jnp.zeros((e, chunk), f32))
        return jnp.sum(acc, axis=1, keepdims=True)

    def bit_body(it, prefix):
        cand = prefix | jnp.left_shift(jnp.int32(1), 30 - it)
        return jnp.where(count_ge(cand) >= cap, cand, prefix)

    thr = lax.fori_loop(0, 31, bit_body, jnp.zeros((e, 1), i32))
    need = cap - count_ge(thr + 1)

    tri = (lax.broadcasted_iota(i32, (LANES, LANES), 0)
           <= lax.broadcasted_iota(i32, (LANES, LANES), 1)).astype(jnp.bfloat16)
    tile_id = lax.broadcasted_iota(i32, (e, nt), 1)

    def tile_body(j, carry):
        c_gt, c_eq, offs_acc = carry
        x = bits_at(pl.multiple_of(j * LANES, LANES), LANES)
        gt = x > thr
        eq = x == thr
        both = jnp.concatenate([jnp.where(gt, 1.0, 0.0), jnp.where(eq, 1.0, 0.0)],
                               axis=0).astype(jnp.bfloat16)
        inc = jnp.dot(both, tri, preferred_element_type=f32)
        inc_gt = inc[:e] + c_gt
        inc_eq = inc[e:] + c_eq
        sel = gt | (eq & (inc_eq <= need))
        incl = inc_gt + jnp.minimum(inc_eq, need)
        pos_ref[:, pl.ds(pl.multiple_of(j * LANES, LANES), LANES)] = jnp.where(
            sel, incl - 1.0, -1.0).astype(i32)
        offs_acc = jnp.where(tile_id == j, c_gt + jnp.minimum(c_eq, need), offs_acc)
        return inc_gt[:, LANES - 1:LANES], inc_eq[:, LANES - 1:LANES], offs_acc

    zero = jnp.zeros((e, 1), f32)
    _, _, offs_acc = lax.fori_loop(0, nt, tile_body, (zero, zero, jnp.zeros((e, nt), f32)))
    offs_ref[...] = offs_acc.astype(i32)


def ec_select(afft, cap):
    e, n = afft.shape
    return pl.pallas_call(
        functools.partial(_select_kernel, cap=cap),
        out_shape=[jax.ShapeDtypeStruct((e, n), jnp.int32),
                   jax.ShapeDtypeStruct((e, n // LANES), jnp.int32)],
        compiler_params=pltpu.CompilerParams(vmem_limit_bytes=_VMEM_LIMIT),
        name="ec_select",
    )(afft)


_SC_LANES = 16
_SC_CORES = 2
_INV_CHUNK = 8192


def ec_invert(pos, cap):
    ne, n = pos.shape
    ch = min(_INV_CHUNK, n)
    mesh = plsc.VectorSubcoreMesh(core_axis_name="c", subcore_axis_name="s")

    @functools.partial(
        pl.kernel, mesh=mesh,
        out_type=jax.ShapeDtypeStruct((ne * cap,), jnp.int32),
        scratch_types=[pltpu.VMEM((ch,), jnp.int32), pltpu.VMEM((cap,), jnp.int32)],
        compiler_params=pltpu.CompilerParams(needs_layout_passes=False))
    def invert(pos_hbm, idx_hbm, pos_v, idx_v):
        wid = lax.axis_index("s") * _SC_CORES + lax.axis_index("c")

        @pl.when(wid < ne)
        def _():
            @pl.loop(0, n // ch)
            def _(c):
                pltpu.sync_copy(pos_hbm.at[pl.ds(wid * n + c * ch, ch)], pos_v)

                @pl.loop(0, ch, step=_SC_LANES)
                def _(i):
                    p = pos_v[pl.ds(i, _SC_LANES)]
                    tok = lax.iota(jnp.int32, _SC_LANES) + (c * ch + i)
                    plsc.store_scatter(idx_v, [p], tok, mask=p >= 0)

            pltpu.sync_copy(idx_v, idx_hbm.at[pl.ds(wid * cap, cap)])

    return invert(pos.reshape(ne * n)).reshape(ne, cap)


_GATHER_WIN = 128


def ec_gather(table, idx_row):
    m = idx_row.shape[1]
    width = table.shape[1]
    steps = m // (_GATHER_WIN * _SC_CORES)
    mesh = plsc.VectorSubcoreMesh(core_axis_name="c", subcore_axis_name="s")

    @functools.partial(
        pl.kernel, mesh=mesh,
        out_type=jax.ShapeDtypeStruct((m, width), table.dtype),
        scratch_types=[])
    def gather(x_hbm, i_hbm, o_hbm):
        def body(i_vmem, o_vmem):
            pltpu.sync_copy(x_hbm.at[i_vmem.at[0]], o_vmem)

        pltpu.emit_pipeline(
            body,
            grid=(_SC_CORES, steps),
            in_specs=[pl.BlockSpec((1, _GATHER_WIN), index_map=lambda a, i: (0, a * steps + i))],
            out_specs=[pl.BlockSpec((_GATHER_WIN, width),
                                    index_map=lambda a, i: (a * steps + i, 0))],
            core_axis_name=("c", "s"),
            dimension_semantics=(pltpu.PARALLEL, pltpu.PARALLEL),
        )(i_hbm, o_hbm)

    return gather(table, idx_row)


def _expert_kernel(xa_ref, xb_ref, wg_ref, wu_ref, wd_ref, o_ref):
    @pl.when(pl.program_id(1) == pl.num_programs(1) - 1)
    def _():
        o_ref[...] = jnp.zeros(o_ref.shape, o_ref.dtype)

    @pl.when(pl.program_id(1) < pl.num_programs(1) - 1)
    def _():
        as_f32 = lambda bits: lax.bitcast_convert_type(bits, jnp.float32)
        wa, wb = xa_ref[...], xb_ref[...]
        hi = jnp.uint32(0xFFFF0000)
        xs = jnp.concatenate([as_f32(wa << 16), as_f32(wa & hi), as_f32(wb << 16),
                              as_f32(wb & hi)], axis=1).astype(jnp.bfloat16)
        a = jnp.dot(xs, wg_ref[...], preferred_element_type=jnp.float32)
        u = jnp.dot(xs, wu_ref[...], preferred_element_type=jnp.float32)
        hid = (a * jax.nn.sigmoid(a) * u).astype(jnp.bfloat16)
        o_ref[...] = jnp.dot(hid, wd_ref[...],
                             preferred_element_type=jnp.float32).astype(o_ref.dtype)


def expert_ffn(xa, xb, wg, wu, wd, tm=512):
    e, c, q = xa.shape
    d = 4 * q
    tm = min(tm, c)
    f = wg.shape[2]
    nt = c // tm
    return pl.pallas_call(
        _expert_kernel,
        grid=(e, nt + 1),
        in_specs=[
            pl.BlockSpec((None, tm, q), lambda ei, i: (ei, jnp.minimum(i, nt - 1), 0)),
            pl.BlockSpec((None, tm, q), lambda ei, i: (ei, jnp.minimum(i, nt - 1), 0)),
            pl.BlockSpec((None, d, f), lambda ei, i: (ei, 0, 0)),
            pl.BlockSpec((None, d, f), lambda ei, i: (ei, 0, 0)),
            pl.BlockSpec((None, f, d), lambda ei, i: (ei, 0, 0)),
        ],
        out_specs=pl.BlockSpec((None, tm, d), lambda ei, i: (ei, i, 0)),
        out_shape=jax.ShapeDtypeStruct((e, c + tm, d), jnp.bfloat16),
        compiler_params=_cparams(("parallel", "arbitrary")),
        name="expert_ffn",
    )(xa, xb, wg, wu, wd)


def _combine_kernel(offs_ref, x_ref, pos_ref, afft_ref, *rest):
    f32 = jnp.float32
    ne = pos_ref.shape[0]
    tiles, o_ref = rest[:2 * ne], rest[2 * ne]
    j = pl.program_id(0)
    pad = jnp.zeros((LANES - ne, LANES), f32)
    pos_t = jnp.concatenate([pos_ref[...].astype(f32), pad], axis=0).T
    gate_t = jnp.concatenate([afft_ref[...], pad], axis=0).T
    row_id = lax.broadcasted_iota(jnp.int32, (LANES, 2 * LANES), 1).astype(f32)
    y = x_ref[...]
    for e in range(ne):
        base = (offs_ref[e, j] // LANES) * LANES
        rel = pos_t[:, e:e + 1] - base.astype(f32)
        onehot = jnp.where(rel == row_id, 1.0, 0.0).astype(jnp.bfloat16)
        rows = jnp.concatenate([tiles[2 * e][...], tiles[2 * e + 1][...]], axis=0)
        y = y + jnp.dot(onehot, rows, preferred_element_type=f32) * gate_t[:, e:e + 1]
    o_ref[...] = y


def ec_combine(x2d, pos, afft, offs, out):
    n, d = x2d.shape
    ne, crows, _ = out.shape
    nblk = crows // LANES

    def tile_spec(e, k):
        return pl.BlockSpec(
            (None, LANES, d),
            lambda j, offs_ref: (e, jnp.minimum(offs_ref[e, j] // LANES + k, nblk - 1), 0))

    grid_spec = pltpu.PrefetchScalarGridSpec(
        num_scalar_prefetch=1,
        grid=(n // LANES,),
        in_specs=[
            pl.BlockSpec((LANES, d), lambda j, offs_ref: (j, 0)),
            pl.BlockSpec((ne, LANES), lambda j, offs_ref: (0, j)),
            pl.BlockSpec((ne, LANES), lambda j, offs_ref: (0, j)),
        ] + [tile_spec(e, k) for e in range(ne) for k in range(2)],
        out_specs=pl.BlockSpec((LANES, d), lambda j, offs_ref: (j, 0)),
    )
    return pl.pallas_call(
        _combine_kernel,
        grid_spec=grid_spec,
        out_shape=jax.ShapeDtypeStruct((n, d), jnp.float32),
        compiler_params=_cparams(("arbitrary",)),
        name="ec_combine",
    )(offs, x2d, pos, afft, *([out] * (2 * ne)))


def _rms_kernel(x_ref, g_ref, o_ref):
    x = x_ref[...]
    o_ref[...] = x * lax.rsqrt(jnp.mean(x * x, axis=-1, keepdims=True) + EPS) * g_ref[...]


def rms_norm_call(x2d, g, tm=1024):
    n, d = x2d.shape
    tm = min(tm, n)
    return pl.pallas_call(
        _rms_kernel,
        grid=(n // tm,),
        in_specs=[pl.BlockSpec((tm, d), lambda i: (i, 0)),
                  pl.BlockSpec((1, d), lambda i: (0, 0))],
        out_specs=pl.BlockSpec((tm, d), lambda i: (i, 0)),
        out_shape=jax.ShapeDtypeStruct((n, d), jnp.float32),
        compiler_params=_cparams(("parallel",)),
        name="final_norm",
    )(x2d, g.reshape(1, d))


def _trunk(x, p):
    b, t, d = x.shape
    n = b * t
    depth = p["w_in"].shape[0]
    x2d = x.reshape(n, d)
    cap = max(1, EC_CAPACITY * n // N_EXPERTS)
    for layer in range(depth):
        proj = norm_proj(x2d, p["norm_mix_g"][layer], p["w_in"][layer])
        proj3d = proj.reshape(b, t, -1)
        att = diff_attention(proj3d, p["lambda_q1"][layer], p["lambda_k1"][layer],
                             p["lambda_q2"][layer], p["lambda_k2"][layer],
                             p["diff_subln_g"][layer], layer)
        hgo = hgrn_bidir(proj3d, p["hgrn_lb"], p["hgrn_norm_g"][layer], layer)
        x2d = out_proj(att.reshape(n, -1), hgo.reshape(n, -1), p["w_out"][layer], x2d)
        ha, hb, afft = router(x2d, p["norm_ffn_g"][layer], p["w_router"][layer])
        pos, offs = ec_select(afft, cap)
        ne = afft.shape[0]
        idx = ec_invert(pos, cap).reshape(1, ne * cap)
        xa = ec_gather(ha, idx).reshape(ne, cap, -1)
        xb = ec_gather(hb, idx).reshape(ne, cap, -1)
        out = expert_ffn(xa, xb, p["w_gate"][layer], p["w_up"][layer], p["w_down"][layer])
        x2d = ec_combine(x2d, pos, afft, offs, out)
    return rms_norm_call(x2d, p["final_norm_g"]).reshape(b, t, d)


def kernel(x_prompt, x_sample, norm_mix_g, w_in, lambda_q1, lambda_k1, lambda_q2, lambda_k2,
           diff_subln_g, hgrn_lb, hgrn_norm_g, w_out, norm_ffn_g, w_router, w_gate, w_up,
           w_down, final_norm_g):
    bf = jnp.bfloat16
    p = dict(norm_mix_g=norm_mix_g, w_in=w_in.astype(bf), lambda_q1=lambda_q1,
             lambda_k1=lambda_k1, lambda_q2=lambda_q2, lambda_k2=lambda_k2,
             diff_subln_g=diff_subln_g, hgrn_lb=hgrn_lb, hgrn_norm_g=hgrn_norm_g,
             w_out=w_out.astype(bf), norm_ffn_g=norm_ffn_g, w_router=w_router,
             w_gate=w_gate.astype(bf), w_up=w_up.astype(bf), w_down=w_down.astype(bf),
             final_norm_g=final_norm_g)
    return (_trunk(x_prompt, p), _trunk(x_sample, p))
```

```python
import functools
import math

import jax
import jax.numpy as jnp
from jax import lax
from jax.experimental import pallas as pl
from jax.experimental.pallas import tpu as pltpu
from jax.experimental.pallas import tpu_sc as plsc

ATT_HEADS = 4
ATT_QK_DIM = 64
ATT_V_DIM = 128
HG_HEADS = 4
HG_K = 128
HG_V = 128
HG_CHUNK = 64
N_EXPERTS = 16
EC_CAPACITY = 2
EPS = 1e-6
LANES = 128
SUBLANES = 8

_QBLK, _KBLK, _VBLK = 0, 4, 8
_HQ, _HFF, _HFB, _HI, _HG = 12, 16, 20, 24, 28

_VMEM_LIMIT = 56 * 1024 * 1024

_NT = (((1,), (1,)), ((), ()))


def _cparams(sem):
    return pltpu.CompilerParams(dimension_semantics=sem, vmem_limit_bytes=_VMEM_LIMIT)


def _norm_proj_kernel(x_ref, g_ref, w_ref, o_ref, *, n_split):
    x = x_ref[...]
    y = x * lax.rsqrt(jnp.mean(x * x, axis=-1, keepdims=True) + EPS) * g_ref[...]
    h = y.astype(jnp.bfloat16)
    wn = w_ref.shape[1] // n_split
    for c in range(n_split):
        o_ref[:, c * wn:(c + 1) * wn] = jnp.dot(
            h, w_ref[:, c * wn:(c + 1) * wn], preferred_element_type=jnp.float32
        ).astype(o_ref.dtype)


def norm_proj(x2d, g, w_bf16, tm=512):
    n, d = x2d.shape
    tm = min(tm, n)
    dout = w_bf16.shape[1]
    return pl.pallas_call(
        functools.partial(_norm_proj_kernel, n_split=max(1, dout // 1024)),
        grid=(n // tm,),
        in_specs=[
            pl.BlockSpec((tm, d), lambda i: (i, 0)),
            pl.BlockSpec((1, d), lambda i: (0, 0)),
            pl.BlockSpec((d, dout), lambda i: (0, 0)),
        ],
        out_specs=pl.BlockSpec((tm, dout), lambda i: (i, 0)),
        out_shape=jax.ShapeDtypeStruct((n, dout), jnp.bfloat16),
        compiler_params=_cparams(("parallel",)),
        name="norm_proj",
    )(x2d, g.reshape(1, d), w_bf16)


_LOG2E = 1.4426950408889634
_Q_STRIP = 128
_ONES_ROWS = 16


def _split3(x):
    bf = jnp.bfloat16
    h = x.astype(bf).astype(jnp.float32)
    r = x - h
    m = r.astype(bf).astype(jnp.float32)
    lo = (r - m).astype(bf).astype(jnp.float32)
    return h, m, lo


def _attn_kernel(beta_ref, qext_ref, lq1_ref, lk1_ref, lq2_ref, lk2_ref, subg_ref,
                 q_ref, k_ref, v_ref, kext_ref, o_ref, vt_sc, acc_sc, s_sc, *, lam_init, tk, tq):
    f32, bf = jnp.float32, jnp.bfloat16
    t = q_ref.shape[0]
    nk = t // tk
    qs_ = _Q_STRIP
    ng = tq // qs_
    ncol = 2 * tq
    dv = v_ref.shape[1]
    beta = beta_ref[0:1, 0:1]

    def vt_body(c, carry):
        r0 = pl.multiple_of(c * tk, tk)
        vt_sc[0:dv, pl.ds(r0, tk)] = v_ref[pl.ds(r0, tk), :].astype(f32).T.astype(bf)
        return carry

    lax.fori_loop(0, nk, vt_body, 0)
    vt_sc[dv:dv + _ONES_ROWS, :] = jnp.ones((_ONES_ROWS, t), bf)

    lam = (jnp.exp(jnp.sum(lq1_ref[...] * lk1_ref[...], axis=-1, keepdims=True))
           - jnp.exp(jnp.sum(lq2_ref[...] * lk2_ref[...], axis=-1, keepdims=True)) + lam_init)
    lane = lax.broadcasted_iota(jnp.int32, (qs_, LANES), 1)
    kr = lax.broadcasted_iota(jnp.int32, (tk, qs_), 0)
    qc = lax.broadcasted_iota(jnp.int32, (tk, qs_), 1)
    cscale = ATT_QK_DIM ** -0.5 * _LOG2E

    def group(gi, carry):
        g0 = pl.multiple_of(gi * tq, tq)
        tiles = []
        for u in range(ng):
            q0 = g0 + u * qs_
            qs = (q_ref[pl.ds(q0, qs_), :].astype(f32) * cscale).astype(bf).astype(f32)
            qext = qext_ref[...]
            q_aug = jnp.concatenate([
                jnp.concatenate([jnp.where(lane < ATT_QK_DIM, qs, 0.0), qext], axis=1),
                jnp.concatenate([jnp.where(lane >= ATT_QK_DIM, qs, 0.0), qext], axis=1)], axis=0)
            tiles.append(q_aug.T.astype(bf))
        w = jnp.concatenate(tiles, axis=1)
        jc = g0 // tk

        def corr_tile(d):
            corr = 2.0 * beta * jnp.minimum((d + qc - kr).astype(f32), 0.0)
            return jnp.concatenate([corr, corr], axis=1)

        def shift(d, sidx):
            sgn = 1.0 if sidx == 0 else -1.0
            return (-sgn) * beta * d.astype(f32)

        def scores(n, sidx, crossing):
            r0 = pl.multiple_of(n * tk, tk)
            k_aug = jnp.concatenate([k_ref[pl.ds(r0, tk), :], kext_ref[sidx]], axis=1)
            for u in range(ng):
                c0 = u * 2 * qs_
                s = jnp.dot(k_aug, w[:, c0:c0 + 2 * qs_], preferred_element_type=f32)
                if crossing:
                    s = s + corr_tile(g0 + u * qs_ - r0)
                s_sc[:, c0:c0 + 2 * qs_] = s

        def step(n, mc, sidx, crossing):
            r0 = pl.multiple_of((n - 1) * tk, tk)
            r1 = pl.multiple_of(n * tk, tk)
            vt = vt_sc[:, pl.ds(r0, tk)]
            if sidx is not None:
                k_aug = jnp.concatenate([k_ref[pl.ds(r1, tk), :], kext_ref[sidx]], axis=1)
            m, csts = mc
            ms, cs = [], []
            for u in range(ng):
                c0 = u * 2 * qs_
                s = s_sc[:, c0:c0 + 2 * qs_]
                m_old = m[:, c0:c0 + 2 * qs_]
                cst = csts[:, c0:c0 + 2 * qs_]
                m_new = jnp.maximum(m_old, jnp.max(s, axis=0, keepdims=True) + cst)
                alpha = jnp.exp2(m_old - m_new)
                p = jnp.exp2(s - (m_new - cst)).astype(bf)
                acc_sc[:, c0:c0 + 2 * qs_] = (acc_sc[:, c0:c0 + 2 * qs_] * alpha
                                              + jnp.dot(vt, p, preferred_element_type=f32))
                ms.append(m_new)
                if sidx is not None:
                    s = jnp.dot(k_aug, w[:, c0:c0 + 2 * qs_], preferred_element_type=f32)
                    if crossing:
                        s = s + corr_tile(g0 + u * qs_ - r1)
                    s_sc[:, c0:c0 + 2 * qs_] = s
                    cs.append(jnp.broadcast_to(shift(g0 + u * qs_ - r1, sidx), (1, 2 * qs_)))
            return jnp.concatenate(ms, axis=1), (jnp.concatenate(cs, axis=1) if cs else csts)

        acc_sc[...] = jnp.zeros(acc_sc.shape, f32)
        m = jnp.full((1, ncol), -1e30, f32)
        scores(0, 0, True)
        n1 = jnp.maximum(jc, 1)
        n2 = jc + tq // tk
        cst0 = jnp.concatenate([jnp.broadcast_to(shift(g0 + u * qs_, 0), (1, 2 * qs_))
                                for u in range(ng)], axis=1)
        mc = (m, cst0)
        mc = lax.fori_loop(1, n1, functools.partial(step, sidx=0, crossing=False), mc)
        mc = lax.fori_loop(n1, n2, functools.partial(step, sidx=0, crossing=True), mc)
        mc = lax.fori_loop(n2, nk, functools.partial(step, sidx=1, crossing=False), mc)
        step(nk, mc, None, False)

        for u in range(ng):
            c0 = u * 2 * qs_
            acc = acc_sc[:, c0:c0 + 2 * qs_]
            den = acc[dv:dv + 1, :]
            o = acc[0:dv, 0:qs_] / den[:, 0:qs_] - lam * (acc[0:dv, qs_:] / den[:, qs_:])
            y = o * lax.rsqrt(jnp.mean(o * o, axis=0, keepdims=True) + EPS) * subg_ref[...]
            o_ref[pl.ds(g0 + u * qs_, qs_), :] = (y * (1.0 - lam_init)).T.astype(o_ref.dtype)
        return carry

    lax.fori_loop(0, t // tq, group, 0)


def _attn_tables(tk):
    f32 = jnp.float32
    slopes = 2.0 ** (-8.0 * (jnp.arange(ATT_HEADS, dtype=f32) + 1.0) / ATT_HEADS)
    beta = slopes * _LOG2E
    lane = jnp.arange(LANES)[None, None, :]
    b = _split3(beta)
    a = _split3(-beta[:, None] * jnp.arange(_Q_STRIP, dtype=f32)[None, :])
    qext = jnp.zeros((ATT_HEADS, _Q_STRIP, LANES), f32)
    for c in range(3):
        qext = jnp.where(lane == c, b[c][:, None, None], qext)
        qext = jnp.where(lane == 3 + c, a[c][:, :, None], qext)
    r = jnp.arange(tk, dtype=f32)[:, None]
    lane2 = jnp.arange(LANES)[None, :]
    base = jnp.where(lane2 < 3, r, jnp.where(lane2 < 6, 1.0, 0.0))
    kext = jnp.stack([base, -base]).astype(jnp.bfloat16)
    beta_t = jnp.broadcast_to(beta[:, None, None], (ATT_HEADS, 1, LANES))
    return beta_t, qext, kext


def diff_attention(proj3d, lq1, lk1, lq2, lk2, subg, layer, tk=256, tq=2048):
    b, t, _ = proj3d.shape
    tk, tq = min(tk, t), min(tq, t)
    assert tq % tk == 0 and tq % _Q_STRIP == 0
    lam_init = 0.8 - 0.6 * math.exp(-0.3 * layer)
    assert tk <= 256
    beta_t, qext, kext = _attn_tables(tk)
    vec = lambda a: a.reshape(1, -1).astype(jnp.float32)
    small = lambda n: pl.BlockSpec((1, n), lambda bi, h: (0, 0))
    head = pl.BlockSpec((None, 1, LANES), lambda bi, h: (h, 0, 0))
    col = lambda blk: pl.BlockSpec((None, t, LANES), lambda bi, h: (bi, 0, blk + h))
    return pl.pallas_call(
        functools.partial(_attn_kernel, lam_init=lam_init, tk=tk, tq=tq),
        grid=(b, ATT_HEADS),
        in_specs=[
            head, pl.BlockSpec((None, _Q_STRIP, LANES), lambda bi, h: (h, 0, 0)),
            small(ATT_QK_DIM), small(ATT_QK_DIM), small(ATT_QK_DIM), small(ATT_QK_DIM),
            pl.BlockSpec((ATT_V_DIM, 1), lambda bi, h: (0, 0)),
            col(_QBLK), col(_KBLK), col(_VBLK),
            pl.BlockSpec((2, tk, LANES), lambda bi, h: (0, 0, 0)),
        ],
        out_specs=pl.BlockSpec((None, t, LANES), lambda bi, h: (bi, 0, h)),
        out_shape=jax.ShapeDtypeStruct((b, t, ATT_HEADS * ATT_V_DIM), jnp.bfloat16),
        scratch_shapes=[pltpu.VMEM((ATT_V_DIM + _ONES_ROWS, t), jnp.bfloat16),
                        pltpu.VMEM((ATT_V_DIM + _ONES_ROWS, 2 * tq), jnp.float32),
                        pltpu.VMEM((tk, 2 * tq), jnp.float32)],
        compiler_params=_cparams(("parallel", "parallel")),
        name="diff_attn",
    )(beta_t, qext, vec(lq1), vec(lk1), vec(lq2), vec(lk2),
      subg.reshape(ATT_V_DIM, 1).astype(jnp.float32), proj3d, proj3d, proj3d, kext)


_HG_CHUNKS_PER_ITER = 4
_HG_FIN_ROWS = 256


def _scan_rows(g, rev):
    c = g.shape[0]
    row = lax.broadcasted_iota(jnp.int32, g.shape, 0)
    b = g
    s = 1
    while s < c:
        if rev:
            b = b + jnp.where(row < c - s, pltpu.roll(b, c - s, axis=0), 0.0)
        else:
            b = b + jnp.where(row >= s, pltpu.roll(b, s, axis=0), 0.0)
        s *= 2
    return b


def _level_ref(b, m, rev):
    c, w = b.shape
    r = m // 2 if rev else m // 2 - 1
    pieces = []
    if m >= SUBLANES:
        for j in range(c // m):
            pieces.append(jnp.broadcast_to(b[j * m + r:j * m + r + 1, :], (m, w)))
    else:
        sub = lax.broadcasted_iota(jnp.int32, (SUBLANES, w), 0)
        for j in range(c // SUBLANES):
            base = j * SUBLANES
            acc = jnp.broadcast_to(b[base + r:base + r + 1, :], (SUBLANES, w))
            for i in range(1, SUBLANES // m):
                cand = jnp.broadcast_to(b[base + i * m + r:base + i * m + r + 1, :], (SUBLANES, w))
                acc = jnp.where(sub >= i * m, cand, acc)
            pieces.append(acc)
    return jnp.concatenate(pieces, axis=0)


def _neg_abs(x):
    bits = lax.bitcast_convert_type(x, jnp.uint32) | jnp.uint32(0x80000000)
    return lax.bitcast_convert_type(bits, jnp.float32)


def _hgrn_masks(c, rev):
    row = lax.broadcasted_iota(jnp.int32, (c, LANES), 0)
    ti = lax.broadcasted_iota(jnp.int32, (c, c), 0)
    si = lax.broadcasted_iota(jnp.int32, (c, c), 1)
    qrows, pairs = {}, {}
    m = c
    while m >= 2:
        late = (row & (m - 1)) >= m // 2
        t_late = (ti & (m - 1)) >= m // 2
        s_late = (si & (m - 1)) >= m // 2
        same = (ti ^ si) < m
        if rev:
            qrows[m] = jnp.logical_not(late)
            pairs[m] = same & jnp.logical_not(t_late) & s_late
        else:
            qrows[m] = late
            pairs[m] = same & t_late & jnp.logical_not(s_late)
        m //= 2
    return qrows, pairs, ti == si


def _hgrn_chunk(q, f, v, st_ref, rev, masks):
    qrows, pairs, diag = masks
    c = q.shape[0]
    bf = jnp.bfloat16
    k_ = 1.0 - f
    b = _scan_rows(jnp.log(f) * _LOG2E, rev)
    a = jnp.where(diag,
                  lax.dot_general(q.astype(bf), k_.astype(bf), _NT,
                                  preferred_element_type=jnp.float32), 0.0)
    m = c
    while m >= 2:
        if m == 2:
            cref = jnp.where(qrows[m], pltpu.roll(b, c - 1 if rev else 1, axis=0), b)
        else:
            cref = _level_ref(b, m, rev)
        e = jnp.exp2(_neg_abs(b - cref))
        z = (jnp.where(qrows[m], q, k_) * e).astype(bf)
        p = lax.dot_general(z, z, _NT, preferred_element_type=jnp.float32)
        a = jnp.where(pairs[m], p, a)
        m //= 2
    st = st_ref[...]
    o = jnp.dot(a.astype(bf), v.astype(bf), preferred_element_type=jnp.float32)
    o = o + lax.dot_general((q * jnp.exp2(b)).astype(bf), st.astype(bf), _NT,
                            preferred_element_type=jnp.float32)
    btot = b[0:1, :] if rev else b[c - 1:c, :]
    khat = (k_ * jnp.exp2(btot - b)).astype(bf)
    st_ref[...] = st * jnp.exp2(btot) + jnp.dot(v.T.astype(bf), khat,
                                                preferred_element_type=jnp.float32)
    return o


def _hgrn_kernel(lb_ref, ng_ref, q_ref, ff_ref, fb_ref, i_ref, g_ref, o_ref,
                 of_sc, ob_sc, stf_sc, stb_sc, *, layer, chunk):
    t = q_ref.shape[0]
    n = t // chunk
    lb_raw = lb_ref[...].astype(jnp.float32)
    ex = jnp.exp(lb_raw - jnp.max(lb_raw, axis=0, keepdims=True))
    sm = ex / jnp.sum(ex, axis=0, keepdims=True)
    lb = jnp.zeros_like(sm[0])
    for l_ in range(1, layer + 1):
        lb = lb + sm[l_]
    lb_f = lb[0:1, :]
    lb_b = lb[1:2, :]

    stf_sc[...] = jnp.zeros(stf_sc.shape, jnp.float32)
    stb_sc[...] = jnp.zeros(stb_sc.shape, jnp.float32)

    def load(ref, r0):
        return ref[pl.ds(r0, chunk), :].astype(jnp.float32)

    masks_f = _hgrn_masks(chunk, False)
    masks_b = _hgrn_masks(chunk, True)

    per_iter = _HG_CHUNKS_PER_ITER if n % _HG_CHUNKS_PER_ITER == 0 else 1

    def body(ci, carry):
        for u in range(per_iter):
            r0 = pl.multiple_of((ci * per_iter + u) * chunk, chunk)
            qr = load(q_ref, r0)
            f = lb_f + (1.0 - lb_f) * jax.nn.sigmoid(load(ff_ref, r0))
            of_sc[pl.ds(r0, chunk), :] = _hgrn_chunk(qr * jax.nn.sigmoid(qr), f,
                                                     load(i_ref, r0), stf_sc, False, masks_f)
            r1 = pl.multiple_of((n - 1 - ci * per_iter - u) * chunk, chunk)
            qr = load(q_ref, r1)
            f = lb_b + (1.0 - lb_b) * jax.nn.sigmoid(load(fb_ref, r1))
            ob_sc[pl.ds(r1, chunk), :] = _hgrn_chunk(qr * jax.nn.sigmoid(qr), f,
                                                     load(i_ref, r1), stb_sc, True, masks_b)
        return carry

    lax.fori_loop(0, n // per_iter, body, 0)

    rows = math.gcd(t, _HG_FIN_ROWS)

    def fin(ci, carry):
        r0 = pl.multiple_of(ci * rows, rows)
        o = of_sc[pl.ds(r0, rows), :] + ob_sc[pl.ds(r0, rows), :]
        y = o * lax.rsqrt(jnp.mean(o * o, axis=-1, keepdims=True) + EPS) * ng_ref[...]
        gr = g_ref[pl.ds(r0, rows), :].astype(jnp.float32)
        o_ref[pl.ds(r0, rows), :] = (y * (gr * jax.nn.sigmoid(gr))).astype(o_ref.dtype)
        return carry

    lax.fori_loop(0, t // rows, fin, 0)


def hgrn_bidir(proj3d, hgrn_lb, norm_g, layer):
    b, t, _ = proj3d.shape
    depth = hgrn_lb.shape[0]
    col = lambda blk: pl.BlockSpec((None, t, LANES), lambda bi, h: (bi, 0, blk + h))
    return pl.pallas_call(
        functools.partial(_hgrn_kernel, layer=layer, chunk=HG_CHUNK),
        grid=(b, HG_HEADS),
        in_specs=[
            pl.BlockSpec((depth, 2, LANES), lambda bi, h: (0, 0, h)),
            pl.BlockSpec((1, HG_V), lambda bi, h: (0, 0)),
            col(_HQ), col(_HFF), col(_HFB), col(_HI), col(_HG),
        ],
        out_specs=pl.BlockSpec((None, t, LANES), lambda bi, h: (bi, 0, h)),
        out_shape=jax.ShapeDtypeStruct((b, t, HG_HEADS * HG_V), jnp.bfloat16),
        scratch_shapes=[
            pltpu.VMEM((t, HG_V), jnp.float32),
            pltpu.VMEM((t, HG_V), jnp.float32),
            pltpu.VMEM((HG_V, HG_K), jnp.float32),
            pltpu.VMEM((HG_V, HG_K), jnp.float32),
        ],
        compiler_params=_cparams(("parallel", "parallel")),
        name="hgrn_bidir",
    )(hgrn_lb, norm_g.reshape(1, HG_V).astype(jnp.float32),
      proj3d, proj3d, proj3d, proj3d, proj3d)


def _out_proj_kernel(a_ref, h_ref, wa_ref, wh_ref, x_ref, o_ref):
    y = jnp.dot(a_ref[...], wa_ref[...], preferred_element_type=jnp.float32)
    y = y + jnp.dot(h_ref[...], wh_ref[...], preferred_element_type=jnp.float32)
    o_ref[...] = x_ref[...] + y


def out_proj(att2d, hgo2d, w_out_bf16, x2d, tm=512):
    n, d = x2d.shape
    tm = min(tm, n)
    wa = att2d.shape[1]
    wh = hgo2d.shape[1]
    return pl.pallas_call(
        _out_proj_kernel,
        grid=(n // tm,),
        in_specs=[
            pl.BlockSpec((tm, wa), lambda i: (i, 0)),
            pl.BlockSpec((tm, wh), lambda i: (i, 0)),
            pl.BlockSpec((wa, d), lambda i: (0, 0)),
            pl.BlockSpec((wh, d), lambda i: (wa // wh, 0)),
            pl.BlockSpec((tm, d), lambda i: (i, 0)),
        ],
        out_specs=pl.BlockSpec((tm, d), lambda i: (i, 0)),
        out_shape=jax.ShapeDtypeStruct((n, d), jnp.float32),
        compiler_params=_cparams(("parallel",)),
        name="out_proj",
    )(att2d, hgo2d, w_out_bf16, w_out_bf16, x2d)


def _router_kernel(x_ref, g_ref, wrt_ref, ha_ref, hb_ref, afft_ref):
    x = x_ref[...]
    h = x * lax.rsqrt(jnp.mean(x * x, axis=-1, keepdims=True) + EPS) * g_ref[...]
    bits = lax.bitcast_convert_type(h.astype(jnp.bfloat16).astype(jnp.float32), jnp.uint32)
    q = ha_ref.shape[1]
    ha_ref[...] = (bits[:, 0:q] >> 16) | bits[:, q:2 * q]
    hb_ref[...] = (bits[:, 2 * q:3 * q] >> 16) | bits[:, 3 * q:4 * q]
    logits = lax.dot_general(wrt_ref[...], h, _NT, preferred_element_type=jnp.float32,
                             precision=lax.Precision.HIGHEST)
    ex = jnp.exp(logits - jnp.max(logits, axis=0, keepdims=True))
    afft_ref[...] = ex / jnp.sum(ex, axis=0, keepdims=True)


def router(x2d, g, w_router, tm=512):
    n, d = x2d.shape
    tm = min(tm, n)
    e = w_router.shape[1]
    return pl.pallas_call(
        _router_kernel,
        grid=(n // tm,),
        in_specs=[
            pl.BlockSpec((tm, d), lambda i: (i, 0)),
            pl.BlockSpec((1, d), lambda i: (0, 0)),
            pl.BlockSpec((e, d), lambda i: (0, 0)),
        ],
        out_specs=[
            pl.BlockSpec((tm, d // 4), lambda i: (i, 0)),
            pl.BlockSpec((tm, d // 4), lambda i: (i, 0)),
            pl.BlockSpec((e, tm), lambda i: (0, i)),
        ],
        out_shape=[
            jax.ShapeDtypeStruct((n, d // 4), jnp.uint32),
            jax.ShapeDtypeStruct((n, d // 4), jnp.uint32),
            jax.ShapeDtypeStruct((e, n), jnp.float32),
        ],
        compiler_params=_cparams(("parallel",)),
        name="router",
    )(x2d, g.reshape(1, d), w_router.T)


_SEL_CHUNK = 1024


def _select_kernel(afft_ref, pos_ref, offs_ref, *, cap):
    f32, i32 = jnp.float32, jnp.int32
    e, n = afft_ref.shape
    chunk = min(_SEL_CHUNK, n)
    nt = n // LANES

    def bits_at(start, size):
        return lax.bitcast_convert_type(afft_ref[:, pl.ds(start, size)], i32)

    def count_ge(cand):
        def body(c, acc):
            x = bits_at(pl.multiple_of(c * chunk, chunk), chunk)
            return acc + jnp.where(x >= cand, 1.0, 0.0)
        acc = lax.fori_loop(0, n // chunk, body, jnp.zeros((e, chunk), f32))
        return jnp.sum(acc, axis=1, keepdims=True)

    def bit_body(it, prefix):
        cand = prefix | jnp.left_shift(jnp.int32(1), 30 - it)
        return jnp.where(count_ge(cand) >= cap, cand, prefix)

    thr = lax.fori_loop(0, 31, bit_body, jnp.zeros((e, 1), i32))
    need = cap - count_ge(thr + 1)

    tri = (lax.broadcasted_iota(i32, (LANES, LANES), 0)
           <= lax.broadcasted_iota(i32, (LANES, LANES), 1)).astype(jnp.bfloat16)
    tile_id = lax.broadcasted_iota(i32, (e, nt), 1)

    def tile_body(j, carry):
        c_gt, c_eq, offs_acc = carry
        x = bits_at(pl.multiple_of(j * LANES, LANES), LANES)
        gt = x > thr
        eq = x == thr
        both = jnp.concatenate([jnp.where(gt, 1.0, 0.0), jnp.where(eq, 1.0, 0.0)],
                               axis=0).astype(jnp.bfloat16)
        inc = jnp.dot(both, tri, preferred_element_type=f32)
        inc_gt = inc[:e] + c_gt
        inc_eq = inc[e:] + c_eq
        sel = gt | (eq & (inc_eq <= need))
        incl = inc_gt + jnp.minimum(inc_eq, need)
        pos_ref[:, pl.ds(pl.multiple_of(j * LANES, LANES), LANES)] = jnp.where(
            sel, incl - 1.0, -1.0).astype(i32)
        offs_acc = jnp.where(tile_id == j, c_gt + jnp.minimum(c_eq, need), offs_acc)
        return inc_gt[:, LANES - 1:LANES], inc_eq[:, LANES - 1:LANES], offs_acc

    zero = jnp.zeros((e, 1), f32)
    _, _, offs_acc = lax.fori_loop(0, nt, tile_body, (zero, zero, jnp.zeros((e, nt), f32)))
    offs_ref[...] = offs_acc.astype(i32)


def ec_select(afft, cap):
    e, n = afft.shape
    return pl.pallas_call(
        functools.partial(_select_kernel, cap=cap),
        out_shape=[jax.ShapeDtypeStruct((e, n), jnp.int32),
                   jax.ShapeDtypeStruct((e, n // LANES), jnp.int32)],
        compiler_params=pltpu.CompilerParams(vmem_limit_bytes=_VMEM_LIMIT),
        name="ec_select",
    )(afft)


_SC_LANES = 16
_SC_CORES = 2
_INV_CHUNK = 8192


def ec_invert(pos, afft, cap):
    ne, n = pos.shape
    ch = min(_INV_CHUNK, n)
    mesh = plsc.VectorSubcoreMesh(core_axis_name="c", subcore_axis_name="s")

    @functools.partial(
        pl.kernel, mesh=mesh,
        out_type=[jax.ShapeDtypeStruct((ne * cap,), jnp.int32),
                  jax.ShapeDtypeStruct((ne * cap,), jnp.float32)],
        scratch_types=[pltpu.VMEM((ch,), jnp.int32), pltpu.VMEM((ch,), jnp.float32),
                       pltpu.VMEM((cap,), jnp.int32), pltpu.VMEM((cap,), jnp.float32)],
        compiler_params=pltpu.CompilerParams(needs_layout_passes=False))
    def invert(pos_hbm, aff_hbm, idx_hbm, gate_hbm, pos_v, aff_v, idx_v, gate_v):
        wid = lax.axis_index("s") * _SC_CORES + lax.axis_index("c")

        @pl.when(wid < ne)
        def _():
            @pl.loop(0, n // ch)
            def _(c):
                pltpu.sync_copy(pos_hbm.at[pl.ds(wid * n + c * ch, ch)], pos_v)
                pltpu.sync_copy(aff_hbm.at[pl.ds(wid * n + c * ch, ch)], aff_v)

                @pl.loop(0, ch, step=_SC_LANES)
                def _(i):
                    p = pos_v[pl.ds(i, _SC_LANES)]
                    tok = lax.iota(jnp.int32, _SC_LANES) + (c * ch + i)
                    plsc.store_scatter(idx_v, [p], tok, mask=p >= 0)
                    plsc.store_scatter(gate_v, [p], aff_v[pl.ds(i, _SC_LANES)], mask=p >= 0)

            pltpu.sync_copy(idx_v, idx_hbm.at[pl.ds(wid * cap, cap)])
            pltpu.sync_copy(gate_v, gate_hbm.at[pl.ds(wid * cap, cap)])

    idx, gates = invert(pos.reshape(ne * n), afft.reshape(ne * n))
    return idx.reshape(ne, cap), gates.reshape(ne, cap)


_GATHER_WIN = 128


def ec_gather(table, idx_row):
    m = idx_row.shape[1]
    width = table.shape[1]
    steps = m // (_GATHER_WIN * _SC_CORES)
    mesh = plsc.VectorSubcoreMesh(core_axis_name="c", subcore_axis_name="s")

    @functools.partial(
        pl.kernel, mesh=mesh,
        out_type=jax.ShapeDtypeStruct((m, width), table.dtype),
        scratch_types=[])
    def gather(x_hbm, i_hbm, o_hbm):
        def body(i_vmem, o_vmem):
            pltpu.sync_copy(x_hbm.at[i_vmem.at[0]], o_vmem)

        pltpu.emit_pipeline(
            body,
            grid=(_SC_CORES, steps),
            in_specs=[pl.BlockSpec((1, _GATHER_WIN), index_map=lambda a, i: (0, a * steps + i))],
            out_specs=[pl.BlockSpec((_GATHER_WIN, width),
                                    index_map=lambda a, i: (a * steps + i, 0))],
            core_axis_name=("c", "s"),
            dimension_semantics=(pltpu.PARALLEL, pltpu.PARALLEL),
        )(i_hbm, o_hbm)

    return gather(table, idx_row)


def _expert_kernel(xa_ref, xb_ref, gate_ref, wg_ref, wu_ref, wd_ref, o_ref):
    @pl.when(pl.program_id(1) == pl.num_programs(1) - 1)
    def _():
        o_ref[...] = jnp.zeros(o_ref.shape, o_ref.dtype)

    @pl.when(pl.program_id(1) < pl.num_programs(1) - 1)
    def _():
        as_f32 = lambda bits: lax.bitcast_convert_type(bits, jnp.float32)
        wa, wb = xa_ref[...], xb_ref[...]
        hi = jnp.uint32(0xFFFF0000)
        xs = jnp.concatenate([as_f32(wa << 16), as_f32(wa & hi), as_f32(wb << 16),
                              as_f32(wb & hi)], axis=1).astype(jnp.bfloat16)
        a = jnp.dot(xs, wg_ref[...], preferred_element_type=jnp.float32)
        u = jnp.dot(xs, wu_ref[...], preferred_element_type=jnp.float32)
        hid = (a * jax.nn.sigmoid(a) * u).astype(jnp.bfloat16)
        y = jnp.dot(hid, wd_ref[...], preferred_element_type=jnp.float32)
        o_ref[...] = (y * gate_ref[...]).astype(o_ref.dtype)


def expert_ffn(xa, xb, gates, wg, wu, wd, tm=512):
    e, c, q = xa.shape
    d = 4 * q
    tm = min(tm, c)
    f = wg.shape[2]
    nt = c // tm
    return pl.pallas_call(
        _expert_kernel,
        grid=(e, nt + 1),
        in_specs=[
            pl.BlockSpec((None, tm, q), lambda ei, i: (ei, jnp.minimum(i, nt - 1), 0)),
            pl.BlockSpec((None, tm, q), lambda ei, i: (ei, jnp.minimum(i, nt - 1), 0)),
            pl.BlockSpec((None, tm, 1), lambda ei, i: (ei, jnp.minimum(i, nt - 1), 0)),
            pl.BlockSpec((None, d, f), lambda ei, i: (ei, 0, 0)),
            pl.BlockSpec((None, d, f), lambda ei, i: (ei, 0, 0)),
            pl.BlockSpec((None, f, d), lambda ei, i: (ei, 0, 0)),
        ],
        out_specs=pl.BlockSpec((None, tm, d), lambda ei, i: (ei, i, 0)),
        out_shape=jax.ShapeDtypeStruct((e, c + tm, d), jnp.bfloat16),
        compiler_params=_cparams(("parallel", "arbitrary")),
        name="expert_ffn",
    )(xa, xb, gates.reshape(e, c, 1), wg, wu, wd)


def _combine_kernel(offs_ref, x_ref, pos_ref, *rest):
    f32 = jnp.float32
    ne = pos_ref.shape[0]
    tiles, o_ref = rest[:2 * ne], rest[2 * ne]
    j = pl.program_id(0)
    pad = jnp.zeros((LANES - ne, LANES), f32)
    pos_t = jnp.concatenate([pos_ref[...].astype(f32), pad], axis=0).T
    row_id = lax.broadcasted_iota(jnp.int32, (LANES, 2 * LANES), 1).astype(f32)
    y = x_ref[...]
    for e in range(ne):
        base = (offs_ref[e, j] // LANES) * LANES
        rel = pos_t[:, e:e + 1] - base.astype(f32)
        onehot = jnp.where(rel == row_id, 1.0, 0.0).astype(jnp.bfloat16)
        rows = jnp.concatenate([tiles[2 * e][...], tiles[2 * e + 1][...]], axis=0)
        y = y + jnp.dot(onehot, rows, preferred_element_type=f32)
    o_ref[...] = y


def ec_combine(x2d, pos, offs, out):
    n, d = x2d.shape
    ne, crows, _ = out.shape
    nblk = crows // LANES

    def tile_spec(e, k):
        return pl.BlockSpec(
            (None, LANES, d),
            lambda j, offs_ref: (e, jnp.minimum(offs_ref[e, j] // LANES + k, nblk - 1), 0))

    grid_spec = pltpu.PrefetchScalarGridSpec(
        num_scalar_prefetch=1,
        grid=(n // LANES,),
        in_specs=[
            pl.BlockSpec((LANES, d), lambda j, offs_ref: (j, 0)),
            pl.BlockSpec((ne, LANES), lambda j, offs_ref: (0, j)),
        ] + [tile_spec(e, k) for e in range(ne) for k in range(2)],
        out_specs=pl.BlockSpec((LANES, d), lambda j, offs_ref: (j, 0)),
    )
    return pl.pallas_call(
        _combine_kernel,
        grid_spec=grid_spec,
        out_shape=jax.ShapeDtypeStruct((n, d), jnp.float32),
        compiler_params=_cparams(("arbitrary",)),
        name="ec_combine",
    )(offs, x2d, pos, *([out] * (2 * ne)))


def _rms_kernel(x_ref, g_ref, o_ref):
    x = x_ref[...]
    o_ref[...] = x * lax.rsqrt(jnp.mean(x * x, axis=-1, keepdims=True) + EPS) * g_ref[...]


def rms_norm_call(x2d, g, tm=1024):
    n, d = x2d.shape
    tm = min(tm, n)
    return pl.pallas_call(
        _rms_kernel,
        grid=(n // tm,),
        in_specs=[pl.BlockSpec((tm, d), lambda i: (i, 0)),
                  pl.BlockSpec((1, d), lambda i: (0, 0))],
        out_specs=pl.BlockSpec((tm, d), lambda i: (i, 0)),
        out_shape=jax.ShapeDtypeStruct((n, d), jnp.float32),
        compiler_params=_cparams(("parallel",)),
        name="final_norm",
    )(x2d, g.reshape(1, d))


def _trunk(x, p):
    b, t, d = x.shape
    n = b * t
    depth = p["w_in"].shape[0]
    x2d = x.reshape(n, d)
    cap = max(1, EC_CAPACITY * n // N_EXPERTS)
    for layer in range(depth):
        proj = norm_proj(x2d, p["norm_mix_g"][layer], p["w_in"][layer])
        proj3d = proj.reshape(b, t, -1)
        att = diff_attention(proj3d, p["lambda_q1"][layer], p["lambda_k1"][layer],
                             p["lambda_q2"][layer], p["lambda_k2"][layer],
                             p["diff_subln_g"][layer], layer)
        hgo = hgrn_bidir(proj3d, p["hgrn_lb"], p["hgrn_norm_g"][layer], layer)
        x2d = out_proj(att.reshape(n, -1), hgo.reshape(n, -1), p["w_out"][layer], x2d)
        ha, hb, afft = router(x2d, p["norm_ffn_g"][layer], p["w_router"][layer])
        pos, offs = ec_select(afft, cap)
        ne = afft.shape[0]
        idx, gates = ec_invert(pos, afft, cap)
        idx = idx.reshape(1, ne * cap)
        xa = ec_gather(ha, idx).reshape(ne, cap, -1)
        xb = ec_gather(hb, idx).reshape(ne, cap, -1)
        out = expert_ffn(xa, xb, gates, p["w_gate"][layer], p["w_up"][layer],
                         p["w_down"][layer])
        x2d = ec_combine(x2d, pos, offs, out)
    return rms_norm_call(x2d, p["final_norm_g"]).reshape(b, t, d)


def kernel(x_prompt, x_sample, norm_mix_g, w_in, lambda_q1, lambda_k1, lambda_q2, lambda_k2,
           diff_subln_g, hgrn_lb, hgrn_norm_g, w_out, norm_ffn_g, w_router, w_gate, w_up,
           w_down, final_norm_g):
    bf = jnp.bfloat16
    p = dict(norm_mix_g=norm_mix_g, w_in=w_in.astype(bf), lambda_q1=lambda_q1,
             lambda_k1=lambda_k1, lambda_q2=lambda_q2, lambda_k2=lambda_k2,
             diff_subln_g=diff_subln_g, hgrn_lb=hgrn_lb, hgrn_norm_g=hgrn_norm_g,
             w_out=w_out.astype(bf), norm_ffn_g=norm_ffn_g, w_router=w_router,
             w_gate=w_gate.astype(bf), w_up=w_up.astype(bf), w_down=w_down.astype(bf),
             final_norm_g=final_norm_g)
    return (_trunk(x_prompt, p), _trunk(x_sample, p))
```

```python
import functools
import math

import jax
import jax.numpy as jnp
from jax import lax
from jax.experimental import pallas as pl
from jax.experimental.pallas import tpu as pltpu
from jax.experimental.pallas import tpu_sc as plsc

ATT_HEADS = 4
ATT_QK_DIM = 64
ATT_V_DIM = 128
HG_HEADS = 4
HG_K = 128
HG_V = 128
HG_CHUNK = 64
N_EXPERTS = 16
EC_CAPACITY = 2
EPS = 1e-6
LANES = 128
SUBLANES = 8

_QBLK, _KBLK, _VBLK = 0, 4, 8
_HQ, _HFF, _HFB, _HI, _HG = 12, 16, 20, 24, 28

_VMEM_LIMIT = 56 * 1024 * 1024

_NT = (((1,), (1,)), ((), ()))


def _cparams(sem):
    return pltpu.CompilerParams(dimension_semantics=sem, vmem_limit_bytes=_VMEM_LIMIT)


def _norm_proj_kernel(x_ref, g_ref, w_ref, o_ref, *, n_split):
    x = x_ref[...]
    y = x * lax.rsqrt(jnp.mean(x * x, axis=-1, keepdims=True) + EPS) * g_ref[...]
    h = y.astype(jnp.bfloat16)
    wn = w_ref.shape[1] // n_split
    for c in range(n_split):
        o_ref[:, c * wn:(c + 1) * wn] = jnp.dot(
            h, w_ref[:, c * wn:(c + 1) * wn], preferred_element_type=jnp.float32
        ).astype(o_ref.dtype)


def norm_proj(x2d, g, w_bf16, tm=512):
    n, d = x2d.shape
    tm = min(tm, n)
    dout = w_bf16.shape[1]
    return pl.pallas_call(
        functools.partial(_norm_proj_kernel, n_split=max(1, dout // 1024)),
        grid=(n // tm,),
        in_specs=[
            pl.BlockSpec((tm, d), lambda i: (i, 0)),
            pl.BlockSpec((1, d), lambda i: (0, 0)),
            pl.BlockSpec((d, dout), lambda i: (0, 0)),
        ],
        out_specs=pl.BlockSpec((tm, dout), lambda i: (i, 0)),
        out_shape=jax.ShapeDtypeStruct((n, dout), jnp.bfloat16),
        compiler_params=_cparams(("parallel",)),
        name="norm_proj",
    )(x2d, g.reshape(1, d), w_bf16)


_LOG2E = 1.4426950408889634
_Q_STRIP = 128
_ONES_ROWS = 16


def _split3(x):
    bf = jnp.bfloat16
    h = x.astype(bf).astype(jnp.float32)
    r = x - h
    m = r.astype(bf).astype(jnp.float32)
    lo = (r - m).astype(bf).astype(jnp.float32)
    return h, m, lo


def _attn_kernel(beta_ref, qext_ref, lq1_ref, lk1_ref, lq2_ref, lk2_ref, subg_ref,
                 q_ref, k_ref, v_ref, kext_ref, o_ref, vt_sc, acc_sc, s_sc, *, lam_init, tk, tq):
    f32, bf = jnp.float32, jnp.bfloat16
    t = q_ref.shape[0]
    nk = t // tk
    qs_ = _Q_STRIP
    ng = tq // qs_
    ncol = 2 * tq
    dv = v_ref.shape[1]
    beta = beta_ref[0:1, 0:1]

    def vt_body(c, carry):
        r0 = pl.multiple_of(c * tk, tk)
        vt_sc[0:dv, pl.ds(r0, tk)] = v_ref[pl.ds(r0, tk), :].astype(f32).T.astype(bf)
        return carry

    lax.fori_loop(0, nk, vt_body, 0)
    vt_sc[dv:dv + _ONES_ROWS, :] = jnp.ones((_ONES_ROWS, t), bf)

    lam = (jnp.exp(jnp.sum(lq1_ref[...] * lk1_ref[...], axis=-1, keepdims=True))
           - jnp.exp(jnp.sum(lq2_ref[...] * lk2_ref[...], axis=-1, keepdims=True)) + lam_init)
    lane = lax.broadcasted_iota(jnp.int32, (qs_, LANES), 1)
    kr = lax.broadcasted_iota(jnp.int32, (tk, qs_), 0)
    qc = lax.broadcasted_iota(jnp.int32, (tk, qs_), 1)
    cscale = ATT_QK_DIM ** -0.5 * _LOG2E

    def group(gi, carry):
        g0 = pl.multiple_of(gi * tq, tq)
        tiles = []
        for u in range(ng):
            q0 = g0 + u * qs_
            qs = (q_ref[pl.ds(q0, qs_), :].astype(f32) * cscale).astype(bf).astype(f32)
            qext = qext_ref[...]
            q_aug = jnp.concatenate([
                jnp.concatenate([jnp.where(lane < ATT_QK_DIM, qs, 0.0), qext], axis=1),
                jnp.concatenate([jnp.where(lane >= ATT_QK_DIM, qs, 0.0), qext], axis=1)], axis=0)
            tiles.append(q_aug.T.astype(bf))
        w = jnp.concatenate(tiles, axis=1)
        jc = g0 // tk

        def corr_tile(d):
            corr = 2.0 * beta * jnp.minimum((d + qc - kr).astype(f32), 0.0)
            return jnp.concatenate([corr, corr], axis=1)

        def shift(d, sidx):
            sgn = 1.0 if sidx == 0 else -1.0
            return (-sgn) * beta * d.astype(f32)

        def scores(n, sidx, crossing):
            r0 = pl.multiple_of(n * tk, tk)
            k_aug = jnp.concatenate([k_ref[pl.ds(r0, tk), :], kext_ref[sidx]], axis=1)
            for u in range(ng):
                c0 = u * 2 * qs_
                s = jnp.dot(k_aug, w[:, c0:c0 + 2 * qs_], preferred_element_type=f32)
                if crossing:
                    s = s + corr_tile(g0 + u * qs_ - r0)
                s_sc[:, c0:c0 + 2 * qs_] = s

        def step(n, mc, sidx, crossing):
            r0 = pl.multiple_of((n - 1) * tk, tk)
            r1 = pl.multiple_of(n * tk, tk)
            vt = vt_sc[:, pl.ds(r0, tk)]
            if sidx is not None:
                k_aug = jnp.concatenate([k_ref[pl.ds(r1, tk), :], kext_ref[sidx]], axis=1)
            m, csts = mc
            ms, cs = [], []
            for u in range(ng):
                c0 = u * 2 * qs_
                s = s_sc[:, c0:c0 + 2 * qs_]
                m_old = m[:, c0:c0 + 2 * qs_]
                cst = csts[:, c0:c0 + 2 * qs_]
                m_new = jnp.maximum(m_old, jnp.max(s, axis=0, keepdims=True) + cst)
                alpha = jnp.exp2(m_old - m_new)
                p = jnp.exp2(s - (m_new - cst)).astype(bf)
                acc_sc[:, c0:c0 + 2 * qs_] = (acc_sc[:, c0:c0 + 2 * qs_] * alpha
                                              + jnp.dot(vt, p, preferred_element_type=f32))
                ms.append(m_new)
                if sidx is not None:
                    s = jnp.dot(k_aug, w[:, c0:c0 + 2 * qs_], preferred_element_type=f32)
                    if crossing:
                        s = s + corr_tile(g0 + u * qs_ - r1)
                    s_sc[:, c0:c0 + 2 * qs_] = s
                    cs.append(jnp.broadcast_to(shift(g0 + u * qs_ - r1, sidx), (1, 2 * qs_)))
            return jnp.concatenate(ms, axis=1), (jnp.concatenate(cs, axis=1) if cs else csts)

        acc_sc[...] = jnp.zeros(acc_sc.shape, f32)
        m = jnp.full((1, ncol), -1e30, f32)
        scores(0, 0, True)
        n1 = jnp.maximum(jc, 1)
        n2 = jc + tq // tk
        cst0 = jnp.concatenate([jnp.broadcast_to(shift(g0 + u * qs_, 0), (1, 2 * qs_))
                                for u in range(ng)], axis=1)
        mc = (m, cst0)
        mc = lax.fori_loop(1, n1, functools.partial(step, sidx=0, crossing=False), mc)
        mc = lax.fori_loop(n1, n2, functools.partial(step, sidx=0, crossing=True), mc)
        mc = lax.fori_loop(n2, nk, functools.partial(step, sidx=1, crossing=False), mc)
        step(nk, mc, None, False)

        for u in range(ng):
            c0 = u * 2 * qs_
            acc = acc_sc[:, c0:c0 + 2 * qs_]
            den = acc[dv:dv + 1, :]
            o = acc[0:dv, 0:qs_] / den[:, 0:qs_] - lam * (acc[0:dv, qs_:] / den[:, qs_:])
            y = o * lax.rsqrt(jnp.mean(o * o, axis=0, keepdims=True) + EPS) * subg_ref[...]
            o_ref[pl.ds(g0 + u * qs_, qs_), :] = (y * (1.0 - lam_init)).T.astype(o_ref.dtype)
        return carry

    lax.fori_loop(0, t // tq, group, 0)


def _attn_tables(tk):
    f32 = jnp.float32
    slopes = 2.0 ** (-8.0 * (jnp.arange(ATT_HEADS, dtype=f32) + 1.0) / ATT_HEADS)
    beta = slopes * _LOG2E
    lane = jnp.arange(LANES)[None, None, :]
    b = _split3(beta)
    a = _split3(-beta[:, None] * jnp.arange(_Q_STRIP, dtype=f32)[None, :])
    qext = jnp.zeros((ATT_HEADS, _Q_STRIP, LANES), f32)
    for c in range(3):
        qext = jnp.where(lane == c, b[c][:, None, None], qext)
        qext = jnp.where(lane == 3 + c, a[c][:, :, None], qext)
    r = jnp.arange(tk, dtype=f32)[:, None]
    lane2 = jnp.arange(LANES)[None, :]
    base = jnp.where(lane2 < 3, r, jnp.where(lane2 < 6, 1.0, 0.0))
    kext = jnp.stack([base, -base]).astype(jnp.bfloat16)
    beta_t = jnp.broadcast_to(beta[:, None, None], (ATT_HEADS, 1, LANES))
    return beta_t, qext, kext


def diff_attention(proj3d, lq1, lk1, lq2, lk2, subg, layer, tk=256, tq=2048):
    b, t, _ = proj3d.shape
    tk, tq = min(tk, t), min(tq, t)
    assert tq % tk == 0 and tq % _Q_STRIP == 0
    lam_init = 0.8 - 0.6 * math.exp(-0.3 * layer)
    assert tk <= 256
    beta_t, qext, kext = _attn_tables(tk)
    vec = lambda a: a.reshape(1, -1).astype(jnp.float32)
    small = lambda n: pl.BlockSpec((1, n), lambda bi, h: (0, 0))
    head = pl.BlockSpec((None, 1, LANES), lambda bi, h: (h, 0, 0))
    col = lambda blk: pl.BlockSpec((None, t, LANES), lambda bi, h: (bi, 0, blk + h))
    return pl.pallas_call(
        functools.partial(_attn_kernel, lam_init=lam_init, tk=tk, tq=tq),
        grid=(b, ATT_HEADS),
        in_specs=[
            head, pl.BlockSpec((None, _Q_STRIP, LANES), lambda bi, h: (h, 0, 0)),
            small(ATT_QK_DIM), small(ATT_QK_DIM), small(ATT_QK_DIM), small(ATT_QK_DIM),
            pl.BlockSpec((ATT_V_DIM, 1), lambda bi, h: (0, 0)),
            col(_QBLK), col(_KBLK), col(_VBLK),
            pl.BlockSpec((2, tk, LANES), lambda bi, h: (0, 0, 0)),
        ],
        out_specs=pl.BlockSpec((None, t, LANES), lambda bi, h: (bi, 0, h)),
        out_shape=jax.ShapeDtypeStruct((b, t, ATT_HEADS * ATT_V_DIM), jnp.bfloat16),
        scratch_shapes=[pltpu.VMEM((ATT_V_DIM + _ONES_ROWS, t), jnp.bfloat16),
                        pltpu.VMEM((ATT_V_DIM + _ONES_ROWS, 2 * tq), jnp.float32),
                        pltpu.VMEM((tk, 2 * tq), jnp.float32)],
        compiler_params=_cparams(("parallel", "parallel")),
        name="diff_attn",
    )(beta_t, qext, vec(lq1), vec(lk1), vec(lq2), vec(lk2),
      subg.reshape(ATT_V_DIM, 1).astype(jnp.float32), proj3d, proj3d, proj3d, kext)


_HG_CHUNKS_PER_ITER = 4
_HG_FIN_ROWS = 256


def _scan_rows(g, rev):
    c = g.shape[0]
    row = lax.broadcasted_iota(jnp.int32, g.shape, 0)
    b = g
    s = 1
    while s < c:
        if rev:
            b = b + jnp.where(row < c - s, pltpu.roll(b, c - s, axis=0), 0.0)
        else:
            b = b + jnp.where(row >= s, pltpu.roll(b, s, axis=0), 0.0)
        s *= 2
    return b


def _level_ref(b, m, rev):
    c, w = b.shape
    r = m // 2 if rev else m // 2 - 1
    pieces = []
    if m >= SUBLANES:
        for j in range(c // m):
            pieces.append(jnp.broadcast_to(b[j * m + r:j * m + r + 1, :], (m, w)))
    else:
        sub = lax.broadcasted_iota(jnp.int32, (SUBLANES, w), 0)
        for j in range(c // SUBLANES):
            base = j * SUBLANES
            acc = jnp.broadcast_to(b[base + r:base + r + 1, :], (SUBLANES, w))
            for i in range(1, SUBLANES // m):
                cand = jnp.broadcast_to(b[base + i * m + r:base + i * m + r + 1, :], (SUBLANES, w))
                acc = jnp.where(sub >= i * m, cand, acc)
            pieces.append(acc)
    return jnp.concatenate(pieces, axis=0)


def _neg_abs(x):
    bits = lax.bitcast_convert_type(x, jnp.uint32) | jnp.uint32(0x80000000)
    return lax.bitcast_convert_type(bits, jnp.float32)


def _hgrn_masks(c, rev):
    row = lax.broadcasted_iota(jnp.int32, (c, LANES), 0)
    ti = lax.broadcasted_iota(jnp.int32, (c, c), 0)
    si = lax.broadcasted_iota(jnp.int32, (c, c), 1)
    qrows, pairs = {}, {}
    m = c
    while m >= 2:
        late = (row & (m - 1)) >= m // 2
        t_late = (ti & (m - 1)) >= m // 2
        s_late = (si & (m - 1)) >= m // 2
        same = (ti ^ si) < m
        if rev:
            qrows[m] = jnp.logical_not(late)
            pairs[m] = same & jnp.logical_not(t_late) & s_late
        else:
            qrows[m] = late
            pairs[m] = same & t_late & jnp.logical_not(s_late)
        m //= 2
    return qrows, pairs, ti == si


def _hgrn_chunk(q, f, v, st_ref, rev, masks):
    qrows, pairs, diag = masks
    c = q.shape[0]
    bf = jnp.bfloat16
    k_ = 1.0 - f
    b = _scan_rows(jnp.log(f) * _LOG2E, rev)
    a = jnp.where(diag,
                  lax.dot_general(q.astype(bf), k_.astype(bf), _NT,
                                  preferred_element_type=jnp.float32), 0.0)
    m = c
    while m >= 2:
        if m == 2:
            cref = jnp.where(qrows[m], pltpu.roll(b, c - 1 if rev else 1, axis=0), b)
        else:
            cref = _level_ref(b, m, rev)
        e = jnp.exp2(_neg_abs(b - cref))
        z = (jnp.where(qrows[m], q, k_) * e).astype(bf)
        p = lax.dot_general(z, z, _NT, preferred_element_type=jnp.float32)
        a = jnp.where(pairs[m], p, a)
        m //= 2
    st = st_ref[...]
    o = jnp.dot(a.astype(bf), v.astype(bf), preferred_element_type=jnp.float32)
    o = o + lax.dot_general((q * jnp.exp2(b)).astype(bf), st.astype(bf), _NT,
                            preferred_element_type=jnp.float32)
    btot = b[0:1, :] if rev else b[c - 1:c, :]
    khat = (k_ * jnp.exp2(btot - b)).astype(bf)
    st_ref[...] = st * jnp.exp2(btot) + jnp.dot(v.T.astype(bf), khat,
                                                preferred_element_type=jnp.float32)
    return o


def _hgrn_kernel(lb_ref, ng_ref, q_ref, ff_ref, fb_ref, i_ref, g_ref, o_ref,
                 of_sc, ob_sc, stf_sc, stb_sc, *, layer, chunk):
    t = q_ref.shape[0]
    n = t // chunk
    lb_raw = lb_ref[...].astype(jnp.float32)
    ex = jnp.exp(lb_raw - jnp.max(lb_raw, axis=0, keepdims=True))
    sm = ex / jnp.sum(ex, axis=0, keepdims=True)
    lb = jnp.zeros_like(sm[0])
    for l_ in range(1, layer + 1):
        lb = lb + sm[l_]
    lb_f = lb[0:1, :]
    lb_b = lb[1:2, :]

    stf_sc[...] = jnp.zeros(stf_sc.shape, jnp.float32)
    stb_sc[...] = jnp.zeros(stb_sc.shape, jnp.float32)

    def load(ref, r0):
        return ref[pl.ds(r0, chunk), :].astype(jnp.float32)

    masks_f = _hgrn_masks(chunk, False)
    masks_b = _hgrn_masks(chunk, True)

    per_iter = _HG_CHUNKS_PER_ITER if n % _HG_CHUNKS_PER_ITER == 0 else 1

    def body(ci, carry):
        for u in range(per_iter):
            r0 = pl.multiple_of((ci * per_iter + u) * chunk, chunk)
            qr = load(q_ref, r0)
            f = lb_f + (1.0 - lb_f) * jax.nn.sigmoid(load(ff_ref, r0))
            of_sc[pl.ds(r0, chunk), :] = _hgrn_chunk(qr * jax.nn.sigmoid(qr), f,
                                                     load(i_ref, r0), stf_sc, False, masks_f)
            r1 = pl.multiple_of((n - 1 - ci * per_iter - u) * chunk, chunk)
            qr = load(q_ref, r1)
            f = lb_b + (1.0 - lb_b) * jax.nn.sigmoid(load(fb_ref, r1))
            ob_sc[pl.ds(r1, chunk), :] = _hgrn_chunk(qr * jax.nn.sigmoid(qr), f,
                                                     load(i_ref, r1), stb_sc, True, masks_b)
        return carry

    lax.fori_loop(0, n // per_iter, body, 0)

    rows = math.gcd(t, _HG_FIN_ROWS)

    def fin(ci, carry):
        r0 = pl.multiple_of(ci * rows, rows)
        o = of_sc[pl.ds(r0, rows), :] + ob_sc[pl.ds(r0, rows), :]
        y = o * lax.rsqrt(jnp.mean(o * o, axis=-1, keepdims=True) + EPS) * ng_ref[...]
        gr = g_ref[pl.ds(r0, rows), :].astype(jnp.float32)
        o_ref[pl.ds(r0, rows), :] = (y * (gr * jax.nn.sigmoid(gr))).astype(o_ref.dtype)
        return carry

    lax.fori_loop(0, t // rows, fin, 0)


def hgrn_bidir(proj3d, hgrn_lb, norm_g, layer):
    b, t, _ = proj3d.shape
    depth = hgrn_lb.shape[0]
    col = lambda blk: pl.BlockSpec((None, t, LANES), lambda bi, h: (bi, 0, blk + h))
    return pl.pallas_call(
        functools.partial(_hgrn_kernel, layer=layer, chunk=HG_CHUNK),
        grid=(b, HG_HEADS),
        in_specs=[
            pl.BlockSpec((depth, 2, LANES), lambda bi, h: (0, 0, h)),
            pl.BlockSpec((1, HG_V), lambda bi, h: (0, 0)),
            col(_HQ), col(_HFF), col(_HFB), col(_HI), col(_HG),
        ],
        out_specs=pl.BlockSpec((None, t, LANES), lambda bi, h: (bi, 0, h)),
        out_shape=jax.ShapeDtypeStruct((b, t, HG_HEADS * HG_V), jnp.bfloat16),
        scratch_shapes=[
            pltpu.VMEM((t, HG_V), jnp.float32),
            pltpu.VMEM((t, HG_V), jnp.float32),
            pltpu.VMEM((HG_V, HG_K), jnp.float32),
            pltpu.VMEM((HG_V, HG_K), jnp.float32),
        ],
        compiler_params=_cparams(("parallel", "parallel")),
        name="hgrn_bidir",
    )(hgrn_lb, norm_g.reshape(1, HG_V).astype(jnp.float32),
      proj3d, proj3d, proj3d, proj3d, proj3d)


def _out_proj_kernel(a_ref, h_ref, wa_ref, wh_ref, x_ref, o_ref):
    y = jnp.dot(a_ref[...], wa_ref[...], preferred_element_type=jnp.float32)
    y = y + jnp.dot(h_ref[...], wh_ref[...], preferred_element_type=jnp.float32)
    o_ref[...] = x_ref[...] + y


def out_proj(att2d, hgo2d, w_out_bf16, x2d, tm=512):
    n, d = x2d.shape
    tm = min(tm, n)
    wa = att2d.shape[1]
    wh = hgo2d.shape[1]
    return pl.pallas_call(
        _out_proj_kernel,
        grid=(n // tm,),
        in_specs=[
            pl.BlockSpec((tm, wa), lambda i: (i, 0)),
            pl.BlockSpec((tm, wh), lambda i: (i, 0)),
            pl.BlockSpec((wa, d), lambda i: (0, 0)),
            pl.BlockSpec((wh, d), lambda i: (wa // wh, 0)),
            pl.BlockSpec((tm, d), lambda i: (i, 0)),
        ],
        out_specs=pl.BlockSpec((tm, d), lambda i: (i, 0)),
        out_shape=jax.ShapeDtypeStruct((n, d), jnp.float32),
        compiler_params=_cparams(("parallel",)),
        name="out_proj",
    )(att2d, hgo2d, w_out_bf16, w_out_bf16, x2d)


def _router_kernel(x_ref, g_ref, wrt_ref, ha_ref, hb_ref, afft_ref):
    x = x_ref[...]
    h = x * lax.rsqrt(jnp.mean(x * x, axis=-1, keepdims=True) + EPS) * g_ref[...]
    bits = lax.bitcast_convert_type(h.astype(jnp.bfloat16).astype(jnp.float32), jnp.uint32)
    q = ha_ref.shape[1]
    ha_ref[...] = (bits[:, 0:q] >> 16) | bits[:, q:2 * q]
    hb_ref[...] = (bits[:, 2 * q:3 * q] >> 16) | bits[:, 3 * q:4 * q]
    logits = lax.dot_general(wrt_ref[...], h, _NT, preferred_element_type=jnp.float32,
                             precision=lax.Precision.HIGHEST)
    ex = jnp.exp(logits - jnp.max(logits, axis=0, keepdims=True))
    afft_ref[...] = ex / jnp.sum(ex, axis=0, keepdims=True)


def router(x2d, g, w_router, tm=512):
    n, d = x2d.shape
    tm = min(tm, n)
    e = w_router.shape[1]
    return pl.pallas_call(
        _router_kernel,
        grid=(n // tm,),
        in_specs=[
            pl.BlockSpec((tm, d), lambda i: (i, 0)),
            pl.BlockSpec((1, d), lambda i: (0, 0)),
            pl.BlockSpec((e, d), lambda i: (0, 0)),
        ],
        out_specs=[
            pl.BlockSpec((tm, d // 4), lambda i: (i, 0)),
            pl.BlockSpec((tm, d // 4), lambda i: (i, 0)),
            pl.BlockSpec((e, tm), lambda i: (0, i)),
        ],
        out_shape=[
            jax.ShapeDtypeStruct((n, d // 4), jnp.uint32),
            jax.ShapeDtypeStruct((n, d // 4), jnp.uint32),
            jax.ShapeDtypeStruct((e, n), jnp.float32),
        ],
        compiler_params=_cparams(("parallel",)),
        name="router",
    )(x2d, g.reshape(1, d), w_router.T)


_SEL_CHUNK = 1024


def _select_kernel(afft_ref, pos_ref, offs_ref, *, cap):
    f32, i32 = jnp.float32, jnp.int32
    e, n = afft_ref.shape
    chunk = min(_SEL_CHUNK, n)
    nt = n // LANES

    def bits_at(start, size):
        return lax.bitcast_convert_type(afft_ref[:, pl.ds(start, size)], i32)

    def count_ge(cand):
        def body(c, acc):
            x = bits_at(pl.multiple_of(c * chunk, chunk), chunk)
            return acc + jnp.where(x >= cand, 1.0, 0.0)
        acc = lax.fori_loop(0, n // chunk, body, jnp.zeros((e, chunk), f32))
        return jnp.sum(acc, axis=1, keepdims=True)

    def bit_body(it, prefix):
        cand = prefix | jnp.left_shift(jnp.int32(1), 30 - it)
        return jnp.where(count_ge(cand) >= cap, cand, prefix)

    thr = lax.fori_loop(0, 31, bit_body, jnp.zeros((e, 1), i32))
    need = cap - count_ge(thr + 1)

    tri = (lax.broadcasted_iota(i32, (LANES, LANES), 0)
           <= lax.broadcasted_iota(i32, (LANES, LANES), 1)).astype(jnp.bfloat16)
    tile_id = lax.broadcasted_iota(i32, (e, nt), 1)

    def tile_body(j, carry):
        c_gt, c_eq, offs_acc = carry
        x = bits_at(pl.multiple_of(j * LANES, LANES), LANES)
        gt = x > thr
        eq = x == thr
        both = jnp.concatenate([jnp.where(gt, 1.0, 0.0), jnp.where(eq, 1.0, 0.0)],
                               axis=0).astype(jnp.bfloat16)
        inc = jnp.dot(both, tri, preferred_element_type=f32)
        inc_gt = inc[:e] + c_gt
        inc_eq = inc[e:] + c_eq
        sel = gt | (eq & (inc_eq <= need))
        incl = inc_gt + jnp.minimum(inc_eq, need)
        pos_ref[:, pl.ds(pl.multiple_of(j * LANES, LANES), LANES)] = jnp.where(
            sel, incl - 1.0, -1.0).astype(i32)
        offs_acc = jnp.where(tile_id == j, c_gt + jnp.minimum(c_eq, need), offs_acc)
        return inc_gt[:, LANES - 1:LANES], inc_eq[:, LANES - 1:LANES], offs_acc

    zero = jnp.zeros((e, 1), f32)
    _, _, offs_acc = lax.fori_loop(0, nt, tile_body, (zero, zero, jnp.zeros((e, nt), f32)))
    offs_ref[...] = offs_acc.astype(i32)


def ec_select(afft, cap):
    e, n = afft.shape
    return pl.pallas_call(
        functools.partial(_select_kernel, cap=cap),
        out_shape=[jax.ShapeDtypeStruct((e, n), jnp.int32),
                   jax.ShapeDtypeStruct((e, n // LANES), jnp.int32)],
        compiler_params=pltpu.CompilerParams(vmem_limit_bytes=_VMEM_LIMIT),
        name="ec_select",
    )(afft)


_SC_LANES = 16
_SC_CORES = 2
_INV_CHUNK = 8192


def ec_invert(pos, afft, cap):
    ne, n = pos.shape
    ch = min(_INV_CHUNK, n)
    mesh = plsc.VectorSubcoreMesh(core_axis_name="c", subcore_axis_name="s")

    @functools.partial(
        pl.kernel, mesh=mesh,
        out_type=[jax.ShapeDtypeStruct((ne * cap,), jnp.int32),
                  jax.ShapeDtypeStruct((ne * cap,), jnp.float32)],
        scratch_types=[pltpu.VMEM((ch,), jnp.int32), pltpu.VMEM((ch,), jnp.float32),
                       pltpu.VMEM((cap,), jnp.int32), pltpu.VMEM((cap,), jnp.float32)],
        compiler_params=pltpu.CompilerParams(needs_layout_passes=False))
    def invert(pos_hbm, aff_hbm, idx_hbm, gate_hbm, pos_v, aff_v, idx_v, gate_v):
        wid = lax.axis_index("s") * _SC_CORES + lax.axis_index("c")

        @pl.when(wid < ne)
        def _():
            @pl.loop(0, n // ch)
            def _(c):
                pltpu.sync_copy(pos_hbm.at[pl.ds(wid * n + c * ch, ch)], pos_v)
                pltpu.sync_copy(aff_hbm.at[pl.ds(wid * n + c * ch, ch)], aff_v)

                @pl.loop(0, ch, step=_SC_LANES)
                def _(i):
                    p = pos_v[pl.ds(i, _SC_LANES)]
                    tok = lax.iota(jnp.int32, _SC_LANES) + (c * ch + i)
                    plsc.store_scatter(idx_v, [p], tok, mask=p >= 0)
                    plsc.store_scatter(gate_v, [p], aff_v[pl.ds(i, _SC_LANES)], mask=p >= 0)

            pltpu.sync_copy(idx_v, idx_hbm.at[pl.ds(wid * cap, cap)])
            pltpu.sync_copy(gate_v, gate_hbm.at[pl.ds(wid * cap, cap)])

    idx, gates = invert(pos.reshape(ne * n), afft.reshape(ne * n))
    return idx.reshape(ne, cap), gates.reshape(ne, cap)


_GATHER_WIN = 128


def ec_gather(table, idx_row):
    m = idx_row.shape[1]
    width = table.shape[1]
    steps = m // (_GATHER_WIN * _SC_CORES)
    mesh = plsc.VectorSubcoreMesh(core_axis_name="c", subcore_axis_name="s")

    @functools.partial(
        pl.kernel, mesh=mesh,
        out_type=jax.ShapeDtypeStruct((m, width), table.dtype),
        scratch_types=[])
    def gather(x_hbm, i_hbm, o_hbm):
        def body(i_vmem, o_vmem):
            pltpu.sync_copy(x_hbm.at[i_vmem.at[0]], o_vmem)

        pltpu.emit_pipeline(
            body,
            grid=(_SC_CORES, steps),
            in_specs=[pl.BlockSpec((1, _GATHER_WIN), index_map=lambda a, i: (0, a * steps + i))],
            out_specs=[pl.BlockSpec((_GATHER_WIN, width),
                                    index_map=lambda a, i: (a * steps + i, 0))],
            core_axis_name=("c", "s"),
            dimension_semantics=(pltpu.PARALLEL, pltpu.PARALLEL),
        )(i_hbm, o_hbm)

    return gather(table, idx_row)


def _expert_kernel(xa_ref, xb_ref, gate_ref, wg_ref, wu_ref, wd_ref, o_ref):
    @pl.when(pl.program_id(1) == pl.num_programs(1) - 1)
    def _():
        o_ref[...] = jnp.zeros(o_ref.shape, o_ref.dtype)

    @pl.when(pl.program_id(1) < pl.num_programs(1) - 1)
    def _():
        as_f32 = lambda bits: lax.bitcast_convert_type(bits, jnp.float32)
        wa, wb = xa_ref[...], xb_ref[...]
        hi = jnp.uint32(0xFFFF0000)
        xs = jnp.concatenate([as_f32(wa << 16), as_f32(wa & hi), as_f32(wb << 16),
                              as_f32(wb & hi)], axis=1).astype(jnp.bfloat16)
        a = jnp.dot(xs, wg_ref[...], preferred_element_type=jnp.float32)
        u = jnp.dot(xs, wu_ref[...], preferred_element_type=jnp.float32)
        hid = (a * jax.nn.sigmoid(a) * u).astype(jnp.bfloat16)
        y = jnp.dot(hid, wd_ref[...], preferred_element_type=jnp.float32)
        o_ref[...] = (y * gate_ref[...]).astype(o_ref.dtype)


def expert_ffn(xa, xb, gates, wg, wu, wd, tm=512):
    e, c, q = xa.shape
    d = 4 * q
    tm = min(tm, c)
    f = wg.shape[2]
    nt = c // tm
    return pl.pallas_call(
        _expert_kernel,
        grid=(e, nt + 1),
        in_specs=[
            pl.BlockSpec((None, tm, q), lambda ei, i: (ei, jnp.minimum(i, nt - 1), 0)),
            pl.BlockSpec((None, tm, q), lambda ei, i: (ei, jnp.minimum(i, nt - 1), 0)),
            pl.BlockSpec((None, tm, 1), lambda ei, i: (ei, jnp.minimum(i, nt - 1), 0)),
            pl.BlockSpec((None, d, f), lambda ei, i: (ei, 0, 0)),
            pl.BlockSpec((None, d, f), lambda ei, i: (ei, 0, 0)),
            pl.BlockSpec((None, f, d), lambda ei, i: (ei, 0, 0)),
        ],
        out_specs=pl.BlockSpec((None, tm, d), lambda ei, i: (ei, i, 0)),
        out_shape=jax.ShapeDtypeStruct((e, c + tm, d), jnp.bfloat16),
        compiler_params=_cparams(("parallel", "arbitrary")),
        name="expert_ffn",
    )(xa, xb, gates.reshape(e, c, 1), wg, wu, wd)


_SLAB_ALIGN = 16
_SLAB_ROWS = LANES + _SLAB_ALIGN


def _slab_copy(offs_ref, out_hbm, slab_ref, sem_ref, e, j, slot):
    start = pl.multiple_of((offs_ref[e, j] // _SLAB_ALIGN) * _SLAB_ALIGN, _SLAB_ALIGN)
    return pltpu.make_async_copy(out_hbm.at[e, pl.ds(start, _SLAB_ROWS), :],
                                 slab_ref.at[slot, e], sem_ref.at[slot, e])


def _combine_kernel(offs_ref, x_ref, pos_ref, out_hbm, o_ref, slab_ref, sem_ref):
    f32 = jnp.float32
    ne = pos_ref.shape[0]
    j = pl.program_id(0)
    slot = j % 2

    @pl.when(j == 0)
    def _():
        for e in range(ne):
            _slab_copy(offs_ref, out_hbm, slab_ref, sem_ref, e, j, slot).start()

    @pl.when(j + 1 < pl.num_programs(0))
    def _():
        for e in range(ne):
            _slab_copy(offs_ref, out_hbm, slab_ref, sem_ref, e, j + 1, 1 - slot).start()

    pad = jnp.zeros((LANES - ne, LANES), f32)
    pos_t = jnp.concatenate([pos_ref[...].astype(f32), pad], axis=0).T
    row_id = lax.broadcasted_iota(jnp.int32, (LANES, _SLAB_ROWS), 1).astype(f32)
    for e in range(ne):
        _slab_copy(offs_ref, out_hbm, slab_ref, sem_ref, e, j, slot).wait()
    y = x_ref[...]
    for e in range(ne):
        start = (offs_ref[e, j] // _SLAB_ALIGN) * _SLAB_ALIGN
        rel = pos_t[:, e:e + 1] - start.astype(f32)
        onehot = jnp.where(rel == row_id, 1.0, 0.0).astype(jnp.bfloat16)
        y = y + jnp.dot(onehot, slab_ref[slot, e], preferred_element_type=f32)
    o_ref[...] = y


def ec_combine(x2d, pos, offs, out):
    n, d = x2d.shape
    ne, crows, _ = out.shape
    assert crows >= n * EC_CAPACITY // ne + _SLAB_ROWS
    grid_spec = pltpu.PrefetchScalarGridSpec(
        num_scalar_prefetch=1,
        grid=(n // LANES,),
        in_specs=[
            pl.BlockSpec((LANES, d), lambda j, offs_ref: (j, 0)),
            pl.BlockSpec((ne, LANES), lambda j, offs_ref: (0, j)),
            pl.BlockSpec(memory_space=pl.ANY),
        ],
        out_specs=pl.BlockSpec((LANES, d), lambda j, offs_ref: (j, 0)),
        scratch_shapes=[pltpu.VMEM((2, ne, _SLAB_ROWS, d), out.dtype),
                        pltpu.SemaphoreType.DMA((2, ne))],
    )
    return pl.pallas_call(
        _combine_kernel,
        grid_spec=grid_spec,
        out_shape=jax.ShapeDtypeStruct((n, d), jnp.float32),
        compiler_params=_cparams(("arbitrary",)),
        name="ec_combine",
    )(offs, x2d, pos, out)


def _rms_kernel(x_ref, g_ref, o_ref):
    x = x_ref[...]
    o_ref[...] = x * lax.rsqrt(jnp.mean(x * x, axis=-1, keepdims=True) + EPS) * g_ref[...]


def rms_norm_call(x2d, g, tm=1024):
    n, d = x2d.shape
    tm = min(tm, n)
    return pl.pallas_call(
        _rms_kernel,
        grid=(n // tm,),
        in_specs=[pl.BlockSpec((tm, d), lambda i: (i, 0)),
                  pl.BlockSpec((1, d), lambda i: (0, 0))],
        out_specs=pl.BlockSpec((tm, d), lambda i: (i, 0)),
        out_shape=jax.ShapeDtypeStruct((n, d), jnp.float32),
        compiler_params=_cparams(("parallel",)),
        name="final_norm",
    )(x2d, g.reshape(1, d))


def _trunk(x, p):
    b, t, d = x.shape
    n = b * t
    depth = p["w_in"].shape[0]
    x2d = x.reshape(n, d)
    cap = max(1, EC_CAPACITY * n // N_EXPERTS)
    for layer in range(depth):
        proj = norm_proj(x2d, p["norm_mix_g"][layer], p["w_in"][layer])
        proj3d = proj.reshape(b, t, -1)
        att = diff_attention(proj3d, p["lambda_q1"][layer], p["lambda_k1"][layer],
                             p["lambda_q2"][layer], p["lambda_k2"][layer],
                             p["diff_subln_g"][layer], layer)
        hgo = hgrn_bidir(proj3d, p["hgrn_lb"], p["hgrn_norm_g"][layer], layer)
        x2d = out_proj(att.reshape(n, -1), hgo.reshape(n, -1), p["w_out"][layer], x2d)
        ha, hb, afft = router(x2d, p["norm_ffn_g"][layer], p["w_router"][layer])
        pos, offs = ec_select(afft, cap)
        ne = afft.shape[0]
        idx, gates = ec_invert(pos, afft, cap)
        idx = idx.reshape(1, ne * cap)
        xa = ec_gather(ha, idx).reshape(ne, cap, -1)
        xb = ec_gather(hb, idx).reshape(ne, cap, -1)
        out = expert_ffn(xa, xb, gates, p["w_gate"][layer], p["w_up"][layer],
                         p["w_down"][layer])
        x2d = ec_combine(x2d, pos, offs, out)
    return rms_norm_call(x2d, p["final_norm_g"]).reshape(b, t, d)


def kernel(x_prompt, x_sample, norm_mix_g, w_in, lambda_q1, lambda_k1, lambda_q2, lambda_k2,
           diff_subln_g, hgrn_lb, hgrn_norm_g, w_out, norm_ffn_g, w_router, w_gate, w_up,
           w_down, final_norm_g):
    bf = jnp.bfloat16
    p = dict(norm_mix_g=norm_mix_g, w_in=w_in.astype(bf), lambda_q1=lambda_q1,
             lambda_k1=lambda_k1, lambda_q2=lambda_q2, lambda_k2=lambda_k2,
             diff_subln_g=diff_subln_g, hgrn_lb=hgrn_lb, hgrn_norm_g=hgrn_norm_g,
             w_out=w_out.astype(bf), norm_ffn_g=norm_ffn_g, w_router=w_router,
             w_gate=w_gate.astype(bf), w_up=w_up.astype(bf), w_down=w_down.astype(bf),
             final_norm_g=final_norm_g)
    return (_trunk(x_prompt, p), _trunk(x_sample, p))
```

```python
import functools
import math

import jax
import jax.numpy as jnp
from jax import lax
from jax.experimental import pallas as pl
from jax.experimental.pallas import tpu as pltpu
from jax.experimental.pallas import tpu_sc as plsc

ATT_HEADS = 4
ATT_QK_DIM = 64
ATT_V_DIM = 128
HG_HEADS = 4
HG_K = 128
HG_V = 128
HG_CHUNK = 64
N_EXPERTS = 16
EC_CAPACITY = 2
EPS = 1e-6
LANES = 128
SUBLANES = 8

_QBLK, _KBLK, _VBLK = 0, 4, 8
_HQ, _HFF, _HFB, _HI, _HG = 12, 16, 20, 24, 28

_VMEM_LIMIT = 56 * 1024 * 1024

_NT = (((1,), (1,)), ((), ()))


def _cparams(sem):
    return pltpu.CompilerParams(dimension_semantics=sem, vmem_limit_bytes=_VMEM_LIMIT)


def _norm_proj_kernel(x_ref, g_ref, w_ref, o_ref, *, n_split):
    x = x_ref[...]
    y = x * lax.rsqrt(jnp.mean(x * x, axis=-1, keepdims=True) + EPS) * g_ref[...]
    h = y.astype(jnp.bfloat16)
    wn = w_ref.shape[1] // n_split
    for c in range(n_split):
        o_ref[:, c * wn:(c + 1) * wn] = jnp.dot(
            h, w_ref[:, c * wn:(c + 1) * wn], preferred_element_type=jnp.float32
        ).astype(o_ref.dtype)


def norm_proj(x2d, g, w_bf16, tm=512):
    n, d = x2d.shape
    tm = min(tm, n)
    dout = w_bf16.shape[1]
    return pl.pallas_call(
        functools.partial(_norm_proj_kernel, n_split=max(1, dout // 1024)),
        grid=(n // tm,),
        in_specs=[
            pl.BlockSpec((tm, d), lambda i: (i, 0)),
            pl.BlockSpec((1, d), lambda i: (0, 0)),
            pl.BlockSpec((d, dout), lambda i: (0, 0)),
        ],
        out_specs=pl.BlockSpec((tm, dout), lambda i: (i, 0)),
        out_shape=jax.ShapeDtypeStruct((n, dout), jnp.bfloat16),
        compiler_params=_cparams(("parallel",)),
        name="norm_proj",
    )(x2d, g.reshape(1, d), w_bf16)


_LOG2E = 1.4426950408889634
_Q_STRIP = 128
_ONES_ROWS = 16


def _split3(x):
    bf = jnp.bfloat16
    h = x.astype(bf).astype(jnp.float32)
    r = x - h
    m = r.astype(bf).astype(jnp.float32)
    lo = (r - m).astype(bf).astype(jnp.float32)
    return h, m, lo


def _attn_kernel(beta_ref, qext_ref, lq1_ref, lk1_ref, lq2_ref, lk2_ref, subg_ref,
                 q_ref, k_ref, v_ref, kext_ref, o_ref, vt_sc, acc_sc, s_sc, *, lam_init, tk, tq):
    f32, bf = jnp.float32, jnp.bfloat16
    t = q_ref.shape[0]
    nk = t // tk
    qs_ = _Q_STRIP
    ng = tq // qs_
    ncol = 2 * tq
    dv = v_ref.shape[1]
    beta = beta_ref[0:1, 0:1]

    def vt_body(c, carry):
        r0 = pl.multiple_of(c * tk, tk)
        vt_sc[0:dv, pl.ds(r0, tk)] = v_ref[pl.ds(r0, tk), :].astype(f32).T.astype(bf)
        return carry

    lax.fori_loop(0, nk, vt_body, 0)
    vt_sc[dv:dv + _ONES_ROWS, :] = jnp.ones((_ONES_ROWS, t), bf)

    lam = (jnp.exp(jnp.sum(lq1_ref[...] * lk1_ref[...], axis=-1, keepdims=True))
           - jnp.exp(jnp.sum(lq2_ref[...] * lk2_ref[...], axis=-1, keepdims=True)) + lam_init)
    lane = lax.broadcasted_iota(jnp.int32, (qs_, LANES), 1)
    kr = lax.broadcasted_iota(jnp.int32, (tk, qs_), 0)
    qc = lax.broadcasted_iota(jnp.int32, (tk, qs_), 1)
    cscale = ATT_QK_DIM ** -0.5 * _LOG2E

    def group(gi, carry):
        g0 = pl.multiple_of(gi * tq, tq)
        tiles = []
        for u in range(ng):
            q0 = g0 + u * qs_
            qs = (q_ref[pl.ds(q0, qs_), :].astype(f32) * cscale).astype(bf).astype(f32)
            qext = qext_ref[...]
            q_aug = jnp.concatenate([
                jnp.concatenate([jnp.where(lane < ATT_QK_DIM, qs, 0.0), qext], axis=1),
                jnp.concatenate([jnp.where(lane >= ATT_QK_DIM, qs, 0.0), qext], axis=1)], axis=0)
            tiles.append(q_aug.T.astype(bf))
        w = jnp.concatenate(tiles, axis=1)
        jc = g0 // tk

        def corr_tile(d):
            corr = 2.0 * beta * jnp.minimum((d + qc - kr).astype(f32), 0.0)
            return jnp.concatenate([corr, corr], axis=1)

        def shift(d, sidx):
            sgn = 1.0 if sidx == 0 else -1.0
            return (-sgn) * beta * d.astype(f32)

        def scores(n, sidx, crossing):
            r0 = pl.multiple_of(n * tk, tk)
            k_aug = jnp.concatenate([k_ref[pl.ds(r0, tk), :], kext_ref[sidx]], axis=1)
            for u in range(ng):
                c0 = u * 2 * qs_
                s = jnp.dot(k_aug, w[:, c0:c0 + 2 * qs_], preferred_element_type=f32)
                if crossing:
                    s = s + corr_tile(g0 + u * qs_ - r0)
                s_sc[:, c0:c0 + 2 * qs_] = s

        def step(n, mc, sidx, crossing):
            r0 = pl.multiple_of((n - 1) * tk, tk)
            r1 = pl.multiple_of(n * tk, tk)
            vt = vt_sc[:, pl.ds(r0, tk)]
            if sidx is not None:
                k_aug = jnp.concatenate([k_ref[pl.ds(r1, tk), :], kext_ref[sidx]], axis=1)
            m, csts = mc
            ms, cs = [], []
            for u in range(ng):
                c0 = u * 2 * qs_
                s = s_sc[:, c0:c0 + 2 * qs_]
                m_old = m[:, c0:c0 + 2 * qs_]
                cst = csts[:, c0:c0 + 2 * qs_]
                m_new = jnp.maximum(m_old, jnp.max(s, axis=0, keepdims=True) + cst)
                alpha = jnp.exp2(m_old - m_new)
                p = jnp.exp2(s - (m_new - cst)).astype(bf)
                acc_sc[:, c0:c0 + 2 * qs_] = (acc_sc[:, c0:c0 + 2 * qs_] * alpha
                                              + jnp.dot(vt, p, preferred_element_type=f32))
                ms.append(m_new)
                if sidx is not None:
                    s = jnp.dot(k_aug, w[:, c0:c0 + 2 * qs_], preferred_element_type=f32)
                    if crossing:
                        s = s + corr_tile(g0 + u * qs_ - r1)
                    s_sc[:, c0:c0 + 2 * qs_] = s
                    cs.append(jnp.broadcast_to(shift(g0 + u * qs_ - r1, sidx), (1, 2 * qs_)))
            return jnp.concatenate(ms, axis=1), (jnp.concatenate(cs, axis=1) if cs else csts)

        acc_sc[...] = jnp.zeros(acc_sc.shape, f32)
        m = jnp.full((1, ncol), -1e30, f32)
        scores(0, 0, True)
        n1 = jnp.maximum(jc, 1)
        n2 = jc + tq // tk
        cst0 = jnp.concatenate([jnp.broadcast_to(shift(g0 + u * qs_, 0), (1, 2 * qs_))
                                for u in range(ng)], axis=1)
        mc = (m, cst0)
        mc = lax.fori_loop(1, n1, functools.partial(step, sidx=0, crossing=False), mc)
        mc = lax.fori_loop(n1, n2, functools.partial(step, sidx=0, crossing=True), mc)
        mc = lax.fori_loop(n2, nk, functools.partial(step, sidx=1, crossing=False), mc)
        step(nk, mc, None, False)

        for u in range(ng):
            c0 = u * 2 * qs_
            acc = acc_sc[:, c0:c0 + 2 * qs_]
            den = acc[dv:dv + 1, :]
            o = acc[0:dv, 0:qs_] / den[:, 0:qs_] - lam * (acc[0:dv, qs_:] / den[:, qs_:])
            y = o * lax.rsqrt(jnp.mean(o * o, axis=0, keepdims=True) + EPS) * subg_ref[...]
            o_ref[pl.ds(g0 + u * qs_, qs_), :] = (y * (1.0 - lam_init)).T.astype(o_ref.dtype)
        return carry

    lax.fori_loop(0, t // tq, group, 0)


def _attn_tables(tk):
    f32 = jnp.float32
    slopes = 2.0 ** (-8.0 * (jnp.arange(ATT_HEADS, dtype=f32) + 1.0) / ATT_HEADS)
    beta = slopes * _LOG2E
    lane = jnp.arange(LANES)[None, None, :]
    b = _split3(beta)
    a = _split3(-beta[:, None] * jnp.arange(_Q_STRIP, dtype=f32)[None, :])
    qext = jnp.zeros((ATT_HEADS, _Q_STRIP, LANES), f32)
    for c in range(3):
        qext = jnp.where(lane == c, b[c][:, None, None], qext)
        qext = jnp.where(lane == 3 + c, a[c][:, :, None], qext)
    r = jnp.arange(tk, dtype=f32)[:, None]
    lane2 = jnp.arange(LANES)[None, :]
    base = jnp.where(lane2 < 3, r, jnp.where(lane2 < 6, 1.0, 0.0))
    kext = jnp.stack([base, -base]).astype(jnp.bfloat16)
    beta_t = jnp.broadcast_to(beta[:, None, None], (ATT_HEADS, 1, LANES))
    return beta_t, qext, kext


def diff_attention(proj3d, lq1, lk1, lq2, lk2, subg, layer, tk=256, tq=2048):
    b, t, _ = proj3d.shape
    tk, tq = min(tk, t), min(tq, t)
    assert tq % tk == 0 and tq % _Q_STRIP == 0
    lam_init = 0.8 - 0.6 * math.exp(-0.3 * layer)
    assert tk <= 256
    beta_t, qext, kext = _attn_tables(tk)
    vec = lambda a: a.reshape(1, -1).astype(jnp.float32)
    small = lambda n: pl.BlockSpec((1, n), lambda bi, h: (0, 0))
    head = pl.BlockSpec((None, 1, LANES), lambda bi, h: (h, 0, 0))
    col = lambda blk: pl.BlockSpec((None, t, LANES), lambda bi, h: (bi, 0, blk + h))
    return pl.pallas_call(
        functools.partial(_attn_kernel, lam_init=lam_init, tk=tk, tq=tq),
        grid=(b, ATT_HEADS),
        in_specs=[
            head, pl.BlockSpec((None, _Q_STRIP, LANES), lambda bi, h: (h, 0, 0)),
            small(ATT_QK_DIM), small(ATT_QK_DIM), small(ATT_QK_DIM), small(ATT_QK_DIM),
            pl.BlockSpec((ATT_V_DIM, 1), lambda bi, h: (0, 0)),
            col(_QBLK), col(_KBLK), col(_VBLK),
            pl.BlockSpec((2, tk, LANES), lambda bi, h: (0, 0, 0)),
        ],
        out_specs=pl.BlockSpec((None, t, LANES), lambda bi, h: (bi, 0, h)),
        out_shape=jax.ShapeDtypeStruct((b, t, ATT_HEADS * ATT_V_DIM), jnp.bfloat16),
        scratch_shapes=[pltpu.VMEM((ATT_V_DIM + _ONES_ROWS, t), jnp.bfloat16),
                        pltpu.VMEM((ATT_V_DIM + _ONES_ROWS, 2 * tq), jnp.float32),
                        pltpu.VMEM((tk, 2 * tq), jnp.float32)],
        compiler_params=_cparams(("parallel", "parallel")),
        name="diff_attn",
    )(beta_t, qext, vec(lq1), vec(lk1), vec(lq2), vec(lk2),
      subg.reshape(ATT_V_DIM, 1).astype(jnp.float32), proj3d, proj3d, proj3d, kext)


_HG_CHUNKS_PER_ITER = 4
_HG_FIN_ROWS = 256


def _scan_rows(g, rev):
    c = g.shape[0]
    row = lax.broadcasted_iota(jnp.int32, g.shape, 0)
    b = g
    s = 1
    while s < c:
        if rev:
            b = b + jnp.where(row < c - s, pltpu.roll(b, c - s, axis=0), 0.0)
        else:
            b = b + jnp.where(row >= s, pltpu.roll(b, s, axis=0), 0.0)
        s *= 2
    return b


def _level_ref(b, m, rev):
    c, w = b.shape
    r = m // 2 if rev else m // 2 - 1
    pieces = []
    if m >= SUBLANES:
        for j in range(c // m):
            pieces.append(jnp.broadcast_to(b[j * m + r:j * m + r + 1, :], (m, w)))
    else:
        sub = lax.broadcasted_iota(jnp.int32, (SUBLANES, w), 0)
        for j in range(c // SUBLANES):
            base = j * SUBLANES
            acc = jnp.broadcast_to(b[base + r:base + r + 1, :], (SUBLANES, w))
            for i in range(1, SUBLANES // m):
                cand = jnp.broadcast_to(b[base + i * m + r:base + i * m + r + 1, :], (SUBLANES, w))
                acc = jnp.where(sub >= i * m, cand, acc)
            pieces.append(acc)
    return jnp.concatenate(pieces, axis=0)


def _neg_abs(x):
    bits = lax.bitcast_convert_type(x, jnp.uint32) | jnp.uint32(0x80000000)
    return lax.bitcast_convert_type(bits, jnp.float32)


def _hgrn_masks(c, rev):
    row = lax.broadcasted_iota(jnp.int32, (c, LANES), 0)
    ti = lax.broadcasted_iota(jnp.int32, (c, c), 0)
    si = lax.broadcasted_iota(jnp.int32, (c, c), 1)
    qrows, pairs = {}, {}
    m = c
    while m >= 2:
        late = (row & (m - 1)) >= m // 2
        t_late = (ti & (m - 1)) >= m // 2
        s_late = (si & (m - 1)) >= m // 2
        same = (ti ^ si) < m
        if rev:
            qrows[m] = jnp.logical_not(late)
            pairs[m] = same & jnp.logical_not(t_late) & s_late
        else:
            qrows[m] = late
            pairs[m] = same & t_late & jnp.logical_not(s_late)
        m //= 2
    return qrows, pairs, ti == si


def _hgrn_chunk(q, f, v, st_ref, rev, masks):
    qrows, pairs, diag = masks
    c = q.shape[0]
    bf = jnp.bfloat16
    k_ = 1.0 - f
    b = _scan_rows(jnp.log(f) * _LOG2E, rev)
    a = jnp.where(diag,
                  lax.dot_general(q.astype(bf), k_.astype(bf), _NT,
                                  preferred_element_type=jnp.float32), 0.0)
    m = c
    while m >= 2:
        if m == 2:
            cref = jnp.where(qrows[m], pltpu.roll(b, c - 1 if rev else 1, axis=0), b)
        else:
            cref = _level_ref(b, m, rev)
        e = jnp.exp2(_neg_abs(b - cref))
        z = (jnp.where(qrows[m], q, k_) * e).astype(bf)
        p = lax.dot_general(z, z, _NT, preferred_element_type=jnp.float32)
        a = jnp.where(pairs[m], p, a)
        m //= 2
    st = st_ref[...]
    o = jnp.dot(a.astype(bf), v.astype(bf), preferred_element_type=jnp.float32)
    o = o + lax.dot_general((q * jnp.exp2(b)).astype(bf), st.astype(bf), _NT,
                            preferred_element_type=jnp.float32)
    btot = b[0:1, :] if rev else b[c - 1:c, :]
    khat = (k_ * jnp.exp2(btot - b)).astype(bf)
    st_ref[...] = st * jnp.exp2(btot) + jnp.dot(v.T.astype(bf), khat,
                                                preferred_element_type=jnp.float32)
    return o


def _hgrn_kernel(lb_ref, ng_ref, q_ref, ff_ref, fb_ref, i_ref, g_ref, o_ref,
                 of_sc, ob_sc, stf_sc, stb_sc, *, layer, chunk):
    t = q_ref.shape[0]
    n = t // chunk
    lb_raw = lb_ref[...].astype(jnp.float32)
    ex = jnp.exp(lb_raw - jnp.max(lb_raw, axis=0, keepdims=True))
    sm = ex / jnp.sum(ex, axis=0, keepdims=True)
    lb = jnp.zeros_like(sm[0])
    for l_ in range(1, layer + 1):
        lb = lb + sm[l_]
    lb_f = lb[0:1, :]
    lb_b = lb[1:2, :]

    stf_sc[...] = jnp.zeros(stf_sc.shape, jnp.float32)
    stb_sc[...] = jnp.zeros(stb_sc.shape, jnp.float32)

    def load(ref, r0):
        return ref[pl.ds(r0, chunk), :].astype(jnp.float32)

    masks_f = _hgrn_masks(chunk, False)
    masks_b = _hgrn_masks(chunk, True)

    per_iter = _HG_CHUNKS_PER_ITER if n % _HG_CHUNKS_PER_ITER == 0 else 1

    def body(ci, carry):
        for u in range(per_iter):
            r0 = pl.multiple_of((ci * per_iter + u) * chunk, chunk)
            qr = load(q_ref, r0)
            f = lb_f + (1.0 - lb_f) * jax.nn.sigmoid(load(ff_ref, r0))
            of_sc[pl.ds(r0, chunk), :] = _hgrn_chunk(qr * jax.nn.sigmoid(qr), f,
                                                     load(i_ref, r0), stf_sc, False, masks_f)
            r1 = pl.multiple_of((n - 1 - ci * per_iter - u) * chunk, chunk)
            qr = load(q_ref, r1)
            f = lb_b + (1.0 - lb_b) * jax.nn.sigmoid(load(fb_ref, r1))
            ob_sc[pl.ds(r1, chunk), :] = _hgrn_chunk(qr * jax.nn.sigmoid(qr), f,
                                                     load(i_ref, r1), stb_sc, True, masks_b)
        return carry

    lax.fori_loop(0, n // per_iter, body, 0)

    rows = math.gcd(t, _HG_FIN_ROWS)

    def fin(ci, carry):
        r0 = pl.multiple_of(ci * rows, rows)
        o = of_sc[pl.ds(r0, rows), :] + ob_sc[pl.ds(r0, rows), :]
        y = o * lax.rsqrt(jnp.mean(o * o, axis=-1, keepdims=True) + EPS) * ng_ref[...]
        gr = g_ref[pl.ds(r0, rows), :].astype(jnp.float32)
        o_ref[pl.ds(r0, rows), :] = (y * (gr * jax.nn.sigmoid(gr))).astype(o_ref.dtype)
        return carry

    lax.fori_loop(0, t // rows, fin, 0)


def hgrn_bidir(proj3d, hgrn_lb, norm_g, layer):
    b, t, _ = proj3d.shape
    depth = hgrn_lb.shape[0]
    col = lambda blk: pl.BlockSpec((None, t, LANES), lambda bi, h: (bi, 0, blk + h))
    return pl.pallas_call(
        functools.partial(_hgrn_kernel, layer=layer, chunk=HG_CHUNK),
        grid=(b, HG_HEADS),
        in_specs=[
            pl.BlockSpec((depth, 2, LANES), lambda bi, h: (0, 0, h)),
            pl.BlockSpec((1, HG_V), lambda bi, h: (0, 0)),
            col(_HQ), col(_HFF), col(_HFB), col(_HI), col(_HG),
        ],
        out_specs=pl.BlockSpec((None, t, LANES), lambda bi, h: (bi, 0, h)),
        out_shape=jax.ShapeDtypeStruct((b, t, HG_HEADS * HG_V), jnp.bfloat16),
        scratch_shapes=[
            pltpu.VMEM((t, HG_V), jnp.float32),
            pltpu.VMEM((t, HG_V), jnp.float32),
            pltpu.VMEM((HG_V, HG_K), jnp.float32),
            pltpu.VMEM((HG_V, HG_K), jnp.float32),
        ],
        compiler_params=_cparams(("parallel", "parallel")),
        name="hgrn_bidir",
    )(hgrn_lb, norm_g.reshape(1, HG_V).astype(jnp.float32),
      proj3d, proj3d, proj3d, proj3d, proj3d)


def _out_proj_kernel(a_ref, h_ref, wa_ref, wh_ref, x_ref, o_ref):
    y = jnp.dot(a_ref[...], wa_ref[...], preferred_element_type=jnp.float32)
    y = y + jnp.dot(h_ref[...], wh_ref[...], preferred_element_type=jnp.float32)
    o_ref[...] = x_ref[...] + y


def out_proj(att2d, hgo2d, w_out_bf16, x2d, tm=512):
    n, d = x2d.shape
    tm = min(tm, n)
    wa = att2d.shape[1]
    wh = hgo2d.shape[1]
    return pl.pallas_call(
        _out_proj_kernel,
        grid=(n // tm,),
        in_specs=[
            pl.BlockSpec((tm, wa), lambda i: (i, 0)),
            pl.BlockSpec((tm, wh), lambda i: (i, 0)),
            pl.BlockSpec((wa, d), lambda i: (0, 0)),
            pl.BlockSpec((wh, d), lambda i: (wa // wh, 0)),
            pl.BlockSpec((tm, d), lambda i: (i, 0)),
        ],
        out_specs=pl.BlockSpec((tm, d), lambda i: (i, 0)),
        out_shape=jax.ShapeDtypeStruct((n, d), jnp.float32),
        compiler_params=_cparams(("parallel",)),
        name="out_proj",
    )(att2d, hgo2d, w_out_bf16, w_out_bf16, x2d)


def _router_kernel(x_ref, g_ref, wrt_ref, ha_ref, hb_ref, afft_ref):
    x = x_ref[...]
    h = x * lax.rsqrt(jnp.mean(x * x, axis=-1, keepdims=True) + EPS) * g_ref[...]
    bits = lax.bitcast_convert_type(h.astype(jnp.bfloat16).astype(jnp.float32), jnp.uint32)
    q = ha_ref.shape[1]
    ha_ref[...] = (bits[:, 0:q] >> 16) | bits[:, q:2 * q]
    hb_ref[...] = (bits[:, 2 * q:3 * q] >> 16) | bits[:, 3 * q:4 * q]
    logits = lax.dot_general(wrt_ref[...], h, _NT, preferred_element_type=jnp.float32,
                             precision=lax.Precision.HIGHEST)
    ex = jnp.exp(logits - jnp.max(logits, axis=0, keepdims=True))
    afft_ref[...] = ex / jnp.sum(ex, axis=0, keepdims=True)


def router(x2d, g, w_router, tm=512):
    n, d = x2d.shape
    tm = min(tm, n)
    e = w_router.shape[1]
    return pl.pallas_call(
        _router_kernel,
        grid=(n // tm,),
        in_specs=[
            pl.BlockSpec((tm, d), lambda i: (i, 0)),
            pl.BlockSpec((1, d), lambda i: (0, 0)),
            pl.BlockSpec((e, d), lambda i: (0, 0)),
        ],
        out_specs=[
            pl.BlockSpec((tm, d // 4), lambda i: (i, 0)),
            pl.BlockSpec((tm, d // 4), lambda i: (i, 0)),
            pl.BlockSpec((e, tm), lambda i: (0, i)),
        ],
        out_shape=[
            jax.ShapeDtypeStruct((n, d // 4), jnp.uint32),
            jax.ShapeDtypeStruct((n, d // 4), jnp.uint32),
            jax.ShapeDtypeStruct((e, n), jnp.float32),
        ],
        compiler_params=_cparams(("parallel",)),
        name="router",
    )(x2d, g.reshape(1, d), w_router.T)


_SEL_CHUNK = 1024


def _select_kernel(afft_ref, pos_ref, offs_ref, *, cap):
    f32, i32 = jnp.float32, jnp.int32
    e, n = afft_ref.shape
    chunk = min(_SEL_CHUNK, n)
    nt = n // LANES

    def bits_at(start, size):
        return lax.bitcast_convert_type(afft_ref[:, pl.ds(start, size)], i32)

    def count_ge(cand):
        def body(c, acc):
            x = bits_at(pl.multiple_of(c * chunk, chunk), chunk)
            return acc + jnp.where(x >= cand, 1.0, 0.0)
        acc = lax.fori_loop(0, n // chunk, body, jnp.zeros((e, chunk), f32))
        return jnp.sum(acc, axis=1, keepdims=True)

    def bit_body(it, prefix):
        cand = prefix | jnp.left_shift(jnp.int32(1), 30 - it)
        return jnp.where(count_ge(cand) >= cap, cand, prefix)

    thr = lax.fori_loop(0, 31, bit_body, jnp.zeros((e, 1), i32))
    need = cap - count_ge(thr + 1)

    tri = (lax.broadcasted_iota(i32, (LANES, LANES), 0)
           <= lax.broadcasted_iota(i32, (LANES, LANES), 1)).astype(jnp.bfloat16)
    tile_id = lax.broadcasted_iota(i32, (e, nt), 1)

    def tile_body(j, carry):
        c_gt, c_eq, offs_acc = carry
        x = bits_at(pl.multiple_of(j * LANES, LANES), LANES)
        gt = x > thr
        eq = x == thr
        both = jnp.concatenate([jnp.where(gt, 1.0, 0.0), jnp.where(eq, 1.0, 0.0)],
                               axis=0).astype(jnp.bfloat16)
        inc = jnp.dot(both, tri, preferred_element_type=f32)
        inc_gt = inc[:e] + c_gt
        inc_eq = inc[e:] + c_eq
        sel = gt | (eq & (inc_eq <= need))
        incl = inc_gt + jnp.minimum(inc_eq, need)
        pos_ref[:, pl.ds(pl.multiple_of(j * LANES, LANES), LANES)] = jnp.where(
            sel, incl - 1.0, -1.0).astype(i32)
        offs_acc = jnp.where(tile_id == j, c_gt + jnp.minimum(c_eq, need), offs_acc)
        return inc_gt[:, LANES - 1:LANES], inc_eq[:, LANES - 1:LANES], offs_acc

    zero = jnp.zeros((e, 1), f32)
    _, _, offs_acc = lax.fori_loop(0, nt, tile_body, (zero, zero, jnp.zeros((e, nt), f32)))
    offs_ref[...] = offs_acc.astype(i32)


def ec_select(afft, cap):
    e, n = afft.shape
    return pl.pallas_call(
        functools.partial(_select_kernel, cap=cap),
        out_shape=[jax.ShapeDtypeStruct((e, n), jnp.int32),
                   jax.ShapeDtypeStruct((e, n // LANES), jnp.int32)],
        compiler_params=pltpu.CompilerParams(vmem_limit_bytes=_VMEM_LIMIT),
        name="ec_select",
    )(afft)


_SC_LANES = 16
_SC_CORES = 2
_INV_CHUNK = 8192


def ec_invert(pos, afft, cap):
    ne, n = pos.shape
    ch = min(_INV_CHUNK, n)
    mesh = plsc.VectorSubcoreMesh(core_axis_name="c", subcore_axis_name="s")

    @functools.partial(
        pl.kernel, mesh=mesh,
        out_type=[jax.ShapeDtypeStruct((ne * cap,), jnp.int32),
                  jax.ShapeDtypeStruct((ne * cap,), jnp.float32)],
        scratch_types=[pltpu.VMEM((ch,), jnp.int32), pltpu.VMEM((ch,), jnp.float32),
                       pltpu.VMEM((cap,), jnp.int32), pltpu.VMEM((cap,), jnp.float32)],
        compiler_params=pltpu.CompilerParams(needs_layout_passes=False))
    def invert(pos_hbm, aff_hbm, idx_hbm, gate_hbm, pos_v, aff_v, idx_v, gate_v):
        wid = lax.axis_index("s") * _SC_CORES + lax.axis_index("c")

        @pl.when(wid < ne)
        def _():
            @pl.loop(0, n // ch)
            def _(c):
                pltpu.sync_copy(pos_hbm.at[pl.ds(wid * n + c * ch, ch)], pos_v)
                pltpu.sync_copy(aff_hbm.at[pl.ds(wid * n + c * ch, ch)], aff_v)

                @pl.loop(0, ch, step=_SC_LANES)
                def _(i):
                    p = pos_v[pl.ds(i, _SC_LANES)]
                    tok = lax.iota(jnp.int32, _SC_LANES) + (c * ch + i)
                    plsc.store_scatter(idx_v, [p], tok, mask=p >= 0)
                    plsc.store_scatter(gate_v, [p], aff_v[pl.ds(i, _SC_LANES)], mask=p >= 0)

            pltpu.sync_copy(idx_v, idx_hbm.at[pl.ds(wid * cap, cap)])
            pltpu.sync_copy(gate_v, gate_hbm.at[pl.ds(wid * cap, cap)])

    idx, gates = invert(pos.reshape(ne * n), afft.reshape(ne * n))
    return idx.reshape(ne, cap), gates.reshape(ne, cap)


_GATHER_WIN = 128


def ec_gather(table, idx_row):
    m = idx_row.shape[1]
    width = table.shape[1]
    steps = m // (_GATHER_WIN * _SC_CORES)
    mesh = plsc.VectorSubcoreMesh(core_axis_name="c", subcore_axis_name="s")

    @functools.partial(
        pl.kernel, mesh=mesh,
        out_type=jax.ShapeDtypeStruct((m, width), table.dtype),
        scratch_types=[])
    def gather(x_hbm, i_hbm, o_hbm):
        def body(i_vmem, o_vmem):
            pltpu.sync_copy(x_hbm.at[i_vmem.at[0]], o_vmem)

        pltpu.emit_pipeline(
            body,
            grid=(_SC_CORES, steps),
            in_specs=[pl.BlockSpec((1, _GATHER_WIN), index_map=lambda a, i: (0, a * steps + i))],
            out_specs=[pl.BlockSpec((_GATHER_WIN, width),
                                    index_map=lambda a, i: (a * steps + i, 0))],
            core_axis_name=("c", "s"),
            dimension_semantics=(pltpu.PARALLEL, pltpu.PARALLEL),
        )(i_hbm, o_hbm)

    return gather(table, idx_row)


def _expert_kernel(xa_ref, xb_ref, gate_ref, wg_ref, wu_ref, wd_ref, o_ref):
    @pl.when(pl.program_id(1) == pl.num_programs(1) - 1)
    def _():
        o_ref[...] = jnp.zeros(o_ref.shape, o_ref.dtype)

    @pl.when(pl.program_id(1) < pl.num_programs(1) - 1)
    def _():
        as_f32 = lambda bits: lax.bitcast_convert_type(bits, jnp.float32)
        wa, wb = xa_ref[...], xb_ref[...]
        hi = jnp.uint32(0xFFFF0000)
        xs = jnp.concatenate([as_f32(wa << 16), as_f32(wa & hi), as_f32(wb << 16),
                              as_f32(wb & hi)], axis=1).astype(jnp.bfloat16)
        a = jnp.dot(xs, wg_ref[...], preferred_element_type=jnp.float32)
        u = jnp.dot(xs, wu_ref[...], preferred_element_type=jnp.float32)
        hid = (a * jax.nn.sigmoid(a) * u).astype(jnp.bfloat16)
        y = jnp.dot(hid, wd_ref[...], preferred_element_type=jnp.float32)
        o_ref[...] = (y * gate_ref[...]).astype(o_ref.dtype)


def expert_ffn(xa, xb, gates, wg, wu, wd, tm=512):
    e, c, q = xa.shape
    d = 4 * q
    tm = min(tm, c)
    f = wg.shape[2]
    nt = c // tm
    return pl.pallas_call(
        _expert_kernel,
        grid=(e, nt + 1),
        in_specs=[
            pl.BlockSpec((None, tm, q), lambda ei, i: (ei, jnp.minimum(i, nt - 1), 0)),
            pl.BlockSpec((None, tm, q), lambda ei, i: (ei, jnp.minimum(i, nt - 1), 0)),
            pl.BlockSpec((None, tm, 1), lambda ei, i: (ei, jnp.minimum(i, nt - 1), 0)),
            pl.BlockSpec((None, d, f), lambda ei, i: (ei, 0, 0)),
            pl.BlockSpec((None, d, f), lambda ei, i: (ei, 0, 0)),
            pl.BlockSpec((None, f, d), lambda ei, i: (ei, 0, 0)),
        ],
        out_specs=pl.BlockSpec((None, tm, d), lambda ei, i: (ei, i, 0)),
        out_shape=jax.ShapeDtypeStruct((e, c + tm, d), jnp.bfloat16),
        compiler_params=_cparams(("parallel", "arbitrary")),
        name="expert_ffn",
    )(xa, xb, gates.reshape(e, c, 1), wg, wu, wd)


_SLAB_ALIGN = 16
_SLAB_ROWS = LANES + _SLAB_ALIGN
_SLAB_HEAD = 48


def _slab_start(offs_ref, e, j):
    return pl.multiple_of((offs_ref[e, j] // _SLAB_ALIGN) * _SLAB_ALIGN, _SLAB_ALIGN)


def _needs_tail(offs_ref, j, cap):
    ne, nt = offs_ref.shape
    need = False
    for e in range(ne):
        end = jnp.where(j + 1 < nt, offs_ref[e, jnp.minimum(j + 1, nt - 1)], cap)
        need = jnp.logical_or(need, end - _slab_start(offs_ref, e, j) > _SLAB_HEAD)
    return need


def _head_copy(offs_ref, out_hbm, head_ref, sem_ref, e, j, slot):
    start = _slab_start(offs_ref, e, j)
    return pltpu.make_async_copy(out_hbm.at[e, pl.ds(start, _SLAB_HEAD), :],
                                 head_ref.at[slot, e], sem_ref.at[0, slot, e])


def _tail_copy(offs_ref, out_hbm, tail_ref, sem_ref, e, j, slot):
    start = pl.multiple_of(_slab_start(offs_ref, e, j) + _SLAB_HEAD, _SLAB_ALIGN)
    return pltpu.make_async_copy(out_hbm.at[e, pl.ds(start, _SLAB_ROWS - _SLAB_HEAD), :],
                                 tail_ref.at[slot, e], sem_ref.at[1, slot, e])


def _combine_kernel(offs_ref, x_ref, pos_ref, out_hbm, o_ref, head_ref, tail_ref, sem_ref, *,
                    cap):
    f32 = jnp.float32
    ne = pos_ref.shape[0]
    j = pl.program_id(0)
    slot = j % 2

    def start_all(jj, sl):
        for e in range(ne):
            _head_copy(offs_ref, out_hbm, head_ref, sem_ref, e, jj, sl).start()

        @pl.when(_needs_tail(offs_ref, jj, cap))
        def _():
            for e in range(ne):
                _tail_copy(offs_ref, out_hbm, tail_ref, sem_ref, e, jj, sl).start()

    @pl.when(j == 0)
    def _():
        start_all(j, slot)

    @pl.when(j + 1 < pl.num_programs(0))
    def _():
        start_all(j + 1, 1 - slot)

    tail = _needs_tail(offs_ref, j, cap)
    for e in range(ne):
        _head_copy(offs_ref, out_hbm, head_ref, sem_ref, e, j, slot).wait()

    @pl.when(tail)
    def _():
        for e in range(ne):
            _tail_copy(offs_ref, out_hbm, tail_ref, sem_ref, e, j, slot).wait()

    pad = jnp.zeros((LANES - ne, LANES), f32)
    pos_t = jnp.concatenate([pos_ref[...].astype(f32), pad], axis=0).T

    def onehot(e, first, rows):
        rel = pos_t[:, e:e + 1] - (_slab_start(offs_ref, e, j) + first).astype(f32)
        row_id = lax.broadcasted_iota(jnp.int32, (LANES, rows), 1).astype(f32)
        return jnp.where(rel == row_id, 1.0, 0.0).astype(jnp.bfloat16)

    y = x_ref[...]
    for e in range(ne):
        y = y + jnp.dot(onehot(e, 0, _SLAB_HEAD), head_ref[slot, e], preferred_element_type=f32)
    o_ref[...] = y

    @pl.when(tail)
    def _():
        y = o_ref[...]
        for e in range(ne):
            y = y + jnp.dot(onehot(e, _SLAB_HEAD, _SLAB_ROWS - _SLAB_HEAD), tail_ref[slot, e],
                            preferred_element_type=f32)
        o_ref[...] = y


def ec_combine(x2d, pos, offs, out):
    n, d = x2d.shape
    ne, crows, _ = out.shape
    cap = n * EC_CAPACITY // ne
    assert crows >= cap + _SLAB_ROWS
    grid_spec = pltpu.PrefetchScalarGridSpec(
        num_scalar_prefetch=1,
        grid=(n // LANES,),
        in_specs=[
            pl.BlockSpec((LANES, d), lambda j, offs_ref: (j, 0)),
            pl.BlockSpec((ne, LANES), lambda j, offs_ref: (0, j)),
            pl.BlockSpec(memory_space=pl.ANY),
        ],
        out_specs=pl.BlockSpec((LANES, d), lambda j, offs_ref: (j, 0)),
        scratch_shapes=[pltpu.VMEM((2, ne, _SLAB_HEAD, d), out.dtype),
                        pltpu.VMEM((2, ne, _SLAB_ROWS - _SLAB_HEAD, d), out.dtype),
                        pltpu.SemaphoreType.DMA((2, 2, ne))],
    )
    return pl.pallas_call(
        functools.partial(_combine_kernel, cap=cap),
        grid_spec=grid_spec,
        out_shape=jax.ShapeDtypeStruct((n, d), jnp.float32),
        compiler_params=_cparams(("arbitrary",)),
        name="ec_combine",
    )(offs, x2d, pos, out)


def _rms_kernel(x_ref, g_ref, o_ref):
    x = x_ref[...]
    o_ref[...] = x * lax.rsqrt(jnp.mean(x * x, axis=-1, keepdims=True) + EPS) * g_ref[...]


def rms_norm_call(x2d, g, tm=1024):
    n, d = x2d.shape
    tm = min(tm, n)
    return pl.pallas_call(
        _rms_kernel,
        grid=(n // tm,),
        in_specs=[pl.BlockSpec((tm, d), lambda i: (i, 0)),
                  pl.BlockSpec((1, d), lambda i: (0, 0))],
        out_specs=pl.BlockSpec((tm, d), lambda i: (i, 0)),
        out_shape=jax.ShapeDtypeStruct((n, d), jnp.float32),
        compiler_params=_cparams(("parallel",)),
        name="final_norm",
    )(x2d, g.reshape(1, d))


def _trunk(x, p):
    b, t, d = x.shape
    n = b * t
    depth = p["w_in"].shape[0]
    x2d = x.reshape(n, d)
    cap = max(1, EC_CAPACITY * n // N_EXPERTS)
    for layer in range(depth):
        proj = norm_proj(x2d, p["norm_mix_g"][layer], p["w_in"][layer])
        proj3d = proj.reshape(b, t, -1)
        att = diff_attention(proj3d, p["lambda_q1"][layer], p["lambda_k1"][layer],
                             p["lambda_q2"][layer], p["lambda_k2"][layer],
                             p["diff_subln_g"][layer], layer)
        hgo = hgrn_bidir(proj3d, p["hgrn_lb"], p["hgrn_norm_g"][layer], layer)
        x2d = out_proj(att.reshape(n, -1), hgo.reshape(n, -1), p["w_out"][layer], x2d)
        ha, hb, afft = router(x2d, p["norm_ffn_g"][layer], p["w_router"][layer])
        pos, offs = ec_select(afft, cap)
        ne = afft.shape[0]
        idx, gates = ec_invert(pos, afft, cap)
        idx = idx.reshape(1, ne * cap)
        xa = ec_gather(ha, idx).reshape(ne, cap, -1)
        xb = ec_gather(hb, idx).reshape(ne, cap, -1)
        out = expert_ffn(xa, xb, gates, p["w_gate"][layer], p["w_up"][layer],
                         p["w_down"][layer])
        x2d = ec_combine(x2d, pos, offs, out)
    return rms_norm_call(x2d, p["final_norm_g"]).reshape(b, t, d)


def kernel(x_prompt, x_sample, norm_mix_g, w_in, lambda_q1, lambda_k1, lambda_q2, lambda_k2,
           diff_subln_g, hgrn_lb, hgrn_norm_g, w_out, norm_ffn_g, w_router, w_gate, w_up,
           w_down, final_norm_g):
    bf = jnp.bfloat16
    p = dict(norm_mix_g=norm_mix_g, w_in=w_in.astype(bf), lambda_q1=lambda_q1,
             lambda_k1=lambda_k1, lambda_q2=lambda_q2, lambda_k2=lambda_k2,
             diff_subln_g=diff_subln_g, hgrn_lb=hgrn_lb, hgrn_norm_g=hgrn_norm_g,
             w_out=w_out.astype(bf), norm_ffn_g=norm_ffn_g, w_router=w_router,
             w_gate=w_gate.astype(bf), w_up=w_up.astype(bf), w_down=w_down.astype(bf),
             final_norm_g=final_norm_g)
    return (_trunk(x_prompt, p), _trunk(x_sample, p))
```

```python
import functools
import math

import jax
import jax.numpy as jnp
from jax import lax
from jax.experimental import pallas as pl
from jax.experimental.pallas import tpu as pltpu
from jax.experimental.pallas import tpu_sc as plsc

ATT_HEADS = 4
ATT_QK_DIM = 64
ATT_V_DIM = 128
HG_HEADS = 4
HG_K = 128
HG_V = 128
HG_CHUNK = 64
N_EXPERTS = 16
EC_CAPACITY = 2
EPS = 1e-6
LANES = 128
SUBLANES = 8

_QBLK, _KBLK, _VBLK = 0, 4, 8
_HQ, _HFF, _HFB, _HI, _HG = 12, 16, 20, 24, 28

_VMEM_LIMIT = 56 * 1024 * 1024

_NT = (((1,), (1,)), ((), ()))


def _cparams(sem):
    return pltpu.CompilerParams(dimension_semantics=sem, vmem_limit_bytes=_VMEM_LIMIT)


def _norm_proj_kernel(x_ref, g_ref, w_ref, o_ref, *, n_split):
    x = x_ref[...]
    y = x * lax.rsqrt(jnp.mean(x * x, axis=-1, keepdims=True) + EPS) * g_ref[...]
    h = y.astype(jnp.bfloat16)
    wn = w_ref.shape[1] // n_split
    for c in range(n_split):
        o_ref[:, c * wn:(c + 1) * wn] = jnp.dot(
            h, w_ref[:, c * wn:(c + 1) * wn], preferred_element_type=jnp.float32
        ).astype(o_ref.dtype)


def norm_proj(x2d, g, w_bf16, tm=512):
    n, d = x2d.shape
    tm = min(tm, n)
    dout = w_bf16.shape[1]
    return pl.pallas_call(
        functools.partial(_norm_proj_kernel, n_split=max(1, dout // 1024)),
        grid=(n // tm,),
        in_specs=[
            pl.BlockSpec((tm, d), lambda i: (i, 0)),
            pl.BlockSpec((1, d), lambda i: (0, 0)),
            pl.BlockSpec((d, dout), lambda i: (0, 0)),
        ],
        out_specs=pl.BlockSpec((tm, dout), lambda i: (i, 0)),
        out_shape=jax.ShapeDtypeStruct((n, dout), jnp.bfloat16),
        compiler_params=_cparams(("parallel",)),
        name="norm_proj",
    )(x2d, g.reshape(1, d), w_bf16)


_LOG2E = 1.4426950408889634
_Q_STRIP = 128
_ONES_ROWS = 16


def _split3(x):
    bf = jnp.bfloat16
    h = x.astype(bf).astype(jnp.float32)
    r = x - h
    m = r.astype(bf).astype(jnp.float32)
    lo = (r - m).astype(bf).astype(jnp.float32)
    return h, m, lo


def _attn_kernel(beta_ref, qext_ref, lq1_ref, lk1_ref, lq2_ref, lk2_ref, subg_ref,
                 q_ref, k_ref, v_ref, kext_ref, o_ref, vt_sc, acc_sc, s_sc, *, lam_init, tk, tq):
    f32, bf = jnp.float32, jnp.bfloat16
    t = q_ref.shape[0]
    nk = t // tk
    qs_ = _Q_STRIP
    ng = tq // qs_
    ncol = 2 * tq
    dv = v_ref.shape[1]
    beta = beta_ref[0:1, 0:1]

    def vt_body(c, carry):
        r0 = pl.multiple_of(c * tk, tk)
        vt_sc[0:dv, pl.ds(r0, tk)] = v_ref[pl.ds(r0, tk), :].astype(f32).T.astype(bf)
        return carry

    lax.fori_loop(0, nk, vt_body, 0)
    vt_sc[dv:dv + _ONES_ROWS, :] = jnp.ones((_ONES_ROWS, t), bf)

    lam = (jnp.exp(jnp.sum(lq1_ref[...] * lk1_ref[...], axis=-1, keepdims=True))
           - jnp.exp(jnp.sum(lq2_ref[...] * lk2_ref[...], axis=-1, keepdims=True)) + lam_init)
    lane = lax.broadcasted_iota(jnp.int32, (qs_, LANES), 1)
    q_minus_r = (lax.broadcasted_iota(jnp.int32, (tk, qs_), 1)
                 - lax.broadcasted_iota(jnp.int32, (tk, qs_), 0)).astype(f32)
    cscale = ATT_QK_DIM ** -0.5 * _LOG2E

    def group(gi, carry):
        g0 = pl.multiple_of(gi * tq, tq)
        tiles = []
        for u in range(ng):
            q0 = g0 + u * qs_
            qs = (q_ref[pl.ds(q0, qs_), :].astype(f32) * cscale).astype(bf).astype(f32)
            qext = qext_ref[...]
            q_aug = jnp.concatenate([
                jnp.concatenate([jnp.where(lane < ATT_QK_DIM, qs, 0.0), qext], axis=1),
                jnp.concatenate([jnp.where(lane >= ATT_QK_DIM, qs, 0.0), qext], axis=1)], axis=0)
            tiles.append(q_aug.T.astype(bf))
        w = jnp.concatenate(tiles, axis=1)
        jc = g0 // tk

        def corr_tile(d):
            corr = 2.0 * beta * jnp.minimum(q_minus_r + d.astype(f32), 0.0)
            return jnp.concatenate([corr, corr], axis=1)

        def shift(d, sidx):
            sgn = 1.0 if sidx == 0 else -1.0
            return (-sgn) * beta * d.astype(f32)

        def scores(n, sidx, crossing):
            r0 = pl.multiple_of(n * tk, tk)
            k_aug = jnp.concatenate([k_ref[pl.ds(r0, tk), :], kext_ref[sidx]], axis=1)
            for u in range(ng):
                c0 = u * 2 * qs_
                s = jnp.dot(k_aug, w[:, c0:c0 + 2 * qs_], preferred_element_type=f32)
                if crossing:
                    s = s + corr_tile(g0 + u * qs_ - r0)
                s_sc[:, c0:c0 + 2 * qs_] = s

        def step(n, mc, sidx, crossing):
            r0 = pl.multiple_of((n - 1) * tk, tk)
            r1 = pl.multiple_of(n * tk, tk)
            vt = vt_sc[:, pl.ds(r0, tk)]
            if sidx is not None:
                k_aug = jnp.concatenate([k_ref[pl.ds(r1, tk), :], kext_ref[sidx]], axis=1)
            m, csts = mc
            ms, cs = [], []
            for u in range(ng):
                c0 = u * 2 * qs_
                s = s_sc[:, c0:c0 + 2 * qs_]
                m_old = m[:, c0:c0 + 2 * qs_]
                cst = csts[:, c0:c0 + 2 * qs_]
                m_new = jnp.maximum(m_old, jnp.max(s, axis=0, keepdims=True) + cst)
                alpha = jnp.exp2(m_old - m_new)
                p = jnp.exp2(s - (m_new - cst)).astype(bf)
                acc_sc[:, c0:c0 + 2 * qs_] = (acc_sc[:, c0:c0 + 2 * qs_] * alpha
                                              + jnp.dot(vt, p, preferred_element_type=f32))
                ms.append(m_new)
                if sidx is not None:
                    s = jnp.dot(k_aug, w[:, c0:c0 + 2 * qs_], preferred_element_type=f32)
                    if crossing:
                        s = s + corr_tile(g0 + u * qs_ - r1)
                    s_sc[:, c0:c0 + 2 * qs_] = s
                    cs.append(jnp.broadcast_to(shift(g0 + u * qs_ - r1, sidx), (1, 2 * qs_)))
            return jnp.concatenate(ms, axis=1), (jnp.concatenate(cs, axis=1) if cs else csts)

        acc_sc[...] = jnp.zeros(acc_sc.shape, f32)
        m = jnp.full((1, ncol), -1e30, f32)
        scores(0, 0, True)
        n1 = jnp.maximum(jc, 1)
        n2 = jc + tq // tk
        cst0 = jnp.concatenate([jnp.broadcast_to(shift(g0 + u * qs_, 0), (1, 2 * qs_))
                                for u in range(ng)], axis=1)
        mc = (m, cst0)
        mc = lax.fori_loop(1, n1, functools.partial(step, sidx=0, crossing=False), mc)
        mc = lax.fori_loop(n1, n2, functools.partial(step, sidx=0, crossing=True), mc)
        mc = lax.fori_loop(n2, nk, functools.partial(step, sidx=1, crossing=False), mc)
        step(nk, mc, None, False)

        for u in range(ng):
            c0 = u * 2 * qs_
            acc = acc_sc[:, c0:c0 + 2 * qs_]
            den = acc[dv:dv + 1, :]
            o = acc[0:dv, 0:qs_] / den[:, 0:qs_] - lam * (acc[0:dv, qs_:] / den[:, qs_:])
            y = o * lax.rsqrt(jnp.mean(o * o, axis=0, keepdims=True) + EPS) * subg_ref[...]
            o_ref[pl.ds(g0 + u * qs_, qs_), :] = (y * (1.0 - lam_init)).T.astype(o_ref.dtype)
        return carry

    lax.fori_loop(0, t // tq, group, 0)


def _attn_tables(tk):
    f32 = jnp.float32
    slopes = 2.0 ** (-8.0 * (jnp.arange(ATT_HEADS, dtype=f32) + 1.0) / ATT_HEADS)
    beta = slopes * _LOG2E
    lane = jnp.arange(LANES)[None, None, :]
    b = _split3(beta)
    a = _split3(-beta[:, None] * jnp.arange(_Q_STRIP, dtype=f32)[None, :])
    qext = jnp.zeros((ATT_HEADS, _Q_STRIP, LANES), f32)
    for c in range(3):
        qext = jnp.where(lane == c, b[c][:, None, None], qext)
        qext = jnp.where(lane == 3 + c, a[c][:, :, None], qext)
    r = jnp.arange(tk, dtype=f32)[:, None]
    lane2 = jnp.arange(LANES)[None, :]
    base = jnp.where(lane2 < 3, r, jnp.where(lane2 < 6, 1.0, 0.0))
    kext = jnp.stack([base, -base]).astype(jnp.bfloat16)
    beta_t = jnp.broadcast_to(beta[:, None, None], (ATT_HEADS, 1, LANES))
    return beta_t, qext, kext


def diff_attention(proj3d, lq1, lk1, lq2, lk2, subg, layer, tk=256, tq=2048):
    b, t, _ = proj3d.shape
    tk, tq = min(tk, t), min(tq, t)
    assert tq % tk == 0 and tq % _Q_STRIP == 0
    lam_init = 0.8 - 0.6 * math.exp(-0.3 * layer)
    assert tk <= 256
    beta_t, qext, kext = _attn_tables(tk)
    vec = lambda a: a.reshape(1, -1).astype(jnp.float32)
    small = lambda n: pl.BlockSpec((1, n), lambda bi, h: (0, 0))
    head = pl.BlockSpec((None, 1, LANES), lambda bi, h: (h, 0, 0))
    col = lambda blk: pl.BlockSpec((None, t, LANES), lambda bi, h: (bi, 0, blk + h))
    return pl.pallas_call(
        functools.partial(_attn_kernel, lam_init=lam_init, tk=tk, tq=tq),
        grid=(b, ATT_HEADS),
        in_specs=[
            head, pl.BlockSpec((None, _Q_STRIP, LANES), lambda bi, h: (h, 0, 0)),
            small(ATT_QK_DIM), small(ATT_QK_DIM), small(ATT_QK_DIM), small(ATT_QK_DIM),
            pl.BlockSpec((ATT_V_DIM, 1), lambda bi, h: (0, 0)),
            col(_QBLK), col(_KBLK), col(_VBLK),
            pl.BlockSpec((2, tk, LANES), lambda bi, h: (0, 0, 0)),
        ],
        out_specs=pl.BlockSpec((None, t, LANES), lambda bi, h: (bi, 0, h)),
        out_shape=jax.ShapeDtypeStruct((b, t, ATT_HEADS * ATT_V_DIM), jnp.bfloat16),
        scratch_shapes=[pltpu.VMEM((ATT_V_DIM + _ONES_ROWS, t), jnp.bfloat16),
                        pltpu.VMEM((ATT_V_DIM + _ONES_ROWS, 2 * tq), jnp.float32),
                        pltpu.VMEM((tk, 2 * tq), jnp.float32)],
        compiler_params=_cparams(("parallel", "parallel")),
        name="diff_attn",
    )(beta_t, qext, vec(lq1), vec(lk1), vec(lq2), vec(lk2),
      subg.reshape(ATT_V_DIM, 1).astype(jnp.float32), proj3d, proj3d, proj3d, kext)


_HG_CHUNKS_PER_ITER = 8
_HG_FIN_ROWS = 256


def _scan_rows(g, rev):
    c = g.shape[0]
    row = lax.broadcasted_iota(jnp.int32, g.shape, 0)
    b = g
    s = 1
    while s < c:
        if rev:
            b = b + jnp.where(row < c - s, pltpu.roll(b, c - s, axis=0), 0.0)
        else:
            b = b + jnp.where(row >= s, pltpu.roll(b, s, axis=0), 0.0)
        s *= 2
    return b


def _level_ref(b, m, rev):
    c, w = b.shape
    r = m // 2 if rev else m // 2 - 1
    pieces = []
    if m >= SUBLANES:
        for j in range(c // m):
            pieces.append(jnp.broadcast_to(b[j * m + r:j * m + r + 1, :], (m, w)))
    else:
        sub = lax.broadcasted_iota(jnp.int32, (SUBLANES, w), 0)
        for j in range(c // SUBLANES):
            base = j * SUBLANES
            acc = jnp.broadcast_to(b[base + r:base + r + 1, :], (SUBLANES, w))
            for i in range(1, SUBLANES // m):
                cand = jnp.broadcast_to(b[base + i * m + r:base + i * m + r + 1, :], (SUBLANES, w))
                acc = jnp.where(sub >= i * m, cand, acc)
            pieces.append(acc)
    return jnp.concatenate(pieces, axis=0)


def _neg_abs(x):
    bits = lax.bitcast_convert_type(x, jnp.uint32) | jnp.uint32(0x80000000)
    return lax.bitcast_convert_type(bits, jnp.float32)


def _hgrn_masks(c, rev):
    row = lax.broadcasted_iota(jnp.int32, (c, LANES), 0)
    ti = lax.broadcasted_iota(jnp.int32, (c, c), 0)
    si = lax.broadcasted_iota(jnp.int32, (c, c), 1)
    qrows, pairs = {}, {}
    m = c
    while m >= 2:
        late = (row & (m - 1)) >= m // 2
        t_late = (ti & (m - 1)) >= m // 2
        s_late = (si & (m - 1)) >= m // 2
        same = (ti ^ si) < m
        if rev:
            qrows[m] = jnp.logical_not(late)
            pairs[m] = same & jnp.logical_not(t_late) & s_late
        else:
            qrows[m] = late
            pairs[m] = same & t_late & jnp.logical_not(s_late)
        m //= 2
    return qrows, pairs, ti == si


def _hgrn_chunk(q, f, v, st_ref, rev, masks):
    qrows, pairs, diag = masks
    c = q.shape[0]
    bf = jnp.bfloat16
    k_ = 1.0 - f
    b = _scan_rows(jnp.log(f) * _LOG2E, rev)
    a = jnp.where(diag,
                  lax.dot_general(q.astype(bf), k_.astype(bf), _NT,
                                  preferred_element_type=jnp.float32), 0.0)
    m = c
    while m >= 2:
        if m == 2:
            cref = jnp.where(qrows[m], pltpu.roll(b, c - 1 if rev else 1, axis=0), b)
        else:
            cref = _level_ref(b, m, rev)
        e = jnp.exp2(_neg_abs(b - cref))
        z = (jnp.where(qrows[m], q, k_) * e).astype(bf)
        p = lax.dot_general(z, z, _NT, preferred_element_type=jnp.float32)
        a = jnp.where(pairs[m], p, a)
        m //= 2
    st = st_ref[...]
    o = jnp.dot(a.astype(bf), v.astype(bf), preferred_element_type=jnp.float32)
    o = o + lax.dot_general((q * jnp.exp2(b)).astype(bf), st.astype(bf), _NT,
                            preferred_element_type=jnp.float32)
    btot = b[0:1, :] if rev else b[c - 1:c, :]
    khat = (k_ * jnp.exp2(btot - b)).astype(bf)
    st_ref[...] = st * jnp.exp2(btot) + jnp.dot(v.T.astype(bf), khat,
                                                preferred_element_type=jnp.float32)
    return o


def _hgrn_kernel(lb_ref, ng_ref, q_ref, ff_ref, fb_ref, i_ref, g_ref, o_ref,
                 of_sc, ob_sc, stf_sc, stb_sc, *, layer, chunk):
    t = q_ref.shape[0]
    n = t // chunk
    lb_raw = lb_ref[...].astype(jnp.float32)
    ex = jnp.exp(lb_raw - jnp.max(lb_raw, axis=0, keepdims=True))
    sm = ex / jnp.sum(ex, axis=0, keepdims=True)
    lb = jnp.zeros_like(sm[0])
    for l_ in range(1, layer + 1):
        lb = lb + sm[l_]
    lb_f = lb[0:1, :]
    lb_b = lb[1:2, :]

    stf_sc[...] = jnp.zeros(stf_sc.shape, jnp.float32)
    stb_sc[...] = jnp.zeros(stb_sc.shape, jnp.float32)

    def load(ref, r0):
        return ref[pl.ds(r0, chunk), :].astype(jnp.float32)

    masks_f = _hgrn_masks(chunk, False)
    masks_b = _hgrn_masks(chunk, True)

    per_iter = _HG_CHUNKS_PER_ITER if n % _HG_CHUNKS_PER_ITER == 0 else 1

    def body(ci, carry):
        for u in range(per_iter):
            r0 = pl.multiple_of((ci * per_iter + u) * chunk, chunk)
            qr = load(q_ref, r0)
            f = lb_f + (1.0 - lb_f) * jax.nn.sigmoid(load(ff_ref, r0))
            of_sc[pl.ds(r0, chunk), :] = _hgrn_chunk(qr * jax.nn.sigmoid(qr), f,
                                                     load(i_ref, r0), stf_sc, False, masks_f)
            r1 = pl.multiple_of((n - 1 - ci * per_iter - u) * chunk, chunk)
            qr = load(q_ref, r1)
            f = lb_b + (1.0 - lb_b) * jax.nn.sigmoid(load(fb_ref, r1))
            ob_sc[pl.ds(r1, chunk), :] = _hgrn_chunk(qr * jax.nn.sigmoid(qr), f,
                                                     load(i_ref, r1), stb_sc, True, masks_b)
        return carry

    lax.fori_loop(0, n // per_iter, body, 0)

    rows = math.gcd(t, _HG_FIN_ROWS)

    def fin(ci, carry):
        r0 = pl.multiple_of(ci * rows, rows)
        o = of_sc[pl.ds(r0, rows), :] + ob_sc[pl.ds(r0, rows), :]
        y = o * lax.rsqrt(jnp.mean(o * o, axis=-1, keepdims=True) + EPS) * ng_ref[...]
        gr = g_ref[pl.ds(r0, rows), :].astype(jnp.float32)
        o_ref[pl.ds(r0, rows), :] = (y * (gr * jax.nn.sigmoid(gr))).astype(o_ref.dtype)
        return carry

    lax.fori_loop(0, t // rows, fin, 0)


def hgrn_bidir(proj3d, hgrn_lb, norm_g, layer):
    b, t, _ = proj3d.shape
    depth = hgrn_lb.shape[0]
    col = lambda blk: pl.BlockSpec((None, t, LANES), lambda bi, h: (bi, 0, blk + h))
    return pl.pallas_call(
        functools.partial(_hgrn_kernel, layer=layer, chunk=HG_CHUNK),
        grid=(b, HG_HEADS),
        in_specs=[
            pl.BlockSpec((depth, 2, LANES), lambda bi, h: (0, 0, h)),
            pl.BlockSpec((1, HG_V), lambda bi, h: (0, 0)),
            col(_HQ), col(_HFF), col(_HFB), col(_HI), col(_HG),
        ],
        out_specs=pl.BlockSpec((None, t, LANES), lambda bi, h: (bi, 0, h)),
        out_shape=jax.ShapeDtypeStruct((b, t, HG_HEADS * HG_V), jnp.bfloat16),
        scratch_shapes=[
            pltpu.VMEM((t, HG_V), jnp.float32),
            pltpu.VMEM((t, HG_V), jnp.float32),
            pltpu.VMEM((HG_V, HG_K), jnp.float32),
            pltpu.VMEM((HG_V, HG_K), jnp.float32),
        ],
        compiler_params=_cparams(("parallel", "parallel")),
        name="hgrn_bidir",
    )(hgrn_lb, norm_g.reshape(1, HG_V).astype(jnp.float32),
      proj3d, proj3d, proj3d, proj3d, proj3d)


def _out_router_kernel(a_ref, h_ref, wa_ref, wh_ref, x_ref, g_ref, wrt_ref,
                       o_ref, ha_ref, hb_ref, afft_ref):
    y = jnp.dot(a_ref[...], wa_ref[...], preferred_element_type=jnp.float32)
    y = y + jnp.dot(h_ref[...], wh_ref[...], preferred_element_type=jnp.float32)
    x = x_ref[...] + y
    o_ref[...] = x
    h = x * lax.rsqrt(jnp.mean(x * x, axis=-1, keepdims=True) + EPS) * g_ref[...]
    bits = lax.bitcast_convert_type(h.astype(jnp.bfloat16).astype(jnp.float32), jnp.uint32)
    q = ha_ref.shape[1]
    ha_ref[...] = (bits[:, 0:q] >> 16) | bits[:, q:2 * q]
    hb_ref[...] = (bits[:, 2 * q:3 * q] >> 16) | bits[:, 3 * q:4 * q]
    logits = lax.dot_general(wrt_ref[...], h, _NT, preferred_element_type=jnp.float32,
                             precision=lax.Precision.HIGHEST)
    ex = jnp.exp(logits - jnp.max(logits, axis=0, keepdims=True))
    afft_ref[...] = ex / jnp.sum(ex, axis=0, keepdims=True)


def out_proj_router(att2d, hgo2d, w_out_bf16, x2d, g, w_router, tm=512):
    n, d = x2d.shape
    tm = min(tm, n)
    wa = att2d.shape[1]
    wh = hgo2d.shape[1]
    e = w_router.shape[1]
    rows = lambda w: pl.BlockSpec((tm, w), lambda i: (i, 0))
    return pl.pallas_call(
        _out_router_kernel,
        grid=(n // tm,),
        in_specs=[
            rows(wa), rows(wh),
            pl.BlockSpec((wa, d), lambda i: (0, 0)),
            pl.BlockSpec((wh, d), lambda i: (wa // wh, 0)),
            rows(d),
            pl.BlockSpec((1, d), lambda i: (0, 0)),
            pl.BlockSpec((e, d), lambda i: (0, 0)),
        ],
        out_specs=[rows(d), rows(d // 4), rows(d // 4), pl.BlockSpec((e, tm), lambda i: (0, i))],
        out_shape=[
            jax.ShapeDtypeStruct((n, d), jnp.float32),
            jax.ShapeDtypeStruct((n, d // 4), jnp.uint32),
            jax.ShapeDtypeStruct((n, d // 4), jnp.uint32),
            jax.ShapeDtypeStruct((e, n), jnp.float32),
        ],
        compiler_params=_cparams(("parallel",)),
        name="out_proj_router",
    )(att2d, hgo2d, w_out_bf16, w_out_bf16, x2d, g.reshape(1, d), w_router.T)


_SEL_CHUNK = 1024


def _select_kernel(afft_ref, pos_ref, offs_ref, *, cap):
    f32, i32 = jnp.float32, jnp.int32
    e, n = afft_ref.shape
    chunk = min(_SEL_CHUNK, n)
    nt = n // LANES

    def bits_at(start, size):
        return lax.bitcast_convert_type(afft_ref[:, pl.ds(start, size)], i32)

    def count_ge(cand):
        def body(c, acc):
            x = bits_at(pl.multiple_of(c * chunk, chunk), chunk)
            return acc + jnp.where(x >= cand, 1.0, 0.0)
        acc = lax.fori_loop(0, n // chunk, body, jnp.zeros((e, chunk), f32))
        return jnp.sum(acc, axis=1, keepdims=True)

    def bit_body(it, prefix):
        cand = prefix | jnp.left_shift(jnp.int32(1), 30 - it)
        return jnp.where(count_ge(cand) >= cap, cand, prefix)

    thr = lax.fori_loop(0, 31, bit_body, jnp.zeros((e, 1), i32))
    need = cap - count_ge(thr + 1)

    tri = (lax.broadcasted_iota(i32, (LANES, LANES), 0)
           <= lax.broadcasted_iota(i32, (LANES, LANES), 1)).astype(jnp.bfloat16)
    tile_id = lax.broadcasted_iota(i32, (e, nt), 1)

    def tile_body(j, carry):
        c_gt, c_eq, offs_acc = carry
        x = bits_at(pl.multiple_of(j * LANES, LANES), LANES)
        gt = x > thr
        eq = x == thr
        both = jnp.concatenate([jnp.where(gt, 1.0, 0.0), jnp.where(eq, 1.0, 0.0)],
                               axis=0).astype(jnp.bfloat16)
        inc = jnp.dot(both, tri, preferred_element_type=f32)
        inc_gt = inc[:e] + c_gt
        inc_eq = inc[e:] + c_eq
        sel = gt | (eq & (inc_eq <= need))
        incl = inc_gt + jnp.minimum(inc_eq, need)
        pos_ref[:, pl.ds(pl.multiple_of(j * LANES, LANES), LANES)] = jnp.where(
            sel, incl - 1.0, -1.0).astype(i32)
        offs_acc = jnp.where(tile_id == j, c_gt + jnp.minimum(c_eq, need), offs_acc)
        return inc_gt[:, LANES - 1:LANES], inc_eq[:, LANES - 1:LANES], offs_acc

    zero = jnp.zeros((e, 1), f32)
    _, _, offs_acc = lax.fori_loop(0, nt, tile_body, (zero, zero, jnp.zeros((e, nt), f32)))
    offs_ref[...] = offs_acc.astype(i32)


def ec_select(afft, cap):
    e, n = afft.shape
    return pl.pallas_call(
        functools.partial(_select_kernel, cap=cap),
        out_shape=[jax.ShapeDtypeStruct((e, n), jnp.int32),
                   jax.ShapeDtypeStruct((e, n // LANES), jnp.int32)],
        compiler_params=pltpu.CompilerParams(vmem_limit_bytes=_VMEM_LIMIT),
        name="ec_select",
    )(afft)


_SC_LANES = 16
_SC_CORES = 2
_INV_CHUNK = 8192


def ec_invert(pos, afft, cap):
    ne, n = pos.shape
    ch = min(_INV_CHUNK, n)
    mesh = plsc.VectorSubcoreMesh(core_axis_name="c", subcore_axis_name="s")

    @functools.partial(
        pl.kernel, mesh=mesh,
        out_type=[jax.ShapeDtypeStruct((ne * cap,), jnp.int32),
                  jax.ShapeDtypeStruct((ne * cap,), jnp.float32)],
        scratch_types=[pltpu.VMEM((ch,), jnp.int32), pltpu.VMEM((ch,), jnp.float32),
                       pltpu.VMEM((cap,), jnp.int32), pltpu.VMEM((cap,), jnp.float32)],
        compiler_params=pltpu.CompilerParams(needs_layout_passes=False))
    def invert(pos_hbm, aff_hbm, idx_hbm, gate_hbm, pos_v, aff_v, idx_v, gate_v):
        wid = lax.axis_index("s") * _SC_CORES + lax.axis_index("c")

        @pl.when(wid < ne)
        def _():
            @pl.loop(0, n // ch)
            def _(c):
                pltpu.sync_copy(pos_hbm.at[pl.ds(wid * n + c * ch, ch)], pos_v)
                pltpu.sync_copy(aff_hbm.at[pl.ds(wid * n + c * ch, ch)], aff_v)

                @pl.loop(0, ch, step=_SC_LANES)
                def _(i):
                    p = pos_v[pl.ds(i, _SC_LANES)]
                    tok = lax.iota(jnp.int32, _SC_LANES) + (c * ch + i)
                    plsc.store_scatter(idx_v, [p], tok, mask=p >= 0)
                    plsc.store_scatter(gate_v, [p], aff_v[pl.ds(i, _SC_LANES)], mask=p >= 0)

            pltpu.sync_copy(idx_v, idx_hbm.at[pl.ds(wid * cap, cap)])
            pltpu.sync_copy(gate_v, gate_hbm.at[pl.ds(wid * cap, cap)])

    idx, gates = invert(pos.reshape(ne * n), afft.reshape(ne * n))
    return idx.reshape(ne, cap), gates.reshape(ne, cap)


_GATHER_WIN = 128


def ec_gather(table, idx_row):
    m = idx_row.shape[1]
    width = table.shape[1]
    steps = m // (_GATHER_WIN * _SC_CORES)
    mesh = plsc.VectorSubcoreMesh(core_axis_name="c", subcore_axis_name="s")

    @functools.partial(
        pl.kernel, mesh=mesh,
        out_type=jax.ShapeDtypeStruct((m, width), table.dtype),
        scratch_types=[])
    def gather(x_hbm, i_hbm, o_hbm):
        def body(i_vmem, o_vmem):
            pltpu.sync_copy(x_hbm.at[i_vmem.at[0]], o_vmem)

        pltpu.emit_pipeline(
            body,
            grid=(_SC_CORES, steps),
            in_specs=[pl.BlockSpec((1, _GATHER_WIN), index_map=lambda a, i: (0, a * steps + i))],
            out_specs=[pl.BlockSpec((_GATHER_WIN, width),
                                    index_map=lambda a, i: (a * steps + i, 0))],
            core_axis_name=("c", "s"),
            dimension_semantics=(pltpu.PARALLEL, pltpu.PARALLEL),
        )(i_hbm, o_hbm)

    return gather(table, idx_row)


def _expert_kernel(xa_ref, xb_ref, gate_ref, wg_ref, wu_ref, wd_ref, o_ref):
    @pl.when(pl.program_id(1) == pl.num_programs(1) - 1)
    def _():
        o_ref[...] = jnp.zeros(o_ref.shape, o_ref.dtype)

    @pl.when(pl.program_id(1) < pl.num_programs(1) - 1)
    def _():
        as_f32 = lambda bits: lax.bitcast_convert_type(bits, jnp.float32)
        wa, wb = xa_ref[...], xb_ref[...]
        hi = jnp.uint32(0xFFFF0000)
        xs = jnp.concatenate([as_f32(wa << 16), as_f32(wa & hi), as_f32(wb << 16),
                              as_f32(wb & hi)], axis=1).astype(jnp.bfloat16)
        a = jnp.dot(xs, wg_ref[...], preferred_element_type=jnp.float32)
        u = jnp.dot(xs, wu_ref[...], preferred_element_type=jnp.float32)
        hid = (a * jax.nn.sigmoid(a) * u).astype(jnp.bfloat16)
        y = jnp.dot(hid, wd_ref[...], preferred_element_type=jnp.float32)
        o_ref[...] = (y * gate_ref[...]).astype(o_ref.dtype)


def expert_ffn(xa, xb, gates, wg, wu, wd, tm=512):
    e, c, q = xa.shape
    d = 4 * q
    tm = min(tm, c)
    f = wg.shape[2]
    nt = c // tm
    return pl.pallas_call(
        _expert_kernel,
        grid=(e, nt + 1),
        in_specs=[
            pl.BlockSpec((None, tm, q), lambda ei, i: (ei, jnp.minimum(i, nt - 1), 0)),
            pl.BlockSpec((None, tm, q), lambda ei, i: (ei, jnp.minimum(i, nt - 1), 0)),
            pl.BlockSpec((None, tm, 1), lambda ei, i: (ei, jnp.minimum(i, nt - 1), 0)),
            pl.BlockSpec((None, d, f), lambda ei, i: (ei, 0, 0)),
            pl.BlockSpec((None, d, f), lambda ei, i: (ei, 0, 0)),
            pl.BlockSpec((None, f, d), lambda ei, i: (ei, 0, 0)),
        ],
        out_specs=pl.BlockSpec((None, tm, d), lambda ei, i: (ei, i, 0)),
        out_shape=jax.ShapeDtypeStruct((e, c + tm, d), jnp.bfloat16),
        compiler_params=_cparams(("parallel", "arbitrary")),
        name="expert_ffn",
    )(xa, xb, gates.reshape(e, c, 1), wg, wu, wd)


_SLAB_ALIGN = 16
_SLAB_ROWS = LANES + _SLAB_ALIGN


def _slab_copy(offs_ref, out_hbm, slab_ref, sem_ref, e, j, slot):
    start = pl.multiple_of((offs_ref[e, j] // _SLAB_ALIGN) * _SLAB_ALIGN, _SLAB_ALIGN)
    return pltpu.make_async_copy(out_hbm.at[e, pl.ds(start, _SLAB_ROWS), :],
                                 slab_ref.at[slot, e], sem_ref.at[slot, e])


def _combine_kernel(offs_ref, x_ref, pos_ref, fg_ref, out_hbm, o_ref, slab_ref, sem_ref, *,
                    final_norm):
    f32 = jnp.float32
    ne = pos_ref.shape[0]
    j = pl.program_id(0)
    slot = j % 2

    @pl.when(j == 0)
    def _():
        for e in range(ne):
            _slab_copy(offs_ref, out_hbm, slab_ref, sem_ref, e, j, slot).start()

    @pl.when(j + 1 < pl.num_programs(0))
    def _():
        for e in range(ne):
            _slab_copy(offs_ref, out_hbm, slab_ref, sem_ref, e, j + 1, 1 - slot).start()

    pad = jnp.zeros((LANES - ne, LANES), f32)
    pos_t = jnp.concatenate([pos_ref[...].astype(f32), pad], axis=0).T
    row_id = lax.broadcasted_iota(jnp.int32, (LANES, _SLAB_ROWS), 1).astype(f32)
    for e in range(ne):
        _slab_copy(offs_ref, out_hbm, slab_ref, sem_ref, e, j, slot).wait()
    y = x_ref[...]
    for e in range(ne):
        start = (offs_ref[e, j] // _SLAB_ALIGN) * _SLAB_ALIGN
        rel = pos_t[:, e:e + 1] - start.astype(f32)
        onehot = jnp.where(rel == row_id, 1.0, 0.0).astype(jnp.bfloat16)
        y = y + jnp.dot(onehot, slab_ref[slot, e], preferred_element_type=f32)
    if final_norm:
        y = y * lax.rsqrt(jnp.mean(y * y, axis=-1, keepdims=True) + EPS) * fg_ref[...]
    o_ref[...] = y


def ec_combine(x2d, pos, offs, out, final_g, final_norm):
    n, d = x2d.shape
    ne, crows, _ = out.shape
    assert crows >= n * EC_CAPACITY // ne + _SLAB_ROWS
    grid_spec = pltpu.PrefetchScalarGridSpec(
        num_scalar_prefetch=1,
        grid=(n // LANES,),
        in_specs=[
            pl.BlockSpec((LANES, d), lambda j, offs_ref: (j, 0)),
            pl.BlockSpec((ne, LANES), lambda j, offs_ref: (0, j)),
            pl.BlockSpec((1, d), lambda j, offs_ref: (0, 0)),
            pl.BlockSpec(memory_space=pl.ANY),
        ],
        out_specs=pl.BlockSpec((LANES, d), lambda j, offs_ref: (j, 0)),
        scratch_shapes=[pltpu.VMEM((2, ne, _SLAB_ROWS, d), out.dtype),
                        pltpu.SemaphoreType.DMA((2, ne))],
    )
    return pl.pallas_call(
        functools.partial(_combine_kernel, final_norm=final_norm),
        grid_spec=grid_spec,
        out_shape=jax.ShapeDtypeStruct((n, d), jnp.float32),
        compiler_params=_cparams(("arbitrary",)),
        name="ec_combine",
    )(offs, x2d, pos, final_g.reshape(1, d), out)


def _trunk(x, p):
    b, t, d = x.shape
    n = b * t
    depth = p["w_in"].shape[0]
    x2d = x.reshape(n, d)
    cap = max(1, EC_CAPACITY * n // N_EXPERTS)
    for layer in range(depth):
        proj = norm_proj(x2d, p["norm_mix_g"][layer], p["w_in"][layer])
        proj3d = proj.reshape(b, t, -1)
        att = diff_attention(proj3d, p["lambda_q1"][layer], p["lambda_k1"][layer],
                             p["lambda_q2"][layer], p["lambda_k2"][layer],
                             p["diff_subln_g"][layer], layer)
        hgo = hgrn_bidir(proj3d, p["hgrn_lb"], p["hgrn_norm_g"][layer], layer)
        x2d, ha, hb, afft = out_proj_router(att.reshape(n, -1), hgo.reshape(n, -1),
                                            p["w_out"][layer], x2d, p["norm_ffn_g"][layer],
                                            p["w_router"][layer])
        pos, offs = ec_select(afft, cap)
        ne = afft.shape[0]
        idx, gates = ec_invert(pos, afft, cap)
        idx = idx.reshape(1, ne * cap)
        xa = ec_gather(ha, idx).reshape(ne, cap, -1)
        xb = ec_gather(hb, idx).reshape(ne, cap, -1)
        out = expert_ffn(xa, xb, gates, p["w_gate"][layer], p["w_up"][layer],
                         p["w_down"][layer])
        x2d = ec_combine(x2d, pos, offs, out, p["final_norm_g"], layer == depth - 1)
    return x2d.reshape(b, t, d)


def kernel(x_prompt, x_sample, norm_mix_g, w_in, lambda_q1, lambda_k1, lambda_q2, lambda_k2,
           diff_subln_g, hgrn_lb, hgrn_norm_g, w_out, norm_ffn_g, w_router, w_gate, w_up,
           w_down, final_norm_g):
    bf = jnp.bfloat16
    p = dict(norm_mix_g=norm_mix_g, w_in=w_in.astype(bf), lambda_q1=lambda_q1,
             lambda_k1=lambda_k1, lambda_q2=lambda_q2, lambda_k2=lambda_k2,
             diff_subln_g=diff_subln_g, hgrn_lb=hgrn_lb, hgrn_norm_g=hgrn_norm_g,
             w_out=w_out.astype(bf), norm_ffn_g=norm_ffn_g, w_router=w_router,
             w_gate=w_gate.astype(bf), w_up=w_up.astype(bf), w_down=w_down.astype(bf),
             final_norm_g=final_norm_g)
    return (_trunk(x_prompt, p), _trunk(x_sample, p))
```

```python
import functools
import math

import jax
import jax.numpy as jnp
from jax import lax
from jax.experimental import pallas as pl
from jax.experimental.pallas import tpu as pltpu
from jax.experimental.pallas import tpu_sc as plsc

ATT_HEADS = 4
ATT_QK_DIM = 64
ATT_V_DIM = 128
HG_HEADS = 4
HG_K = 128
HG_V = 128
HG_CHUNK = 64
N_EXPERTS = 16
EC_CAPACITY = 2
EPS = 1e-6
LANES = 128
SUBLANES = 8

_QBLK, _KBLK, _VBLK = 0, 4, 8
_HQ, _HFF, _HFB, _HI, _HG = 12, 16, 20, 24, 28

_VMEM_LIMIT = 56 * 1024 * 1024

_NT = (((1,), (1,)), ((), ()))


def _cparams(sem):
    return pltpu.CompilerParams(dimension_semantics=sem, vmem_limit_bytes=_VMEM_LIMIT)


def _norm_proj_kernel(x_ref, g_ref, w_ref, o_ref, *, n_split):
    x = x_ref[...]
    y = x * lax.rsqrt(jnp.mean(x * x, axis=-1, keepdims=True) + EPS) * g_ref[...]
    h = y.astype(jnp.bfloat16)
    wn = w_ref.shape[1] // n_split
    for c in range(n_split):
        o_ref[:, c * wn:(c + 1) * wn] = jnp.dot(
            h, w_ref[:, c * wn:(c + 1) * wn], preferred_element_type=jnp.float32
        ).astype(o_ref.dtype)


def norm_proj(x2d, g, w_bf16, tm=512):
    n, d = x2d.shape
    tm = min(tm, n)
    dout = w_bf16.shape[1]
    return pl.pallas_call(
        functools.partial(_norm_proj_kernel, n_split=max(1, dout // 1024)),
        grid=(n // tm,),
        in_specs=[
            pl.BlockSpec((tm, d), lambda i: (i, 0)),
            pl.BlockSpec((1, d), lambda i: (0, 0)),
            pl.BlockSpec((d, dout), lambda i: (0, 0)),
        ],
        out_specs=pl.BlockSpec((tm, dout), lambda i: (i, 0)),
        out_shape=jax.ShapeDtypeStruct((n, dout), jnp.bfloat16),
        compiler_params=_cparams(("parallel",)),
        name="norm_proj",
    )(x2d, g.reshape(1, d), w_bf16)


_LOG2E = 1.4426950408889634
_Q_STRIP = 128
_ONES_ROWS = 16


def _split3(x):
    bf = jnp.bfloat16
    h = x.astype(bf).astype(jnp.float32)
    r = x - h
    m = r.astype(bf).astype(jnp.float32)
    lo = (r - m).astype(bf).astype(jnp.float32)
    return h, m, lo


def _attn_kernel(beta_ref, qext_ref, lq1_ref, lk1_ref, lq2_ref, lk2_ref, subg_ref,
                 q_ref, k_ref, v_ref, kext_ref, o_ref, vt_sc, acc_sc, s_sc, *, lam_init, tk, tq):
    f32, bf = jnp.float32, jnp.bfloat16
    t = q_ref.shape[0]
    nk = t // tk
    qs_ = _Q_STRIP
    ng = tq // qs_
    ncol = 2 * tq
    dv = v_ref.shape[1]
    beta = beta_ref[0:1, 0:1]

    def vt_body(c, carry):
        r0 = pl.multiple_of(c * tk, tk)
        vt_sc[0:dv, pl.ds(r0, tk)] = v_ref[pl.ds(r0, tk), :].astype(f32).T.astype(bf)
        return carry

    lax.fori_loop(0, nk, vt_body, 0)
    vt_sc[dv:dv + _ONES_ROWS, :] = jnp.ones((_ONES_ROWS, t), bf)

    lam = (jnp.exp(jnp.sum(lq1_ref[...] * lk1_ref[...], axis=-1, keepdims=True))
           - jnp.exp(jnp.sum(lq2_ref[...] * lk2_ref[...], axis=-1, keepdims=True)) + lam_init)
    lane = lax.broadcasted_iota(jnp.int32, (qs_, LANES), 1)
    q_minus_r = (lax.broadcasted_iota(jnp.int32, (tk, qs_), 1)
                 - lax.broadcasted_iota(jnp.int32, (tk, qs_), 0)).astype(f32)
    cscale = ATT_QK_DIM ** -0.5 * _LOG2E

    def group(gi, carry):
        g0 = pl.multiple_of(gi * tq, tq)
        tiles = []
        for u in range(ng):
            q0 = g0 + u * qs_
            qs = (q_ref[pl.ds(q0, qs_), :].astype(f32) * cscale).astype(bf).astype(f32)
            qext = qext_ref[...]
            q_aug = jnp.concatenate([
                jnp.concatenate([jnp.where(lane < ATT_QK_DIM, qs, 0.0), qext], axis=1),
                jnp.concatenate([jnp.where(lane >= ATT_QK_DIM, qs, 0.0), qext], axis=1)], axis=0)
            tiles.append(q_aug.T.astype(bf))
        w = jnp.concatenate(tiles, axis=1)
        jc = g0 // tk

        def corr_tile(d):
            corr = 2.0 * beta * jnp.minimum(q_minus_r + d.astype(f32), 0.0)
            return jnp.concatenate([corr, corr], axis=1)

        def shift(d, sidx):
            sgn = 1.0 if sidx == 0 else -1.0
            return (-sgn) * beta * d.astype(f32)

        def scores(n, sidx, crossing):
            r0 = pl.multiple_of(n * tk, tk)
            k_aug = jnp.concatenate([k_ref[pl.ds(r0, tk), :], kext_ref[sidx]], axis=1)
            for u in range(ng):
                c0 = u * 2 * qs_
                s = jnp.dot(k_aug, w[:, c0:c0 + 2 * qs_], preferred_element_type=f32)
                if crossing:
                    s = s + corr_tile(g0 + u * qs_ - r0)
                s_sc[:, c0:c0 + 2 * qs_] = s

        def step(n, mc, sidx, crossing):
            r0 = pl.multiple_of((n - 1) * tk, tk)
            r1 = pl.multiple_of(n * tk, tk)
            vt = vt_sc[:, pl.ds(r0, tk)]
            if sidx is not None:
                k_aug = jnp.concatenate([k_ref[pl.ds(r1, tk), :], kext_ref[sidx]], axis=1)
            m, csts = mc
            ms, cs = [], []
            for u in range(ng):
                c0 = u * 2 * qs_
                s = s_sc[:, c0:c0 + 2 * qs_]
                m_old = m[:, c0:c0 + 2 * qs_]
                cst = csts[:, c0:c0 + 2 * qs_]
                m_new = jnp.maximum(m_old, jnp.max(s, axis=0, keepdims=True) + cst)
                alpha = jnp.exp2(m_old - m_new)
                p = jnp.exp2(s - (m_new - cst)).astype(bf)
                acc_sc[:, c0:c0 + 2 * qs_] = (acc_sc[:, c0:c0 + 2 * qs_] * alpha
                                              + jnp.dot(vt, p, preferred_element_type=f32))
                ms.append(m_new)
                if sidx is not None:
                    s = jnp.dot(k_aug, w[:, c0:c0 + 2 * qs_], preferred_element_type=f32)
                    if crossing:
                        s = s + corr_tile(g0 + u * qs_ - r1)
                    s_sc[:, c0:c0 + 2 * qs_] = s
                    cs.append(jnp.broadcast_to(shift(g0 + u * qs_ - r1, sidx), (1, 2 * qs_)))
            return jnp.concatenate(ms, axis=1), (jnp.concatenate(cs, axis=1) if cs else csts)

        acc_sc[...] = jnp.zeros(acc_sc.shape, f32)
        m = jnp.full((1, ncol), -1e30, f32)
        scores(0, 0, True)
        n1 = jnp.maximum(jc, 1)
        n2 = jc + tq // tk
        cst0 = jnp.concatenate([jnp.broadcast_to(shift(g0 + u * qs_, 0), (1, 2 * qs_))
                                for u in range(ng)], axis=1)
        mc = (m, cst0)
        mc = lax.fori_loop(1, n1, functools.partial(step, sidx=0, crossing=False), mc)
        mc = lax.fori_loop(n1, n2, functools.partial(step, sidx=0, crossing=True), mc)
        mc = lax.fori_loop(n2, nk, functools.partial(step, sidx=1, crossing=False), mc)
        step(nk, mc, None, False)

        for u in range(ng):
            c0 = u * 2 * qs_
            acc = acc_sc[:, c0:c0 + 2 * qs_]
            den = acc[dv:dv + 1, :]
            o = acc[0:dv, 0:qs_] / den[:, 0:qs_] - lam * (acc[0:dv, qs_:] / den[:, qs_:])
            y = o * lax.rsqrt(jnp.mean(o * o, axis=0, keepdims=True) + EPS) * subg_ref[...]
            o_ref[pl.ds(g0 + u * qs_, qs_), :] = (y * (1.0 - lam_init)).T.astype(o_ref.dtype)
        return carry

    lax.fori_loop(0, t // tq, group, 0)


def _attn_tables(tk):
    f32 = jnp.float32
    slopes = 2.0 ** (-8.0 * (jnp.arange(ATT_HEADS, dtype=f32) + 1.0) / ATT_HEADS)
    beta = slopes * _LOG2E
    lane = jnp.arange(LANES)[None, None, :]
    b = _split3(beta)
    a = _split3(-beta[:, None] * jnp.arange(_Q_STRIP, dtype=f32)[None, :])
    qext = jnp.zeros((ATT_HEADS, _Q_STRIP, LANES), f32)
    for c in range(3):
        qext = jnp.where(lane == c, b[c][:, None, None], qext)
        qext = jnp.where(lane == 3 + c, a[c][:, :, None], qext)
    r = jnp.arange(tk, dtype=f32)[:, None]
    lane2 = jnp.arange(LANES)[None, :]
    base = jnp.where(lane2 < 3, r, jnp.where(lane2 < 6, 1.0, 0.0))
    kext = jnp.stack([base, -base]).astype(jnp.bfloat16)
    beta_t = jnp.broadcast_to(beta[:, None, None], (ATT_HEADS, 1, LANES))
    return beta_t, qext, kext


def diff_attention(proj3d, lq1, lk1, lq2, lk2, subg, layer, tk=256, tq=2048):
    b, t, _ = proj3d.shape
    tk, tq = min(tk, t), min(tq, t)
    assert tq % tk == 0 and tq % _Q_STRIP == 0
    lam_init = 0.8 - 0.6 * math.exp(-0.3 * layer)
    assert tk <= 256
    beta_t, qext, kext = _attn_tables(tk)
    vec = lambda a: a.reshape(1, -1).astype(jnp.float32)
    small = lambda n: pl.BlockSpec((1, n), lambda bi, h: (0, 0))
    head = pl.BlockSpec((None, 1, LANES), lambda bi, h: (h, 0, 0))
    col = lambda blk: pl.BlockSpec((None, t, LANES), lambda bi, h: (bi, 0, blk + h))
    return pl.pallas_call(
        functools.partial(_attn_kernel, lam_init=lam_init, tk=tk, tq=tq),
        grid=(b, ATT_HEADS),
        in_specs=[
            head, pl.BlockSpec((None, _Q_STRIP, LANES), lambda bi, h: (h, 0, 0)),
            small(ATT_QK_DIM), small(ATT_QK_DIM), small(ATT_QK_DIM), small(ATT_QK_DIM),
            pl.BlockSpec((ATT_V_DIM, 1), lambda bi, h: (0, 0)),
            col(_QBLK), col(_KBLK), col(_VBLK),
            pl.BlockSpec((2, tk, LANES), lambda bi, h: (0, 0, 0)),
        ],
        out_specs=pl.BlockSpec((None, t, LANES), lambda bi, h: (bi, 0, h)),
        out_shape=jax.ShapeDtypeStruct((b, t, ATT_HEADS * ATT_V_DIM), jnp.bfloat16),
        scratch_shapes=[pltpu.VMEM((ATT_V_DIM + _ONES_ROWS, t), jnp.bfloat16),
                        pltpu.VMEM((ATT_V_DIM + _ONES_ROWS, 2 * tq), jnp.float32),
                        pltpu.VMEM((tk, 2 * tq), jnp.float32)],
        compiler_params=_cparams(("parallel", "parallel")),
        name="diff_attn",
    )(beta_t, qext, vec(lq1), vec(lk1), vec(lq2), vec(lk2),
      subg.reshape(ATT_V_DIM, 1).astype(jnp.float32), proj3d, proj3d, proj3d, kext)


_HG_CHUNKS_PER_ITER = 8
_HG_FIN_ROWS = 256


def _scan_rows(g, rev):
    c = g.shape[0]
    row = lax.broadcasted_iota(jnp.int32, g.shape, 0)
    b = g
    s = 1
    while s < c:
        if rev:
            b = b + jnp.where(row < c - s, pltpu.roll(b, c - s, axis=0), 0.0)
        else:
            b = b + jnp.where(row >= s, pltpu.roll(b, s, axis=0), 0.0)
        s *= 2
    return b


def _level_ref(b, m, rev):
    c, w = b.shape
    r = m // 2 if rev else m // 2 - 1
    pieces = []
    if m >= SUBLANES:
        for j in range(c // m):
            pieces.append(jnp.broadcast_to(b[j * m + r:j * m + r + 1, :], (m, w)))
    else:
        sub = lax.broadcasted_iota(jnp.int32, (SUBLANES, w), 0)
        for j in range(c // SUBLANES):
            base = j * SUBLANES
            acc = jnp.broadcast_to(b[base + r:base + r + 1, :], (SUBLANES, w))
            for i in range(1, SUBLANES // m):
                cand = jnp.broadcast_to(b[base + i * m + r:base + i * m + r + 1, :], (SUBLANES, w))
                acc = jnp.where(sub >= i * m, cand, acc)
            pieces.append(acc)
    return jnp.concatenate(pieces, axis=0)


def _neg_abs(x):
    bits = lax.bitcast_convert_type(x, jnp.uint32) | jnp.uint32(0x80000000)
    return lax.bitcast_convert_type(bits, jnp.float32)


def _hgrn_masks(c, rev):
    row = lax.broadcasted_iota(jnp.int32, (c, LANES), 0)
    ti = lax.broadcasted_iota(jnp.int32, (c, c), 0)
    si = lax.broadcasted_iota(jnp.int32, (c, c), 1)
    qrows, pairs = {}, {}
    m = c
    while m >= 2:
        late = (row & (m - 1)) >= m // 2
        t_late = (ti & (m - 1)) >= m // 2
        s_late = (si & (m - 1)) >= m // 2
        same = (ti ^ si) < m
        if rev:
            qrows[m] = jnp.logical_not(late)
            pairs[m] = same & jnp.logical_not(t_late) & s_late
        else:
            qrows[m] = late
            pairs[m] = same & t_late & jnp.logical_not(s_late)
        m //= 2
    return qrows, pairs, ti == si


def _hgrn_chunk(q, f, v, st_ref, rev, masks):
    qrows, pairs, diag = masks
    c = q.shape[0]
    bf = jnp.bfloat16
    k_ = 1.0 - f
    b = _scan_rows(jnp.log(f) * _LOG2E, rev)
    a = jnp.where(diag,
                  lax.dot_general(q.astype(bf), k_.astype(bf), _NT,
                                  preferred_element_type=jnp.float32), 0.0)
    m = c
    while m >= 2:
        if m == 2:
            cref = jnp.where(qrows[m], pltpu.roll(b, c - 1 if rev else 1, axis=0), b)
        else:
            cref = _level_ref(b, m, rev)
        e = jnp.exp2(_neg_abs(b - cref))
        z = (jnp.where(qrows[m], q, k_) * e).astype(bf)
        p = lax.dot_general(z, z, _NT, preferred_element_type=jnp.float32)
        a = jnp.where(pairs[m], p, a)
        m //= 2
    st = st_ref[...]
    o = jnp.dot(a.astype(bf), v.astype(bf), preferred_element_type=jnp.float32)
    o = o + lax.dot_general((q * jnp.exp2(b)).astype(bf), st.astype(bf), _NT,
                            preferred_element_type=jnp.float32)
    btot = b[0:1, :] if rev else b[c - 1:c, :]
    khat = (k_ * jnp.exp2(btot - b)).astype(bf)
    st_ref[...] = st * jnp.exp2(btot) + jnp.dot(v.T.astype(bf), khat,
                                                preferred_element_type=jnp.float32)
    return o


def _hgrn_kernel(lb_ref, ng_ref, q_ref, ff_ref, fb_ref, i_ref, g_ref, o_ref,
                 of_sc, ob_sc, stf_sc, stb_sc, *, layer, chunk):
    t = q_ref.shape[0]
    n = t // chunk
    lb_raw = lb_ref[...].astype(jnp.float32)
    ex = jnp.exp(lb_raw - jnp.max(lb_raw, axis=0, keepdims=True))
    sm = ex / jnp.sum(ex, axis=0, keepdims=True)
    lb = jnp.zeros_like(sm[0])
    for l_ in range(1, layer + 1):
        lb = lb + sm[l_]
    lb_f = lb[0:1, :]
    lb_b = lb[1:2, :]

    stf_sc[...] = jnp.zeros(stf_sc.shape, jnp.float32)
    stb_sc[...] = jnp.zeros(stb_sc.shape, jnp.float32)

    def load(ref, r0):
        return ref[pl.ds(r0, chunk), :].astype(jnp.float32)

    masks_f = _hgrn_masks(chunk, False)
    masks_b = _hgrn_masks(chunk, True)

    per_iter = _HG_CHUNKS_PER_ITER if n % _HG_CHUNKS_PER_ITER == 0 else 1

    def body(ci, carry):
        for u in range(per_iter):
            r0 = pl.multiple_of((ci * per_iter + u) * chunk, chunk)
            qr = load(q_ref, r0)
            f = lb_f + (1.0 - lb_f) * jax.nn.sigmoid(load(ff_ref, r0))
            of_sc[pl.ds(r0, chunk), :] = _hgrn_chunk(qr * jax.nn.sigmoid(qr), f,
                                                     load(i_ref, r0), stf_sc, False, masks_f)
            r1 = pl.multiple_of((n - 1 - ci * per_iter - u) * chunk, chunk)
            qr = load(q_ref, r1)
            f = lb_b + (1.0 - lb_b) * jax.nn.sigmoid(load(fb_ref, r1))
            ob_sc[pl.ds(r1, chunk), :] = _hgrn_chunk(qr * jax.nn.sigmoid(qr), f,
                                                     load(i_ref, r1), stb_sc, True, masks_b)
        return carry

    lax.fori_loop(0, n // per_iter, body, 0)

    rows = math.gcd(t, _HG_FIN_ROWS)

    def fin(ci, carry):
        r0 = pl.multiple_of(ci * rows, rows)
        o = of_sc[pl.ds(r0, rows), :] + ob_sc[pl.ds(r0, rows), :]
        y = o * lax.rsqrt(jnp.mean(o * o, axis=-1, keepdims=True) + EPS) * ng_ref[...]
        gr = g_ref[pl.ds(r0, rows), :].astype(jnp.float32)
        o_ref[pl.ds(r0, rows), :] = (y * (gr * jax.nn.sigmoid(gr))).astype(o_ref.dtype)
        return carry

    lax.fori_loop(0, t // rows, fin, 0)


def hgrn_bidir(proj3d, hgrn_lb, norm_g, layer):
    b, t, _ = proj3d.shape
    depth = hgrn_lb.shape[0]
    col = lambda blk: pl.BlockSpec((None, t, LANES), lambda bi, h: (bi, 0, blk + h))
    return pl.pallas_call(
        functools.partial(_hgrn_kernel, layer=layer, chunk=HG_CHUNK),
        grid=(b, HG_HEADS),
        in_specs=[
            pl.BlockSpec((depth, 2, LANES), lambda bi, h: (0, 0, h)),
            pl.BlockSpec((1, HG_V), lambda bi, h: (0, 0)),
            col(_HQ), col(_HFF), col(_HFB), col(_HI), col(_HG),
        ],
        out_specs=pl.BlockSpec((None, t, LANES), lambda bi, h: (bi, 0, h)),
        out_shape=jax.ShapeDtypeStruct((b, t, HG_HEADS * HG_V), jnp.bfloat16),
        scratch_shapes=[
            pltpu.VMEM((t, HG_V), jnp.float32),
            pltpu.VMEM((t, HG_V), jnp.float32),
            pltpu.VMEM((HG_V, HG_K), jnp.float32),
            pltpu.VMEM((HG_V, HG_K), jnp.float32),
        ],
        compiler_params=_cparams(("parallel", "parallel")),
        name="hgrn_bidir",
    )(hgrn_lb, norm_g.reshape(1, HG_V).astype(jnp.float32),
      proj3d, proj3d, proj3d, proj3d, proj3d)


def _out_router_kernel(a_ref, h_ref, wa_ref, wh_ref, x_ref, g_ref, wrt_ref,
                       o_ref, ha_ref, hb_ref, afft_ref):
    y = jnp.dot(a_ref[...], wa_ref[...], preferred_element_type=jnp.float32)
    y = y + jnp.dot(h_ref[...], wh_ref[...], preferred_element_type=jnp.float32)
    x = x_ref[...] + y
    o_ref[...] = x
    h = x * lax.rsqrt(jnp.mean(x * x, axis=-1, keepdims=True) + EPS) * g_ref[...]
    bits = lax.bitcast_convert_type(h.astype(jnp.bfloat16).astype(jnp.float32), jnp.uint32)
    q = ha_ref.shape[1]
    ha_ref[...] = (bits[:, 0:q] >> 16) | bits[:, q:2 * q]
    hb_ref[...] = (bits[:, 2 * q:3 * q] >> 16) | bits[:, 3 * q:4 * q]
    logits = lax.dot_general(wrt_ref[...], h, _NT, preferred_element_type=jnp.float32,
                             precision=lax.Precision.HIGHEST)
    ex = jnp.exp(logits - jnp.max(logits, axis=0, keepdims=True))
    afft_ref[...] = ex / jnp.sum(ex, axis=0, keepdims=True)


def out_proj_router(att2d, hgo2d, w_out_bf16, x2d, g, w_router, tm=512):
    n, d = x2d.shape
    tm = min(tm, n)
    wa = att2d.shape[1]
    wh = hgo2d.shape[1]
    e = w_router.shape[1]
    rows = lambda w: pl.BlockSpec((tm, w), lambda i: (i, 0))
    return pl.pallas_call(
        _out_router_kernel,
        grid=(n // tm,),
        in_specs=[
            rows(wa), rows(wh),
            pl.BlockSpec((wa, d), lambda i: (0, 0)),
            pl.BlockSpec((wh, d), lambda i: (wa // wh, 0)),
            rows(d),
            pl.BlockSpec((1, d), lambda i: (0, 0)),
            pl.BlockSpec((e, d), lambda i: (0, 0)),
        ],
        out_specs=[rows(d), rows(d // 4), rows(d // 4), pl.BlockSpec((e, tm), lambda i: (0, i))],
        out_shape=[
            jax.ShapeDtypeStruct((n, d), jnp.float32),
            jax.ShapeDtypeStruct((n, d // 4), jnp.uint32),
            jax.ShapeDtypeStruct((n, d // 4), jnp.uint32),
            jax.ShapeDtypeStruct((e, n), jnp.float32),
        ],
        compiler_params=_cparams(("parallel",)),
        name="out_proj_router",
    )(att2d, hgo2d, w_out_bf16, w_out_bf16, x2d, g.reshape(1, d), w_router.T)


_SEL_CHUNK = 1024


def _select_kernel(afft_ref, pos_ref, offs_ref, *, cap):
    f32, i32 = jnp.float32, jnp.int32
    e, n = afft_ref.shape
    chunk = min(_SEL_CHUNK, n)
    nt = n // LANES

    def bits_at(start, size):
        return lax.bitcast_convert_type(afft_ref[:, pl.ds(start, size)], i32)

    def count_ge(cand):
        def body(c, acc):
            x = bits_at(pl.multiple_of(c * chunk, chunk), chunk)
            return acc + jnp.where(x >= cand, 1.0, 0.0)
        acc = lax.fori_loop(0, n // chunk, body, jnp.zeros((e, chunk), f32))
        return jnp.sum(acc, axis=1, keepdims=True)

    def bit_body(it, prefix):
        cand = prefix | jnp.left_shift(jnp.int32(1), 30 - it)
        return jnp.where(count_ge(cand) >= cap, cand, prefix)

    thr = lax.fori_loop(0, 31, bit_body, jnp.zeros((e, 1), i32))
    need = cap - count_ge(thr + 1)

    tri = (lax.broadcasted_iota(i32, (LANES, LANES), 0)
           <= lax.broadcasted_iota(i32, (LANES, LANES), 1)).astype(jnp.bfloat16)
    tile_id = lax.broadcasted_iota(i32, (e, nt), 1)

    def tile_body(j, carry):
        c_gt, c_eq, offs_acc = carry
        x = bits_at(pl.multiple_of(j * LANES, LANES), LANES)
        gt = x > thr
        eq = x == thr
        both = jnp.concatenate([jnp.where(gt, 1.0, 0.0), jnp.where(eq, 1.0, 0.0)],
                               axis=0).astype(jnp.bfloat16)
        inc = jnp.dot(both, tri, preferred_element_type=f32)
        inc_gt = inc[:e] + c_gt
        inc_eq = inc[e:] + c_eq
        sel = gt | (eq & (inc_eq <= need))
        incl = inc_gt + jnp.minimum(inc_eq, need)
        pos_ref[:, pl.ds(pl.multiple_of(j * LANES, LANES), LANES)] = jnp.where(
            sel, incl - 1.0, -1.0).astype(i32)
        offs_acc = jnp.where(tile_id == j, c_gt + jnp.minimum(c_eq, need), offs_acc)
        return inc_gt[:, LANES - 1:LANES], inc_eq[:, LANES - 1:LANES], offs_acc

    zero = jnp.zeros((e, 1), f32)
    _, _, offs_acc = lax.fori_loop(0, nt, tile_body, (zero, zero, jnp.zeros((e, nt), f32)))
    offs_ref[...] = offs_acc.astype(i32)


def ec_select(afft, cap):
    e, n = afft.shape
    return pl.pallas_call(
        functools.partial(_select_kernel, cap=cap),
        out_shape=[jax.ShapeDtypeStruct((e, n), jnp.int32),
                   jax.ShapeDtypeStruct((e, n // LANES), jnp.int32)],
        compiler_params=pltpu.CompilerParams(vmem_limit_bytes=_VMEM_LIMIT),
        name="ec_select",
    )(afft)


_SC_LANES = 16
_SC_CORES = 2
_INV_CHUNK = 8192


def ec_invert(pos, afft, cap):
    ne, n = pos.shape
    ch = min(_INV_CHUNK, n)
    mesh = plsc.VectorSubcoreMesh(core_axis_name="c", subcore_axis_name="s")

    @functools.partial(
        pl.kernel, mesh=mesh,
        out_type=[jax.ShapeDtypeStruct((ne * cap,), jnp.int32),
                  jax.ShapeDtypeStruct((ne * cap,), jnp.float32)],
        scratch_types=[pltpu.VMEM((ch,), jnp.int32), pltpu.VMEM((ch,), jnp.float32),
                       pltpu.VMEM((cap,), jnp.int32), pltpu.VMEM((cap,), jnp.float32)],
        compiler_params=pltpu.CompilerParams(needs_layout_passes=False))
    def invert(pos_hbm, aff_hbm, idx_hbm, gate_hbm, pos_v, aff_v, idx_v, gate_v):
        wid = lax.axis_index("s") * _SC_CORES + lax.axis_index("c")

        @pl.when(wid < ne)
        def _():
            @pl.loop(0, n // ch)
            def _(c):
                pltpu.sync_copy(pos_hbm.at[pl.ds(wid * n + c * ch, ch)], pos_v)
                pltpu.sync_copy(aff_hbm.at[pl.ds(wid * n + c * ch, ch)], aff_v)

                @pl.loop(0, ch, step=_SC_LANES)
                def _(i):
                    p = pos_v[pl.ds(i, _SC_LANES)]
                    tok = lax.iota(jnp.int32, _SC_LANES) + (c * ch + i)
                    plsc.store_scatter(idx_v, [p], tok, mask=p >= 0)
                    plsc.store_scatter(gate_v, [p], aff_v[pl.ds(i, _SC_LANES)], mask=p >= 0)

            pltpu.sync_copy(idx_v, idx_hbm.at[pl.ds(wid * cap, cap)])
            pltpu.sync_copy(gate_v, gate_hbm.at[pl.ds(wid * cap, cap)])

    idx, gates = invert(pos.reshape(ne * n), afft.reshape(ne * n))
    return idx.reshape(ne, cap), gates.reshape(ne, cap)


_GATHER_WIN = 128


def ec_gather(table, idx_row):
    m = idx_row.shape[1]
    width = table.shape[1]
    steps = m // (_GATHER_WIN * _SC_CORES)
    mesh = plsc.VectorSubcoreMesh(core_axis_name="c", subcore_axis_name="s")

    @functools.partial(
        pl.kernel, mesh=mesh,
        out_type=jax.ShapeDtypeStruct((m, width), table.dtype),
        scratch_types=[])
    def gather(x_hbm, i_hbm, o_hbm):
        def body(i_vmem, o_vmem):
            pltpu.sync_copy(x_hbm.at[i_vmem.at[0]], o_vmem)

        pltpu.emit_pipeline(
            body,
            grid=(_SC_CORES, steps),
            in_specs=[pl.BlockSpec((1, _GATHER_WIN), index_map=lambda a, i: (0, a * steps + i))],
            out_specs=[pl.BlockSpec((_GATHER_WIN, width),
                                    index_map=lambda a, i: (a * steps + i, 0))],
            core_axis_name=("c", "s"),
            dimension_semantics=(pltpu.PARALLEL, pltpu.PARALLEL),
        )(i_hbm, o_hbm)

    return gather(table, idx_row)


def _expert_kernel(xa_ref, xb_ref, gate_ref, wg_ref, wu_ref, wd_ref, o_ref, wg_sc, wu_sc, wd_sc):
    @pl.when(pl.program_id(1) == 0)
    def _():
        wg_sc[...] = wg_ref[...].astype(jnp.bfloat16)
        wu_sc[...] = wu_ref[...].astype(jnp.bfloat16)
        wd_sc[...] = wd_ref[...].astype(jnp.bfloat16)

    @pl.when(pl.program_id(1) == pl.num_programs(1) - 1)
    def _():
        o_ref[...] = jnp.zeros(o_ref.shape, o_ref.dtype)

    @pl.when(pl.program_id(1) < pl.num_programs(1) - 1)
    def _():
        as_f32 = lambda bits: lax.bitcast_convert_type(bits, jnp.float32)
        wa, wb = xa_ref[...], xb_ref[...]
        hi = jnp.uint32(0xFFFF0000)
        xs = jnp.concatenate([as_f32(wa << 16), as_f32(wa & hi), as_f32(wb << 16),
                              as_f32(wb & hi)], axis=1).astype(jnp.bfloat16)
        a = jnp.dot(xs, wg_sc[...], preferred_element_type=jnp.float32)
        u = jnp.dot(xs, wu_sc[...], preferred_element_type=jnp.float32)
        hid = (a * jax.nn.sigmoid(a) * u).astype(jnp.bfloat16)
        y = jnp.dot(hid, wd_sc[...], preferred_element_type=jnp.float32)
        o_ref[...] = (y * gate_ref[...]).astype(o_ref.dtype)


def expert_ffn(xa, xb, gates, wg, wu, wd, layer, tm=512):
    e, c, q = xa.shape
    d = 4 * q
    tm = min(tm, c)
    f = wg.shape[3]
    nt = c // tm
    return pl.pallas_call(
        _expert_kernel,
        grid=(e, nt + 1),
        in_specs=[
            pl.BlockSpec((None, tm, q), lambda ei, i: (ei, jnp.minimum(i, nt - 1), 0)),
            pl.BlockSpec((None, tm, q), lambda ei, i: (ei, jnp.minimum(i, nt - 1), 0)),
            pl.BlockSpec((None, tm, 1), lambda ei, i: (ei, jnp.minimum(i, nt - 1), 0)),
            pl.BlockSpec((None, None, d, f), lambda ei, i: (layer, ei, 0, 0)),
            pl.BlockSpec((None, None, d, f), lambda ei, i: (layer, ei, 0, 0)),
            pl.BlockSpec((None, None, f, d), lambda ei, i: (layer, ei, 0, 0)),
        ],
        out_specs=pl.BlockSpec((None, tm, d), lambda ei, i: (ei, i, 0)),
        out_shape=jax.ShapeDtypeStruct((e, c + tm, d), jnp.bfloat16),
        scratch_shapes=[pltpu.VMEM((d, f), jnp.bfloat16), pltpu.VMEM((d, f), jnp.bfloat16),
                        pltpu.VMEM((f, d), jnp.bfloat16)],
        compiler_params=_cparams(("parallel", "arbitrary")),
        name="expert_ffn",
    )(xa, xb, gates.reshape(e, c, 1), wg, wu, wd)


_SLAB_ALIGN = 16
_SLAB_ROWS = LANES + _SLAB_ALIGN


def _slab_copy(offs_ref, out_hbm, slab_ref, sem_ref, e, j, slot):
    start = pl.multiple_of((offs_ref[e, j] // _SLAB_ALIGN) * _SLAB_ALIGN, _SLAB_ALIGN)
    return pltpu.make_async_copy(out_hbm.at[e, pl.ds(start, _SLAB_ROWS), :],
                                 slab_ref.at[slot, e], sem_ref.at[slot, e])


def _combine_kernel(offs_ref, x_ref, pos_ref, fg_ref, out_hbm, o_ref, slab_ref, sem_ref, *,
                    final_norm):
    f32 = jnp.float32
    ne = pos_ref.shape[0]
    j = pl.program_id(0)
    slot = j % 2

    @pl.when(j == 0)
    def _():
        for e in range(ne):
            _slab_copy(offs_ref, out_hbm, slab_ref, sem_ref, e, j, slot).start()

    @pl.when(j + 1 < pl.num_programs(0))
    def _():
        for e in range(ne):
            _slab_copy(offs_ref, out_hbm, slab_ref, sem_ref, e, j + 1, 1 - slot).start()

    pad = jnp.zeros((LANES - ne, LANES), f32)
    pos_t = jnp.concatenate([pos_ref[...].astype(f32), pad], axis=0).T
    row_id = lax.broadcasted_iota(jnp.int32, (LANES, _SLAB_ROWS), 1).astype(f32)
    for e in range(ne):
        _slab_copy(offs_ref, out_hbm, slab_ref, sem_ref, e, j, slot).wait()
    y = x_ref[...]
    for e in range(ne):
        start = (offs_ref[e, j] // _SLAB_ALIGN) * _SLAB_ALIGN
        rel = pos_t[:, e:e + 1] - start.astype(f32)
        onehot = jnp.where(rel == row_id, 1.0, 0.0).astype(jnp.bfloat16)
        y = y + jnp.dot(onehot, slab_ref[slot, e], preferred_element_type=f32)
    if final_norm:
        y = y * lax.rsqrt(jnp.mean(y * y, axis=-1, keepdims=True) + EPS) * fg_ref[...]
    o_ref[...] = y


def ec_combine(x2d, pos, offs, out, final_g, final_norm):
    n, d = x2d.shape
    ne, crows, _ = out.shape
    assert crows >= n * EC_CAPACITY // ne + _SLAB_ROWS
    grid_spec = pltpu.PrefetchScalarGridSpec(
        num_scalar_prefetch=1,
        grid=(n // LANES,),
        in_specs=[
            pl.BlockSpec((LANES, d), lambda j, offs_ref: (j, 0)),
            pl.BlockSpec((ne, LANES), lambda j, offs_ref: (0, j)),
            pl.BlockSpec((1, d), lambda j, offs_ref: (0, 0)),
            pl.BlockSpec(memory_space=pl.ANY),
        ],
        out_specs=pl.BlockSpec((LANES, d), lambda j, offs_ref: (j, 0)),
        scratch_shapes=[pltpu.VMEM((2, ne, _SLAB_ROWS, d), out.dtype),
                        pltpu.SemaphoreType.DMA((2, ne))],
    )
    return pl.pallas_call(
        functools.partial(_combine_kernel, final_norm=final_norm),
        grid_spec=grid_spec,
        out_shape=jax.ShapeDtypeStruct((n, d), jnp.float32),
        compiler_params=_cparams(("arbitrary",)),
        name="ec_combine",
    )(offs, x2d, pos, final_g.reshape(1, d), out)


def _trunk(x, p):
    b, t, d = x.shape
    n = b * t
    depth = p["w_in"].shape[0]
    x2d = x.reshape(n, d)
    cap = max(1, EC_CAPACITY * n // N_EXPERTS)
    for layer in range(depth):
        proj = norm_proj(x2d, p["norm_mix_g"][layer], p["w_in"][layer])
        proj3d = proj.reshape(b, t, -1)
        att = diff_attention(proj3d, p["lambda_q1"][layer], p["lambda_k1"][layer],
                             p["lambda_q2"][layer], p["lambda_k2"][layer],
                             p["diff_subln_g"][layer], layer)
        hgo = hgrn_bidir(proj3d, p["hgrn_lb"], p["hgrn_norm_g"][layer], layer)
        x2d, ha, hb, afft = out_proj_router(att.reshape(n, -1), hgo.reshape(n, -1),
                                            p["w_out"][layer], x2d, p["norm_ffn_g"][layer],
                                            p["w_router"][layer])
        pos, offs = ec_select(afft, cap)
        ne = afft.shape[0]
        idx, gates = ec_invert(pos, afft, cap)
        idx = idx.reshape(1, ne * cap)
        xa = ec_gather(ha, idx).reshape(ne, cap, -1)
        xb = ec_gather(hb, idx).reshape(ne, cap, -1)
        out = expert_ffn(xa, xb, gates, p["w_gate"], p["w_up"], p["w_down"], layer)
        x2d = ec_combine(x2d, pos, offs, out, p["final_norm_g"], layer == depth - 1)
    return x2d.reshape(b, t, d)


def kernel(x_prompt, x_sample, norm_mix_g, w_in, lambda_q1, lambda_k1, lambda_q2, lambda_k2,
           diff_subln_g, hgrn_lb, hgrn_norm_g, w_out, norm_ffn_g, w_router, w_gate, w_up,
           w_down, final_norm_g):
    bf = jnp.bfloat16
    p = dict(norm_mix_g=norm_mix_g, w_in=w_in.astype(bf), lambda_q1=lambda_q1,
             lambda_k1=lambda_k1, lambda_q2=lambda_q2, lambda_k2=lambda_k2,
             diff_subln_g=diff_subln_g, hgrn_lb=hgrn_lb, hgrn_norm_g=hgrn_norm_g,
             w_out=w_out.astype(bf), norm_ffn_g=norm_ffn_g, w_router=w_router,
             w_gate=w_gate, w_up=w_up, w_down=w_down,
             final_norm_g=final_norm_g)
    return (_trunk(x_prompt, p), _trunk(x_sample, p))
```

```python
import functools
import math

import jax
import jax.numpy as jnp
from jax import lax
from jax.experimental import pallas as pl
from jax.experimental.pallas import tpu as pltpu
from jax.experimental.pallas import tpu_sc as plsc

ATT_HEADS = 4
ATT_QK_DIM = 64
ATT_V_DIM = 128
HG_HEADS = 4
HG_K = 128
HG_V = 128
HG_CHUNK = 64
N_EXPERTS = 16
EC_CAPACITY = 2
EPS = 1e-6
LANES = 128
SUBLANES = 8

_QBLK, _KBLK, _VBLK = 0, 4, 8
_HQ, _HFF, _HFB, _HI, _HG = 12, 16, 20, 24, 28

_VMEM_LIMIT = 56 * 1024 * 1024

_NT = (((1,), (1,)), ((), ()))


def _cparams(sem):
    return pltpu.CompilerParams(dimension_semantics=sem, vmem_limit_bytes=_VMEM_LIMIT)


def _norm_proj_kernel(x_ref, g_ref, w_ref, o_ref, *, n_split):
    x = x_ref[...]
    y = x * lax.rsqrt(jnp.mean(x * x, axis=-1, keepdims=True) + EPS) * g_ref[...]
    h = y.astype(jnp.bfloat16)
    wn = w_ref.shape[1] // n_split
    for c in range(n_split):
        o_ref[:, c * wn:(c + 1) * wn] = jnp.dot(
            h, w_ref[:, c * wn:(c + 1) * wn], preferred_element_type=jnp.float32
        ).astype(o_ref.dtype)


def norm_proj(x2d, g, w_bf16, tm=512):
    n, d = x2d.shape
    tm = min(tm, n)
    dout = w_bf16.shape[1]
    return pl.pallas_call(
        functools.partial(_norm_proj_kernel, n_split=max(1, dout // 1024)),
        grid=(n // tm,),
        in_specs=[
            pl.BlockSpec((tm, d), lambda i: (i, 0)),
            pl.BlockSpec((1, d), lambda i: (0, 0)),
            pl.BlockSpec((d, dout), lambda i: (0, 0)),
        ],
        out_specs=pl.BlockSpec((tm, dout), lambda i: (i, 0)),
        out_shape=jax.ShapeDtypeStruct((n, dout), jnp.bfloat16),
        compiler_params=_cparams(("parallel",)),
        name="norm_proj",
    )(x2d, g.reshape(1, d), w_bf16)


_LOG2E = 1.4426950408889634
_Q_STRIP = 128
_ONES_ROWS = 16


def _split3(x):
    bf = jnp.bfloat16
    h = x.astype(bf).astype(jnp.float32)
    r = x - h
    m = r.astype(bf).astype(jnp.float32)
    lo = (r - m).astype(bf).astype(jnp.float32)
    return h, m, lo


def _attn_kernel(beta_ref, qext_ref, lq1_ref, lk1_ref, lq2_ref, lk2_ref, subg_ref,
                 q_ref, k_ref, v_ref, kext_ref, o_ref, vt_sc, acc_sc, s_sc, *, lam_init, tk, tq):
    f32, bf = jnp.float32, jnp.bfloat16
    t = q_ref.shape[0]
    nk = t // tk
    qs_ = _Q_STRIP
    ng = tq // qs_
    ncol = 2 * tq
    dv = v_ref.shape[1]
    beta = beta_ref[0:1, 0:1]

    def vt_body(c, carry):
        r0 = pl.multiple_of(c * tk, tk)
        vt_sc[0:dv, pl.ds(r0, tk)] = v_ref[pl.ds(r0, tk), :].astype(f32).T.astype(bf)
        return carry

    lax.fori_loop(0, nk, vt_body, 0)
    vt_sc[dv:dv + _ONES_ROWS, :] = jnp.ones((_ONES_ROWS, t), bf)

    lam = (jnp.exp(jnp.sum(lq1_ref[...] * lk1_ref[...], axis=-1, keepdims=True))
           - jnp.exp(jnp.sum(lq2_ref[...] * lk2_ref[...], axis=-1, keepdims=True)) + lam_init)
    lane = lax.broadcasted_iota(jnp.int32, (qs_, LANES), 1)
    q_minus_r = (lax.broadcasted_iota(jnp.int32, (tk, qs_), 1)
                 - lax.broadcasted_iota(jnp.int32, (tk, qs_), 0)).astype(f32)
    cscale = ATT_QK_DIM ** -0.5 * _LOG2E

    def group(gi, carry):
        g0 = pl.multiple_of(gi * tq, tq)
        tiles = []
        for u in range(ng):
            q0 = g0 + u * qs_
            qs = (q_ref[pl.ds(q0, qs_), :].astype(f32) * cscale).astype(bf).astype(f32)
            qext = qext_ref[...]
            q_aug = jnp.concatenate([
                jnp.concatenate([jnp.where(lane < ATT_QK_DIM, qs, 0.0), qext], axis=1),
                jnp.concatenate([jnp.where(lane >= ATT_QK_DIM, qs, 0.0), qext], axis=1)], axis=0)
            tiles.append(q_aug.T.astype(bf))
        w = jnp.concatenate(tiles, axis=1)
        jc = g0 // tk

        def corr_tile(d):
            corr = 2.0 * beta * jnp.minimum(q_minus_r + d.astype(f32), 0.0)
            return jnp.concatenate([corr, corr], axis=1)

        def shift(d, sidx):
            sgn = 1.0 if sidx == 0 else -1.0
            return (-sgn) * beta * d.astype(f32)

        def scores(n, sidx, crossing):
            r0 = pl.multiple_of(n * tk, tk)
            k_aug = jnp.concatenate([k_ref[pl.ds(r0, tk), :], kext_ref[sidx]], axis=1)
            for u in range(ng):
                c0 = u * 2 * qs_
                s = jnp.dot(k_aug, w[:, c0:c0 + 2 * qs_], preferred_element_type=f32)
                if crossing:
                    s = s + corr_tile(g0 + u * qs_ - r0)
                s_sc[:, c0:c0 + 2 * qs_] = s

        def step(n, mc, sidx, crossing):
            r0 = pl.multiple_of((n - 1) * tk, tk)
            r1 = pl.multiple_of(n * tk, tk)
            vt = vt_sc[:, pl.ds(r0, tk)]
            if sidx is not None:
                k_aug = jnp.concatenate([k_ref[pl.ds(r1, tk), :], kext_ref[sidx]], axis=1)
            m, csts = mc
            ms, cs = [], []
            for u in range(ng):
                c0 = u * 2 * qs_
                s = s_sc[:, c0:c0 + 2 * qs_]
                m_old = m[:, c0:c0 + 2 * qs_]
                cst = csts[:, c0:c0 + 2 * qs_]
                m_new = jnp.maximum(m_old, jnp.max(s, axis=0, keepdims=True) + cst)
                alpha = jnp.exp2(m_old - m_new)
                p = jnp.exp2(s - (m_new - cst)).astype(bf)
                acc_sc[:, c0:c0 + 2 * qs_] = (acc_sc[:, c0:c0 + 2 * qs_] * alpha
                                              + jnp.dot(vt, p, preferred_element_type=f32))
                ms.append(m_new)
                if sidx is not None:
                    s = jnp.dot(k_aug, w[:, c0:c0 + 2 * qs_], preferred_element_type=f32)
                    if crossing:
                        s = s + corr_tile(g0 + u * qs_ - r1)
                    s_sc[:, c0:c0 + 2 * qs_] = s
                    cs.append(jnp.broadcast_to(shift(g0 + u * qs_ - r1, sidx), (1, 2 * qs_)))
            return jnp.concatenate(ms, axis=1), (jnp.concatenate(cs, axis=1) if cs else csts)

        acc_sc[...] = jnp.zeros(acc_sc.shape, f32)
        m = jnp.full((1, ncol), -1e30, f32)
        scores(0, 0, True)
        n1 = jnp.maximum(jc, 1)
        n2 = jc + tq // tk
        cst0 = jnp.concatenate([jnp.broadcast_to(shift(g0 + u * qs_, 0), (1, 2 * qs_))
                                for u in range(ng)], axis=1)
        mc = (m, cst0)
        mc = lax.fori_loop(1, n1, functools.partial(step, sidx=0, crossing=False), mc)
        mc = lax.fori_loop(n1, n2, functools.partial(step, sidx=0, crossing=True), mc)
        mc = lax.fori_loop(n2, nk, functools.partial(step, sidx=1, crossing=False), mc)
        step(nk, mc, None, False)

        for u in range(ng):
            c0 = u * 2 * qs_
            acc = acc_sc[:, c0:c0 + 2 * qs_]
            den = acc[dv:dv + 1, :]
            o = acc[0:dv, 0:qs_] / den[:, 0:qs_] - lam * (acc[0:dv, qs_:] / den[:, qs_:])
            y = o * lax.rsqrt(jnp.mean(o * o, axis=0, keepdims=True) + EPS) * subg_ref[...]
            o_ref[pl.ds(g0 + u * qs_, qs_), :] = (y * (1.0 - lam_init)).T.astype(o_ref.dtype)
        return carry

    lax.fori_loop(0, t // tq, group, 0)


def _attn_tables(tk):
    f32 = jnp.float32
    slopes = 2.0 ** (-8.0 * (jnp.arange(ATT_HEADS, dtype=f32) + 1.0) / ATT_HEADS)
    beta = slopes * _LOG2E
    lane = jnp.arange(LANES)[None, None, :]
    b = _split3(beta)
    a = _split3(-beta[:, None] * jnp.arange(_Q_STRIP, dtype=f32)[None, :])
    qext = jnp.zeros((ATT_HEADS, _Q_STRIP, LANES), f32)
    for c in range(3):
        qext = jnp.where(lane == c, b[c][:, None, None], qext)
        qext = jnp.where(lane == 3 + c, a[c][:, :, None], qext)
    r = jnp.arange(tk, dtype=f32)[:, None]
    lane2 = jnp.arange(LANES)[None, :]
    base = jnp.where(lane2 < 3, r, jnp.where(lane2 < 6, 1.0, 0.0))
    kext = jnp.stack([base, -base]).astype(jnp.bfloat16)
    beta_t = jnp.broadcast_to(beta[:, None, None], (ATT_HEADS, 1, LANES))
    return beta_t, qext, kext


def diff_attention(proj3d, lq1, lk1, lq2, lk2, subg, layer, tk=256, tq=2048):
    b, t, _ = proj3d.shape
    tk, tq = min(tk, t), min(tq, t)
    assert tq % tk == 0 and tq % _Q_STRIP == 0
    lam_init = 0.8 - 0.6 * math.exp(-0.3 * layer)
    assert tk <= 256
    beta_t, qext, kext = _attn_tables(tk)
    vec = lambda a: a.reshape(1, -1).astype(jnp.float32)
    small = lambda n: pl.BlockSpec((1, n), lambda bi, h: (0, 0))
    head = pl.BlockSpec((None, 1, LANES), lambda bi, h: (h, 0, 0))
    col = lambda blk: pl.BlockSpec((None, t, LANES), lambda bi, h: (bi, 0, blk + h))
    return pl.pallas_call(
        functools.partial(_attn_kernel, lam_init=lam_init, tk=tk, tq=tq),
        grid=(b, ATT_HEADS),
        in_specs=[
            head, pl.BlockSpec((None, _Q_STRIP, LANES), lambda bi, h: (h, 0, 0)),
            small(ATT_QK_DIM), small(ATT_QK_DIM), small(ATT_QK_DIM), small(ATT_QK_DIM),
            pl.BlockSpec((ATT_V_DIM, 1), lambda bi, h: (0, 0)),
            col(_QBLK), col(_KBLK), col(_VBLK),
            pl.BlockSpec((2, tk, LANES), lambda bi, h: (0, 0, 0)),
        ],
        out_specs=pl.BlockSpec((None, t, LANES), lambda bi, h: (bi, 0, h)),
        out_shape=jax.ShapeDtypeStruct((b, t, ATT_HEADS * ATT_V_DIM), jnp.bfloat16),
        scratch_shapes=[pltpu.VMEM((ATT_V_DIM + _ONES_ROWS, t), jnp.bfloat16),
                        pltpu.VMEM((ATT_V_DIM + _ONES_ROWS, 2 * tq), jnp.float32),
                        pltpu.VMEM((tk, 2 * tq), jnp.float32)],
        compiler_params=_cparams(("parallel", "parallel")),
        name="diff_attn",
    )(beta_t, qext, vec(lq1), vec(lk1), vec(lq2), vec(lk2),
      subg.reshape(ATT_V_DIM, 1).astype(jnp.float32), proj3d, proj3d, proj3d, kext)


_HG_CHUNKS_PER_ITER = 8
_HG_FIN_ROWS = 256


def _scan_rows(g, rev):
    c = g.shape[0]
    row = lax.broadcasted_iota(jnp.int32, g.shape, 0)
    b = g
    s = 1
    while s < c:
        if rev:
            b = b + jnp.where(row < c - s, pltpu.roll(b, c - s, axis=0), 0.0)
        else:
            b = b + jnp.where(row >= s, pltpu.roll(b, s, axis=0), 0.0)
        s *= 2
    return b


def _level_ref(b, m, rev):
    c, w = b.shape
    r = m // 2 if rev else m // 2 - 1
    pieces = []
    if m >= SUBLANES:
        for j in range(c // m):
            pieces.append(jnp.broadcast_to(b[j * m + r:j * m + r + 1, :], (m, w)))
    else:
        sub = lax.broadcasted_iota(jnp.int32, (SUBLANES, w), 0)
        for j in range(c // SUBLANES):
            base = j * SUBLANES
            acc = jnp.broadcast_to(b[base + r:base + r + 1, :], (SUBLANES, w))
            for i in range(1, SUBLANES // m):
                cand = jnp.broadcast_to(b[base + i * m + r:base + i * m + r + 1, :], (SUBLANES, w))
                acc = jnp.where(sub >= i * m, cand, acc)
            pieces.append(acc)
    return jnp.concatenate(pieces, axis=0)


def _neg_abs(x):
    bits = lax.bitcast_convert_type(x, jnp.uint32) | jnp.uint32(0x80000000)
    return lax.bitcast_convert_type(bits, jnp.float32)


def _hgrn_masks(c, rev):
    row = lax.broadcasted_iota(jnp.int32, (c, LANES), 0)
    ti = lax.broadcasted_iota(jnp.int32, (c, c), 0)
    si = lax.broadcasted_iota(jnp.int32, (c, c), 1)
    qrows, pairs = {}, {}
    m = c
    while m >= 2:
        late = (row & (m - 1)) >= m // 2
        t_late = (ti & (m - 1)) >= m // 2
        s_late = (si & (m - 1)) >= m // 2
        same = (ti ^ si) < m
        if rev:
            qrows[m] = jnp.logical_not(late)
            pairs[m] = same & jnp.logical_not(t_late) & s_late
        else:
            qrows[m] = late
            pairs[m] = same & t_late & jnp.logical_not(s_late)
        m //= 2
    return qrows, pairs, ti == si


def _hgrn_chunk(q, f, v, st_ref, rev, masks):
    qrows, pairs, diag = masks
    c = q.shape[0]
    bf = jnp.bfloat16
    k_ = 1.0 - f
    b = _scan_rows(jnp.log(f) * _LOG2E, rev)
    a = jnp.where(diag,
                  lax.dot_general(q.astype(bf), k_.astype(bf), _NT,
                                  preferred_element_type=jnp.float32), 0.0)
    m = c
    while m >= 2:
        if m == 2:
            cref = jnp.where(qrows[m], pltpu.roll(b, c - 1 if rev else 1, axis=0), b)
        else:
            cref = _level_ref(b, m, rev)
        e = jnp.exp2(_neg_abs(b - cref))
        z = (jnp.where(qrows[m], q, k_) * e).astype(bf)
        p = lax.dot_general(z, z, _NT, preferred_element_type=jnp.float32)
        a = jnp.where(pairs[m], p, a)
        m //= 2
    st = st_ref[...]
    o = jnp.dot(a.astype(bf), v.astype(bf), preferred_element_type=jnp.float32)
    o = o + lax.dot_general((q * jnp.exp2(b)).astype(bf), st.astype(bf), _NT,
                            preferred_element_type=jnp.float32)
    btot = b[0:1, :] if rev else b[c - 1:c, :]
    khat = (k_ * jnp.exp2(btot - b)).astype(bf)
    st_ref[...] = st * jnp.exp2(btot) + jnp.dot(v.T.astype(bf), khat,
                                                preferred_element_type=jnp.float32)
    return o


def _hgrn_kernel(lb_ref, ng_ref, q_ref, ff_ref, fb_ref, i_ref, g_ref, o_ref,
                 of_sc, ob_sc, stf_sc, stb_sc, *, layer, chunk):
    t = q_ref.shape[0]
    n = t // chunk
    lb_raw = lb_ref[...].astype(jnp.float32)
    ex = jnp.exp(lb_raw - jnp.max(lb_raw, axis=0, keepdims=True))
    sm = ex / jnp.sum(ex, axis=0, keepdims=True)
    lb = jnp.zeros_like(sm[0])
    for l_ in range(1, layer + 1):
        lb = lb + sm[l_]
    lb_f = lb[0:1, :]
    lb_b = lb[1:2, :]

    stf_sc[...] = jnp.zeros(stf_sc.shape, jnp.float32)
    stb_sc[...] = jnp.zeros(stb_sc.shape, jnp.float32)

    def load(ref, r0):
        return ref[pl.ds(r0, chunk), :].astype(jnp.float32)

    masks_f = _hgrn_masks(chunk, False)
    masks_b = _hgrn_masks(chunk, True)

    per_iter = _HG_CHUNKS_PER_ITER if n % _HG_CHUNKS_PER_ITER == 0 else 1

    def body(ci, carry):
        for u in range(per_iter):
            r0 = pl.multiple_of((ci * per_iter + u) * chunk, chunk)
            qr = load(q_ref, r0)
            f = lb_f + (1.0 - lb_f) * jax.nn.sigmoid(load(ff_ref, r0))
            of_sc[pl.ds(r0, chunk), :] = _hgrn_chunk(qr * jax.nn.sigmoid(qr), f,
                                                     load(i_ref, r0), stf_sc, False, masks_f)
            r1 = pl.multiple_of((n - 1 - ci * per_iter - u) * chunk, chunk)
            qr = load(q_ref, r1)
            f = lb_b + (1.0 - lb_b) * jax.nn.sigmoid(load(fb_ref, r1))
            ob_sc[pl.ds(r1, chunk), :] = _hgrn_chunk(qr * jax.nn.sigmoid(qr), f,
                                                     load(i_ref, r1), stb_sc, True, masks_b)
        return carry

    lax.fori_loop(0, n // per_iter, body, 0)

    rows = math.gcd(t, _HG_FIN_ROWS)

    def fin(ci, carry):
        r0 = pl.multiple_of(ci * rows, rows)
        o = of_sc[pl.ds(r0, rows), :] + ob_sc[pl.ds(r0, rows), :]
        y = o * lax.rsqrt(jnp.mean(o * o, axis=-1, keepdims=True) + EPS) * ng_ref[...]
        gr = g_ref[pl.ds(r0, rows), :].astype(jnp.float32)
        o_ref[pl.ds(r0, rows), :] = (y * (gr * jax.nn.sigmoid(gr))).astype(o_ref.dtype)
        return carry

    lax.fori_loop(0, t // rows, fin, 0)


def hgrn_bidir(proj3d, hgrn_lb, norm_g, layer):
    b, t, _ = proj3d.shape
    depth = hgrn_lb.shape[0]
    col = lambda blk: pl.BlockSpec((None, t, LANES), lambda bi, h: (bi, 0, blk + h))
    return pl.pallas_call(
        functools.partial(_hgrn_kernel, layer=layer, chunk=HG_CHUNK),
        grid=(b, HG_HEADS),
        in_specs=[
            pl.BlockSpec((depth, 2, LANES), lambda bi, h: (0, 0, h)),
            pl.BlockSpec((1, HG_V), lambda bi, h: (0, 0)),
            col(_HQ), col(_HFF), col(_HFB), col(_HI), col(_HG),
        ],
        out_specs=pl.BlockSpec((None, t, LANES), lambda bi, h: (bi, 0, h)),
        out_shape=jax.ShapeDtypeStruct((b, t, HG_HEADS * HG_V), jnp.bfloat16),
        scratch_shapes=[
            pltpu.VMEM((t, HG_V), jnp.float32),
            pltpu.VMEM((t, HG_V), jnp.float32),
            pltpu.VMEM((HG_V, HG_K), jnp.float32),
            pltpu.VMEM((HG_V, HG_K), jnp.float32),
        ],
        compiler_params=_cparams(("parallel", "parallel")),
        name="hgrn_bidir",
    )(hgrn_lb, norm_g.reshape(1, HG_V).astype(jnp.float32),
      proj3d, proj3d, proj3d, proj3d, proj3d)


def _out_router_kernel(a_ref, h_ref, wa_ref, wh_ref, x_ref, g_ref, wrt_ref,
                       o_ref, ha_ref, hb_ref, afft_ref):
    y = jnp.dot(a_ref[...], wa_ref[...], preferred_element_type=jnp.float32)
    y = y + jnp.dot(h_ref[...], wh_ref[...], preferred_element_type=jnp.float32)
    x = x_ref[...] + y
    o_ref[...] = x
    h = x * lax.rsqrt(jnp.mean(x * x, axis=-1, keepdims=True) + EPS) * g_ref[...]
    bits = lax.bitcast_convert_type(h.astype(jnp.bfloat16).astype(jnp.float32), jnp.uint32)
    q = ha_ref.shape[1]
    ha_ref[...] = (bits[:, 0:q] >> 16) | bits[:, q:2 * q]
    hb_ref[...] = (bits[:, 2 * q:3 * q] >> 16) | bits[:, 3 * q:4 * q]
    logits = lax.dot_general(wrt_ref[...], h, _NT, preferred_element_type=jnp.float32,
                             precision=lax.Precision.HIGHEST)
    ex = jnp.exp(logits - jnp.max(logits, axis=0, keepdims=True))
    afft_ref[...] = ex / jnp.sum(ex, axis=0, keepdims=True)


def out_proj_router(att2d, hgo2d, w_out_bf16, x2d, g, w_router, tm=512):
    n, d = x2d.shape
    tm = min(tm, n)
    wa = att2d.shape[1]
    wh = hgo2d.shape[1]
    e = w_router.shape[1]
    rows = lambda w: pl.BlockSpec((tm, w), lambda i: (i, 0))
    return pl.pallas_call(
        _out_router_kernel,
        grid=(n // tm,),
        in_specs=[
            rows(wa), rows(wh),
            pl.BlockSpec((wa, d), lambda i: (0, 0)),
            pl.BlockSpec((wh, d), lambda i: (wa // wh, 0)),
            rows(d),
            pl.BlockSpec((1, d), lambda i: (0, 0)),
            pl.BlockSpec((e, d), lambda i: (0, 0)),
        ],
        out_specs=[rows(d), rows(d // 4), rows(d // 4), pl.BlockSpec((e, tm), lambda i: (0, i))],
        out_shape=[
            jax.ShapeDtypeStruct((n, d), jnp.float32),
            jax.ShapeDtypeStruct((n, d // 4), jnp.uint32),
            jax.ShapeDtypeStruct((n, d // 4), jnp.uint32),
            jax.ShapeDtypeStruct((e, n), jnp.float32),
        ],
        compiler_params=_cparams(("parallel",)),
        name="out_proj_router",
    )(att2d, hgo2d, w_out_bf16, w_out_bf16, x2d, g.reshape(1, d), w_router.T)


_SEL_CHUNK = 1024


def _select_kernel(afft_ref, pos_ref, offs_ref, *, cap):
    f32, i32 = jnp.float32, jnp.int32
    e, n = afft_ref.shape
    chunk = min(_SEL_CHUNK, n)
    nt = n // LANES

    def bits_at(start, size):
        return lax.bitcast_convert_type(afft_ref[:, pl.ds(start, size)], i32)

    def count_ge(cand):
        def body(c, acc):
            x = bits_at(pl.multiple_of(c * chunk, chunk), chunk)
            return acc + jnp.where(x >= cand, 1.0, 0.0)
        acc = lax.fori_loop(0, n // chunk, body, jnp.zeros((e, chunk), f32))
        return jnp.sum(acc, axis=1, keepdims=True)

    def bit_body(it, prefix):
        cand = prefix | jnp.left_shift(jnp.int32(1), 30 - it)
        return jnp.where(count_ge(cand) >= cap, cand, prefix)

    thr = lax.fori_loop(0, 31, bit_body, jnp.zeros((e, 1), i32))
    need = cap - count_ge(thr + 1)

    tri = (lax.broadcasted_iota(i32, (LANES, LANES), 0)
           <= lax.broadcasted_iota(i32, (LANES, LANES), 1)).astype(jnp.bfloat16)
    tile_id = lax.broadcasted_iota(i32, (e, nt), 1)

    def tile_body(j, carry):
        c_gt, c_eq, offs_acc = carry
        x = bits_at(pl.multiple_of(j * LANES, LANES), LANES)
        gt = x > thr
        eq = x == thr
        both = jnp.concatenate([jnp.where(gt, 1.0, 0.0), jnp.where(eq, 1.0, 0.0)],
                               axis=0).astype(jnp.bfloat16)
        inc = jnp.dot(both, tri, preferred_element_type=f32)
        inc_gt = inc[:e] + c_gt
        inc_eq = inc[e:] + c_eq
        sel = gt | (eq & (inc_eq <= need))
        incl = inc_gt + jnp.minimum(inc_eq, need)
        pos_ref[:, pl.ds(pl.multiple_of(j * LANES, LANES), LANES)] = jnp.where(
            sel, incl - 1.0, -1.0).astype(i32)
        offs_acc = jnp.where(tile_id == j, c_gt + jnp.minimum(c_eq, need), offs_acc)
        return inc_gt[:, LANES - 1:LANES], inc_eq[:, LANES - 1:LANES], offs_acc

    zero = jnp.zeros((e, 1), f32)
    _, _, offs_acc = lax.fori_loop(0, nt, tile_body, (zero, zero, jnp.zeros((e, nt), f32)))
    offs_ref[...] = offs_acc.astype(i32)


def ec_select(afft, cap):
    e, n = afft.shape
    return pl.pallas_call(
        functools.partial(_select_kernel, cap=cap),
        out_shape=[jax.ShapeDtypeStruct((e, n), jnp.int32),
                   jax.ShapeDtypeStruct((e, n // LANES), jnp.int32)],
        compiler_params=pltpu.CompilerParams(vmem_limit_bytes=_VMEM_LIMIT),
        name="ec_select",
    )(afft)


_SC_LANES = 16
_SC_CORES = 2
_INV_CHUNK = 8192


def ec_invert(pos, afft, cap):
    ne, n = pos.shape
    ch = min(_INV_CHUNK, n)
    mesh = plsc.VectorSubcoreMesh(core_axis_name="c", subcore_axis_name="s")

    @functools.partial(
        pl.kernel, mesh=mesh,
        out_type=[jax.ShapeDtypeStruct((ne * cap,), jnp.int32),
                  jax.ShapeDtypeStruct((ne * cap,), jnp.float32)],
        scratch_types=[pltpu.VMEM((ch,), jnp.int32), pltpu.VMEM((ch,), jnp.float32),
                       pltpu.VMEM((cap,), jnp.int32), pltpu.VMEM((cap,), jnp.float32)],
        compiler_params=pltpu.CompilerParams(needs_layout_passes=False))
    def invert(pos_hbm, aff_hbm, idx_hbm, gate_hbm, pos_v, aff_v, idx_v, gate_v):
        wid = lax.axis_index("s") * _SC_CORES + lax.axis_index("c")

        @pl.when(wid < ne)
        def _():
            @pl.loop(0, n // ch)
            def _(c):
                pltpu.sync_copy(pos_hbm.at[pl.ds(wid * n + c * ch, ch)], pos_v)
                pltpu.sync_copy(aff_hbm.at[pl.ds(wid * n + c * ch, ch)], aff_v)

                @pl.loop(0, ch, step=_SC_LANES)
                def _(i):
                    p = pos_v[pl.ds(i, _SC_LANES)]
                    tok = lax.iota(jnp.int32, _SC_LANES) + (c * ch + i)
                    plsc.store_scatter(idx_v, [p], tok, mask=p >= 0)
                    plsc.store_scatter(gate_v, [p], aff_v[pl.ds(i, _SC_LANES)], mask=p >= 0)

            pltpu.sync_copy(idx_v, idx_hbm.at[pl.ds(wid * cap, cap)])
            pltpu.sync_copy(gate_v, gate_hbm.at[pl.ds(wid * cap, cap)])

    idx, gates = invert(pos.reshape(ne * n), afft.reshape(ne * n))
    return idx.reshape(ne, cap), gates.reshape(ne, cap)


_GATHER_WIN = 128


def ec_gather(table, idx_row):
    m = idx_row.shape[1]
    width = table.shape[1]
    steps = m // (_GATHER_WIN * _SC_CORES)
    mesh = plsc.VectorSubcoreMesh(core_axis_name="c", subcore_axis_name="s")

    @functools.partial(
        pl.kernel, mesh=mesh,
        out_type=jax.ShapeDtypeStruct((m, width), table.dtype),
        scratch_types=[])
    def gather(x_hbm, i_hbm, o_hbm):
        def body(i_vmem, o_vmem):
            pltpu.sync_copy(x_hbm.at[i_vmem.at[0]], o_vmem)

        pltpu.emit_pipeline(
            body,
            grid=(_SC_CORES, steps),
            in_specs=[pl.BlockSpec((1, _GATHER_WIN), index_map=lambda a, i: (0, a * steps + i))],
            out_specs=[pl.BlockSpec((_GATHER_WIN, width),
                                    index_map=lambda a, i: (a * steps + i, 0))],
            core_axis_name=("c", "s"),
            dimension_semantics=(pltpu.PARALLEL, pltpu.PARALLEL),
        )(i_hbm, o_hbm)

    return gather(table, idx_row)


def _expert_kernel(xa_ref, xb_ref, gate_ref, wg_ref, wu_ref, wd_ref, o_ref, wg_sc, wu_sc, wd_sc):
    @pl.when(pl.program_id(1) == 0)
    def _():
        wg_sc[...] = wg_ref[...].astype(jnp.bfloat16)
        wu_sc[...] = wu_ref[...].astype(jnp.bfloat16)
        wd_sc[...] = wd_ref[...].astype(jnp.bfloat16)

    @pl.when(pl.program_id(1) == pl.num_programs(1) - 1)
    def _():
        o_ref[...] = jnp.zeros(o_ref.shape, o_ref.dtype)

    @pl.when(pl.program_id(1) < pl.num_programs(1) - 1)
    def _():
        as_f32 = lambda bits: lax.bitcast_convert_type(bits, jnp.float32)
        wa, wb = xa_ref[...], xb_ref[...]
        hi = jnp.uint32(0xFFFF0000)
        xs = jnp.concatenate([as_f32(wa << 16), as_f32(wa & hi), as_f32(wb << 16),
                              as_f32(wb & hi)], axis=1).astype(jnp.bfloat16)
        a = jnp.dot(xs, wg_sc[...], preferred_element_type=jnp.float32)
        u = jnp.dot(xs, wu_sc[...], preferred_element_type=jnp.float32)
        hid = (a * jax.nn.sigmoid(a) * u).astype(jnp.bfloat16)
        y = jnp.dot(hid, wd_sc[...], preferred_element_type=jnp.float32)
        o_ref[...] = (y * gate_ref[...]).astype(o_ref.dtype)


def expert_ffn(xa, xb, gates, wg, wu, wd, layer, tm=512):
    e, c, q = xa.shape
    d = 4 * q
    tm = min(tm, c)
    f = wg.shape[3]
    nt = c // tm
    return pl.pallas_call(
        _expert_kernel,
        grid=(e, nt + 1),
        in_specs=[
            pl.BlockSpec((None, tm, q), lambda ei, i: (ei, jnp.minimum(i, nt - 1), 0)),
            pl.BlockSpec((None, tm, q), lambda ei, i: (ei, jnp.minimum(i, nt - 1), 0)),
            pl.BlockSpec((None, tm, 1), lambda ei, i: (ei, jnp.minimum(i, nt - 1), 0)),
            pl.BlockSpec((None, None, d, f), lambda ei, i: (layer, ei, 0, 0)),
            pl.BlockSpec((None, None, d, f), lambda ei, i: (layer, ei, 0, 0)),
            pl.BlockSpec((None, None, f, d), lambda ei, i: (layer, ei, 0, 0)),
        ],
        out_specs=pl.BlockSpec((None, tm, d), lambda ei, i: (ei, i, 0)),
        out_shape=jax.ShapeDtypeStruct((e, c + tm, d), jnp.bfloat16),
        scratch_shapes=[pltpu.VMEM((d, f), jnp.bfloat16), pltpu.VMEM((d, f), jnp.bfloat16),
                        pltpu.VMEM((f, d), jnp.bfloat16)],
        compiler_params=_cparams(("parallel", "arbitrary")),
        name="expert_ffn",
    )(xa, xb, gates.reshape(e, c, 1), wg, wu, wd)


_SLAB_ALIGN = 16
_SLAB_ROWS = LANES + _SLAB_ALIGN


def _slab_copy(offs_ref, out_hbm, slab_ref, sem_ref, e, j, slot):
    start = pl.multiple_of((offs_ref[e, j] // _SLAB_ALIGN) * _SLAB_ALIGN, _SLAB_ALIGN)
    return pltpu.make_async_copy(out_hbm.at[e, pl.ds(start, _SLAB_ROWS), :],
                                 slab_ref.at[slot, e], sem_ref.at[slot])


def _combine_kernel(offs_ref, x_ref, pos_ref, fg_ref, out_hbm, o_ref, slab_ref, sem_ref, *,
                    final_norm):
    f32 = jnp.float32
    ne = pos_ref.shape[0]
    j = pl.program_id(0)
    slot = j % 2

    @pl.when(j == 0)
    def _():
        for e in range(ne):
            _slab_copy(offs_ref, out_hbm, slab_ref, sem_ref, e, j, slot).start()

    @pl.when(j + 1 < pl.num_programs(0))
    def _():
        for e in range(ne):
            _slab_copy(offs_ref, out_hbm, slab_ref, sem_ref, e, j + 1, 1 - slot).start()

    pad = jnp.zeros((LANES - ne, LANES), f32)
    pos_t = jnp.concatenate([pos_ref[...].astype(f32), pad], axis=0).T
    row_id = lax.broadcasted_iota(jnp.int32, (LANES, _SLAB_ROWS), 1).astype(f32)
    pltpu.make_async_copy(out_hbm.at[pl.ds(0, ne), pl.ds(0, _SLAB_ROWS), :], slab_ref.at[slot],
                          sem_ref.at[slot]).wait()
    y = x_ref[...]
    for e in range(ne):
        start = (offs_ref[e, j] // _SLAB_ALIGN) * _SLAB_ALIGN
        rel = pos_t[:, e:e + 1] - start.astype(f32)
        onehot = jnp.where(rel == row_id, 1.0, 0.0).astype(jnp.bfloat16)
        y = y + jnp.dot(onehot, slab_ref[slot, e], preferred_element_type=f32)
    if final_norm:
        y = y * lax.rsqrt(jnp.mean(y * y, axis=-1, keepdims=True) + EPS) * fg_ref[...]
    o_ref[...] = y


def ec_combine(x2d, pos, offs, out, final_g, final_norm):
    n, d = x2d.shape
    ne, crows, _ = out.shape
    assert crows >= n * EC_CAPACITY // ne + _SLAB_ROWS
    grid_spec = pltpu.PrefetchScalarGridSpec(
        num_scalar_prefetch=1,
        grid=(n // LANES,),
        in_specs=[
            pl.BlockSpec((LANES, d), lambda j, offs_ref: (j, 0)),
            pl.BlockSpec((ne, LANES), lambda j, offs_ref: (0, j)),
            pl.BlockSpec((1, d), lambda j, offs_ref: (0, 0)),
            pl.BlockSpec(memory_space=pl.ANY),
        ],
        out_specs=pl.BlockSpec((LANES, d), lambda j, offs_ref: (j, 0)),
        scratch_shapes=[pltpu.VMEM((2, ne, _SLAB_ROWS, d), out.dtype),
                        pltpu.SemaphoreType.DMA((2,))],
    )
    return pl.pallas_call(
        functools.partial(_combine_kernel, final_norm=final_norm),
        grid_spec=grid_spec,
        out_shape=jax.ShapeDtypeStruct((n, d), jnp.float32),
        compiler_params=_cparams(("arbitrary",)),
        name="ec_combine",
    )(offs, x2d, pos, final_g.reshape(1, d), out)


def _trunk(x, p):
    b, t, d = x.shape
    n = b * t
    depth = p["w_in"].shape[0]
    x2d = x.reshape(n, d)
    cap = max(1, EC_CAPACITY * n // N_EXPERTS)
    for layer in range(depth):
        proj = norm_proj(x2d, p["norm_mix_g"][layer], p["w_in"][layer])
        proj3d = proj.reshape(b, t, -1)
        att = diff_attention(proj3d, p["lambda_q1"][layer], p["lambda_k1"][layer],
                             p["lambda_q2"][layer], p["lambda_k2"][layer],
                             p["diff_subln_g"][layer], layer)
        hgo = hgrn_bidir(proj3d, p["hgrn_lb"], p["hgrn_norm_g"][layer], layer)
        x2d, ha, hb, afft = out_proj_router(att.reshape(n, -1), hgo.reshape(n, -1),
                                            p["w_out"][layer], x2d, p["norm_ffn_g"][layer],
                                            p["w_router"][layer])
        pos, offs = ec_select(afft, cap)
        ne = afft.shape[0]
        idx, gates = ec_invert(pos, afft, cap)
        idx = idx.reshape(1, ne * cap)
        xa = ec_gather(ha, idx).reshape(ne, cap, -1)
        xb = ec_gather(hb, idx).reshape(ne, cap, -1)
        out = expert_ffn(xa, xb, gates, p["w_gate"], p["w_up"], p["w_down"], layer)
        x2d = ec_combine(x2d, pos, offs, out, p["final_norm_g"], layer == depth - 1)
    return x2d.reshape(b, t, d)


def kernel(x_prompt, x_sample, norm_mix_g, w_in, lambda_q1, lambda_k1, lambda_q2, lambda_k2,
           diff_subln_g, hgrn_lb, hgrn_norm_g, w_out, norm_ffn_g, w_router, w_gate, w_up,
           w_down, final_norm_g):
    bf = jnp.bfloat16
    p = dict(norm_mix_g=norm_mix_g, w_in=w_in.astype(bf), lambda_q1=lambda_q1,
             lambda_k1=lambda_k1, lambda_q2=lambda_q2, lambda_k2=lambda_k2,
             diff_subln_g=diff_subln_g, hgrn_lb=hgrn_lb, hgrn_norm_g=hgrn_norm_g,
             w_out=w_out.astype(bf), norm_ffn_g=norm_ffn_g, w_router=w_router,
             w_gate=w_gate, w_up=w_up, w_down=w_down,
             final_norm_g=final_norm_g)
    return (_trunk(x_prompt, p), _trunk(x_sample, p))
```

```python
import functools
import math

import jax
import jax.numpy as jnp
from jax import lax
from jax.experimental import pallas as pl
from jax.experimental.pallas import tpu as pltpu
from jax.experimental.pallas import tpu_sc as plsc

ATT_HEADS = 4
ATT_QK_DIM = 64
ATT_V_DIM = 128
HG_HEADS = 4
HG_K = 128
HG_V = 128
HG_CHUNK = 64
N_EXPERTS = 16
EC_CAPACITY = 2
EPS = 1e-6
LANES = 128
SUBLANES = 8

_QBLK, _KBLK, _VBLK = 0, 4, 8
_HQ, _HFF, _HFB, _HI, _HG = 12, 16, 20, 24, 28

_VMEM_LIMIT = 56 * 1024 * 1024

_NT = (((1,), (1,)), ((), ()))


def _cparams(sem):
    return pltpu.CompilerParams(dimension_semantics=sem, vmem_limit_bytes=_VMEM_LIMIT)


def _norm_proj_kernel(x_ref, g_ref, w_ref, o_ref, *, n_split):
    x = x_ref[...]
    y = x * lax.rsqrt(jnp.mean(x * x, axis=-1, keepdims=True) + EPS) * g_ref[...]
    h = y.astype(jnp.bfloat16)
    wn = w_ref.shape[1] // n_split
    for c in range(n_split):
        o_ref[:, c * wn:(c + 1) * wn] = jnp.dot(
            h, w_ref[:, c * wn:(c + 1) * wn], preferred_element_type=jnp.float32
        ).astype(o_ref.dtype)


def norm_proj(x2d, g, w_bf16, tm=512):
    n, d = x2d.shape
    tm = min(tm, n)
    dout = w_bf16.shape[1]
    return pl.pallas_call(
        functools.partial(_norm_proj_kernel, n_split=max(1, dout // 1024)),
        grid=(n // tm,),
        in_specs=[
            pl.BlockSpec((tm, d), lambda i: (i, 0)),
            pl.BlockSpec((1, d), lambda i: (0, 0)),
            pl.BlockSpec((d, dout), lambda i: (0, 0)),
        ],
        out_specs=pl.BlockSpec((tm, dout), lambda i: (i, 0)),
        out_shape=jax.ShapeDtypeStruct((n, dout), jnp.bfloat16),
        compiler_params=_cparams(("parallel",)),
        name="norm_proj",
    )(x2d, g.reshape(1, d), w_bf16)


_LOG2E = 1.4426950408889634
_Q_STRIP = 128
_ONES_ROWS = 16


def _split3(x):
    bf = jnp.bfloat16
    h = x.astype(bf).astype(jnp.float32)
    r = x - h
    m = r.astype(bf).astype(jnp.float32)
    lo = (r - m).astype(bf).astype(jnp.float32)
    return h, m, lo


def _attn_kernel(beta_ref, qext_ref, lq1_ref, lk1_ref, lq2_ref, lk2_ref, subg_ref,
                 q_ref, k_ref, v_ref, kext_ref, o_ref, vt_sc, acc_sc, s_sc, *, lam_init, tk, tq):
    f32, bf = jnp.float32, jnp.bfloat16
    t = q_ref.shape[0]
    nk = t // tk
    qs_ = _Q_STRIP
    ng = tq // qs_
    ncol = 2 * tq
    dv = v_ref.shape[1]
    beta = beta_ref[0:1, 0:1]

    def vt_body(c, carry):
        r0 = pl.multiple_of(c * tk, tk)
        vt_sc[0:dv, pl.ds(r0, tk)] = v_ref[pl.ds(r0, tk), :].astype(f32).T.astype(bf)
        return carry

    lax.fori_loop(0, nk, vt_body, 0)
    vt_sc[dv:dv + _ONES_ROWS, :] = jnp.ones((_ONES_ROWS, t), bf)

    lam = (jnp.exp(jnp.sum(lq1_ref[...] * lk1_ref[...], axis=-1, keepdims=True))
           - jnp.exp(jnp.sum(lq2_ref[...] * lk2_ref[...], axis=-1, keepdims=True)) + lam_init)
    lane = lax.broadcasted_iota(jnp.int32, (qs_, LANES), 1)
    q_minus_r = (lax.broadcasted_iota(jnp.int32, (tk, qs_), 1)
                 - lax.broadcasted_iota(jnp.int32, (tk, qs_), 0)).astype(f32)
    cscale = ATT_QK_DIM ** -0.5 * _LOG2E

    def group(gi, carry):
        g0 = pl.multiple_of(gi * tq, tq)
        tiles = []
        for u in range(ng):
            q0 = g0 + u * qs_
            qs = (q_ref[pl.ds(q0, qs_), :].astype(f32) * cscale).astype(bf).astype(f32)
            qext = qext_ref[...]
            q_aug = jnp.concatenate([
                jnp.concatenate([jnp.where(lane < ATT_QK_DIM, qs, 0.0), qext], axis=1),
                jnp.concatenate([jnp.where(lane >= ATT_QK_DIM, qs, 0.0), qext], axis=1)], axis=0)
            tiles.append(q_aug.T.astype(bf))
        w = jnp.concatenate(tiles, axis=1)
        jc = g0 // tk

        def corr_tile(d):
            corr = 2.0 * beta * jnp.minimum(q_minus_r + d.astype(f32), 0.0)
            return jnp.concatenate([corr, corr], axis=1)

        def shift(d, sidx):
            sgn = 1.0 if sidx == 0 else -1.0
            return (-sgn) * beta * d.astype(f32)

        def scores(n, sidx, crossing):
            r0 = pl.multiple_of(n * tk, tk)
            k_aug = jnp.concatenate([k_ref[pl.ds(r0, tk), :], kext_ref[sidx]], axis=1)
            for u in range(ng):
                c0 = u * 2 * qs_
                s = jnp.dot(k_aug, w[:, c0:c0 + 2 * qs_], preferred_element_type=f32)
                if crossing:
                    s = s + corr_tile(g0 + u * qs_ - r0)
                s_sc[:, c0:c0 + 2 * qs_] = s

        def step(n, mc, sidx, crossing):
            r0 = pl.multiple_of((n - 1) * tk, tk)
            r1 = pl.multiple_of(n * tk, tk)
            vt = vt_sc[:, pl.ds(r0, tk)]
            if sidx is not None:
                k_aug = jnp.concatenate([k_ref[pl.ds(r1, tk), :], kext_ref[sidx]], axis=1)
            m, csts = mc
            ms, cs = [], []
            for u in range(ng):
                c0 = u * 2 * qs_
                s = s_sc[:, c0:c0 + 2 * qs_]
                m_old = m[:, c0:c0 + 2 * qs_]
                cst = csts[:, c0:c0 + 2 * qs_]
                m_new = jnp.maximum(m_old, jnp.max(s, axis=0, keepdims=True) + cst)
                alpha = jnp.exp2(m_old - m_new)
                p = jnp.exp2(s - (m_new - cst)).astype(bf)
                acc_sc[:, c0:c0 + 2 * qs_] = (acc_sc[:, c0:c0 + 2 * qs_] * alpha
                                              + jnp.dot(vt, p, preferred_element_type=f32))
                ms.append(m_new)
                if sidx is not None:
                    s = jnp.dot(k_aug, w[:, c0:c0 + 2 * qs_], preferred_element_type=f32)
                    if crossing:
                        s = s + corr_tile(g0 + u * qs_ - r1)
                    s_sc[:, c0:c0 + 2 * qs_] = s
                    cs.append(jnp.broadcast_to(shift(g0 + u * qs_ - r1, sidx), (1, 2 * qs_)))
            return jnp.concatenate(ms, axis=1), (jnp.concatenate(cs, axis=1) if cs else csts)

        acc_sc[...] = jnp.zeros(acc_sc.shape, f32)
        m = jnp.full((1, ncol), -1e30, f32)
        scores(0, 0, True)
        n1 = jnp.maximum(jc, 1)
        n2 = jc + tq // tk
        cst0 = jnp.concatenate([jnp.broadcast_to(shift(g0 + u * qs_, 0), (1, 2 * qs_))
                                for u in range(ng)], axis=1)
        mc = (m, cst0)
        mc = lax.fori_loop(1, n1, functools.partial(step, sidx=0, crossing=False), mc)
        mc = lax.fori_loop(n1, n2, functools.partial(step, sidx=0, crossing=True), mc)
        mc = lax.fori_loop(n2, nk, functools.partial(step, sidx=1, crossing=False), mc)
        step(nk, mc, None, False)

        for u in range(ng):
            c0 = u * 2 * qs_
            acc = acc_sc[:, c0:c0 + 2 * qs_]
            den = acc[dv:dv + 1, :]
            o = acc[0:dv, 0:qs_] / den[:, 0:qs_] - lam * (acc[0:dv, qs_:] / den[:, qs_:])
            y = o * lax.rsqrt(jnp.mean(o * o, axis=0, keepdims=True) + EPS) * subg_ref[...]
            o_ref[pl.ds(g0 + u * qs_, qs_), :] = (y * (1.0 - lam_init)).T.astype(o_ref.dtype)
        return carry

    lax.fori_loop(0, t // tq, group, 0)


def _attn_tables(tk):
    f32 = jnp.float32
    slopes = 2.0 ** (-8.0 * (jnp.arange(ATT_HEADS, dtype=f32) + 1.0) / ATT_HEADS)
    beta = slopes * _LOG2E
    lane = jnp.arange(LANES)[None, None, :]
    b = _split3(beta)
    a = _split3(-beta[:, None] * jnp.arange(_Q_STRIP, dtype=f32)[None, :])
    qext = jnp.zeros((ATT_HEADS, _Q_STRIP, LANES), f32)
    for c in range(3):
        qext = jnp.where(lane == c, b[c][:, None, None], qext)
        qext = jnp.where(lane == 3 + c, a[c][:, :, None], qext)
    r = jnp.arange(tk, dtype=f32)[:, None]
    lane2 = jnp.arange(LANES)[None, :]
    base = jnp.where(lane2 < 3, r, jnp.where(lane2 < 6, 1.0, 0.0))
    kext = jnp.stack([base, -base]).astype(jnp.bfloat16)
    beta_t = jnp.broadcast_to(beta[:, None, None], (ATT_HEADS, 1, LANES))
    return beta_t, qext, kext


def diff_attention(proj3d, lq1, lk1, lq2, lk2, subg, layer, tk=256, tq=2048):
    b, t, _ = proj3d.shape
    tk, tq = min(tk, t), min(tq, t)
    assert tq % tk == 0 and tq % _Q_STRIP == 0
    lam_init = 0.8 - 0.6 * math.exp(-0.3 * layer)
    assert tk <= 256
    beta_t, qext, kext = _attn_tables(tk)
    vec = lambda a: a.reshape(1, -1).astype(jnp.float32)
    small = lambda n: pl.BlockSpec((1, n), lambda bi, h: (0, 0))
    head = pl.BlockSpec((None, 1, LANES), lambda bi, h: (h, 0, 0))
    col = lambda blk: pl.BlockSpec((None, t, LANES), lambda bi, h: (bi, 0, blk + h))
    return pl.pallas_call(
        functools.partial(_attn_kernel, lam_init=lam_init, tk=tk, tq=tq),
        grid=(b, ATT_HEADS),
        in_specs=[
            head, pl.BlockSpec((None, _Q_STRIP, LANES), lambda bi, h: (h, 0, 0)),
            small(ATT_QK_DIM), small(ATT_QK_DIM), small(ATT_QK_DIM), small(ATT_QK_DIM),
            pl.BlockSpec((ATT_V_DIM, 1), lambda bi, h: (0, 0)),
            col(_QBLK), col(_KBLK), col(_VBLK),
            pl.BlockSpec((2, tk, LANES), lambda bi, h: (0, 0, 0)),
        ],
        out_specs=pl.BlockSpec((None, t, LANES), lambda bi, h: (bi, 0, h)),
        out_shape=jax.ShapeDtypeStruct((b, t, ATT_HEADS * ATT_V_DIM), jnp.bfloat16),
        scratch_shapes=[pltpu.VMEM((ATT_V_DIM + _ONES_ROWS, t), jnp.bfloat16),
                        pltpu.VMEM((ATT_V_DIM + _ONES_ROWS, 2 * tq), jnp.float32),
                        pltpu.VMEM((tk, 2 * tq), jnp.float32)],
        compiler_params=_cparams(("parallel", "parallel")),
        name="diff_attn",
    )(beta_t, qext, vec(lq1), vec(lk1), vec(lq2), vec(lk2),
      subg.reshape(ATT_V_DIM, 1).astype(jnp.float32), proj3d, proj3d, proj3d, kext)


_HG_CHUNKS_PER_ITER = 8
_HG_FIN_ROWS = 256


def _scan_rows(g, rev):
    c = g.shape[0]
    row = lax.broadcasted_iota(jnp.int32, g.shape, 0)
    b = g
    s = 1
    while s < c:
        if rev:
            b = b + jnp.where(row < c - s, pltpu.roll(b, c - s, axis=0), 0.0)
        else:
            b = b + jnp.where(row >= s, pltpu.roll(b, s, axis=0), 0.0)
        s *= 2
    return b


def _level_ref(b, m, rev):
    c, w = b.shape
    r = m // 2 if rev else m // 2 - 1
    pieces = []
    if m >= SUBLANES:
        for j in range(c // m):
            pieces.append(jnp.broadcast_to(b[j * m + r:j * m + r + 1, :], (m, w)))
    else:
        sub = lax.broadcasted_iota(jnp.int32, (SUBLANES, w), 0)
        for j in range(c // SUBLANES):
            base = j * SUBLANES
            acc = jnp.broadcast_to(b[base + r:base + r + 1, :], (SUBLANES, w))
            for i in range(1, SUBLANES // m):
                cand = jnp.broadcast_to(b[base + i * m + r:base + i * m + r + 1, :], (SUBLANES, w))
                acc = jnp.where(sub >= i * m, cand, acc)
            pieces.append(acc)
    return jnp.concatenate(pieces, axis=0)


def _neg_abs(x):
    bits = lax.bitcast_convert_type(x, jnp.uint32) | jnp.uint32(0x80000000)
    return lax.bitcast_convert_type(bits, jnp.float32)


def _hgrn_masks(c, rev):
    row = lax.broadcasted_iota(jnp.int32, (c, LANES), 0)
    ti = lax.broadcasted_iota(jnp.int32, (c, c), 0)
    si = lax.broadcasted_iota(jnp.int32, (c, c), 1)
    qrows, pairs = {}, {}
    m = c
    while m >= 2:
        late = (row & (m - 1)) >= m // 2
        t_late = (ti & (m - 1)) >= m // 2
        s_late = (si & (m - 1)) >= m // 2
        same = (ti ^ si) < m
        if rev:
            qrows[m] = jnp.logical_not(late)
            pairs[m] = same & jnp.logical_not(t_late) & s_late
        else:
            qrows[m] = late
            pairs[m] = same & t_late & jnp.logical_not(s_late)
        m //= 2
    return qrows, pairs, ti == si


def _hgrn_chunk(q, f, v, st_ref, rev, masks):
    qrows, pairs, diag = masks
    c = q.shape[0]
    bf = jnp.bfloat16
    k_ = 1.0 - f
    b = _scan_rows(jnp.log(f) * _LOG2E, rev)
    a = jnp.where(diag,
                  lax.dot_general(q.astype(bf), k_.astype(bf), _NT,
                                  preferred_element_type=jnp.float32), 0.0)
    m = c
    while m >= 2:
        if m == 2:
            cref = jnp.where(qrows[m], pltpu.roll(b, c - 1 if rev else 1, axis=0), b)
        else:
            cref = _level_ref(b, m, rev)
        e = jnp.exp2(_neg_abs(b - cref))
        z = (jnp.where(qrows[m], q, k_) * e).astype(bf)
        p = lax.dot_general(z, z, _NT, preferred_element_type=jnp.float32)
        a = jnp.where(pairs[m], p, a)
        m //= 2
    st = st_ref[...]
    o = jnp.dot(a.astype(bf), v.astype(bf), preferred_element_type=jnp.float32)
    o = o + lax.dot_general((q * jnp.exp2(b)).astype(bf), st.astype(bf), _NT,
                            preferred_element_type=jnp.float32)
    btot = b[0:1, :] if rev else b[c - 1:c, :]
    khat = (k_ * jnp.exp2(btot - b)).astype(bf)
    st_ref[...] = st * jnp.exp2(btot) + jnp.dot(v.T.astype(bf), khat,
                                                preferred_element_type=jnp.float32)
    return o


def _hgrn_kernel(lb_ref, ng_ref, q_ref, ff_ref, fb_ref, i_ref, g_ref, o_ref,
                 of_sc, ob_sc, stf_sc, stb_sc, *, layer, chunk):
    t = q_ref.shape[0]
    n = t // chunk
    lb_raw = lb_ref[...].astype(jnp.float32)
    ex = jnp.exp(lb_raw - jnp.max(lb_raw, axis=0, keepdims=True))
    sm = ex / jnp.sum(ex, axis=0, keepdims=True)
    lb = jnp.zeros_like(sm[0])
    for l_ in range(1, layer + 1):
        lb = lb + sm[l_]
    lb_f = lb[0:1, :]
    lb_b = lb[1:2, :]

    stf_sc[...] = jnp.zeros(stf_sc.shape, jnp.float32)
    stb_sc[...] = jnp.zeros(stb_sc.shape, jnp.float32)

    def load(ref, r0):
        return ref[pl.ds(r0, chunk), :].astype(jnp.float32)

    masks_f = _hgrn_masks(chunk, False)
    masks_b = _hgrn_masks(chunk, True)

    per_iter = _HG_CHUNKS_PER_ITER if n % _HG_CHUNKS_PER_ITER == 0 else 1

    def body(ci, carry):
        for u in range(per_iter):
            r0 = pl.multiple_of((ci * per_iter + u) * chunk, chunk)
            qr = load(q_ref, r0)
            f = lb_f + (1.0 - lb_f) * jax.nn.sigmoid(load(ff_ref, r0))
            of_sc[pl.ds(r0, chunk), :] = _hgrn_chunk(qr * jax.nn.sigmoid(qr), f,
                                                     load(i_ref, r0), stf_sc, False, masks_f)
            r1 = pl.multiple_of((n - 1 - ci * per_iter - u) * chunk, chunk)
            qr = load(q_ref, r1)
            f = lb_b + (1.0 - lb_b) * jax.nn.sigmoid(load(fb_ref, r1))
            ob_sc[pl.ds(r1, chunk), :] = _hgrn_chunk(qr * jax.nn.sigmoid(qr), f,
                                                     load(i_ref, r1), stb_sc, True, masks_b)
        return carry

    lax.fori_loop(0, n // per_iter, body, 0)

    rows = math.gcd(t, _HG_FIN_ROWS)

    def fin(ci, carry):
        r0 = pl.multiple_of(ci * rows, rows)
        o = of_sc[pl.ds(r0, rows), :] + ob_sc[pl.ds(r0, rows), :]
        y = o * lax.rsqrt(jnp.mean(o * o, axis=-1, keepdims=True) + EPS) * ng_ref[...]
        gr = g_ref[pl.ds(r0, rows), :].astype(jnp.float32)
        o_ref[pl.ds(r0, rows), :] = (y * (gr * jax.nn.sigmoid(gr))).astype(o_ref.dtype)
        return carry

    lax.fori_loop(0, t // rows, fin, 0)


def hgrn_bidir(proj3d, hgrn_lb, norm_g, layer):
    b, t, _ = proj3d.shape
    depth = hgrn_lb.shape[0]
    col = lambda blk: pl.BlockSpec((None, t, LANES), lambda bi, h: (bi, 0, blk + h))
    return pl.pallas_call(
        functools.partial(_hgrn_kernel, layer=layer, chunk=HG_CHUNK),
        grid=(b, HG_HEADS),
        in_specs=[
            pl.BlockSpec((depth, 2, LANES), lambda bi, h: (0, 0, h)),
            pl.BlockSpec((1, HG_V), lambda bi, h: (0, 0)),
            col(_HQ), col(_HFF), col(_HFB), col(_HI), col(_HG),
        ],
        out_specs=pl.BlockSpec((None, t, LANES), lambda bi, h: (bi, 0, h)),
        out_shape=jax.ShapeDtypeStruct((b, t, HG_HEADS * HG_V), jnp.bfloat16),
        scratch_shapes=[
            pltpu.VMEM((t, HG_V), jnp.float32),
            pltpu.VMEM((t, HG_V), jnp.float32),
            pltpu.VMEM((HG_V, HG_K), jnp.float32),
            pltpu.VMEM((HG_V, HG_K), jnp.float32),
        ],
        compiler_params=_cparams(("parallel", "parallel")),
        name="hgrn_bidir",
    )(hgrn_lb, norm_g.reshape(1, HG_V).astype(jnp.float32),
      proj3d, proj3d, proj3d, proj3d, proj3d)


def _out_router_kernel(a_ref, h_ref, wa_ref, wh_ref, x_ref, g_ref, wrt_ref,
                       o_ref, ha_ref, hb_ref, afft_ref):
    y = jnp.dot(a_ref[...], wa_ref[...], preferred_element_type=jnp.float32)
    y = y + jnp.dot(h_ref[...], wh_ref[...], preferred_element_type=jnp.float32)
    x = x_ref[...] + y
    o_ref[...] = x
    h = x * lax.rsqrt(jnp.mean(x * x, axis=-1, keepdims=True) + EPS) * g_ref[...]
    bits = lax.bitcast_convert_type(h.astype(jnp.bfloat16).astype(jnp.float32), jnp.uint32)
    q = ha_ref.shape[1]
    ha_ref[...] = (bits[:, 0:q] >> 16) | bits[:, q:2 * q]
    hb_ref[...] = (bits[:, 2 * q:3 * q] >> 16) | bits[:, 3 * q:4 * q]
    logits = lax.dot_general(wrt_ref[...], h, _NT, preferred_element_type=jnp.float32,
                             precision=lax.Precision.HIGHEST)
    ex = jnp.exp(logits - jnp.max(logits, axis=0, keepdims=True))
    afft_ref[...] = ex / jnp.sum(ex, axis=0, keepdims=True)


def out_proj_router(att2d, hgo2d, w_out_bf16, x2d, g, w_router, tm=512):
    n, d = x2d.shape
    tm = min(tm, n)
    wa = att2d.shape[1]
    wh = hgo2d.shape[1]
    e = w_router.shape[1]
    rows = lambda w: pl.BlockSpec((tm, w), lambda i: (i, 0))
    return pl.pallas_call(
        _out_router_kernel,
        grid=(n // tm,),
        in_specs=[
            rows(wa), rows(wh),
            pl.BlockSpec((wa, d), lambda i: (0, 0)),
            pl.BlockSpec((wh, d), lambda i: (wa // wh, 0)),
            rows(d),
            pl.BlockSpec((1, d), lambda i: (0, 0)),
            pl.BlockSpec((e, d), lambda i: (0, 0)),
        ],
        out_specs=[rows(d), rows(d // 4), rows(d // 4), pl.BlockSpec((e, tm), lambda i: (0, i))],
        out_shape=[
            jax.ShapeDtypeStruct((n, d), jnp.float32),
            jax.ShapeDtypeStruct((n, d // 4), jnp.uint32),
            jax.ShapeDtypeStruct((n, d // 4), jnp.uint32),
            jax.ShapeDtypeStruct((e, n), jnp.float32),
        ],
        compiler_params=_cparams(("parallel",)),
        name="out_proj_router",
    )(att2d, hgo2d, w_out_bf16, w_out_bf16, x2d, g.reshape(1, d), w_router.T)


_SEL_CHUNK = 1024


def _select_kernel(afft_ref, pos_ref, offs_ref, *, cap):
    f32, i32 = jnp.float32, jnp.int32
    e, n = afft_ref.shape
    chunk = min(_SEL_CHUNK, n)
    nt = n // LANES

    def bits_at(start, size):
        return lax.bitcast_convert_type(afft_ref[:, pl.ds(start, size)], i32)

    def count_ge(cand):
        def body(c, acc):
            x = bits_at(pl.multiple_of(c * chunk, chunk), chunk)
            return acc + jnp.where(x >= cand, 1.0, 0.0)
        acc = lax.fori_loop(0, n // chunk, body, jnp.zeros((e, chunk), f32))
        return jnp.sum(acc, axis=1, keepdims=True)

    def bit_body(it, prefix):
        cand = prefix | jnp.left_shift(jnp.int32(1), 30 - it)
        return jnp.where(count_ge(cand) >= cap, cand, prefix)

    thr = lax.fori_loop(0, 31, bit_body, jnp.zeros((e, 1), i32))
    need = cap - count_ge(thr + 1)

    tri = (lax.broadcasted_iota(i32, (LANES, LANES), 0)
           <= lax.broadcasted_iota(i32, (LANES, LANES), 1)).astype(jnp.bfloat16)
    tile_id = lax.broadcasted_iota(i32, (e, nt), 1)

    def tile_body(j, carry):
        c_gt, c_eq, offs_acc = carry
        x = bits_at(pl.multiple_of(j * LANES, LANES), LANES)
        gt = x > thr
        eq = x == thr
        both = jnp.concatenate([jnp.where(gt, 1.0, 0.0), jnp.where(eq, 1.0, 0.0)],
                               axis=0).astype(jnp.bfloat16)
        inc = jnp.dot(both, tri, preferred_element_type=f32)
        inc_gt = inc[:e] + c_gt
        inc_eq = inc[e:] + c_eq
        sel = gt | (eq & (inc_eq <= need))
        incl = inc_gt + jnp.minimum(inc_eq, need)
        pos_ref[:, pl.ds(pl.multiple_of(j * LANES, LANES), LANES)] = jnp.where(
            sel, incl - 1.0, -1.0).astype(i32)
        offs_acc = jnp.where(tile_id == j, c_gt + jnp.minimum(c_eq, need), offs_acc)
        return inc_gt[:, LANES - 1:LANES], inc_eq[:, LANES - 1:LANES], offs_acc

    zero = jnp.zeros((e, 1), f32)
    _, _, offs_acc = lax.fori_loop(0, nt, tile_body, (zero, zero, jnp.zeros((e, nt), f32)))
    offs_ref[...] = offs_acc.astype(i32)


def ec_select(afft, cap):
    e, n = afft.shape
    return pl.pallas_call(
        functools.partial(_select_kernel, cap=cap),
        out_shape=[jax.ShapeDtypeStruct((e, n), jnp.int32),
                   jax.ShapeDtypeStruct((e, n // LANES), jnp.int32)],
        compiler_params=pltpu.CompilerParams(vmem_limit_bytes=_VMEM_LIMIT),
        name="ec_select",
    )(afft)


_SC_LANES = 16
_SC_CORES = 2
_INV_CHUNK = 8192


def ec_invert(pos, afft, cap):
    ne, n = pos.shape
    ch = min(_INV_CHUNK, n)
    mesh = plsc.VectorSubcoreMesh(core_axis_name="c", subcore_axis_name="s")

    @functools.partial(
        pl.kernel, mesh=mesh,
        out_type=[jax.ShapeDtypeStruct((ne * cap,), jnp.int32),
                  jax.ShapeDtypeStruct((ne * cap,), jnp.float32)],
        scratch_types=[pltpu.VMEM((ch,), jnp.int32), pltpu.VMEM((ch,), jnp.float32),
                       pltpu.VMEM((cap,), jnp.int32), pltpu.VMEM((cap,), jnp.float32)],
        compiler_params=pltpu.CompilerParams(needs_layout_passes=False))
    def invert(pos_hbm, aff_hbm, idx_hbm, gate_hbm, pos_v, aff_v, idx_v, gate_v):
        wid = lax.axis_index("s") * _SC_CORES + lax.axis_index("c")

        @pl.when(wid < ne)
        def _():
            @pl.loop(0, n // ch)
            def _(c):
                pltpu.sync_copy(pos_hbm.at[pl.ds(wid * n + c * ch, ch)], pos_v)
                pltpu.sync_copy(aff_hbm.at[pl.ds(wid * n + c * ch, ch)], aff_v)

                @pl.loop(0, ch, step=_SC_LANES)
                def _(i):
                    p = pos_v[pl.ds(i, _SC_LANES)]
                    tok = lax.iota(jnp.int32, _SC_LANES) + (c * ch + i)
                    plsc.store_scatter(idx_v, [p], tok, mask=p >= 0)
                    plsc.store_scatter(gate_v, [p], aff_v[pl.ds(i, _SC_LANES)], mask=p >= 0)

            pltpu.sync_copy(idx_v, idx_hbm.at[pl.ds(wid * cap, cap)])
            pltpu.sync_copy(gate_v, gate_hbm.at[pl.ds(wid * cap, cap)])

    idx, gates = invert(pos.reshape(ne * n), afft.reshape(ne * n))
    return idx.reshape(ne, cap), gates.reshape(ne, cap)


_GATHER_WIN = 128


def ec_gather(table, idx_row):
    m = idx_row.shape[1]
    width = table.shape[1]
    steps = m // (_GATHER_WIN * _SC_CORES)
    mesh = plsc.VectorSubcoreMesh(core_axis_name="c", subcore_axis_name="s")

    @functools.partial(
        pl.kernel, mesh=mesh,
        out_type=jax.ShapeDtypeStruct((m, width), table.dtype),
        scratch_types=[])
    def gather(x_hbm, i_hbm, o_hbm):
        def body(i_vmem, o_vmem):
            pltpu.sync_copy(x_hbm.at[i_vmem.at[0]], o_vmem)

        pltpu.emit_pipeline(
            body,
            grid=(_SC_CORES, steps),
            in_specs=[pl.BlockSpec((1, _GATHER_WIN), index_map=lambda a, i: (0, a * steps + i))],
            out_specs=[pl.BlockSpec((_GATHER_WIN, width),
                                    index_map=lambda a, i: (a * steps + i, 0))],
            core_axis_name=("c", "s"),
            dimension_semantics=(pltpu.PARALLEL, pltpu.PARALLEL),
        )(i_hbm, o_hbm)

    return gather(table, idx_row)


def _expert_kernel(xa_ref, xb_ref, gate_ref, wg_ref, wu_ref, wd_ref, o_ref, wg_sc, wu_sc, wd_sc):
    @pl.when(pl.program_id(1) == 0)
    def _():
        wg_sc[...] = wg_ref[...].astype(jnp.bfloat16)
        wu_sc[...] = wu_ref[...].astype(jnp.bfloat16)
        wd_sc[...] = wd_ref[...].astype(jnp.bfloat16)

    @pl.when(pl.program_id(1) == pl.num_programs(1) - 1)
    def _():
        o_ref[...] = jnp.zeros(o_ref.shape, o_ref.dtype)

    @pl.when(pl.program_id(1) < pl.num_programs(1) - 1)
    def _():
        as_f32 = lambda bits: lax.bitcast_convert_type(bits, jnp.float32)
        wa, wb = xa_ref[...], xb_ref[...]
        hi = jnp.uint32(0xFFFF0000)
        xs = jnp.concatenate([as_f32(wa << 16), as_f32(wa & hi), as_f32(wb << 16),
                              as_f32(wb & hi)], axis=1).astype(jnp.bfloat16)
        a = jnp.dot(xs, wg_sc[...], preferred_element_type=jnp.float32)
        u = jnp.dot(xs, wu_sc[...], preferred_element_type=jnp.float32)
        hid = (a * jax.nn.sigmoid(a) * u).astype(jnp.bfloat16)
        y = jnp.dot(hid, wd_sc[...], preferred_element_type=jnp.float32)
        gate_col = jnp.broadcast_to(gate_ref[...], (LANES, gate_ref.shape[1])).T[:, 0:1]
        o_ref[...] = (y * gate_col).astype(o_ref.dtype)


def expert_ffn(xa, xb, gates, wg, wu, wd, layer, tm=512):
    e, c, q = xa.shape
    d = 4 * q
    tm = min(tm, c)
    f = wg.shape[3]
    nt = c // tm
    return pl.pallas_call(
        _expert_kernel,
        grid=(e, nt + 1),
        in_specs=[
            pl.BlockSpec((None, tm, q), lambda ei, i: (ei, jnp.minimum(i, nt - 1), 0)),
            pl.BlockSpec((None, tm, q), lambda ei, i: (ei, jnp.minimum(i, nt - 1), 0)),
            pl.BlockSpec((None, 1, tm), lambda ei, i: (ei, 0, jnp.minimum(i, nt - 1))),
            pl.BlockSpec((None, None, d, f), lambda ei, i: (layer, ei, 0, 0)),
            pl.BlockSpec((None, None, d, f), lambda ei, i: (layer, ei, 0, 0)),
            pl.BlockSpec((None, None, f, d), lambda ei, i: (layer, ei, 0, 0)),
        ],
        out_specs=pl.BlockSpec((None, tm, d), lambda ei, i: (ei, i, 0)),
        out_shape=jax.ShapeDtypeStruct((e, c + tm, d), jnp.bfloat16),
        scratch_shapes=[pltpu.VMEM((d, f), jnp.bfloat16), pltpu.VMEM((d, f), jnp.bfloat16),
                        pltpu.VMEM((f, d), jnp.bfloat16)],
        compiler_params=_cparams(("parallel", "arbitrary")),
        name="expert_ffn",
    )(xa, xb, gates.reshape(e, 1, c), wg, wu, wd)


_SLAB_ALIGN = 16
_SLAB_ROWS = LANES + _SLAB_ALIGN


def _slab_copy(offs_ref, out_hbm, slab_ref, sem_ref, e, j, slot):
    start = pl.multiple_of((offs_ref[e, j] // _SLAB_ALIGN) * _SLAB_ALIGN, _SLAB_ALIGN)
    return pltpu.make_async_copy(out_hbm.at[e, pl.ds(start, _SLAB_ROWS), :],
                                 slab_ref.at[slot, e], sem_ref.at[slot])


def _combine_kernel(offs_ref, x_ref, pos_ref, fg_ref, out_hbm, o_ref, slab_ref, sem_ref, *,
                    final_norm):
    f32 = jnp.float32
    ne = pos_ref.shape[0]
    j = pl.program_id(0)
    slot = j % 2

    @pl.when(j == 0)
    def _():
        for e in range(ne):
            _slab_copy(offs_ref, out_hbm, slab_ref, sem_ref, e, j, slot).start()

    @pl.when(j + 1 < pl.num_programs(0))
    def _():
        for e in range(ne):
            _slab_copy(offs_ref, out_hbm, slab_ref, sem_ref, e, j + 1, 1 - slot).start()

    pad = jnp.zeros((LANES - ne, LANES), f32)
    pos_t = jnp.concatenate([pos_ref[...].astype(f32), pad], axis=0).T
    row_id = lax.broadcasted_iota(jnp.int32, (LANES, _SLAB_ROWS), 1).astype(f32)
    pltpu.make_async_copy(out_hbm.at[pl.ds(0, ne), pl.ds(0, _SLAB_ROWS), :], slab_ref.at[slot],
                          sem_ref.at[slot]).wait()
    y = x_ref[...]
    for e in range(ne):
        start = (offs_ref[e, j] // _SLAB_ALIGN) * _SLAB_ALIGN
        rel = pos_t[:, e:e + 1] - start.astype(f32)
        onehot = jnp.where(rel == row_id, 1.0, 0.0).astype(jnp.bfloat16)
        y = y + jnp.dot(onehot, slab_ref[slot, e], preferred_element_type=f32)
    if final_norm:
        y = y * lax.rsqrt(jnp.mean(y * y, axis=-1, keepdims=True) + EPS) * fg_ref[...]
    o_ref[...] = y


def ec_combine(x2d, pos, offs, out, final_g, final_norm):
    n, d = x2d.shape
    ne, crows, _ = out.shape
    assert crows >= n * EC_CAPACITY // ne + _SLAB_ROWS
    grid_spec = pltpu.PrefetchScalarGridSpec(
        num_scalar_prefetch=1,
        grid=(n // LANES,),
        in_specs=[
            pl.BlockSpec((LANES, d), lambda j, offs_ref: (j, 0)),
            pl.BlockSpec((ne, LANES), lambda j, offs_ref: (0, j)),
            pl.BlockSpec((1, d), lambda j, offs_ref: (0, 0)),
            pl.BlockSpec(memory_space=pl.ANY),
        ],
        out_specs=pl.BlockSpec((LANES, d), lambda j, offs_ref: (j, 0)),
        scratch_shapes=[pltpu.VMEM((2, ne, _SLAB_ROWS, d), out.dtype),
                        pltpu.SemaphoreType.DMA((2,))],
    )
    return pl.pallas_call(
        functools.partial(_combine_kernel, final_norm=final_norm),
        grid_spec=grid_spec,
        out_shape=jax.ShapeDtypeStruct((n, d), jnp.float32),
        compiler_params=_cparams(("arbitrary",)),
        name="ec_combine",
    )(offs, x2d, pos, final_g.reshape(1, d), out)


def _trunk(x, p):
    b, t, d = x.shape
    n = b * t
    depth = p["w_in"].shape[0]
    x2d = x.reshape(n, d)
    cap = max(1, EC_CAPACITY * n // N_EXPERTS)
    for layer in range(depth):
        proj = norm_proj(x2d, p["norm_mix_g"][layer], p["w_in"][layer])
        proj3d = proj.reshape(b, t, -1)
        att = diff_attention(proj3d, p["lambda_q1"][layer], p["lambda_k1"][layer],
                             p["lambda_q2"][layer], p["lambda_k2"][layer],
                             p["diff_subln_g"][layer], layer)
        hgo = hgrn_bidir(proj3d, p["hgrn_lb"], p["hgrn_norm_g"][layer], layer)
        x2d, ha, hb, afft = out_proj_router(att.reshape(n, -1), hgo.reshape(n, -1),
                                            p["w_out"][layer], x2d, p["norm_ffn_g"][layer],
                                            p["w_router"][layer])
        pos, offs = ec_select(afft, cap)
        ne = afft.shape[0]
        idx, gates = ec_invert(pos, afft, cap)
        idx = idx.reshape(1, ne * cap)
        xa = ec_gather(ha, idx).reshape(ne, cap, -1)
        xb = ec_gather(hb, idx).reshape(ne, cap, -1)
        out = expert_ffn(xa, xb, gates, p["w_gate"], p["w_up"], p["w_down"], layer)
        x2d = ec_combine(x2d, pos, offs, out, p["final_norm_g"], layer == depth - 1)
    return x2d.reshape(b, t, d)


def kernel(x_prompt, x_sample, norm_mix_g, w_in, lambda_q1, lambda_k1, lambda_q2, lambda_k2,
           diff_subln_g, hgrn_lb, hgrn_norm_g, w_out, norm_ffn_g, w_router, w_gate, w_up,
           w_down, final_norm_g):
    bf = jnp.bfloat16
    p = dict(norm_mix_g=norm_mix_g, w_in=w_in.astype(bf), lambda_q1=lambda_q1,
             lambda_k1=lambda_k1, lambda_q2=lambda_q2, lambda_k2=lambda_k2,
             diff_subln_g=diff_subln_g, hgrn_lb=hgrn_lb, hgrn_norm_g=hgrn_norm_g,
             w_out=w_out.astype(bf), norm_ffn_g=norm_ffn_g, w_router=w_router,
             w_gate=w_gate, w_up=w_up, w_down=w_down,
             final_norm_g=final_norm_g)
    return (_trunk(x_prompt, p), _trunk(x_sample, p))
```

```python
import functools
import math

import jax
import jax.numpy as jnp
from jax import lax
from jax.experimental import pallas as pl
from jax.experimental.pallas import tpu as pltpu
from jax.experimental.pallas import tpu_sc as plsc

ATT_HEADS = 4
ATT_QK_DIM = 64
ATT_V_DIM = 128
HG_HEADS = 4
HG_K = 128
HG_V = 128
HG_CHUNK = 64
N_EXPERTS = 16
EC_CAPACITY = 2
EPS = 1e-6
LANES = 128
SUBLANES = 8

_QBLK, _KBLK, _VBLK = 0, 4, 8
_HQ, _HFF, _HFB, _HI, _HG = 12, 16, 20, 24, 28

_VMEM_LIMIT = 56 * 1024 * 1024

_NT = (((1,), (1,)), ((), ()))


def _cparams(sem):
    return pltpu.CompilerParams(dimension_semantics=sem, vmem_limit_bytes=_VMEM_LIMIT)


def _norm_proj_kernel(x_ref, g_ref, w_ref, o_ref, *, n_split):
    x = x_ref[...]
    y = x * lax.rsqrt(jnp.mean(x * x, axis=-1, keepdims=True) + EPS) * g_ref[...]
    h = y.astype(jnp.bfloat16)
    wn = w_ref.shape[1] // n_split
    for c in range(n_split):
        o_ref[:, c * wn:(c + 1) * wn] = jnp.dot(
            h, w_ref[:, c * wn:(c + 1) * wn], preferred_element_type=jnp.float32
        ).astype(o_ref.dtype)


def norm_proj(x2d, g, w_bf16, tm=512):
    n, d = x2d.shape
    tm = min(tm, n)
    dout = w_bf16.shape[1]
    return pl.pallas_call(
        functools.partial(_norm_proj_kernel, n_split=max(1, dout // 1024)),
        grid=(n // tm,),
        in_specs=[
            pl.BlockSpec((tm, d), lambda i: (i, 0)),
            pl.BlockSpec((1, d), lambda i: (0, 0)),
            pl.BlockSpec((d, dout), lambda i: (0, 0)),
        ],
        out_specs=pl.BlockSpec((tm, dout), lambda i: (i, 0)),
        out_shape=jax.ShapeDtypeStruct((n, dout), jnp.bfloat16),
        compiler_params=_cparams(("parallel",)),
        name="norm_proj",
    )(x2d, g.reshape(1, d), w_bf16)


_LOG2E = 1.4426950408889634
_Q_STRIP = 128
_ONES_ROWS = 16


def _split3(x):
    bf = jnp.bfloat16
    h = x.astype(bf).astype(jnp.float32)
    r = x - h
    m = r.astype(bf).astype(jnp.float32)
    lo = (r - m).astype(bf).astype(jnp.float32)
    return h, m, lo


def _attn_kernel(beta_ref, qext_ref, lq1_ref, lk1_ref, lq2_ref, lk2_ref, subg_ref,
                 q_ref, k_ref, v_ref, kext_ref, o_ref, vt_sc, acc_sc, s_sc, *, lam_init, tk, tq):
    f32, bf = jnp.float32, jnp.bfloat16
    t = q_ref.shape[0]
    nk = t // tk
    qs_ = _Q_STRIP
    ng = tq // qs_
    ncol = 2 * tq
    dv = v_ref.shape[1]
    beta = beta_ref[0:1, 0:1]

    def vt_body(c, carry):
        r0 = pl.multiple_of(c * tk, tk)
        vt_sc[0:dv, pl.ds(r0, tk)] = v_ref[pl.ds(r0, tk), :].astype(f32).T.astype(bf)
        return carry

    lax.fori_loop(0, nk, vt_body, 0)
    vt_sc[dv:dv + _ONES_ROWS, :] = jnp.ones((_ONES_ROWS, t), bf)

    lam = (jnp.exp(jnp.sum(lq1_ref[...] * lk1_ref[...], axis=-1, keepdims=True))
           - jnp.exp(jnp.sum(lq2_ref[...] * lk2_ref[...], axis=-1, keepdims=True)) + lam_init)
    lane = lax.broadcasted_iota(jnp.int32, (qs_, LANES), 1)
    q_minus_r = (lax.broadcasted_iota(jnp.int32, (tk, qs_), 1)
                 - lax.broadcasted_iota(jnp.int32, (tk, qs_), 0)).astype(f32)
    cscale = ATT_QK_DIM ** -0.5 * _LOG2E

    def group(gi, carry):
        g0 = pl.multiple_of(gi * tq, tq)
        tiles = []
        for u in range(ng):
            q0 = g0 + u * qs_
            qs = (q_ref[pl.ds(q0, qs_), :].astype(f32) * cscale).astype(bf).astype(f32)
            qext = qext_ref[...]
            q_aug = jnp.concatenate([
                jnp.concatenate([jnp.where(lane < ATT_QK_DIM, qs, 0.0), qext], axis=1),
                jnp.concatenate([jnp.where(lane >= ATT_QK_DIM, qs, 0.0), qext], axis=1)], axis=0)
            tiles.append(q_aug.T.astype(bf))
        w = jnp.concatenate(tiles, axis=1)
        jc = g0 // tk

        def corr_tile(d):
            corr = 2.0 * beta * jnp.minimum(q_minus_r + d.astype(f32), 0.0)
            return jnp.concatenate([corr, corr], axis=1)

        def shift(d, sidx):
            sgn = 1.0 if sidx == 0 else -1.0
            return (-sgn) * beta * d.astype(f32)

        def scores(n, sidx, crossing):
            r0 = pl.multiple_of(n * tk, tk)
            k_aug = jnp.concatenate([k_ref[pl.ds(r0, tk), :], kext_ref[sidx]], axis=1)
            for u in range(ng):
                c0 = u * 2 * qs_
                s = jnp.dot(k_aug, w[:, c0:c0 + 2 * qs_], preferred_element_type=f32)
                if crossing:
                    s = s + corr_tile(g0 + u * qs_ - r0)
                s_sc[:, c0:c0 + 2 * qs_] = s

        def step(n, mc, sidx, crossing):
            r0 = pl.multiple_of((n - 1) * tk, tk)
            r1 = pl.multiple_of(n * tk, tk)
            vt = vt_sc[:, pl.ds(r0, tk)]
            if sidx is not None:
                k_aug = jnp.concatenate([k_ref[pl.ds(r1, tk), :], kext_ref[sidx]], axis=1)
            m, csts = mc
            ms, cs = [], []
            for u in range(ng):
                c0 = u * 2 * qs_
                s = s_sc[:, c0:c0 + 2 * qs_]
                m_old = m[:, c0:c0 + 2 * qs_]
                cst = csts[:, c0:c0 + 2 * qs_]
                m_new = jnp.maximum(m_old, jnp.max(s, axis=0, keepdims=True) + cst)
                alpha = jnp.exp2(m_old - m_new)
                p = jnp.exp2(s - (m_new - cst)).astype(bf)
                acc_sc[:, c0:c0 + 2 * qs_] = (acc_sc[:, c0:c0 + 2 * qs_] * alpha
                                              + jnp.dot(vt, p, preferred_element_type=f32))
                ms.append(m_new)
                if sidx is not None:
                    s = jnp.dot(k_aug, w[:, c0:c0 + 2 * qs_], preferred_element_type=f32)
                    if crossing:
                        s = s + corr_tile(g0 + u * qs_ - r1)
                    s_sc[:, c0:c0 + 2 * qs_] = s
                    cs.append(jnp.broadcast_to(shift(g0 + u * qs_ - r1, sidx), (1, 2 * qs_)))
            return jnp.concatenate(ms, axis=1), (jnp.concatenate(cs, axis=1) if cs else csts)

        acc_sc[...] = jnp.zeros(acc_sc.shape, f32)
        m = jnp.full((1, ncol), -1e30, f32)
        scores(0, 0, True)
        n1 = jnp.maximum(jc, 1)
        n2 = jc + tq // tk
        cst0 = jnp.concatenate([jnp.broadcast_to(shift(g0 + u * qs_, 0), (1, 2 * qs_))
                                for u in range(ng)], axis=1)
        mc = (m, cst0)
        mc = lax.fori_loop(1, n1, functools.partial(step, sidx=0, crossing=False), mc)
        mc = lax.fori_loop(n1, n2, functools.partial(step, sidx=0, crossing=True), mc)
        mc = lax.fori_loop(n2, nk, functools.partial(step, sidx=1, crossing=False), mc)
        step(nk, mc, None, False)

        for u in range(ng):
            c0 = u * 2 * qs_
            acc = acc_sc[:, c0:c0 + 2 * qs_]
            den = acc[dv:dv + 1, :]
            o = acc[0:dv, 0:qs_] / den[:, 0:qs_] - lam * (acc[0:dv, qs_:] / den[:, qs_:])
            y = o * lax.rsqrt(jnp.mean(o * o, axis=0, keepdims=True) + EPS) * subg_ref[...]
            o_ref[pl.ds(g0 + u * qs_, qs_), :] = (y * (1.0 - lam_init)).T.astype(o_ref.dtype)
        return carry

    lax.fori_loop(0, t // tq, group, 0)


def _attn_tables(tk):
    f32 = jnp.float32
    slopes = 2.0 ** (-8.0 * (jnp.arange(ATT_HEADS, dtype=f32) + 1.0) / ATT_HEADS)
    beta = slopes * _LOG2E
    lane = jnp.arange(LANES)[None, None, :]
    b = _split3(beta)
    a = _split3(-beta[:, None] * jnp.arange(_Q_STRIP, dtype=f32)[None, :])
    qext = jnp.zeros((ATT_HEADS, _Q_STRIP, LANES), f32)
    for c in range(3):
        qext = jnp.where(lane == c, b[c][:, None, None], qext)
        qext = jnp.where(lane == 3 + c, a[c][:, :, None], qext)
    r = jnp.arange(tk, dtype=f32)[:, None]
    lane2 = jnp.arange(LANES)[None, :]
    base = jnp.where(lane2 < 3, r, jnp.where(lane2 < 6, 1.0, 0.0))
    kext = jnp.stack([base, -base]).astype(jnp.bfloat16)
    beta_t = jnp.broadcast_to(beta[:, None, None], (ATT_HEADS, 1, LANES))
    return beta_t, qext, kext


def diff_attention(proj3d, lq1, lk1, lq2, lk2, subg, layer, tk=256, tq=2048):
    b, t, _ = proj3d.shape
    tk, tq = min(tk, t), min(tq, t)
    assert tq % tk == 0 and tq % _Q_STRIP == 0
    lam_init = 0.8 - 0.6 * math.exp(-0.3 * layer)
    assert tk <= 256
    beta_t, qext, kext = _attn_tables(tk)
    vec = lambda a: a.reshape(1, -1).astype(jnp.float32)
    small = lambda n: pl.BlockSpec((1, n), lambda bi, h: (0, 0))
    head = pl.BlockSpec((None, 1, LANES), lambda bi, h: (h, 0, 0))
    col = lambda blk: pl.BlockSpec((None, t, LANES), lambda bi, h: (bi, 0, blk + h))
    return pl.pallas_call(
        functools.partial(_attn_kernel, lam_init=lam_init, tk=tk, tq=tq),
        grid=(b, ATT_HEADS),
        in_specs=[
            head, pl.BlockSpec((None, _Q_STRIP, LANES), lambda bi, h: (h, 0, 0)),
            small(ATT_QK_DIM), small(ATT_QK_DIM), small(ATT_QK_DIM), small(ATT_QK_DIM),
            pl.BlockSpec((ATT_V_DIM, 1), lambda bi, h: (0, 0)),
            col(_QBLK), col(_KBLK), col(_VBLK),
            pl.BlockSpec((2, tk, LANES), lambda bi, h: (0, 0, 0)),
        ],
        out_specs=pl.BlockSpec((None, t, LANES), lambda bi, h: (bi, 0, h)),
        out_shape=jax.ShapeDtypeStruct((b, t, ATT_HEADS * ATT_V_DIM), jnp.bfloat16),
        scratch_shapes=[pltpu.VMEM((ATT_V_DIM + _ONES_ROWS, t), jnp.bfloat16),
                        pltpu.VMEM((ATT_V_DIM + _ONES_ROWS, 2 * tq), jnp.float32),
                        pltpu.VMEM((tk, 2 * tq), jnp.float32)],
        compiler_params=_cparams(("parallel", "parallel")),
        name="diff_attn",
    )(beta_t, qext, vec(lq1), vec(lk1), vec(lq2), vec(lk2),
      subg.reshape(ATT_V_DIM, 1).astype(jnp.float32), proj3d, proj3d, proj3d, kext)


_HG_CHUNKS_PER_ITER = 8
_HG_FIN_ROWS = 256


def _scan_rows(g, rev):
    c = g.shape[0]
    row = lax.broadcasted_iota(jnp.int32, g.shape, 0)
    b = g
    s = 1
    while s < c:
        if rev:
            b = b + jnp.where(row < c - s, pltpu.roll(b, c - s, axis=0), 0.0)
        else:
            b = b + jnp.where(row >= s, pltpu.roll(b, s, axis=0), 0.0)
        s *= 2
    return b


def _level_ref(b, m, rev):
    c, w = b.shape
    r = m // 2 if rev else m // 2 - 1
    pieces = []
    if m >= SUBLANES:
        for j in range(c // m):
            pieces.append(jnp.broadcast_to(b[j * m + r:j * m + r + 1, :], (m, w)))
    else:
        sub = lax.broadcasted_iota(jnp.int32, (SUBLANES, w), 0)
        for j in range(c // SUBLANES):
            base = j * SUBLANES
            acc = jnp.broadcast_to(b[base + r:base + r + 1, :], (SUBLANES, w))
            for i in range(1, SUBLANES // m):
                cand = jnp.broadcast_to(b[base + i * m + r:base + i * m + r + 1, :], (SUBLANES, w))
                acc = jnp.where(sub >= i * m, cand, acc)
            pieces.append(acc)
    return jnp.concatenate(pieces, axis=0)


def _neg_abs(x):
    bits = lax.bitcast_convert_type(x, jnp.uint32) | jnp.uint32(0x80000000)
    return lax.bitcast_convert_type(bits, jnp.float32)


def _hgrn_masks(c, rev):
    row = lax.broadcasted_iota(jnp.int32, (c, LANES), 0)
    ti = lax.broadcasted_iota(jnp.int32, (c, c), 0)
    si = lax.broadcasted_iota(jnp.int32, (c, c), 1)
    qrows, pairs = {}, {}
    m = c
    while m >= 2:
        late = (row & (m - 1)) >= m // 2
        t_late = (ti & (m - 1)) >= m // 2
        s_late = (si & (m - 1)) >= m // 2
        same = (ti ^ si) < m
        if rev:
            qrows[m] = jnp.logical_not(late)
            pairs[m] = same & jnp.logical_not(t_late) & s_late
        else:
            qrows[m] = late
            pairs[m] = same & t_late & jnp.logical_not(s_late)
        m //= 2
    return qrows, pairs, ti == si


def _hgrn_chunk(q, f, v, st_ref, rev, masks):
    qrows, pairs, diag = masks
    c = q.shape[0]
    bf = jnp.bfloat16
    k_ = 1.0 - f
    b = _scan_rows(jnp.log(f) * _LOG2E, rev)
    a = jnp.where(diag,
                  lax.dot_general(q.astype(bf), k_.astype(bf), _NT,
                                  preferred_element_type=jnp.float32), 0.0)
    m = c
    while m >= 2:
        if m == 2:
            cref = jnp.where(qrows[m], pltpu.roll(b, c - 1 if rev else 1, axis=0), b)
        else:
            cref = _level_ref(b, m, rev)
        e = jnp.exp2(_neg_abs(b - cref))
        z = (jnp.where(qrows[m], q, k_) * e).astype(bf)
        p = lax.dot_general(z, z, _NT, preferred_element_type=jnp.float32)
        a = jnp.where(pairs[m], p, a)
        m //= 2
    st = st_ref[...]
    o = jnp.dot(a.astype(bf), v.astype(bf), preferred_element_type=jnp.float32)
    o = o + lax.dot_general((q * jnp.exp2(b)).astype(bf), st.astype(bf), _NT,
                            preferred_element_type=jnp.float32)
    btot = b[0:1, :] if rev else b[c - 1:c, :]
    khat = (k_ * jnp.exp2(btot - b)).astype(bf)
    st_ref[...] = st * jnp.exp2(btot) + jnp.dot(v.T.astype(bf), khat,
                                                preferred_element_type=jnp.float32)
    return o


def _hgrn_kernel(lb_ref, ng_ref, q_ref, ff_ref, fb_ref, i_ref, g_ref, o_ref,
                 of_sc, ob_sc, stf_sc, stb_sc, *, layer, chunk):
    t = q_ref.shape[0]
    n = t // chunk
    lb_raw = lb_ref[...].astype(jnp.float32)
    ex = jnp.exp(lb_raw - jnp.max(lb_raw, axis=0, keepdims=True))
    sm = ex / jnp.sum(ex, axis=0, keepdims=True)
    lb = jnp.zeros_like(sm[0])
    for l_ in range(1, layer + 1):
        lb = lb + sm[l_]
    lb_f = lb[0:1, :]
    lb_b = lb[1:2, :]

    stf_sc[...] = jnp.zeros(stf_sc.shape, jnp.float32)
    stb_sc[...] = jnp.zeros(stb_sc.shape, jnp.float32)

    def load(ref, r0):
        return ref[pl.ds(r0, chunk), :].astype(jnp.float32)

    masks_f = _hgrn_masks(chunk, False)
    masks_b = _hgrn_masks(chunk, True)

    per_iter = _HG_CHUNKS_PER_ITER if n % _HG_CHUNKS_PER_ITER == 0 else 1

    def body(ci, carry):
        for u in range(per_iter):
            r0 = pl.multiple_of((ci * per_iter + u) * chunk, chunk)
            qr = load(q_ref, r0)
            f = lb_f + (1.0 - lb_f) * jax.nn.sigmoid(load(ff_ref, r0))
            of_sc[pl.ds(r0, chunk), :] = _hgrn_chunk(qr * jax.nn.sigmoid(qr), f,
                                                     load(i_ref, r0), stf_sc, False, masks_f)
            r1 = pl.multiple_of((n - 1 - ci * per_iter - u) * chunk, chunk)
            qr = load(q_ref, r1)
            f = lb_b + (1.0 - lb_b) * jax.nn.sigmoid(load(fb_ref, r1))
            ob_sc[pl.ds(r1, chunk), :] = _hgrn_chunk(qr * jax.nn.sigmoid(qr), f,
                                                     load(i_ref, r1), stb_sc, True, masks_b)
        return carry

    lax.fori_loop(0, n // per_iter, body, 0)

    rows = math.gcd(t, _HG_FIN_ROWS)

    def fin(ci, carry):
        r0 = pl.multiple_of(ci * rows, rows)
        o = of_sc[pl.ds(r0, rows), :] + ob_sc[pl.ds(r0, rows), :]
        y = o * lax.rsqrt(jnp.mean(o * o, axis=-1, keepdims=True) + EPS) * ng_ref[...]
        gr = g_ref[pl.ds(r0, rows), :].astype(jnp.float32)
        o_ref[pl.ds(r0, rows), :] = (y * (gr * jax.nn.sigmoid(gr))).astype(o_ref.dtype)
        return carry

    lax.fori_loop(0, t // rows, fin, 0)


def hgrn_bidir(proj3d, hgrn_lb, norm_g, layer):
    b, t, _ = proj3d.shape
    depth = hgrn_lb.shape[0]
    col = lambda blk: pl.BlockSpec((None, t, LANES), lambda bi, h: (bi, 0, blk + h))
    return pl.pallas_call(
        functools.partial(_hgrn_kernel, layer=layer, chunk=HG_CHUNK),
        grid=(b, HG_HEADS),
        in_specs=[
            pl.BlockSpec((depth, 2, LANES), lambda bi, h: (0, 0, h)),
            pl.BlockSpec((1, HG_V), lambda bi, h: (0, 0)),
            col(_HQ), col(_HFF), col(_HFB), col(_HI), col(_HG),
        ],
        out_specs=pl.BlockSpec((None, t, LANES), lambda bi, h: (bi, 0, h)),
        out_shape=jax.ShapeDtypeStruct((b, t, HG_HEADS * HG_V), jnp.bfloat16),
        scratch_shapes=[
            pltpu.VMEM((t, HG_V), jnp.float32),
            pltpu.VMEM((t, HG_V), jnp.float32),
            pltpu.VMEM((HG_V, HG_K), jnp.float32),
            pltpu.VMEM((HG_V, HG_K), jnp.float32),
        ],
        compiler_params=_cparams(("parallel", "parallel")),
        name="hgrn_bidir",
    )(hgrn_lb, norm_g.reshape(1, HG_V).astype(jnp.float32),
      proj3d, proj3d, proj3d, proj3d, proj3d)


def _out_router_kernel(a_ref, h_ref, wa_ref, wh_ref, x_ref, g_ref, wrt_ref,
                       o_ref, ha_ref, hb_ref, afft_ref):
    y = jnp.dot(a_ref[...], wa_ref[...], preferred_element_type=jnp.float32)
    y = y + jnp.dot(h_ref[...], wh_ref[...], preferred_element_type=jnp.float32)
    x = x_ref[...] + y
    o_ref[...] = x
    h = x * lax.rsqrt(jnp.mean(x * x, axis=-1, keepdims=True) + EPS) * g_ref[...]
    h_hi = h.astype(jnp.bfloat16)
    h_lo = (h - h_hi.astype(jnp.float32)).astype(jnp.bfloat16)
    bits = lax.bitcast_convert_type(h_hi.astype(jnp.float32), jnp.uint32)
    q = ha_ref.shape[1]
    ha_ref[...] = (bits[:, 0:q] >> 16) | bits[:, q:2 * q]
    hb_ref[...] = (bits[:, 2 * q:3 * q] >> 16) | bits[:, 3 * q:4 * q]
    ne = afft_ref.shape[0]
    part = lax.dot_general(wrt_ref[...], h_hi, _NT, preferred_element_type=jnp.float32)
    logits = (part[:ne] + part[ne:]
              + lax.dot_general(wrt_ref[0:ne, :], h_lo, _NT,
                                preferred_element_type=jnp.float32))
    ex = jnp.exp(logits - jnp.max(logits, axis=0, keepdims=True))
    afft_ref[...] = ex / jnp.sum(ex, axis=0, keepdims=True)


def out_proj_router(att2d, hgo2d, w_out_bf16, x2d, g, w_router, tm=512):
    n, d = x2d.shape
    tm = min(tm, n)
    wa = att2d.shape[1]
    wh = hgo2d.shape[1]
    e = w_router.shape[1]
    w_hi = w_router.T.astype(jnp.bfloat16)
    w_lo = (w_router.T - w_hi.astype(jnp.float32)).astype(jnp.bfloat16)
    wrt = jnp.concatenate([w_hi, w_lo], axis=0)
    rows = lambda w: pl.BlockSpec((tm, w), lambda i: (i, 0))
    return pl.pallas_call(
        _out_router_kernel,
        grid=(n // tm,),
        in_specs=[
            rows(wa), rows(wh),
            pl.BlockSpec((wa, d), lambda i: (0, 0)),
            pl.BlockSpec((wh, d), lambda i: (wa // wh, 0)),
            rows(d),
            pl.BlockSpec((1, d), lambda i: (0, 0)),
            pl.BlockSpec((2 * e, d), lambda i: (0, 0)),
        ],
        out_specs=[rows(d), rows(d // 4), rows(d // 4), pl.BlockSpec((e, tm), lambda i: (0, i))],
        out_shape=[
            jax.ShapeDtypeStruct((n, d), jnp.float32),
            jax.ShapeDtypeStruct((n, d // 4), jnp.uint32),
            jax.ShapeDtypeStruct((n, d // 4), jnp.uint32),
            jax.ShapeDtypeStruct((e, n), jnp.float32),
        ],
        compiler_params=_cparams(("parallel",)),
        name="out_proj_router",
    )(att2d, hgo2d, w_out_bf16, w_out_bf16, x2d, g.reshape(1, d), wrt)


_SEL_CHUNK = 1024


def _select_kernel(afft_ref, pos_ref, offs_ref, *, cap):
    f32, i32 = jnp.float32, jnp.int32
    e, n = afft_ref.shape
    chunk = min(_SEL_CHUNK, n)
    nt = n // LANES

    def bits_at(start, size):
        return lax.bitcast_convert_type(afft_ref[:, pl.ds(start, size)], i32)

    def count_ge(cand):
        def body(c, acc):
            x = bits_at(pl.multiple_of(c * chunk, chunk), chunk)
            return acc + jnp.where(x >= cand, 1.0, 0.0)
        acc = lax.fori_loop(0, n // chunk, body, jnp.zeros((e, chunk), f32))
        return jnp.sum(acc, axis=1, keepdims=True)

    def bit_body(it, prefix):
        cand = prefix | jnp.left_shift(jnp.int32(1), 30 - it)
        return jnp.where(count_ge(cand) >= cap, cand, prefix)

    thr = lax.fori_loop(0, 31, bit_body, jnp.zeros((e, 1), i32))
    need = cap - count_ge(thr + 1)

    tri = (lax.broadcasted_iota(i32, (LANES, LANES), 0)
           <= lax.broadcasted_iota(i32, (LANES, LANES), 1)).astype(jnp.bfloat16)
    tile_id = lax.broadcasted_iota(i32, (e, nt), 1)

    def tile_body(j, carry):
        c_gt, c_eq, offs_acc = carry
        x = bits_at(pl.multiple_of(j * LANES, LANES), LANES)
        gt = x > thr
        eq = x == thr
        both = jnp.concatenate([jnp.where(gt, 1.0, 0.0), jnp.where(eq, 1.0, 0.0)],
                               axis=0).astype(jnp.bfloat16)
        inc = jnp.dot(both, tri, preferred_element_type=f32)
        inc_gt = inc[:e] + c_gt
        inc_eq = inc[e:] + c_eq
        sel = gt | (eq & (inc_eq <= need))
        incl = inc_gt + jnp.minimum(inc_eq, need)
        pos_ref[:, pl.ds(pl.multiple_of(j * LANES, LANES), LANES)] = jnp.where(
            sel, incl - 1.0, -1.0).astype(i32)
        offs_acc = jnp.where(tile_id == j, c_gt + jnp.minimum(c_eq, need), offs_acc)
        return inc_gt[:, LANES - 1:LANES], inc_eq[:, LANES - 1:LANES], offs_acc

    zero = jnp.zeros((e, 1), f32)
    _, _, offs_acc = lax.fori_loop(0, nt, tile_body, (zero, zero, jnp.zeros((e, nt), f32)))
    offs_ref[...] = offs_acc.astype(i32)


def ec_select(afft, cap):
    e, n = afft.shape
    return pl.pallas_call(
        functools.partial(_select_kernel, cap=cap),
        out_shape=[jax.ShapeDtypeStruct((e, n), jnp.int32),
                   jax.ShapeDtypeStruct((e, n // LANES), jnp.int32)],
        compiler_params=pltpu.CompilerParams(vmem_limit_bytes=_VMEM_LIMIT),
        name="ec_select",
    )(afft)


_SC_LANES = 16
_SC_CORES = 2
_INV_CHUNK = 8192


def ec_invert(pos, afft, cap):
    ne, n = pos.shape
    ch = min(_INV_CHUNK, n)
    mesh = plsc.VectorSubcoreMesh(core_axis_name="c", subcore_axis_name="s")

    @functools.partial(
        pl.kernel, mesh=mesh,
        out_type=[jax.ShapeDtypeStruct((ne * cap,), jnp.int32),
                  jax.ShapeDtypeStruct((ne * cap,), jnp.float32)],
        scratch_types=[pltpu.VMEM((ch,), jnp.int32), pltpu.VMEM((ch,), jnp.float32),
                       pltpu.VMEM((cap,), jnp.int32), pltpu.VMEM((cap,), jnp.float32)],
        compiler_params=pltpu.CompilerParams(needs_layout_passes=False))
    def invert(pos_hbm, aff_hbm, idx_hbm, gate_hbm, pos_v, aff_v, idx_v, gate_v):
        wid = lax.axis_index("s") * _SC_CORES + lax.axis_index("c")

        @pl.when(wid < ne)
        def _():
            @pl.loop(0, n // ch)
            def _(c):
                pltpu.sync_copy(pos_hbm.at[pl.ds(wid * n + c * ch, ch)], pos_v)
                pltpu.sync_copy(aff_hbm.at[pl.ds(wid * n + c * ch, ch)], aff_v)

                @pl.loop(0, ch, step=_SC_LANES)
                def _(i):
                    p = pos_v[pl.ds(i, _SC_LANES)]
                    tok = lax.iota(jnp.int32, _SC_LANES) + (c * ch + i)
                    plsc.store_scatter(idx_v, [p], tok, mask=p >= 0)
                    plsc.store_scatter(gate_v, [p], aff_v[pl.ds(i, _SC_LANES)], mask=p >= 0)

            pltpu.sync_copy(idx_v, idx_hbm.at[pl.ds(wid * cap, cap)])
            pltpu.sync_copy(gate_v, gate_hbm.at[pl.ds(wid * cap, cap)])

    idx, gates = invert(pos.reshape(ne * n), afft.reshape(ne * n))
    return idx.reshape(ne, cap), gates.reshape(ne, cap)


_GATHER_WIN = 128


def ec_gather(table, idx_row):
    m = idx_row.shape[1]
    width = table.shape[1]
    steps = m // (_GATHER_WIN * _SC_CORES)
    mesh = plsc.VectorSubcoreMesh(core_axis_name="c", subcore_axis_name="s")

    @functools.partial(
        pl.kernel, mesh=mesh,
        out_type=jax.ShapeDtypeStruct((m, width), table.dtype),
        scratch_types=[])
    def gather(x_hbm, i_hbm, o_hbm):
        def body(i_vmem, o_vmem):
            pltpu.sync_copy(x_hbm.at[i_vmem.at[0]], o_vmem)

        pltpu.emit_pipeline(
            body,
            grid=(_SC_CORES, steps),
            in_specs=[pl.BlockSpec((1, _GATHER_WIN), index_map=lambda a, i: (0, a * steps + i))],
            out_specs=[pl.BlockSpec((_GATHER_WIN, width),
                                    index_map=lambda a, i: (a * steps + i, 0))],
            core_axis_name=("c", "s"),
            dimension_semantics=(pltpu.PARALLEL, pltpu.PARALLEL),
        )(i_hbm, o_hbm)

    return gather(table, idx_row)


def _expert_kernel(xa_ref, xb_ref, gate_ref, wg_ref, wu_ref, wd_ref, o_ref, wg_sc, wu_sc, wd_sc):
    @pl.when(pl.program_id(1) == 0)
    def _():
        wg_sc[...] = wg_ref[...].astype(jnp.bfloat16)
        wu_sc[...] = wu_ref[...].astype(jnp.bfloat16)
        wd_sc[...] = wd_ref[...].astype(jnp.bfloat16)

    @pl.when(pl.program_id(1) == pl.num_programs(1) - 1)
    def _():
        o_ref[...] = jnp.zeros(o_ref.shape, o_ref.dtype)

    @pl.when(pl.program_id(1) < pl.num_programs(1) - 1)
    def _():
        as_f32 = lambda bits: lax.bitcast_convert_type(bits, jnp.float32)
        wa, wb = xa_ref[...], xb_ref[...]
        hi = jnp.uint32(0xFFFF0000)
        xs = jnp.concatenate([as_f32(wa << 16), as_f32(wa & hi), as_f32(wb << 16),
                              as_f32(wb & hi)], axis=1).astype(jnp.bfloat16)
        a = jnp.dot(xs, wg_sc[...], preferred_element_type=jnp.float32)
        u = jnp.dot(xs, wu_sc[...], preferred_element_type=jnp.float32)
        hid = (a * jax.nn.sigmoid(a) * u).astype(jnp.bfloat16)
        y = jnp.dot(hid, wd_sc[...], preferred_element_type=jnp.float32)
        gate_col = jnp.broadcast_to(gate_ref[...], (LANES, gate_ref.shape[1])).T[:, 0:1]
        o_ref[...] = (y * gate_col).astype(o_ref.dtype)


def expert_ffn(xa, xb, gates, wg, wu, wd, layer, tm=512):
    e, c, q = xa.shape
    d = 4 * q
    tm = min(tm, c)
    f = wg.shape[3]
    nt = c // tm
    return pl.pallas_call(
        _expert_kernel,
        grid=(e, nt + 1),
        in_specs=[
            pl.BlockSpec((None, tm, q), lambda ei, i: (ei, jnp.minimum(i, nt - 1), 0)),
            pl.BlockSpec((None, tm, q), lambda ei, i: (ei, jnp.minimum(i, nt - 1), 0)),
            pl.BlockSpec((None, 1, tm), lambda ei, i: (ei, 0, jnp.minimum(i, nt - 1))),
            pl.BlockSpec((None, None, d, f), lambda ei, i: (layer, ei, 0, 0)),
            pl.BlockSpec((None, None, d, f), lambda ei, i: (layer, ei, 0, 0)),
            pl.BlockSpec((None, None, f, d), lambda ei, i: (layer, ei, 0, 0)),
        ],
        out_specs=pl.BlockSpec((None, tm, d), lambda ei, i: (ei, i, 0)),
        out_shape=jax.ShapeDtypeStruct((e, c + tm, d), jnp.bfloat16),
        scratch_shapes=[pltpu.VMEM((d, f), jnp.bfloat16), pltpu.VMEM((d, f), jnp.bfloat16),
                        pltpu.VMEM((f, d), jnp.bfloat16)],
        compiler_params=_cparams(("parallel", "arbitrary")),
        name="expert_ffn",
    )(xa, xb, gates.reshape(e, 1, c), wg, wu, wd)


_SLAB_ALIGN = 16
_SLAB_ROWS = LANES + _SLAB_ALIGN


def _slab_copy(offs_ref, out_hbm, slab_ref, sem_ref, e, j, slot):
    start = pl.multiple_of((offs_ref[e, j] // _SLAB_ALIGN) * _SLAB_ALIGN, _SLAB_ALIGN)
    return pltpu.make_async_copy(out_hbm.at[e, pl.ds(start, _SLAB_ROWS), :],
                                 slab_ref.at[slot, e], sem_ref.at[slot])


def _combine_kernel(offs_ref, x_ref, pos_ref, fg_ref, out_hbm, o_ref, slab_ref, sem_ref, *,
                    final_norm):
    f32 = jnp.float32
    ne = pos_ref.shape[0]
    j = pl.program_id(0)
    slot = j % 2

    @pl.when(j == 0)
    def _():
        for e in range(ne):
            _slab_copy(offs_ref, out_hbm, slab_ref, sem_ref, e, j, slot).start()

    @pl.when(j + 1 < pl.num_programs(0))
    def _():
        for e in range(ne):
            _slab_copy(offs_ref, out_hbm, slab_ref, sem_ref, e, j + 1, 1 - slot).start()

    pad = jnp.zeros((LANES - ne, LANES), f32)
    pos_t = jnp.concatenate([pos_ref[...].astype(f32), pad], axis=0).T
    row_id = lax.broadcasted_iota(jnp.int32, (LANES, _SLAB_ROWS), 1).astype(f32)
    pltpu.make_async_copy(out_hbm.at[pl.ds(0, ne), pl.ds(0, _SLAB_ROWS), :], slab_ref.at[slot],
                          sem_ref.at[slot]).wait()
    y = x_ref[...]
    for e in range(ne):
        start = (offs_ref[e, j] // _SLAB_ALIGN) * _SLAB_ALIGN
        rel = pos_t[:, e:e + 1] - start.astype(f32)
        onehot = jnp.where(rel == row_id, 1.0, 0.0).astype(jnp.bfloat16)
        y = y + jnp.dot(onehot, slab_ref[slot, e], preferred_element_type=f32)
    if final_norm:
        y = y * lax.rsqrt(jnp.mean(y * y, axis=-1, keepdims=True) + EPS) * fg_ref[...]
    o_ref[...] = y


def ec_combine(x2d, pos, offs, out, final_g, final_norm):
    n, d = x2d.shape
    ne, crows, _ = out.shape
    assert crows >= n * EC_CAPACITY // ne + _SLAB_ROWS
    grid_spec = pltpu.PrefetchScalarGridSpec(
        num_scalar_prefetch=1,
        grid=(n // LANES,),
        in_specs=[
            pl.BlockSpec((LANES, d), lambda j, offs_ref: (j, 0)),
            pl.BlockSpec((ne, LANES), lambda j, offs_ref: (0, j)),
            pl.BlockSpec((1, d), lambda j, offs_ref: (0, 0)),
            pl.BlockSpec(memory_space=pl.ANY),
        ],
        out_specs=pl.BlockSpec((LANES, d), lambda j, offs_ref: (j, 0)),
        scratch_shapes=[pltpu.VMEM((2, ne, _SLAB_ROWS, d), out.dtype),
                        pltpu.SemaphoreType.DMA((2,))],
    )
    return pl.pallas_call(
        functools.partial(_combine_kernel, final_norm=final_norm),
        grid_spec=grid_spec,
        out_shape=jax.ShapeDtypeStruct((n, d), jnp.float32),
        compiler_params=_cparams(("arbitrary",)),
        name="ec_combine",
    )(offs, x2d, pos, final_g.reshape(1, d), out)


def _trunk(x, p):
    b, t, d = x.shape
    n = b * t
    depth = p["w_in"].shape[0]
    x2d = x.reshape(n, d)
    cap = max(1, EC_CAPACITY * n // N_EXPERTS)
    for layer in range(depth):
        proj = norm_proj(x2d, p["norm_mix_g"][layer], p["w_in"][layer])
        proj3d = proj.reshape(b, t, -1)
        att = diff_attention(proj3d, p["lambda_q1"][layer], p["lambda_k1"][layer],
                             p["lambda_q2"][layer], p["lambda_k2"][layer],
                             p["diff_subln_g"][layer], layer)
        hgo = hgrn_bidir(proj3d, p["hgrn_lb"], p["hgrn_norm_g"][layer], layer)
        x2d, ha, hb, afft = out_proj_router(att.reshape(n, -1), hgo.reshape(n, -1),
                                            p["w_out"][layer], x2d, p["norm_ffn_g"][layer],
                                            p["w_router"][layer])
        pos, offs = ec_select(afft, cap)
        ne = afft.shape[0]
        idx, gates = ec_invert(pos, afft, cap)
        idx = idx.reshape(1, ne * cap)
        xa = ec_gather(ha, idx).reshape(ne, cap, -1)
        xb = ec_gather(hb, idx).reshape(ne, cap, -1)
        out = expert_ffn(xa, xb, gates, p["w_gate"], p["w_up"], p["w_down"], layer)
        x2d = ec_combine(x2d, pos, offs, out, p["final_norm_g"], layer == depth - 1)
    return x2d.reshape(b, t, d)


def kernel(x_prompt, x_sample, norm_mix_g, w_in, lambda_q1, lambda_k1, lambda_q2, lambda_k2,
           diff_subln_g, hgrn_lb, hgrn_norm_g, w_out, norm_ffn_g, w_router, w_gate, w_up,
           w_down, final_norm_g):
    bf = jnp.bfloat16
    p = dict(norm_mix_g=norm_mix_g, w_in=w_in.astype(bf), lambda_q1=lambda_q1,
             lambda_k1=lambda_k1, lambda_q2=lambda_q2, lambda_k2=lambda_k2,
             diff_subln_g=diff_subln_g, hgrn_lb=hgrn_lb, hgrn_norm_g=hgrn_norm_g,
             w_out=w_out.astype(bf), norm_ffn_g=norm_ffn_g, w_router=w_router,
             w_gate=w_gate, w_up=w_up, w_down=w_down,
             final_norm_g=final_norm_g)
    return (_trunk(x_prompt, p), _trunk(x_sample, p))
```

```python
import functools
import math

import jax
import jax.numpy as jnp
from jax import lax
from jax.experimental import pallas as pl
from jax.experimental.pallas import tpu as pltpu
from jax.experimental.pallas import tpu_sc as plsc

ATT_HEADS = 4
ATT_QK_DIM = 64
ATT_V_DIM = 128
HG_HEADS = 4
HG_K = 128
HG_V = 128
HG_CHUNK = 64
N_EXPERTS = 16
EC_CAPACITY = 2
EPS = 1e-6
LANES = 128
SUBLANES = 8

_QBLK, _KBLK, _VBLK = 0, 4, 8
_HQ, _HFF, _HFB, _HI, _HG = 12, 16, 20, 24, 28

_VMEM_LIMIT = 56 * 1024 * 1024

_NT = (((1,), (1,)), ((), ()))


def _cparams(sem):
    return pltpu.CompilerParams(dimension_semantics=sem, vmem_limit_bytes=_VMEM_LIMIT)


def _norm_proj_kernel(x_ref, g_ref, w_ref, o_ref, *, n_split):
    x = x_ref[...]
    y = x * lax.rsqrt(jnp.mean(x * x, axis=-1, keepdims=True) + EPS) * g_ref[...]
    h = y.astype(jnp.bfloat16)
    wn = w_ref.shape[1] // n_split
    for c in range(n_split):
        o_ref[:, c * wn:(c + 1) * wn] = jnp.dot(
            h, w_ref[:, c * wn:(c + 1) * wn], preferred_element_type=jnp.float32
        ).astype(o_ref.dtype)


def norm_proj(x2d, g, w_bf16, tm=512):
    n, d = x2d.shape
    tm = min(tm, n)
    dout = w_bf16.shape[1]
    return pl.pallas_call(
        functools.partial(_norm_proj_kernel, n_split=max(1, dout // 1024)),
        grid=(n // tm,),
        in_specs=[
            pl.BlockSpec((tm, d), lambda i: (i, 0)),
            pl.BlockSpec((1, d), lambda i: (0, 0)),
            pl.BlockSpec((d, dout), lambda i: (0, 0)),
        ],
        out_specs=pl.BlockSpec((tm, dout), lambda i: (i, 0)),
        out_shape=jax.ShapeDtypeStruct((n, dout), jnp.bfloat16),
        compiler_params=_cparams(("parallel",)),
        name="norm_proj",
    )(x2d, g.reshape(1, d), w_bf16)


_LOG2E = 1.4426950408889634
_Q_STRIP = 128
_ONES_ROWS = 16


def _split3(x):
    bf = jnp.bfloat16
    h = x.astype(bf).astype(jnp.float32)
    r = x - h
    m = r.astype(bf).astype(jnp.float32)
    lo = (r - m).astype(bf).astype(jnp.float32)
    return h, m, lo


def _attn_kernel(beta_ref, qext_ref, lq1_ref, lk1_ref, lq2_ref, lk2_ref, subg_ref,
                 q_ref, k_ref, v_ref, kext_ref, o_ref, vt_sc, acc_sc, s_sc, *, lam_init, tk, tq):
    f32, bf = jnp.float32, jnp.bfloat16
    t = q_ref.shape[0]
    nk = t // tk
    qs_ = _Q_STRIP
    ng = tq // qs_
    ncol = 2 * tq
    dv = v_ref.shape[1]
    beta = beta_ref[0:1, 0:1]

    def vt_body(c, carry):
        r0 = pl.multiple_of(c * tk, tk)
        vt_sc[0:dv, pl.ds(r0, tk)] = v_ref[pl.ds(r0, tk), :].astype(f32).T.astype(bf)
        return carry

    lax.fori_loop(0, nk, vt_body, 0)
    vt_sc[dv:dv + _ONES_ROWS, :] = jnp.ones((_ONES_ROWS, t), bf)

    lam = (jnp.exp(jnp.sum(lq1_ref[...] * lk1_ref[...], axis=-1, keepdims=True))
           - jnp.exp(jnp.sum(lq2_ref[...] * lk2_ref[...], axis=-1, keepdims=True)) + lam_init)
    lane = lax.broadcasted_iota(jnp.int32, (qs_, LANES), 1)
    q_minus_r = (lax.broadcasted_iota(jnp.int32, (tk, qs_), 1)
                 - lax.broadcasted_iota(jnp.int32, (tk, qs_), 0)).astype(f32)
    cscale = ATT_QK_DIM ** -0.5 * _LOG2E

    def group(gi, carry):
        g0 = pl.multiple_of(gi * tq, tq)
        tiles = []
        for u in range(ng):
            q0 = g0 + u * qs_
            qs = (q_ref[pl.ds(q0, qs_), :].astype(f32) * cscale).astype(bf).astype(f32)
            qext = qext_ref[...]
            q_aug = jnp.concatenate([
                jnp.concatenate([jnp.where(lane < ATT_QK_DIM, qs, 0.0), qext], axis=1),
                jnp.concatenate([jnp.where(lane >= ATT_QK_DIM, qs, 0.0), qext], axis=1)], axis=0)
            tiles.append(q_aug.T.astype(bf))
        w = jnp.concatenate(tiles, axis=1)
        jc = g0 // tk

        def corr_tile(d):
            corr = 2.0 * beta * jnp.minimum(q_minus_r + d.astype(f32), 0.0)
            return jnp.concatenate([corr, corr], axis=1)

        def shift(d, sidx):
            sgn = 1.0 if sidx == 0 else -1.0
            return (-sgn) * beta * d.astype(f32)

        def scores(n, sidx, crossing):
            r0 = pl.multiple_of(n * tk, tk)
            k_aug = jnp.concatenate([k_ref[pl.ds(r0, tk), :], kext_ref[sidx]], axis=1)
            for u in range(ng):
                c0 = u * 2 * qs_
                s = jnp.dot(k_aug, w[:, c0:c0 + 2 * qs_], preferred_element_type=f32)
                if crossing:
                    s = s + corr_tile(g0 + u * qs_ - r0)
                s_sc[:, c0:c0 + 2 * qs_] = s

        def step(n, mc, sidx, crossing):
            r0 = pl.multiple_of((n - 1) * tk, tk)
            r1 = pl.multiple_of(n * tk, tk)
            vt = vt_sc[:, pl.ds(r0, tk)]
            if sidx is not None:
                k_aug = jnp.concatenate([k_ref[pl.ds(r1, tk), :], kext_ref[sidx]], axis=1)
            m, csts = mc
            ms, cs = [], []
            for u in range(ng):
                c0 = u * 2 * qs_
                s = s_sc[:, c0:c0 + 2 * qs_]
                m_old = m[:, c0:c0 + 2 * qs_]
                cst = csts[:, c0:c0 + 2 * qs_]
                m_new = jnp.maximum(m_old, jnp.max(s, axis=0, keepdims=True) + cst)
                alpha = jnp.exp2(m_old - m_new)
                p = jnp.exp2(s - (m_new - cst)).astype(bf)
                acc_sc[:, c0:c0 + 2 * qs_] = (acc_sc[:, c0:c0 + 2 * qs_] * alpha
                                              + jnp.dot(vt, p, preferred_element_type=f32))
                ms.append(m_new)
                if sidx is not None:
                    s = jnp.dot(k_aug, w[:, c0:c0 + 2 * qs_], preferred_element_type=f32)
                    if crossing:
                        s = s + corr_tile(g0 + u * qs_ - r1)
                    s_sc[:, c0:c0 + 2 * qs_] = s
                    cs.append(jnp.broadcast_to(shift(g0 + u * qs_ - r1, sidx), (1, 2 * qs_)))
            return jnp.concatenate(ms, axis=1), (jnp.concatenate(cs, axis=1) if cs else csts)

        acc_sc[...] = jnp.zeros(acc_sc.shape, f32)
        m = jnp.full((1, ncol), -1e30, f32)
        scores(0, 0, True)
        n1 = jnp.maximum(jc, 1)
        n2 = jc + tq // tk
        cst0 = jnp.concatenate([jnp.broadcast_to(shift(g0 + u * qs_, 0), (1, 2 * qs_))
                                for u in range(ng)], axis=1)
        mc = (m, cst0)
        mc = lax.fori_loop(1, n1, functools.partial(step, sidx=0, crossing=False), mc)
        mc = lax.fori_loop(n1, n2, functools.partial(step, sidx=0, crossing=True), mc)
        mc = lax.fori_loop(n2, nk, functools.partial(step, sidx=1, crossing=False), mc)
        step(nk, mc, None, False)

        for u in range(ng):
            c0 = u * 2 * qs_
            acc = acc_sc[:, c0:c0 + 2 * qs_]
            den = acc[dv:dv + 1, :]
            o = acc[0:dv, 0:qs_] / den[:, 0:qs_] - lam * (acc[0:dv, qs_:] / den[:, qs_:])
            y = o * lax.rsqrt(jnp.mean(o * o, axis=0, keepdims=True) + EPS) * subg_ref[...]
            o_ref[pl.ds(g0 + u * qs_, qs_), :] = (y * (1.0 - lam_init)).T.astype(o_ref.dtype)
        return carry

    lax.fori_loop(0, t // tq, group, 0)


def _attn_tables(tk):
    f32 = jnp.float32
    slopes = 2.0 ** (-8.0 * (jnp.arange(ATT_HEADS, dtype=f32) + 1.0) / ATT_HEADS)
    beta = slopes * _LOG2E
    lane = jnp.arange(LANES)[None, None, :]
    b = _split3(beta)
    a = _split3(-beta[:, None] * jnp.arange(_Q_STRIP, dtype=f32)[None, :])
    qext = jnp.zeros((ATT_HEADS, _Q_STRIP, LANES), f32)
    for c in range(3):
        qext = jnp.where(lane == c, b[c][:, None, None], qext)
        qext = jnp.where(lane == 3 + c, a[c][:, :, None], qext)
    r = jnp.arange(tk, dtype=f32)[:, None]
    lane2 = jnp.arange(LANES)[None, :]
    base = jnp.where(lane2 < 3, r, jnp.where(lane2 < 6, 1.0, 0.0))
    kext = jnp.stack([base, -base]).astype(jnp.bfloat16)
    beta_t = jnp.broadcast_to(beta[:, None, None], (ATT_HEADS, 1, LANES))
    return beta_t, qext, kext


def diff_attention(proj3d, lq1, lk1, lq2, lk2, subg, layer, tk=256, tq=2048):
    b, t, _ = proj3d.shape
    tk, tq = min(tk, t), min(tq, t)
    assert tq % tk == 0 and tq % _Q_STRIP == 0
    lam_init = 0.8 - 0.6 * math.exp(-0.3 * layer)
    assert tk <= 256
    beta_t, qext, kext = _attn_tables(tk)
    vec = lambda a: a.reshape(1, -1).astype(jnp.float32)
    small = lambda n: pl.BlockSpec((1, n), lambda bi, h: (0, 0))
    head = pl.BlockSpec((None, 1, LANES), lambda bi, h: (h, 0, 0))
    col = lambda blk: pl.BlockSpec((None, t, LANES), lambda bi, h: (bi, 0, blk + h))
    return pl.pallas_call(
        functools.partial(_attn_kernel, lam_init=lam_init, tk=tk, tq=tq),
        grid=(b, ATT_HEADS),
        in_specs=[
            head, pl.BlockSpec((None, _Q_STRIP, LANES), lambda bi, h: (h, 0, 0)),
            small(ATT_QK_DIM), small(ATT_QK_DIM), small(ATT_QK_DIM), small(ATT_QK_DIM),
            pl.BlockSpec((ATT_V_DIM, 1), lambda bi, h: (0, 0)),
            col(_QBLK), col(_KBLK), col(_VBLK),
            pl.BlockSpec((2, tk, LANES), lambda bi, h: (0, 0, 0)),
        ],
        out_specs=pl.BlockSpec((None, t, LANES), lambda bi, h: (bi, 0, h)),
        out_shape=jax.ShapeDtypeStruct((b, t, ATT_HEADS * ATT_V_DIM), jnp.bfloat16),
        scratch_shapes=[pltpu.VMEM((ATT_V_DIM + _ONES_ROWS, t), jnp.bfloat16),
                        pltpu.VMEM((ATT_V_DIM + _ONES_ROWS, 2 * tq), jnp.float32),
                        pltpu.VMEM((tk, 2 * tq), jnp.float32)],
        compiler_params=_cparams(("parallel", "parallel")),
        name="diff_attn",
    )(beta_t, qext, vec(lq1), vec(lk1), vec(lq2), vec(lk2),
      subg.reshape(ATT_V_DIM, 1).astype(jnp.float32), proj3d, proj3d, proj3d, kext)


_HG_CHUNKS_PER_ITER = 16
_HG_FIN_ROWS = 256


def _scan_rows(g, rev):
    c = g.shape[0]
    row = lax.broadcasted_iota(jnp.int32, g.shape, 0)
    b = g
    s = 1
    while s < c:
        if rev:
            b = b + jnp.where(row < c - s, pltpu.roll(b, c - s, axis=0), 0.0)
        else:
            b = b + jnp.where(row >= s, pltpu.roll(b, s, axis=0), 0.0)
        s *= 2
    return b


def _level_ref(b, m, rev):
    c, w = b.shape
    r = m // 2 if rev else m // 2 - 1
    pieces = []
    if m >= SUBLANES:
        for j in range(c // m):
            pieces.append(jnp.broadcast_to(b[j * m + r:j * m + r + 1, :], (m, w)))
    else:
        sub = lax.broadcasted_iota(jnp.int32, (SUBLANES, w), 0)
        for j in range(c // SUBLANES):
            base = j * SUBLANES
            acc = jnp.broadcast_to(b[base + r:base + r + 1, :], (SUBLANES, w))
            for i in range(1, SUBLANES // m):
                cand = jnp.broadcast_to(b[base + i * m + r:base + i * m + r + 1, :], (SUBLANES, w))
                acc = jnp.where(sub >= i * m, cand, acc)
            pieces.append(acc)
    return jnp.concatenate(pieces, axis=0)


def _neg_abs(x):
    bits = lax.bitcast_convert_type(x, jnp.uint32) | jnp.uint32(0x80000000)
    return lax.bitcast_convert_type(bits, jnp.float32)


def _hgrn_masks(c, rev):
    row = lax.broadcasted_iota(jnp.int32, (c, LANES), 0)
    ti = lax.broadcasted_iota(jnp.int32, (c, c), 0)
    si = lax.broadcasted_iota(jnp.int32, (c, c), 1)
    qrows, pairs = {}, {}
    m = c
    while m >= 2:
        late = (row & (m - 1)) >= m // 2
        t_late = (ti & (m - 1)) >= m // 2
        s_late = (si & (m - 1)) >= m // 2
        same = (ti ^ si) < m
        if rev:
            qrows[m] = jnp.logical_not(late)
            pairs[m] = same & jnp.logical_not(t_late) & s_late
        else:
            qrows[m] = late
            pairs[m] = same & t_late & jnp.logical_not(s_late)
        m //= 2
    return qrows, pairs, ti == si


def _hgrn_chunk(q, f, v, st_ref, rev, masks):
    qrows, pairs, diag = masks
    c = q.shape[0]
    bf = jnp.bfloat16
    k_ = 1.0 - f
    b = _scan_rows(jnp.log(f) * _LOG2E, rev)
    a = jnp.where(diag,
                  lax.dot_general(q.astype(bf), k_.astype(bf), _NT,
                                  preferred_element_type=jnp.float32), 0.0)
    m = c
    while m >= 2:
        if m == 2:
            cref = jnp.where(qrows[m], pltpu.roll(b, c - 1 if rev else 1, axis=0), b)
        else:
            cref = _level_ref(b, m, rev)
        e = jnp.exp2(_neg_abs(b - cref))
        z = (jnp.where(qrows[m], q, k_) * e).astype(bf)
        p = lax.dot_general(z, z, _NT, preferred_element_type=jnp.float32)
        a = jnp.where(pairs[m], p, a)
        m //= 2
    st = st_ref[...]
    o = jnp.dot(a.astype(bf), v.astype(bf), preferred_element_type=jnp.float32)
    o = o + lax.dot_general((q * jnp.exp2(b)).astype(bf), st.astype(bf), _NT,
                            preferred_element_type=jnp.float32)
    btot = b[0:1, :] if rev else b[c - 1:c, :]
    khat = (k_ * jnp.exp2(btot - b)).astype(bf)
    st_ref[...] = st * jnp.exp2(btot) + jnp.dot(v.T.astype(bf), khat,
                                                preferred_element_type=jnp.float32)
    return o


def _hgrn_kernel(lb_ref, ng_ref, q_ref, ff_ref, fb_ref, i_ref, g_ref, o_ref,
                 of_sc, ob_sc, stf_sc, stb_sc, *, layer, chunk):
    t = q_ref.shape[0]
    n = t // chunk
    lb_raw = lb_ref[...].astype(jnp.float32)
    ex = jnp.exp(lb_raw - jnp.max(lb_raw, axis=0, keepdims=True))
    sm = ex / jnp.sum(ex, axis=0, keepdims=True)
    lb = jnp.zeros_like(sm[0])
    for l_ in range(1, layer + 1):
        lb = lb + sm[l_]
    lb_f = lb[0:1, :]
    lb_b = lb[1:2, :]

    stf_sc[...] = jnp.zeros(stf_sc.shape, jnp.float32)
    stb_sc[...] = jnp.zeros(stb_sc.shape, jnp.float32)

    def load(ref, r0):
        return ref[pl.ds(r0, chunk), :].astype(jnp.float32)

    masks_f = _hgrn_masks(chunk, False)
    masks_b = _hgrn_masks(chunk, True)

    per_iter = _HG_CHUNKS_PER_ITER if n % _HG_CHUNKS_PER_ITER == 0 else 1

    def body(ci, carry):
        for u in range(per_iter):
            r0 = pl.multiple_of((ci * per_iter + u) * chunk, chunk)
            qr = load(q_ref, r0)
            f = lb_f + (1.0 - lb_f) * jax.nn.sigmoid(load(ff_ref, r0))
            of_sc[pl.ds(r0, chunk), :] = _hgrn_chunk(qr * jax.nn.sigmoid(qr), f,
                                                     load(i_ref, r0), stf_sc, False, masks_f)
            r1 = pl.multiple_of((n - 1 - ci * per_iter - u) * chunk, chunk)
            qr = load(q_ref, r1)
            f = lb_b + (1.0 - lb_b) * jax.nn.sigmoid(load(fb_ref, r1))
            ob_sc[pl.ds(r1, chunk), :] = _hgrn_chunk(qr * jax.nn.sigmoid(qr), f,
                                                     load(i_ref, r1), stb_sc, True, masks_b)
        return carry

    lax.fori_loop(0, n // per_iter, body, 0)

    rows = math.gcd(t, _HG_FIN_ROWS)

    def fin(ci, carry):
        r0 = pl.multiple_of(ci * rows, rows)
        o = of_sc[pl.ds(r0, rows), :] + ob_sc[pl.ds(r0, rows), :]
        y = o * lax.rsqrt(jnp.mean(o * o, axis=-1, keepdims=True) + EPS) * ng_ref[...]
        gr = g_ref[pl.ds(r0, rows), :].astype(jnp.float32)
        o_ref[pl.ds(r0, rows), :] = (y * (gr * jax.nn.sigmoid(gr))).astype(o_ref.dtype)
        return carry

    lax.fori_loop(0, t // rows, fin, 0)


def hgrn_bidir(proj3d, hgrn_lb, norm_g, layer):
    b, t, _ = proj3d.shape
    depth = hgrn_lb.shape[0]
    col = lambda blk: pl.BlockSpec((None, t, LANES), lambda bi, h: (bi, 0, blk + h))
    return pl.pallas_call(
        functools.partial(_hgrn_kernel, layer=layer, chunk=HG_CHUNK),
        grid=(b, HG_HEADS),
        in_specs=[
            pl.BlockSpec((depth, 2, LANES), lambda bi, h: (0, 0, h)),
            pl.BlockSpec((1, HG_V), lambda bi, h: (0, 0)),
            col(_HQ), col(_HFF), col(_HFB), col(_HI), col(_HG),
        ],
        out_specs=pl.BlockSpec((None, t, LANES), lambda bi, h: (bi, 0, h)),
        out_shape=jax.ShapeDtypeStruct((b, t, HG_HEADS * HG_V), jnp.bfloat16),
        scratch_shapes=[
            pltpu.VMEM((t, HG_V), jnp.float32),
            pltpu.VMEM((t, HG_V), jnp.float32),
            pltpu.VMEM((HG_V, HG_K), jnp.float32),
            pltpu.VMEM((HG_V, HG_K), jnp.float32),
        ],
        compiler_params=_cparams(("parallel", "parallel")),
        name="hgrn_bidir",
    )(hgrn_lb, norm_g.reshape(1, HG_V).astype(jnp.float32),
      proj3d, proj3d, proj3d, proj3d, proj3d)


def _out_router_kernel(a_ref, h_ref, wa_ref, wh_ref, x_ref, g_ref, wrt_ref,
                       o_ref, ha_ref, hb_ref, afft_ref):
    y = jnp.dot(a_ref[...], wa_ref[...], preferred_element_type=jnp.float32)
    y = y + jnp.dot(h_ref[...], wh_ref[...], preferred_element_type=jnp.float32)
    x = x_ref[...] + y
    o_ref[...] = x
    h = x * lax.rsqrt(jnp.mean(x * x, axis=-1, keepdims=True) + EPS) * g_ref[...]
    h_hi = h.astype(jnp.bfloat16)
    h_lo = (h - h_hi.astype(jnp.float32)).astype(jnp.bfloat16)
    bits = lax.bitcast_convert_type(h_hi.astype(jnp.float32), jnp.uint32)
    q = ha_ref.shape[1]
    ha_ref[...] = (bits[:, 0:q] >> 16) | bits[:, q:2 * q]
    hb_ref[...] = (bits[:, 2 * q:3 * q] >> 16) | bits[:, 3 * q:4 * q]
    ne = afft_ref.shape[0]
    part = lax.dot_general(wrt_ref[...], h_hi, _NT, preferred_element_type=jnp.float32)
    logits = (part[:ne] + part[ne:]
              + lax.dot_general(wrt_ref[0:ne, :], h_lo, _NT,
                                preferred_element_type=jnp.float32))
    ex = jnp.exp(logits - jnp.max(logits, axis=0, keepdims=True))
    afft_ref[...] = ex / jnp.sum(ex, axis=0, keepdims=True)


def out_proj_router(att2d, hgo2d, w_out_bf16, x2d, g, w_router, tm=512):
    n, d = x2d.shape
    tm = min(tm, n)
    wa = att2d.shape[1]
    wh = hgo2d.shape[1]
    e = w_router.shape[1]
    w_hi = w_router.T.astype(jnp.bfloat16)
    w_lo = (w_router.T - w_hi.astype(jnp.float32)).astype(jnp.bfloat16)
    wrt = jnp.concatenate([w_hi, w_lo], axis=0)
    rows = lambda w: pl.BlockSpec((tm, w), lambda i: (i, 0))
    return pl.pallas_call(
        _out_router_kernel,
        grid=(n // tm,),
        in_specs=[
            rows(wa), rows(wh),
            pl.BlockSpec((wa, d), lambda i: (0, 0)),
            pl.BlockSpec((wh, d), lambda i: (wa // wh, 0)),
            rows(d),
            pl.BlockSpec((1, d), lambda i: (0, 0)),
            pl.BlockSpec((2 * e, d), lambda i: (0, 0)),
        ],
        out_specs=[rows(d), rows(d // 4), rows(d // 4), pl.BlockSpec((e, tm), lambda i: (0, i))],
        out_shape=[
            jax.ShapeDtypeStruct((n, d), jnp.float32),
            jax.ShapeDtypeStruct((n, d // 4), jnp.uint32),
            jax.ShapeDtypeStruct((n, d // 4), jnp.uint32),
            jax.ShapeDtypeStruct((e, n), jnp.float32),
        ],
        compiler_params=_cparams(("parallel",)),
        name="out_proj_router",
    )(att2d, hgo2d, w_out_bf16, w_out_bf16, x2d, g.reshape(1, d), wrt)


_SEL_CHUNK = 1024


def _select_kernel(afft_ref, pos_ref, offs_ref, *, cap):
    f32, i32 = jnp.float32, jnp.int32
    e, n = afft_ref.shape
    chunk = min(_SEL_CHUNK, n)
    nt = n // LANES

    def bits_at(start, size):
        return lax.bitcast_convert_type(afft_ref[:, pl.ds(start, size)], i32)

    def count_ge(cand):
        def body(c, acc):
            x = bits_at(pl.multiple_of(c * chunk, chunk), chunk)
            return acc + jnp.where(x >= cand, 1.0, 0.0)
        acc = lax.fori_loop(0, n // chunk, body, jnp.zeros((e, chunk), f32))
        return jnp.sum(acc, axis=1, keepdims=True)

    def bit_body(it, prefix):
        cand = prefix | jnp.left_shift(jnp.int32(1), 30 - it)
        return jnp.where(count_ge(cand) >= cap, cand, prefix)

    thr = lax.fori_loop(0, 31, bit_body, jnp.zeros((e, 1), i32))
    need = cap - count_ge(thr + 1)

    tri = (lax.broadcasted_iota(i32, (LANES, LANES), 0)
           <= lax.broadcasted_iota(i32, (LANES, LANES), 1)).astype(jnp.bfloat16)
    tile_id = lax.broadcasted_iota(i32, (e, nt), 1)

    def tile_body(j, carry):
        c_gt, c_eq, offs_acc = carry
        x = bits_at(pl.multiple_of(j * LANES, LANES), LANES)
        gt = x > thr
        eq = x == thr
        both = jnp.concatenate([jnp.where(gt, 1.0, 0.0), jnp.where(eq, 1.0, 0.0)],
                               axis=0).astype(jnp.bfloat16)
        inc = jnp.dot(both, tri, preferred_element_type=f32)
        inc_gt = inc[:e] + c_gt
        inc_eq = inc[e:] + c_eq
        sel = gt | (eq & (inc_eq <= need))
        incl = inc_gt + jnp.minimum(inc_eq, need)
        pos_ref[:, pl.ds(pl.multiple_of(j * LANES, LANES), LANES)] = jnp.where(
            sel, incl - 1.0, -1.0).astype(i32)
        offs_acc = jnp.where(tile_id == j, c_gt + jnp.minimum(c_eq, need), offs_acc)
        return inc_gt[:, LANES - 1:LANES], inc_eq[:, LANES - 1:LANES], offs_acc

    zero = jnp.zeros((e, 1), f32)
    _, _, offs_acc = lax.fori_loop(0, nt, tile_body, (zero, zero, jnp.zeros((e, nt), f32)))
    offs_ref[...] = offs_acc.astype(i32)


def ec_select(afft, cap):
    e, n = afft.shape
    return pl.pallas_call(
        functools.partial(_select_kernel, cap=cap),
        out_shape=[jax.ShapeDtypeStruct((e, n), jnp.int32),
                   jax.ShapeDtypeStruct((e, n // LANES), jnp.int32)],
        compiler_params=pltpu.CompilerParams(vmem_limit_bytes=_VMEM_LIMIT),
        name="ec_select",
    )(afft)


_SC_LANES = 16
_SC_CORES = 2
_INV_CHUNK = 8192


def ec_invert(pos, afft, cap):
    ne, n = pos.shape
    ch = min(_INV_CHUNK, n)
    mesh = plsc.VectorSubcoreMesh(core_axis_name="c", subcore_axis_name="s")

    @functools.partial(
        pl.kernel, mesh=mesh,
        out_type=[jax.ShapeDtypeStruct((ne * cap,), jnp.int32),
                  jax.ShapeDtypeStruct((ne * cap,), jnp.float32)],
        scratch_types=[pltpu.VMEM((ch,), jnp.int32), pltpu.VMEM((ch,), jnp.float32),
                       pltpu.VMEM((cap,), jnp.int32), pltpu.VMEM((cap,), jnp.float32)],
        compiler_params=pltpu.CompilerParams(needs_layout_passes=False))
    def invert(pos_hbm, aff_hbm, idx_hbm, gate_hbm, pos_v, aff_v, idx_v, gate_v):
        wid = lax.axis_index("s") * _SC_CORES + lax.axis_index("c")

        @pl.when(wid < ne)
        def _():
            @pl.loop(0, n // ch)
            def _(c):
                pltpu.sync_copy(pos_hbm.at[pl.ds(wid * n + c * ch, ch)], pos_v)
                pltpu.sync_copy(aff_hbm.at[pl.ds(wid * n + c * ch, ch)], aff_v)

                @pl.loop(0, ch, step=_SC_LANES)
                def _(i):
                    p = pos_v[pl.ds(i, _SC_LANES)]
                    tok = lax.iota(jnp.int32, _SC_LANES) + (c * ch + i)
                    plsc.store_scatter(idx_v, [p], tok, mask=p >= 0)
                    plsc.store_scatter(gate_v, [p], aff_v[pl.ds(i, _SC_LANES)], mask=p >= 0)

            pltpu.sync_copy(idx_v, idx_hbm.at[pl.ds(wid * cap, cap)])
            pltpu.sync_copy(gate_v, gate_hbm.at[pl.ds(wid * cap, cap)])

    idx, gates = invert(pos.reshape(ne * n), afft.reshape(ne * n))
    return idx.reshape(ne, cap), gates.reshape(ne, cap)


_GATHER_WIN = 128


def ec_gather(table, idx_row):
    m = idx_row.shape[1]
    width = table.shape[1]
    steps = m // (_GATHER_WIN * _SC_CORES)
    mesh = plsc.VectorSubcoreMesh(core_axis_name="c", subcore_axis_name="s")

    @functools.partial(
        pl.kernel, mesh=mesh,
        out_type=jax.ShapeDtypeStruct((m, width), table.dtype),
        scratch_types=[])
    def gather(x_hbm, i_hbm, o_hbm):
        def body(i_vmem, o_vmem):
            pltpu.sync_copy(x_hbm.at[i_vmem.at[0]], o_vmem)

        pltpu.emit_pipeline(
            body,
            grid=(_SC_CORES, steps),
            in_specs=[pl.BlockSpec((1, _GATHER_WIN), index_map=lambda a, i: (0, a * steps + i))],
            out_specs=[pl.BlockSpec((_GATHER_WIN, width),
                                    index_map=lambda a, i: (a * steps + i, 0))],
            core_axis_name=("c", "s"),
            dimension_semantics=(pltpu.PARALLEL, pltpu.PARALLEL),
        )(i_hbm, o_hbm)

    return gather(table, idx_row)


def _expert_kernel(xa_ref, xb_ref, gate_ref, wg_ref, wu_ref, wd_ref, o_ref, wg_sc, wu_sc, wd_sc):
    @pl.when(pl.program_id(1) == 0)
    def _():
        wg_sc[...] = wg_ref[...].astype(jnp.bfloat16)
        wu_sc[...] = wu_ref[...].astype(jnp.bfloat16)
        wd_sc[...] = wd_ref[...].astype(jnp.bfloat16)

    @pl.when(pl.program_id(1) == pl.num_programs(1) - 1)
    def _():
        o_ref[...] = jnp.zeros(o_ref.shape, o_ref.dtype)

    @pl.when(pl.program_id(1) < pl.num_programs(1) - 1)
    def _():
        as_f32 = lambda bits: lax.bitcast_convert_type(bits, jnp.float32)
        wa, wb = xa_ref[...], xb_ref[...]
        hi = jnp.uint32(0xFFFF0000)
        xs = jnp.concatenate([as_f32(wa << 16), as_f32(wa & hi), as_f32(wb << 16),
                              as_f32(wb & hi)], axis=1).astype(jnp.bfloat16)
        a = jnp.dot(xs, wg_sc[...], preferred_element_type=jnp.float32)
        u = jnp.dot(xs, wu_sc[...], preferred_element_type=jnp.float32)
        hid = (a * jax.nn.sigmoid(a) * u).astype(jnp.bfloat16)
        y = jnp.dot(hid, wd_sc[...], preferred_element_type=jnp.float32)
        gate_col = jnp.broadcast_to(gate_ref[...], (LANES, gate_ref.shape[1])).T[:, 0:1]
        o_ref[...] = (y * gate_col).astype(o_ref.dtype)


def expert_ffn(xa, xb, gates, wg, wu, wd, layer, tm=512):
    e, c, q = xa.shape
    d = 4 * q
    tm = min(tm, c)
    f = wg.shape[3]
    nt = c // tm
    return pl.pallas_call(
        _expert_kernel,
        grid=(e, nt + 1),
        in_specs=[
            pl.BlockSpec((None, tm, q), lambda ei, i: (ei, jnp.minimum(i, nt - 1), 0)),
            pl.BlockSpec((None, tm, q), lambda ei, i: (ei, jnp.minimum(i, nt - 1), 0)),
            pl.BlockSpec((None, 1, tm), lambda ei, i: (ei, 0, jnp.minimum(i, nt - 1))),
            pl.BlockSpec((None, None, d, f), lambda ei, i: (layer, ei, 0, 0)),
            pl.BlockSpec((None, None, d, f), lambda ei, i: (layer, ei, 0, 0)),
            pl.BlockSpec((None, None, f, d), lambda ei, i: (layer, ei, 0, 0)),
        ],
        out_specs=pl.BlockSpec((None, tm, d), lambda ei, i: (ei, i, 0)),
        out_shape=jax.ShapeDtypeStruct((e, c + tm, d), jnp.bfloat16),
        scratch_shapes=[pltpu.VMEM((d, f), jnp.bfloat16), pltpu.VMEM((d, f), jnp.bfloat16),
                        pltpu.VMEM((f, d), jnp.bfloat16)],
        compiler_params=_cparams(("parallel", "arbitrary")),
        name="expert_ffn",
    )(xa, xb, gates.reshape(e, 1, c), wg, wu, wd)


_SLAB_ALIGN = 16
_SLAB_ROWS = LANES + _SLAB_ALIGN


def _slab_copy(offs_ref, out_hbm, slab_ref, sem_ref, e, j, slot):
    start = pl.multiple_of((offs_ref[e, j] // _SLAB_ALIGN) * _SLAB_ALIGN, _SLAB_ALIGN)
    return pltpu.make_async_copy(out_hbm.at[e, pl.ds(start, _SLAB_ROWS), :],
                                 slab_ref.at[slot, e], sem_ref.at[slot])


def _combine_kernel(offs_ref, x_ref, pos_ref, fg_ref, out_hbm, o_ref, slab_ref, sem_ref, *,
                    final_norm):
    f32 = jnp.float32
    ne = pos_ref.shape[0]
    j = pl.program_id(0)
    slot = j % 2

    @pl.when(j == 0)
    def _():
        for e in range(ne):
            _slab_copy(offs_ref, out_hbm, slab_ref, sem_ref, e, j, slot).start()

    @pl.when(j + 1 < pl.num_programs(0))
    def _():
        for e in range(ne):
            _slab_copy(offs_ref, out_hbm, slab_ref, sem_ref, e, j + 1, 1 - slot).start()

    pad = jnp.zeros((LANES - ne, LANES), f32)
    pos_t = jnp.concatenate([pos_ref[...].astype(f32), pad], axis=0).T
    row_id = lax.broadcasted_iota(jnp.int32, (LANES, _SLAB_ROWS), 1).astype(f32)
    pltpu.make_async_copy(out_hbm.at[pl.ds(0, ne), pl.ds(0, _SLAB_ROWS), :], slab_ref.at[slot],
                          sem_ref.at[slot]).wait()
    y = x_ref[...]
    for e in range(ne):
        start = (offs_ref[e, j] // _SLAB_ALIGN) * _SLAB_ALIGN
        rel = pos_t[:, e:e + 1] - start.astype(f32)
        onehot = jnp.where(rel == row_id, 1.0, 0.0).astype(jnp.bfloat16)
        y = y + jnp.dot(onehot, slab_ref[slot, e], preferred_element_type=f32)
    if final_norm:
        y = y * lax.rsqrt(jnp.mean(y * y, axis=-1, keepdims=True) + EPS) * fg_ref[...]
    o_ref[...] = y


def ec_combine(x2d, pos, offs, out, final_g, final_norm):
    n, d = x2d.shape
    ne, crows, _ = out.shape
    assert crows >= n * EC_CAPACITY // ne + _SLAB_ROWS
    grid_spec = pltpu.PrefetchScalarGridSpec(
        num_scalar_prefetch=1,
        grid=(n // LANES,),
        in_specs=[
            pl.BlockSpec((LANES, d), lambda j, offs_ref: (j, 0)),
            pl.BlockSpec((ne, LANES), lambda j, offs_ref: (0, j)),
            pl.BlockSpec((1, d), lambda j, offs_ref: (0, 0)),
            pl.BlockSpec(memory_space=pl.ANY),
        ],
        out_specs=pl.BlockSpec((LANES, d), lambda j, offs_ref: (j, 0)),
        scratch_shapes=[pltpu.VMEM((2, ne, _SLAB_ROWS, d), out.dtype),
                        pltpu.SemaphoreType.DMA((2,))],
    )
    return pl.pallas_call(
        functools.partial(_combine_kernel, final_norm=final_norm),
        grid_spec=grid_spec,
        out_shape=jax.ShapeDtypeStruct((n, d), jnp.float32),
        compiler_params=_cparams(("arbitrary",)),
        name="ec_combine",
    )(offs, x2d, pos, final_g.reshape(1, d), out)


def _trunk(x, p):
    b, t, d = x.shape
    n = b * t
    depth = p["w_in"].shape[0]
    x2d = x.reshape(n, d)
    cap = max(1, EC_CAPACITY * n // N_EXPERTS)
    for layer in range(depth):
        proj = norm_proj(x2d, p["norm_mix_g"][layer], p["w_in"][layer])
        proj3d = proj.reshape(b, t, -1)
        att = diff_attention(proj3d, p["lambda_q1"][layer], p["lambda_k1"][layer],
                             p["lambda_q2"][layer], p["lambda_k2"][layer],
                             p["diff_subln_g"][layer], layer)
        hgo = hgrn_bidir(proj3d, p["hgrn_lb"], p["hgrn_norm_g"][layer], layer)
        x2d, ha, hb, afft = out_proj_router(att.reshape(n, -1), hgo.reshape(n, -1),
                                            p["w_out"][layer], x2d, p["norm_ffn_g"][layer],
                                            p["w_router"][layer])
        pos, offs = ec_select(afft, cap)
        ne = afft.shape[0]
        idx, gates = ec_invert(pos, afft, cap)
        idx = idx.reshape(1, ne * cap)
        xa = ec_gather(ha, idx).reshape(ne, cap, -1)
        xb = ec_gather(hb, idx).reshape(ne, cap, -1)
        out = expert_ffn(xa, xb, gates, p["w_gate"], p["w_up"], p["w_down"], layer)
        x2d = ec_combine(x2d, pos, offs, out, p["final_norm_g"], layer == depth - 1)
    return x2d.reshape(b, t, d)


def kernel(x_prompt, x_sample, norm_mix_g, w_in, lambda_q1, lambda_k1, lambda_q2, lambda_k2,
           diff_subln_g, hgrn_lb, hgrn_norm_g, w_out, norm_ffn_g, w_router, w_gate, w_up,
           w_down, final_norm_g):
    bf = jnp.bfloat16
    p = dict(norm_mix_g=norm_mix_g, w_in=w_in.astype(bf), lambda_q1=lambda_q1,
             lambda_k1=lambda_k1, lambda_q2=lambda_q2, lambda_k2=lambda_k2,
             diff_subln_g=diff_subln_g, hgrn_lb=hgrn_lb, hgrn_norm_g=hgrn_norm_g,
             w_out=w_out.astype(bf), norm_ffn_g=norm_ffn_g, w_router=w_router,
             w_gate=w_gate, w_up=w_up, w_down=w_down,
             final_norm_g=final_norm_g)
    return (_trunk(x_prompt, p), _trunk(x_sample, p))
```
